```python
import jax, jax.numpy as jnp
from jax import lax
import numpy as np

D_MODEL = 1024
BATCH = 4
SEQ = 8192
DEPTH = 1

HEAD_DIM = 64
N_HEADS = D_MODEL // HEAD_DIM
N_HEADS_A = 10
N_KV_A = 2
N_HEADS_B = N_HEADS - N_HEADS_A
D_MIX = N_HEADS * HEAD_DIM
N_IDX_HEADS = 8
IDX_DIM = 64
TOPK_MAX = 256
QBLOCK = 128
DILATED_PATTERNS = ((128, 1), (512, 4), (2048, 16))
ROPE_THETA = 10000.0
EPS = 1e-6

SPLIT_SIZES = (
    N_HEADS_A * HEAD_DIM,
    N_KV_A * HEAD_DIM,
    N_KV_A * HEAD_DIM,
    N_IDX_HEADS * IDX_DIM,
    IDX_DIM,
    N_IDX_HEADS,
    N_HEADS_B * HEAD_DIM,
    N_HEADS_B * HEAD_DIM,
    N_HEADS_B * HEAD_DIM,
)
D_IN = sum(SPLIT_SIZES)
SPLIT_POINTS = tuple(int(v) for v in np.cumsum(SPLIT_SIZES)[:-1])

N_EXPERTS = 256
TOP_K = 8
N_GROUPS = 8
TOPK_GROUPS = 4
D_EXPERT = 256
D_SHARED = 256
ROUTED_SCALE = 2.5
MOE_BLOCK = 256

kernel_name = "hybrid_dsa_dilated_moe_adaln"

f32 = jnp.float32


def rmsnorm(t, g):
    tf = t.astype(f32)
    y = tf * lax.rsqrt(jnp.mean(tf * tf, axis=-1, keepdims=True) + EPS)
    return (y * g.astype(f32)).astype(t.dtype)


def rope(t, pos):
    half = t.shape[-1] // 2
    inv = ROPE_THETA ** (-jnp.arange(half, dtype=f32) / half)
    ang = pos.astype(f32)[..., None] * inv
    cos = jnp.cos(ang)[:, :, None, :]
    sin = jnp.sin(ang)[:, :, None, :]
    tf = t.astype(f32)
    t1, t2 = tf[..., :half], tf[..., half:]
    return jnp.concatenate([t1 * cos - t2 * sin, t2 * cos + t1 * sin], axis=-1).astype(t.dtype)


def sparse_index_attention(q, k, v, q_idx, k_idx, w_idx):
    B, S, HA, hd = q.shape
    n_kv = k.shape[2]
    grp = HA // n_kv
    n_sel = min(TOPK_MAX, S // 4)
    nb = S // QBLOCK
    w_scaled = w_idx.astype(f32) * (N_IDX_HEADS ** -0.5 * IDX_DIM ** -0.5)
    key_index = jnp.arange(S)

    def blocks(t):
        return jnp.moveaxis(t.reshape((B, nb, QBLOCK) + t.shape[2:]), 1, 0)

    def one_block(args):
        qb, qib, wb, bi = args
        qpos = bi * QBLOCK + jnp.arange(QBLOCK)
        causal = key_index[None, :] <= qpos[:, None]
        logits = jnp.einsum('bqhd,bsd->bqhs', qib, k_idx).astype(f32)
        score = jnp.einsum('bqhs,bqh->bqs', jax.nn.relu(logits), wb)
        score = jnp.where(causal[None], score, -jnp.inf)
        _, sel = lax.top_k(score, n_sel)
        valid = sel <= qpos[None, :, None]
        k_sel = jax.vmap(lambda kk, ii: kk[ii])(k, sel)
        v_sel = jax.vmap(lambda vv, ii: vv[ii])(v, sel)
        qg = qb.reshape(B, QBLOCK, n_kv, grp, hd)
        s = jnp.einsum('bqhgd,bqkhd->bqhgk', qg, k_sel).astype(f32) * hd ** -0.5
        s = jnp.where(valid[:, :, None, None, :], s, -jnp.inf)
        p = jax.nn.softmax(s, axis=-1).astype(v.dtype)
        o = jnp.einsum('bqhgk,bqkhd->bqhgd', p, v_sel)
        return o.reshape(B, QBLOCK, HA * hd)

    out = lax.map(one_block, (blocks(q), blocks(q_idx), blocks(w_scaled), jnp.arange(nb)))
    return jnp.moveaxis(out, 0, 1).reshape(B, S, HA * hd)


def dilated_branch(q, k, v, window, dil):
    B, S, H, hd = q.shape
    span = window // dil
    n = S // dil
    nb = -(-n // span)
    n_pad = nb * span

    def to_sub(t):
        t = t.reshape(B, n, dil, H, hd)
        t = jnp.pad(t, ((0, 0), (0, n_pad - n), (0, 0), (0, 0), (0, 0)))
        return t.reshape(B, nb, span, dil, H, hd)

    def with_prev(t):
        prev = jnp.pad(t[:, :-1], ((0, 0), (1, 0), (0, 0), (0, 0), (0, 0), (0, 0)))
        return jnp.concatenate([prev, t], axis=2)

    qs = to_sub(q)
    kw = with_prev(to_sub(k))
    vw = with_prev(to_sub(v))
    s = jnp.einsum('bnqrhd,bnkrhd->bnrhqk', qs, kw).astype(f32) * hd ** -0.5
    qq = jnp.arange(span)[:, None]
    kk = jnp.arange(2 * span)[None, :]
    band = (kk >= qq) & (kk <= qq + span)
    first = jnp.arange(nb)[:, None, None] > 0
    mask = band[None] & (first | (kk >= span)[None])
    s = jnp.where(mask[None, :, None, None], s, -jnp.inf)
    m = jnp.max(s, axis=-1)
    p = jnp.exp(s - m[..., None])
    l = jnp.sum(p, axis=-1)
    num = jnp.einsum('bnrhqk,bnkrhd->bnqrhd', p, vw.astype(f32))
    num = num.reshape(B, n_pad, dil, H, hd)[:, :n].reshape(B, S, H, hd)

    def stats_back(t):
        t = jnp.transpose(t, (0, 1, 4, 2, 3)).reshape(B, n_pad, dil, H)
        return t[:, :n].reshape(B, S, H)

    return num, stats_back(m), stats_back(l)


def dilated_attention(q, k, v):
    B, S, H, hd = q.shape
    branches = [dilated_branch(q, k, v, w, d) for (w, d) in DILATED_PATTERNS]
    m_all = jnp.stack([br[1] for br in branches])
    m_max = jnp.max(m_all, axis=0)
    num = sum(jnp.exp(br[1] - m_max)[..., None] * br[0] for br in branches)
    den = sum(jnp.exp(br[1] - m_max) * br[2] for br in branches)
    return (num / den[..., None]).astype(q.dtype).reshape(B, S, H * hd)


def swiglu(h, wg, wu, wd):
    return (jax.nn.silu(h @ wg) * (h @ wu)) @ wd


def routed_experts(h, w_router, router_bias, w_gate, w_up, w_down):
    N, D = h.shape
    E = w_router.shape[1]
    per_group = E // N_GROUPS
    scores = jax.nn.sigmoid((h @ w_router).astype(f32))
    biased = scores + router_bias.astype(f32)
    grp_score = jnp.sum(lax.top_k(biased.reshape(N, N_GROUPS, per_group), 2)[0], axis=-1)
    _, top_g = lax.top_k(grp_score, TOPK_GROUPS)
    gmask = jnp.any(top_g[:, :, None] == jnp.arange(N_GROUPS)[None, None, :], axis=1)
    allowed = jnp.repeat(gmask, per_group, axis=1)
    _, sel = lax.top_k(jnp.where(allowed, biased, -jnp.inf), TOP_K)
    gate = jnp.take_along_axis(scores, sel, axis=1)
    gate = gate / jnp.sum(gate, axis=-1, keepdims=True) * ROUTED_SCALE

    n_assign = N * TOP_K
    flat_e = sel.reshape(-1)
    flat_tok = jnp.arange(n_assign, dtype=jnp.int32) // TOP_K
    order = jnp.argsort(flat_e)
    e_sorted = flat_e[order]
    counts = jnp.bincount(flat_e, length=E)
    padded = (counts + MOE_BLOCK - 1) // MOE_BLOCK * MOE_BLOCK
    pad_end = jnp.cumsum(padded)
    start = jnp.cumsum(counts) - counts
    slot = (pad_end - padded)[e_sorted] + jnp.arange(n_assign) - start[e_sorted]
    n_blocks = -(-n_assign // MOE_BLOCK) + E
    n_slots = n_blocks * MOE_BLOCK
    slot_tok = jnp.full((n_slots,), N, jnp.int32).at[slot].set(flat_tok[order])
    slot_w = jnp.zeros((n_slots,), f32).at[slot].set(gate.reshape(-1)[order])
    block_start = jnp.arange(n_blocks) * MOE_BLOCK
    block_expert = jnp.minimum(jnp.searchsorted(pad_end, block_start, side='right'), E - 1)
    h_pad = jnp.concatenate([h, jnp.zeros((1, D), h.dtype)], axis=0)

    def expert_block(args):
        tok, w, e = args
        y = swiglu(h_pad[tok], w_gate[e], w_up[e], w_down[e])
        return y.astype(f32) * w[:, None]

    yb = lax.map(expert_block, (slot_tok.reshape(n_blocks, MOE_BLOCK),
                                slot_w.reshape(n_blocks, MOE_BLOCK), block_expert))
    y = jnp.zeros((N + 1, D), f32).at[slot_tok].add(yb.reshape(n_slots, D))
    return y[:N].astype(h.dtype)


def setup_inputs(seed: int = 0) -> dict:
    key = jax.random.key(seed)
    ks = jax.random.split(key, 20)
    L = DEPTH

    def nrm(k, shape, fan_in):
        return jax.random.normal(k, shape, f32) * fan_in ** -0.5

    def gain(k, shape):
        return 1.0 + 0.05 * jax.random.normal(k, shape, f32)

    return {
        "x": jax.random.normal(ks[0], (BATCH, SEQ, D_MODEL), f32),
        "c": jax.random.normal(ks[1], (BATCH, D_MODEL), f32),
        "positions": jnp.tile(jnp.arange(SEQ, dtype=jnp.int32)[None, :], (BATCH, 1)),
        "w_ada": 0.5 * nrm(ks[2], (L, D_MODEL, 6 * D_MODEL), D_MODEL),
        "b_ada": 0.01 * jax.random.normal(ks[3], (L, 6 * D_MODEL), f32),
        "g_mix": gain(ks[4], (L, D_MODEL)),
        "w_in": nrm(ks[5], (L, D_MODEL, D_IN), D_MODEL),
        "q_norm_a": gain(ks[6], (L, HEAD_DIM)),
        "k_norm_a": gain(ks[7], (L, HEAD_DIM)),
        "q_norm_b": gain(ks[8], (L, HEAD_DIM)),
        "k_norm_b": gain(ks[9], (L, HEAD_DIM)),
        "w_out": nrm(ks[10], (L, D_MIX, D_MODEL), D_MIX),
        "g_ffn": gain(ks[11], (L, D_MODEL)),
        "w_router": nrm(ks[12], (L, D_MODEL, N_EXPERTS), D_MODEL),
        "router_bias": 0.01 * jax.random.normal(ks[13], (L, N_EXPERTS), f32),
        "w_gate": nrm(ks[14], (L, N_EXPERTS, D_MODEL, D_EXPERT), D_MODEL),
        "w_up": nrm(ks[15], (L, N_EXPERTS, D_MODEL, D_EXPERT), D_MODEL),
        "w_down": nrm(ks[16], (L, N_EXPERTS, D_EXPERT, D_MODEL), D_EXPERT),
        "ws_gate": nrm(ks[17], (L, D_MODEL, D_SHARED), D_MODEL),
        "ws_up": nrm(ks[18], (L, D_MODEL, D_SHARED), D_MODEL),
        "ws_down": nrm(ks[19], (L, D_SHARED, D_MODEL), D_SHARED),
    }


def reference(x, c, positions, w_ada, b_ada, g_mix, w_in, q_norm_a, k_norm_a, q_norm_b, k_norm_b,
              w_out, g_ffn, w_router, router_bias, w_gate, w_up, w_down, ws_gate, ws_up, ws_down):
    B, S, D = x.shape
    for l in range(DEPTH):
        mod = (jax.nn.silu(c) @ w_ada[l] + b_ada[l])[:, None, :]
        sh_a, sc_a, gt_a, sh_f, sc_f, gt_f = jnp.split(mod, 6, axis=-1)

        h = rmsnorm(x, g_mix[l]) * (1.0 + sc_a) + sh_a
        proj = h @ w_in[l]
        qa, ka, va, qi, ki, wi, qb, kb, vb = jnp.split(proj, SPLIT_POINTS, axis=-1)
        qa = rope(rmsnorm(qa.reshape(B, S, N_HEADS_A, HEAD_DIM), q_norm_a[l]), positions)
        ka = rope(rmsnorm(ka.reshape(B, S, N_KV_A, HEAD_DIM), k_norm_a[l]), positions)
        va = va.reshape(B, S, N_KV_A, HEAD_DIM)
        qi = rope(qi.reshape(B, S, N_IDX_HEADS, IDX_DIM), positions)
        ki = rope(ki[:, :, None, :], positions)[:, :, 0, :]
        qb = rope(rmsnorm(qb.reshape(B, S, N_HEADS_B, HEAD_DIM), q_norm_b[l]), positions)
        kb = rope(rmsnorm(kb.reshape(B, S, N_HEADS_B, HEAD_DIM), k_norm_b[l]), positions)
        vb = vb.reshape(B, S, N_HEADS_B, HEAD_DIM)

        o_a = sparse_index_attention(qa, ka, va, qi, ki, wi)
        o_b = dilated_attention(qb, kb, vb)
        mix = jnp.concatenate([o_a, o_b], axis=-1) @ w_out[l]
        x = x + gt_a * mix

        h = (rmsnorm(x, g_ffn[l]) * (1.0 + sc_f) + sh_f).reshape(B * S, D)
        ffn = swiglu(h, ws_gate[l], ws_up[l], ws_down[l]) + routed_experts(
            h, w_router[l], router_bias[l], w_gate[l], w_up[l], w_down[l])
        x = x + gt_f * ffn.reshape(B, S, D)
    return x
```

```python
import functools

import numpy as np
import jax
import jax.numpy as jnp
from jax import lax
from jax.experimental import pallas as pl
from jax.experimental.pallas import tpu as pltpu

f32 = jnp.float32
bf16 = jnp.bfloat16
i32 = jnp.int32

HEAD_DIM = 64
HALF = HEAD_DIM // 2
N_HEADS_A = 10
N_KV_A = 2
N_HEADS_B = 6
N_IDX_HEADS = 8
IDX_DIM = 64
TOPK_MAX = 256
DILATED_PATTERNS = ((128, 1), (512, 4), (2048, 16))
ROPE_THETA = 10000.0
EPS = 1e-6
TOP_K = 8
N_GROUPS = 8
TOPK_GROUPS = 4
ROUTED_SCALE = 2.5
MOE_BLOCK = 256

LANES = 128
VMEM_LIMIT = 48 * 1024 * 1024

NEG_BIG = -1e30
INT_MIN = -(2 ** 31)
KEY_NEG_INF = int(np.int32(np.uint32(0xFF800000) ^ np.uint32(0x7FFFFFFF)))

_OFF_QA = 0
_OFF_KA = _OFF_QA + N_HEADS_A * HEAD_DIM
_OFF_VA = _OFF_KA + N_KV_A * HEAD_DIM
_OFF_QI = _OFF_VA + N_KV_A * HEAD_DIM
_OFF_KI = _OFF_QI + N_IDX_HEADS * IDX_DIM
_OFF_WI = _OFF_KI + IDX_DIM
_OFF_QB = _OFF_WI + N_IDX_HEADS
_OFF_KB = _OFF_QB + N_HEADS_B * HEAD_DIM
_OFF_VB = _OFF_KB + N_HEADS_B * HEAD_DIM
D_IN = _OFF_VB + N_HEADS_B * HEAD_DIM

QA_PAIR_ORDER = (0, 5, 1, 6, 2, 7, 3, 8, 4, 9)

_SLOTS = (
    [("qa", h) for h in QA_PAIR_ORDER[:8]] + [("qa", 4), ("qa", 9), ("ka", 0), ("ka", 1)]
    + [("qb", h) for h in range(6)] + [("kb", h) for h in range(6)]
    + [("qi", h) for h in range(8)] + [("ki", 0), ("ki", 0), ("pad", 0), ("pad", 0)]
)
N_NORM_SLOTS = 24
N_CHUNKS = len(_SLOTS) // 4
SLAB = N_CHUNKS * LANES
_COL_VA = 2 * SLAB
_COL_VB = _COL_VA + N_KV_A * HEAD_DIM
_COL_WI = _COL_VB + N_HEADS_B * HEAD_DIM
N_COL = _COL_WI + LANES


def _slot_offset(kind, h):
    base = {"qa": _OFF_QA, "ka": _OFF_KA, "qb": _OFF_QB, "kb": _OFF_KB, "qi": _OFF_QI, "ki": _OFF_KI}
    return base[kind] + h * HEAD_DIM


def _projection_columns():
    zero_col = D_IN
    cols_a, cols_b = [], []
    for kind, h in _SLOTS:
        if kind == "pad":
            cols_a += [zero_col] * HALF
            cols_b += [zero_col] * HALF
        else:
            off = _slot_offset(kind, h)
            cols_a += list(range(off, off + HALF))
            cols_b += list(range(off + HALF, off + HEAD_DIM))
    cols = cols_a + cols_b
    cols += list(range(_OFF_VA, _OFF_VA + N_KV_A * HEAD_DIM))
    cols += list(range(_OFF_VB, _OFF_VB + N_HEADS_B * HEAD_DIM))
    cols += list(range(_OFF_WI, _OFF_WI + N_IDX_HEADS)) + [zero_col] * (LANES - N_IDX_HEADS)
    assert len(cols) == N_COL
    return np.asarray(cols, np.int32)


def _interleave_matrix():
    p = np.zeros((2 * LANES, 2 * LANES), np.float32)
    for head in range(4):
        for i in range(HALF):
            p[HALF * head + i, HEAD_DIM * head + i] = 1.0
            p[LANES + HALF * head + i, HEAD_DIM * head + HALF + i] = 1.0
    return p


def _group_sum_matrix():
    g = np.zeros((LANES, LANES), np.float32)
    for k in range(LANES // HALF):
        g[HALF * k:HALF * (k + 1), HALF * k:HALF * (k + 1)] = 1.0
    return g


def _dilated_bias(tq):
    max_win = max(w for w, _ in DILATED_PATTERNS)
    nd = max_win // tq + 1
    d = np.arange(nd)[:, None, None] * tq + np.arange(tq)[None, :, None] - np.arange(tq)[None, None, :]
    mult = np.zeros(d.shape, np.float64)
    for win, dil in DILATED_PATTERNS:
        mult += ((d >= 0) & (d <= win) & (d % dil == 0)).astype(np.float64)
    with np.errstate(divide="ignore"):
        bias = np.where(mult > 0, np.log(np.maximum(mult, 1.0)), NEG_BIG)
    return bias.astype(np.float32), nd


def _nt_dot(a, b):
    return lax.dot_general(a, b, (((1,), (1,)), ((), ())), preferred_element_type=f32)


def _adaln_kernel(c_ref, w_ref, b_ref, o_ref):
    c = c_ref[...]
    a = c * jax.nn.sigmoid(c)
    o_ref[...] = jnp.dot(a, w_ref[...], preferred_element_type=f32) + b_ref[...]


def _adaln(c, w_ada, b_ada):
    bsz, d = c.shape
    n = w_ada.shape[1]
    rows = -(-bsz // 8) * 8
    c_pad = jnp.zeros((rows, d), f32).at[:bsz].set(c)
    tn = 512
    out = pl.pallas_call(
        _adaln_kernel,
        grid=(n // tn,),
        in_specs=[pl.BlockSpec((rows, d), lambda j: (0, 0)),
                  pl.BlockSpec((d, tn), lambda j: (0, j)),
                  pl.BlockSpec((1, tn), lambda j: (0, j))],
        out_specs=pl.BlockSpec((rows, tn), lambda j: (0, j)),
        out_shape=jax.ShapeDtypeStruct((rows, n), f32),
        name="adaln",
    )(c_pad, w_ada, b_ada.reshape(1, n))
    return out[:bsz]


def _proj_kernel(x_ref, pos_ref, sc_ref, sh_ref, g_ref, w_ref, ga_ref, gb_ref, gsum_ref, perm_ref, invf_ref,
                 qa_ref, ka_ref, va_ref, qi_ref, ki_ref, wi_ref, qb_ref, kb_ref, vb_ref):
    x = x_ref[0]
    ms = jnp.mean(x * x, axis=-1, keepdims=True)
    h = (x * lax.rsqrt(ms + EPS)) * g_ref[...]
    h = h * (1.0 + sc_ref[0]) + sh_ref[0]
    proj = jnp.dot(h.astype(bf16), w_ref[...], preferred_element_type=f32)

    ang = pos_ref[0] * invf_ref[...]
    cos = jnp.cos(ang)
    sin = jnp.sin(ang)
    gsum = gsum_ref[...]
    perm = perm_ref[...]
    heads = []
    for c in range(N_CHUNKS):
        a = proj[:, LANES * c:LANES * (c + 1)]
        b = proj[:, SLAB + LANES * c:SLAB + LANES * (c + 1)]
        if 4 * c < N_NORM_SLOTS:
            ss = a * a + b * b
            hi = ss.astype(bf16)
            lo = (ss - hi.astype(f32)).astype(bf16)
            tot = (jnp.dot(hi, gsum, preferred_element_type=f32)
                   + jnp.dot(lo, gsum, preferred_element_type=f32))
            inv = lax.rsqrt(tot * (1.0 / HEAD_DIM) + EPS)
            a = a * inv * ga_ref[:, LANES * c:LANES * (c + 1)]
            b = b * inv * gb_ref[:, LANES * c:LANES * (c + 1)]
        ra = a * cos - b * sin
        rb = b * cos + a * sin
        ab = jnp.concatenate([ra, rb], axis=1).astype(bf16)
        heads.append(jnp.dot(ab, perm, preferred_element_type=f32).astype(bf16))

    qa_ref[0, :, 0:256] = heads[0]
    qa_ref[0, :, 256:512] = heads[1]
    qa_ref[0, :, 512:640] = heads[2][:, 0:128]
    ka_ref[0] = heads[2][:, 128:256]
    qb_ref[0, :, 0:256] = heads[3]
    qb_ref[0, :, 256:384] = heads[4][:, 0:128]
    kb_ref[0, :, 0:128] = heads[4][:, 128:256]
    kb_ref[0, :, 128:384] = heads[5]
    qi_ref[0, :, 0:256] = heads[6]
    qi_ref[0, :, 256:512] = heads[7]
    ki_ref[0] = heads[8][:, 0:128]
    va_ref[0] = proj[:, _COL_VA:_COL_VB].astype(bf16)
    vb_ref[0] = proj[:, _COL_VB:_COL_WI].astype(bf16)
    wi_ref[0] = proj[:, _COL_WI:N_COL]


def _project(x, posf, sc, sh, g, w_perm, gain_a, gain_b):
    bsz, s, d = x.shape
    ts = min(256, s)
    gsum = jnp.asarray(_group_sum_matrix(), bf16)
    perm = jnp.asarray(_interleave_matrix(), bf16)
    inv = ROPE_THETA ** (-np.arange(HALF, dtype=np.float32) / HALF)
    invf = jnp.asarray(np.tile(inv, LANES // HALF)[None, :], f32)
    wa = N_HEADS_A * HEAD_DIM
    wb = N_HEADS_B * HEAD_DIM
    wq = N_IDX_HEADS * IDX_DIM
    const = lambda b, i: (0, 0)
    tile = lambda b, i: (b, i, 0)
    per_b = lambda b, i: (b, 0, 0)
    out_shape = [jax.ShapeDtypeStruct((bsz, s, w), dt) for w, dt in
                 ((wa, bf16), (LANES, bf16), (LANES, bf16), (wq, bf16), (LANES, bf16), (LANES, f32),
                  (wb, bf16), (wb, bf16), (wb, bf16))]
    out_specs = [pl.BlockSpec((1, ts, sh_.shape[2]), tile) for sh_ in out_shape]
    return pl.pallas_call(
        _proj_kernel,
        grid=(bsz, s // ts),
        in_specs=[pl.BlockSpec((1, ts, d), tile),
                  pl.BlockSpec((1, ts, 1), tile),
                  pl.BlockSpec((1, 1, d), per_b),
                  pl.BlockSpec((1, 1, d), per_b),
                  pl.BlockSpec((1, d), const),
                  pl.BlockSpec((d, N_COL), const),
                  pl.BlockSpec((1, N_NORM_SLOTS * HALF), const),
                  pl.BlockSpec((1, N_NORM_SLOTS * HALF), const),
                  pl.BlockSpec((LANES, LANES), const),
                  pl.BlockSpec((2 * LANES, 2 * LANES), const),
                  pl.BlockSpec((1, LANES), const)],
        out_specs=out_specs,
        out_shape=out_shape,
        compiler_params=pltpu.CompilerParams(
            dimension_semantics=("arbitrary", "arbitrary"), vmem_limit_bytes=VMEM_LIMIT),
        name="in_proj",
    )(x, posf, sc, sh, g, w_perm, gain_a, gain_b, gsum, perm, invf)


def _sparse_attn_kernel(qa_ref, ka_ref, va_ref, qi_ref, ki_ref, wi_ref, o_ref,
                        keys_ref, qis_ref, qas_ref, wrep_ref, m_ref, l_ref, acc_ref,
                        *, qb, tk, n_sel, w_scale):
    i = pl.program_id(1)
    n_pairs_a = N_HEADS_A // 2
    lane = lax.broadcasted_iota(i32, (qb, LANES), 1)
    left = lane < HEAD_DIM
    mask_l = left.astype(f32).astype(bf16)
    mask_r = (1.0 - left.astype(f32)).astype(bf16)

    for c in range(N_IDX_HEADS // 2):
        ch = qi_ref[0, :, LANES * c:LANES * (c + 1)]
        qis_ref[(2 * c) * qb:(2 * c + 1) * qb, :] = ch * mask_l
        qis_ref[(2 * c + 1) * qb:(2 * c + 2) * qb, :] = ch * mask_r
    q_scale = jnp.asarray(HEAD_DIM ** -0.5, bf16)
    for c in range(n_pairs_a):
        ch = qa_ref[0, :, LANES * c:LANES * (c + 1)] * q_scale
        qas_ref[0, c * qb:(c + 1) * qb, :] = ch * mask_l
        qas_ref[1, c * qb:(c + 1) * qb, :] = ch * mask_r
    wi = wi_ref[0] * w_scale
    for h in range(N_IDX_HEADS):
        wrep_ref[h] = jnp.broadcast_to(wi[:, h:h + 1], (qb, LANES))

    n_tiles = (i * qb) // tk + 1
    reps = tk // LANES
    row = lax.broadcasted_iota(i32, (qb, tk), 0) + i * qb
    col = lax.broadcasted_iota(i32, (qb, tk), 1)

    def score_body(j, carry):
        start = pl.multiple_of(j * tk, tk)
        kt = ki_ref[0, pl.ds(start, tk), :]
        lg = _nt_dot(qis_ref[...], kt)
        acc = jnp.zeros((qb, tk), f32)
        for h in range(N_IDX_HEADS):
            wr = jnp.concatenate([wrep_ref[h]] * reps, axis=1)
            acc = acc + jnp.maximum(lg[h * qb:(h + 1) * qb], 0.0) * wr
        bits = lax.bitcast_convert_type(acc, i32)
        key = bits ^ ((bits >> 31) & 0x7FFFFFFF)
        key = jnp.where(key == -1, 0, key)
        key = jnp.where(col + j * tk <= row, key, KEY_NEG_INF)
        keys_ref[j] = key
        return carry

    lax.fori_loop(0, n_tiles, score_body, 0)

    def count_ge(trial):
        def body(j, cnt):
            kt = keys_ref[j]
            for r in range(reps):
                cnt = cnt + jnp.where(kt[:, LANES * r:LANES * (r + 1)] >= trial, 1.0, 0.0)
            return cnt
        cnt = lax.fori_loop(0, n_tiles, body, jnp.zeros((qb, LANES), f32))
        return jnp.sum(cnt, axis=1, keepdims=True)

    zero = jnp.zeros((qb, LANES), i32)
    cand = jnp.where(count_ge(zero) >= n_sel, zero, jnp.full((qb, LANES), INT_MIN, i32))

    def pass_body(b, cand):
        trial = cand | jnp.left_shift(jnp.int32(1), 30 - b)
        return jnp.where(count_ge(trial) >= n_sel, trial, cand)

    thr = lax.fori_loop(0, 31, pass_body, cand)
    thr_t = jnp.concatenate([thr] * reps, axis=1)

    m_ref[...] = jnp.full(m_ref.shape, NEG_BIG, f32)
    l_ref[...] = jnp.zeros(l_ref.shape, f32)
    acc_ref[...] = jnp.zeros(acc_ref.shape, f32)

    def attn_body(j, carry):
        start = pl.multiple_of(j * tk, tk)
        sel = (keys_ref[j] >= thr_t) & (col + j * tk <= row)
        bias = jnp.where(sel, 0.0, NEG_BIG)
        bias = jnp.concatenate([bias] * n_pairs_a, axis=0)
        kk = ka_ref[0, pl.ds(start, tk), :]
        vv = va_ref[0, pl.ds(start, tk), :]
        for g in range(N_KV_A):
            s = _nt_dot(qas_ref[g], kk) + bias
            m_old = m_ref[g]
            m_new = jnp.maximum(m_old, jnp.max(s, axis=1, keepdims=True))
            alpha = jnp.exp(m_old - m_new)
            p = jnp.exp(s - m_new)
            l_ref[g] = alpha * l_ref[g] + jnp.sum(p, axis=1, keepdims=True)
            acc_ref[g] = alpha * acc_ref[g] + jnp.dot(p.astype(bf16), vv, preferred_element_type=f32)
            m_ref[g] = m_new
        return carry

    lax.fori_loop(0, n_tiles, attn_body, 0)

    for c in range(n_pairs_a):
        rows = slice(c * qb, (c + 1) * qb)
        o0 = acc_ref[0, rows, :] / l_ref[0, rows, :]
        o1 = acc_ref[1, rows, :] / l_ref[1, rows, :]
        o_ref[0, :, LANES * c:LANES * (c + 1)] = jnp.where(left, o0, o1).astype(bf16)


def _sparse_attention(qa, ka, va, qi, ki, wi):
    bsz, s, wa = qa.shape
    qb = 128
    tk = min(256, s)
    n_sel = min(TOPK_MAX, s // 4)
    w_scale = N_IDX_HEADS ** -0.5 * IDX_DIM ** -0.5
    n_pairs_a = N_HEADS_A // 2
    tile = lambda b, i: (b, i, 0)
    per_b = lambda b, i: (b, 0, 0)
    kern = functools.partial(_sparse_attn_kernel, qb=qb, tk=tk, n_sel=float(n_sel), w_scale=w_scale)
    return pl.pallas_call(
        kern,
        grid=(bsz, s // qb),
        in_specs=[pl.BlockSpec((1, qb, wa), tile),
                  pl.BlockSpec((1, s, LANES), per_b),
                  pl.BlockSpec((1, s, LANES), per_b),
                  pl.BlockSpec((1, qb, qi.shape[2]), tile),
                  pl.BlockSpec((1, s, LANES), per_b),
                  pl.BlockSpec((1, qb, LANES), tile)],
        out_specs=pl.BlockSpec((1, qb, wa), tile),
        out_shape=jax.ShapeDtypeStruct((bsz, s, wa), bf16),
        scratch_shapes=[pltpu.VMEM((s // tk, qb, tk), i32),
                        pltpu.VMEM((N_IDX_HEADS * qb, LANES), bf16),
                        pltpu.VMEM((N_KV_A, n_pairs_a * qb, LANES), bf16),
                        pltpu.VMEM((N_IDX_HEADS, qb, LANES), f32),
                        pltpu.VMEM((N_KV_A, n_pairs_a * qb, 1), f32),
                        pltpu.VMEM((N_KV_A, n_pairs_a * qb, 1), f32),
                        pltpu.VMEM((N_KV_A, n_pairs_a * qb, LANES), f32)],
        compiler_params=pltpu.CompilerParams(
            dimension_semantics=("arbitrary", "arbitrary"), vmem_limit_bytes=VMEM_LIMIT),
        name="sparse_attn",
    )(qa, ka, va, qi, ki, wi)


def _dilated_kernel(q_ref, k_ref, v_ref, bias_ref, o_ref, qs_ref, m_ref, l_ref, acc_ref, *, tq, nd):
    i = pl.program_id(2)
    lane = lax.broadcasted_iota(i32, (tq, LANES), 1)
    left = lane < HEAD_DIM
    mask_l = left.astype(f32).astype(bf16)
    mask_r = (1.0 - left.astype(f32)).astype(bf16)
    q = q_ref[0] * jnp.asarray(HEAD_DIM ** -0.5, bf16)
    qs_ref[0:tq, :] = q * mask_l
    qs_ref[tq:2 * tq, :] = q * mask_r
    m_ref[...] = jnp.full(m_ref.shape, NEG_BIG, f32)
    l_ref[...] = jnp.zeros(l_ref.shape, f32)
    acc_ref[...] = jnp.zeros(acc_ref.shape, f32)

    def body(j, carry):
        start = pl.multiple_of(j * tq, tq)
        kk = k_ref[0, pl.ds(start, tq), :]
        vv = v_ref[0, pl.ds(start, tq), :]
        b = bias_ref[i - j]
        s = _nt_dot(qs_ref[...], kk) + jnp.concatenate([b, b], axis=0)
        m_old = m_ref[...]
        m_new = jnp.maximum(m_old, jnp.max(s, axis=1, keepdims=True))
        alpha = jnp.exp(m_old - m_new)
        p = jnp.exp(s - m_new)
        l_ref[...] = alpha * l_ref[...] + jnp.sum(p, axis=1, keepdims=True)
        acc_ref[...] = alpha * acc_ref[...] + jnp.dot(p.astype(bf16), vv, preferred_element_type=f32)
        m_ref[...] = m_new
        return carry

    lax.fori_loop(jnp.maximum(i - (nd - 1), 0), i + 1, body, 0)
    o0 = acc_ref[0:tq, :] / l_ref[0:tq, :]
    o1 = acc_ref[tq:2 * tq, :] / l_ref[tq:2 * tq, :]
    o_ref[0] = jnp.where(left, o0, o1).astype(bf16)


def _dilated_attention(qb_, kb_, vb_):
    bsz, s, wb = qb_.shape
    tq = min(256, s)
    bias_np, nd = _dilated_bias(tq)
    bias = jnp.asarray(bias_np)
    n_pairs = wb // LANES
    kern = functools.partial(_dilated_kernel, tq=tq, nd=nd)
    return pl.pallas_call(
        kern,
        grid=(bsz, n_pairs, s // tq),
        in_specs=[pl.BlockSpec((1, tq, LANES), lambda b, p, i: (b, i, p)),
                  pl.BlockSpec((1, s, LANES), lambda b, p, i: (b, 0, p)),
                  pl.BlockSpec((1, s, LANES), lambda b, p, i: (b, 0, p)),
                  pl.BlockSpec((nd, tq, tq), lambda b, p, i: (0, 0, 0))],
        out_specs=pl.BlockSpec((1, tq, LANES), lambda b, p, i: (b, i, p)),
        out_shape=jax.ShapeDtypeStruct((bsz, s, wb), bf16),
        scratch_shapes=[pltpu.VMEM((2 * tq, LANES), bf16),
                        pltpu.VMEM((2 * tq, 1), f32),
                        pltpu.VMEM((2 * tq, 1), f32),
                        pltpu.VMEM((2 * tq, LANES), f32)],
        compiler_params=pltpu.CompilerParams(
            dimension_semantics=("arbitrary", "arbitrary", "arbitrary"), vmem_limit_bytes=VMEM_LIMIT),
        name="dilated_attn",
    )(qb_, kb_, vb_, bias)


def _mid_kernel(oa_ref, ob_ref, x_ref, gta_ref, scf_ref, shf_ref, gtf_ref, woa_ref, wob_ref, gffn_ref,
                wsgu_ref, wsd_ref, wrt_ref, rbias_ref, triu_ref,
                base_ref, h2_ref, sel_ref, gate_ref, rank_ref, cnt_ref, carry_ref, *, n_exp, t, d_sh):
    step = pl.program_id(0)

    @pl.when(step == 0)
    def _():
        carry_ref[...] = jnp.zeros(carry_ref.shape, f32)

    mix = (jnp.dot(oa_ref[...], woa_ref[...], preferred_element_type=f32)
           + jnp.dot(ob_ref[...], wob_ref[...], preferred_element_type=f32))
    x1 = x_ref[...] + gta_ref[0] * mix
    ms = jnp.mean(x1 * x1, axis=-1, keepdims=True)
    h2 = ((x1 * lax.rsqrt(ms + EPS)) * gffn_ref[...]) * (1.0 + scf_ref[0]) + shf_ref[0]
    h2b = h2.astype(bf16)
    h2f = h2b.astype(f32)
    for a in range(h2_ref.shape[1]):
        h2_ref[:, a, :] = h2f[:, LANES * a:LANES * (a + 1)]

    gu = jnp.dot(h2b, wsgu_ref[...], preferred_element_type=f32)
    g = gu[:, :d_sh]
    u = gu[:, d_sh:]
    act = (g * jax.nn.sigmoid(g)) * u
    shared = jnp.dot(act.astype(bf16), wsd_ref[...], preferred_element_type=f32)
    base_ref[...] = x1 + gtf_ref[0] * shared

    scores = jax.nn.sigmoid(_nt_dot(wrt_ref[...], h2b))
    biased = scores + rbias_ref[...]
    per = n_exp // N_GROUPS
    neg_inf = jnp.float32(-jnp.inf)
    ri_g = lax.broadcasted_iota(i32, (per, t), 0).astype(f32)
    gs = []
    for grp in range(N_GROUPS):
        blk = biased[grp * per:(grp + 1) * per]
        m1 = jnp.max(blk, axis=0, keepdims=True)
        idx1 = jnp.min(jnp.where(blk == m1, ri_g, float(per)), axis=0, keepdims=True)
        m2 = jnp.max(jnp.where(ri_g == idx1, neg_inf, blk), axis=0, keepdims=True)
        gs.append(m1 + m2)
    masked_rows = []
    for grp in range(N_GROUPS):
        beaten = jnp.zeros((1, t), f32)
        for g2 in range(N_GROUPS):
            if g2 == grp:
                continue
            wins = gs[g2] > gs[grp]
            if g2 < grp:
                wins = wins | (gs[g2] == gs[grp])
            beaten = beaten + jnp.where(wins, 1.0, 0.0)
        keep = jnp.broadcast_to(beaten < TOPK_GROUPS, (per, t))
        masked_rows.append(jnp.where(keep, biased[grp * per:(grp + 1) * per], neg_inf))
    masked = jnp.concatenate(masked_rows, axis=0)

    ri = lax.broadcasted_iota(i32, (n_exp, t), 0).astype(f32)
    selmask = jnp.zeros((n_exp, t), f32)
    idxs, graw = [], []
    for _ in range(TOP_K):
        m = jnp.max(masked, axis=0, keepdims=True)
        idx = jnp.min(jnp.where(masked == m, ri, float(n_exp)), axis=0, keepdims=True)
        onehot = ri == idx
        graw.append(jnp.sum(jnp.where(onehot, scores, 0.0), axis=0, keepdims=True))
        masked = jnp.where(onehot, neg_inf, masked)
        selmask = jnp.where(onehot, 1.0, selmask)
        idxs.append(idx)
    den = graw[0]
    for k in range(1, TOP_K):
        den = den + graw[k]

    prefix = jnp.dot(selmask.astype(bf16), triu_ref[...], preferred_element_type=f32)
    prefix = prefix + jnp.concatenate([carry_ref[...]] * (t // LANES), axis=1)
    for k in range(TOP_K):
        rank_k = jnp.sum(jnp.where(ri == idxs[k], prefix, 0.0), axis=0, keepdims=True)
        sel_ref[k:k + 1, :] = idxs[k].astype(i32)
        rank_ref[k:k + 1, :] = rank_k.astype(i32)
        gate_ref[k:k + 1, :] = graw[k] / den * ROUTED_SCALE
    carry_ref[...] = carry_ref[...] + jnp.broadcast_to(
        jnp.sum(selmask, axis=1, keepdims=True), carry_ref.shape)
    cnt_ref[...] = carry_ref[...]


def _mid(oa, ob, x2, gta, scf, shf, gtf, woa, wob, gffn, wsgu, wsd, wrt, rbias_rep, s):
    n, d = x2.shape
    t = rbias_rep.shape[1]
    n_exp = wrt.shape[0]
    d_sh = wsd.shape[0]
    triu = jnp.asarray(np.triu(np.ones((t, t), np.float32), k=1), bf16)
    tile = lambda i: (i, 0)
    const = lambda i: (0, 0)
    per_b = lambda i: ((i * t) // s, 0, 0)
    lane_tile = lambda i: (0, i)
    out_shape = [jax.ShapeDtypeStruct((n, d), f32), jax.ShapeDtypeStruct((n, d // LANES, LANES), f32),
                 jax.ShapeDtypeStruct((TOP_K, n), i32), jax.ShapeDtypeStruct((TOP_K, n), f32),
                 jax.ShapeDtypeStruct((TOP_K, n), i32), jax.ShapeDtypeStruct((n_exp, LANES), f32)]
    out_specs = [pl.BlockSpec((t, d), tile), pl.BlockSpec((t, d // LANES, LANES), lambda i: (i, 0, 0)),
                 pl.BlockSpec((TOP_K, t), lane_tile), pl.BlockSpec((TOP_K, t), lane_tile),
                 pl.BlockSpec((TOP_K, t), lane_tile), pl.BlockSpec((n_exp, LANES), const)]
    kern = functools.partial(_mid_kernel, n_exp=n_exp, t=t, d_sh=d_sh)
    return pl.pallas_call(
        kern,
        grid=(n // t,),
        in_specs=[pl.BlockSpec((t, oa.shape[1]), tile),
                  pl.BlockSpec((t, ob.shape[1]), tile),
                  pl.BlockSpec((t, d), tile),
                  pl.BlockSpec((1, 1, d), per_b),
                  pl.BlockSpec((1, 1, d), per_b),
                  pl.BlockSpec((1, 1, d), per_b),
                  pl.BlockSpec((1, 1, d), per_b),
                  pl.BlockSpec(woa.shape, const),
                  pl.BlockSpec(wob.shape, const),
                  pl.BlockSpec((1, d), const),
                  pl.BlockSpec(wsgu.shape, const),
                  pl.BlockSpec(wsd.shape, const),
                  pl.BlockSpec(wrt.shape, const),
                  pl.BlockSpec(rbias_rep.shape, const),
                  pl.BlockSpec((t, t), const)],
        out_specs=out_specs,
        out_shape=out_shape,
        scratch_shapes=[pltpu.VMEM((n_exp, LANES), f32)],
        compiler_params=pltpu.CompilerParams(
            dimension_semantics=("arbitrary",), vmem_limit_bytes=VMEM_LIMIT),
        name="mid",
    )(oa, ob, x2, gta, scf, shf, gtf, woa, wob, gffn, wsgu, wsd, wrt, rbias_rep, triu)


def _dispatch_kernel(tail_ref, slot_ref, h_ref, xs_ref, zero_ref, sem_ref, *, td, n_exp):
    step = pl.program_id(0)

    def tail_copy(e):
        return pltpu.make_async_copy(zero_ref, xs_ref.at[pl.ds(tail_ref[e], MOE_BLOCK)], sem_ref.at[0])

    @pl.when(step == 0)
    def _():
        zero_ref[...] = jnp.zeros(zero_ref.shape, f32)

        def start(e, c):
            @pl.when(tail_ref[e] >= 0)
            def _():
                tail_copy(e).start()
            return c

        def wait(e, c):
            @pl.when(tail_ref[e] >= 0)
            def _():
                tail_copy(e).wait()
            return c

        lax.fori_loop(0, n_exp, start, 0)
        lax.fori_loop(0, n_exp, wait, 0)

    def row_copy(tok, k):
        return pltpu.make_async_copy(h_ref.at[pl.ds(tok, 1)], xs_ref.at[pl.ds(slot_ref[k, tok], 1)],
                                     sem_ref.at[1])

    def start_rows(tok, c):
        for k in range(TOP_K):
            row_copy(tok, k).start()
        return c

    def wait_rows(tok, c):
        for k in range(TOP_K):
            row_copy(tok, k).wait()
        return c

    lax.fori_loop(0, td, start_rows, 0)
    lax.fori_loop(0, td, wait_rows, 0)


def _dispatch(tail_start, slot, h2, n_slots):
    n, rows, lanes = h2.shape
    td = min(256, n)
    n_exp = tail_start.shape[0]
    kern = functools.partial(_dispatch_kernel, td=td, n_exp=n_exp)
    grid_spec = pltpu.PrefetchScalarGridSpec(
        num_scalar_prefetch=1,
        grid=(n // td,),
        in_specs=[pl.BlockSpec((TOP_K, td), lambda i, tail: (0, i), memory_space=pltpu.SMEM),
                  pl.BlockSpec((td, rows, lanes), lambda i, tail: (i, 0, 0))],
        out_specs=pl.BlockSpec(memory_space=pl.ANY),
        scratch_shapes=[pltpu.VMEM((MOE_BLOCK, rows, lanes), f32), pltpu.SemaphoreType.DMA((2,))],
    )
    return pl.pallas_call(
        kern,
        grid_spec=grid_spec,
        out_shape=jax.ShapeDtypeStruct((n_slots, rows, lanes), f32),
        compiler_params=pltpu.CompilerParams(
            dimension_semantics=("arbitrary",), vmem_limit_bytes=VMEM_LIMIT),
        name="dispatch",
    )(tail_start, slot, h2)


def _expert_kernel(be_ref, nu_ref, xs_ref, wg_ref, wu_ref, wd_ref, ys_ref):
    @pl.when(pl.program_id(0) < nu_ref[0])
    def _():
        n_chunks = xs_ref.shape[1]
        xb = jnp.concatenate([xs_ref[:, a, :] for a in range(n_chunks)], axis=1).astype(bf16)
        g = jnp.dot(xb, wg_ref[0].astype(bf16), preferred_element_type=f32)
        u = jnp.dot(xb, wu_ref[0].astype(bf16), preferred_element_type=f32)
        act = (g * jax.nn.sigmoid(g)) * u
        y = jnp.dot(act.astype(bf16), wd_ref[0].astype(bf16), preferred_element_type=f32)
        for a in range(n_chunks):
            ys_ref[:, a, :] = y[:, LANES * a:LANES * (a + 1)]


def _experts(block_expert, n_used, xs, w_gate, w_up, w_down):
    n_slots, rows, lanes = xs.shape
    d = rows * lanes
    n_blocks = n_slots // MOE_BLOCK
    f = w_gate.shape[2]
    blk = lambda i, be, nu: (jnp.minimum(i, nu[0] - 1), 0, 0)
    wsel = lambda i, be, nu: (be[jnp.minimum(i, nu[0] - 1)], 0, 0)
    grid_spec = pltpu.PrefetchScalarGridSpec(
        num_scalar_prefetch=2,
        grid=(n_blocks,),
        in_specs=[pl.BlockSpec((MOE_BLOCK, rows, lanes), blk),
                  pl.BlockSpec((1, d, f), wsel),
                  pl.BlockSpec((1, d, f), wsel),
                  pl.BlockSpec((1, f, d), wsel)],
        out_specs=pl.BlockSpec((MOE_BLOCK, rows, lanes), blk),
    )
    return pl.pallas_call(
        _expert_kernel,
        grid_spec=grid_spec,
        out_shape=jax.ShapeDtypeStruct((n_slots, rows, lanes), f32),
        compiler_params=pltpu.CompilerParams(
            dimension_semantics=("arbitrary",), vmem_limit_bytes=VMEM_LIMIT),
        name="experts",
    )(block_expert, n_used, xs, w_gate, w_up, w_down)


def _combine_kernel(slot_ref, gate_ref, base_ref, gtf_ref, ys_ref, o_ref, buf_ref, sem_ref, *, tc):
    def row_copy(tok, k):
        return pltpu.make_async_copy(ys_ref.at[pl.ds(slot_ref[k, tok], 1)],
                                     buf_ref.at[k, pl.ds(tok, 1)], sem_ref.at[0])

    def start_rows(tok, c):
        for k in range(TOP_K):
            row_copy(tok, k).start()
        return c

    def wait_rows(tok, c):
        for k in range(TOP_K):
            row_copy(tok, k).wait()
        return c

    lax.fori_loop(0, tc, start_rows, 0)
    lax.fori_loop(0, tc, wait_rows, 0)
    gate = gate_ref[...]
    gtf = gtf_ref[0]
    for a in range(buf_ref.shape[2]):
        cols = slice(LANES * a, LANES * (a + 1))
        routed = buf_ref[0, :, a, :] * gate[:, 0:1]
        for k in range(1, TOP_K):
            routed = routed + buf_ref[k, :, a, :] * gate[:, k:k + 1]
        o_ref[:, cols] = base_ref[:, cols] + gtf[:, cols] * routed


def _combine(slot, gate_t, base, gtf, ys, s):
    n, d = base.shape
    rows, lanes = ys.shape[1:]
    tc = min(128, n)
    kern = functools.partial(_combine_kernel, tc=tc)
    return pl.pallas_call(
        kern,
        grid=(n // tc,),
        in_specs=[pl.BlockSpec((TOP_K, tc), lambda i: (0, i), memory_space=pltpu.SMEM),
                  pl.BlockSpec((tc, TOP_K), lambda i: (i, 0)),
                  pl.BlockSpec((tc, d), lambda i: (i, 0)),
                  pl.BlockSpec((1, 1, d), lambda i: ((i * tc) // s, 0, 0)),
                  pl.BlockSpec(memory_space=pl.ANY)],
        out_specs=pl.BlockSpec((tc, d), lambda i: (i, 0)),
        out_shape=jax.ShapeDtypeStruct((n, d), f32),
        scratch_shapes=[pltpu.VMEM((TOP_K, tc, rows, lanes), f32), pltpu.SemaphoreType.DMA((1,))],
        compiler_params=pltpu.CompilerParams(
            dimension_semantics=("arbitrary",), vmem_limit_bytes=VMEM_LIMIT),
        name="combine",
    )(slot, gate_t, base, gtf, ys)


def _layer(x, c, posf, w_ada, b_ada, g_mix, w_in, q_norm_a, k_norm_a, q_norm_b, k_norm_b, w_out, g_ffn,
           w_router, router_bias, w_gate, w_up, w_down, ws_gate, ws_up, ws_down):
    bsz, s, d = x.shape
    n = bsz * s
    n_exp = w_router.shape[1]

    mod = _adaln(c, w_ada, b_ada)[:, None, :]
    sh_a, sc_a, gt_a, sh_f, sc_f, gt_f = jnp.split(mod, 6, axis=-1)

    cols = _projection_columns()
    w_ext = jnp.concatenate([w_in, jnp.zeros((d, 1), w_in.dtype)], axis=1)
    w_perm = jnp.take(w_ext, cols, axis=1).astype(bf16)
    gains = {"qa": q_norm_a, "ka": k_norm_a, "qb": q_norm_b, "kb": k_norm_b}
    gain_a = jnp.concatenate([gains[kind][:HALF] for kind, _ in _SLOTS[:N_NORM_SLOTS]])[None, :].astype(f32)
    gain_b = jnp.concatenate([gains[kind][HALF:] for kind, _ in _SLOTS[:N_NORM_SLOTS]])[None, :].astype(f32)

    qa, ka, va, qi, ki, wi, qb_, kb_, vb_ = _project(x, posf, sc_a, sh_a, g_mix[None, :], w_perm, gain_a, gain_b)
    o_a = _sparse_attention(qa, ka, va, qi, ki, wi)
    o_b = _dilated_attention(qb_, kb_, vb_)

    wa = N_HEADS_A * HEAD_DIM
    rows_a = np.concatenate([np.arange(h * HEAD_DIM, (h + 1) * HEAD_DIM) for h in QA_PAIR_ORDER])
    woa = w_out[rows_a].astype(bf16)
    wob = w_out[wa:].astype(bf16)
    wsgu = jnp.concatenate([ws_gate, ws_up], axis=1).astype(bf16)
    wsd = ws_down.astype(bf16)
    wrt = w_router.T.astype(bf16)
    t_mid = min(256, n)
    rbias_rep = jnp.broadcast_to(router_bias.astype(f32)[:, None], (n_exp, t_mid))

    base, h2, sel, gate, rank, cnt = _mid(
        o_a.reshape(n, wa), o_b.reshape(n, -1), x.reshape(n, d), gt_a, sc_f, sh_f, gt_f,
        woa, wob, g_ffn[None, :], wsgu, wsd, wrt, rbias_rep, s)

    counts = cnt[:, 0].astype(i32)
    padded = (counts + MOE_BLOCK - 1) // MOE_BLOCK * MOE_BLOCK
    pad_end = jnp.cumsum(padded)
    pad_start = pad_end - padded
    slot = pad_start[sel] + rank
    n_blocks = -(-(n * TOP_K) // MOE_BLOCK) + n_exp
    n_slots = n_blocks * MOE_BLOCK
    block_start = jnp.arange(n_blocks, dtype=i32) * MOE_BLOCK
    block_expert = jnp.minimum(jnp.searchsorted(pad_end, block_start, side="right"), n_exp - 1).astype(i32)
    n_used = (pad_end[-1] // MOE_BLOCK).astype(i32)[None]
    tail_start = jnp.where(padded > 0, pad_end - MOE_BLOCK, -1).astype(i32)

    xs = _dispatch(tail_start, slot, h2, n_slots)
    ys = _experts(block_expert, n_used, xs, w_gate, w_up, w_down)
    out = _combine(slot, gate.T, base, gt_f, ys, s)
    return out.reshape(bsz, s, d)


def kernel(x, c, positions, w_ada, b_ada, g_mix, w_in, q_norm_a, k_norm_a, q_norm_b, k_norm_b, w_out, g_ffn,
           w_router, router_bias, w_gate, w_up, w_down, ws_gate, ws_up, ws_down):
    posf = positions.astype(f32)[..., None]
    for l in range(w_ada.shape[0]):
        x = _layer(x, c, posf, w_ada[l], b_ada[l], g_mix[l], w_in[l], q_norm_a[l], k_norm_a[l], q_norm_b[l],
                   k_norm_b[l], w_out[l], g_ffn[l], w_router[l], router_bias[l], w_gate[l], w_up[l], w_down[l],
                   ws_gate[l], ws_up[l], ws_down[l])
    return x
```

```python
import functools

import numpy as np
import jax
import jax.numpy as jnp
from jax import lax
from jax.experimental import pallas as pl
from jax.experimental.pallas import tpu as pltpu

f32 = jnp.float32
bf16 = jnp.bfloat16
i32 = jnp.int32

HEAD_DIM = 64
HALF = HEAD_DIM // 2
N_HEADS_A = 10
N_KV_A = 2
N_HEADS_B = 6
N_IDX_HEADS = 8
IDX_DIM = 64
TOPK_MAX = 256
DILATED_PATTERNS = ((128, 1), (512, 4), (2048, 16))
ROPE_THETA = 10000.0
EPS = 1e-6
TOP_K = 8
N_GROUPS = 8
TOPK_GROUPS = 4
ROUTED_SCALE = 2.5
MOE_BLOCK = 256

LANES = 128
VMEM_LIMIT = 48 * 1024 * 1024

NEG_BIG = -1e30
INT_MIN = -(2 ** 31)
KEY_NEG_INF = int(np.int32(np.uint32(0xFF800000) ^ np.uint32(0x7FFFFFFF)))

_OFF_QA = 0
_OFF_KA = _OFF_QA + N_HEADS_A * HEAD_DIM
_OFF_VA = _OFF_KA + N_KV_A * HEAD_DIM
_OFF_QI = _OFF_VA + N_KV_A * HEAD_DIM
_OFF_KI = _OFF_QI + N_IDX_HEADS * IDX_DIM
_OFF_WI = _OFF_KI + IDX_DIM
_OFF_QB = _OFF_WI + N_IDX_HEADS
_OFF_KB = _OFF_QB + N_HEADS_B * HEAD_DIM
_OFF_VB = _OFF_KB + N_HEADS_B * HEAD_DIM
D_IN = _OFF_VB + N_HEADS_B * HEAD_DIM

QA_PAIR_ORDER = (0, 5, 1, 6, 2, 7, 3, 8, 4, 9)

_SLOTS = (
    [("qa", h) for h in QA_PAIR_ORDER[:8]] + [("qa", 4), ("qa", 9), ("ka", 0), ("ka", 1)]
    + [("qb", h) for h in range(6)] + [("kb", h) for h in range(6)]
    + [("qi", h) for h in range(8)] + [("ki", 0), ("ki", 0), ("pad", 0), ("pad", 0)]
)
N_NORM_SLOTS = 24
N_CHUNKS = len(_SLOTS) // 4
SLAB = N_CHUNKS * LANES
_COL_VA = 2 * SLAB
_COL_VB = _COL_VA + N_KV_A * HEAD_DIM
_COL_WI = _COL_VB + N_HEADS_B * HEAD_DIM
N_COL = _COL_WI + LANES


def _slot_offset(kind, h):
    base = {"qa": _OFF_QA, "ka": _OFF_KA, "qb": _OFF_QB, "kb": _OFF_KB, "qi": _OFF_QI, "ki": _OFF_KI}
    return base[kind] + h * HEAD_DIM


def _projection_columns():
    zero_col = D_IN
    cols_a, cols_b = [], []
    for kind, h in _SLOTS:
        if kind == "pad":
            cols_a += [zero_col] * HALF
            cols_b += [zero_col] * HALF
        else:
            off = _slot_offset(kind, h)
            cols_a += list(range(off, off + HALF))
            cols_b += list(range(off + HALF, off + HEAD_DIM))
    cols = cols_a + cols_b
    cols += list(range(_OFF_VA, _OFF_VA + N_KV_A * HEAD_DIM))
    cols += list(range(_OFF_VB, _OFF_VB + N_HEADS_B * HEAD_DIM))
    cols += list(range(_OFF_WI, _OFF_WI + N_IDX_HEADS)) + [zero_col] * (LANES - N_IDX_HEADS)
    assert len(cols) == N_COL
    return np.asarray(cols, np.int32)


def _take_runs(w, idx, axis):
    size = w.shape[axis]
    pieces, a = [], 0
    idx = [int(v) for v in idx]
    while a < len(idx):
        b = a + 1
        if idx[a] == size:
            while b < len(idx) and idx[b] == size:
                b += 1
            shape = list(w.shape)
            shape[axis] = b - a
            pieces.append(jnp.zeros(shape, w.dtype))
        else:
            while b < len(idx) and idx[b] == idx[b - 1] + 1:
                b += 1
            pieces.append(lax.slice_in_dim(w, idx[a], idx[b - 1] + 1, axis=axis))
        a = b
    return jnp.concatenate(pieces, axis=axis)


def _interleave_matrix():
    p = np.zeros((2 * LANES, 2 * LANES), np.float32)
    for head in range(4):
        for i in range(HALF):
            p[HALF * head + i, HEAD_DIM * head + i] = 1.0
            p[LANES + HALF * head + i, HEAD_DIM * head + HALF + i] = 1.0
    return p


def _group_sum_matrix():
    g = np.zeros((LANES, LANES), np.float32)
    for k in range(LANES // HALF):
        g[HALF * k:HALF * (k + 1), HALF * k:HALF * (k + 1)] = 1.0
    return g


def _dilated_bias(tq):
    max_win = max(w for w, _ in DILATED_PATTERNS)
    nd = max_win // tq + 1
    d = np.arange(nd)[:, None, None] * tq + np.arange(tq)[None, :, None] - np.arange(tq)[None, None, :]
    mult = np.zeros(d.shape, np.float64)
    for win, dil in DILATED_PATTERNS:
        mult += ((d >= 0) & (d <= win) & (d % dil == 0)).astype(np.float64)
    with np.errstate(divide="ignore"):
        bias = np.where(mult > 0, np.log(np.maximum(mult, 1.0)), NEG_BIG)
    return bias.astype(np.float32), nd


def _nt_dot(a, b):
    return lax.dot_general(a, b, (((1,), (1,)), ((), ())), preferred_element_type=f32)


def _adaln_kernel(c_ref, w_ref, b_ref, o_ref):
    c = c_ref[...]
    a = c * jax.nn.sigmoid(c)
    o_ref[...] = jnp.dot(a, w_ref[...], preferred_element_type=f32) + b_ref[...]


def _adaln(c, w_ada, b_ada):
    bsz, d = c.shape
    n = w_ada.shape[1]
    rows = -(-bsz // 8) * 8
    c_pad = jnp.zeros((rows, d), f32).at[:bsz].set(c)
    tn = 512
    out = pl.pallas_call(
        _adaln_kernel,
        grid=(n // tn,),
        in_specs=[pl.BlockSpec((rows, d), lambda j: (0, 0)),
                  pl.BlockSpec((d, tn), lambda j: (0, j)),
                  pl.BlockSpec((1, tn), lambda j: (0, j))],
        out_specs=pl.BlockSpec((rows, tn), lambda j: (0, j)),
        out_shape=jax.ShapeDtypeStruct((rows, n), f32),
        name="adaln",
    )(c_pad, w_ada, b_ada.reshape(1, n))
    return out[:bsz]


def _proj_kernel(x_ref, pos_ref, sc_ref, sh_ref, g_ref, w_ref, ga_ref, gb_ref, gsum_ref, perm_ref, invf_ref,
                 qa_ref, ka_ref, va_ref, qi_ref, ki_ref, wi_ref, qb_ref, kb_ref, vb_ref):
    x = x_ref[0]
    ms = jnp.mean(x * x, axis=-1, keepdims=True)
    h = (x * lax.rsqrt(ms + EPS)) * g_ref[...]
    h = h * (1.0 + sc_ref[0]) + sh_ref[0]
    proj = jnp.dot(h.astype(bf16), w_ref[...], preferred_element_type=f32)

    ang = pos_ref[0] * invf_ref[...]
    cos = jnp.cos(ang)
    sin = jnp.sin(ang)
    gsum = gsum_ref[...]
    perm = perm_ref[...]
    heads = []
    for c in range(N_CHUNKS):
        a = proj[:, LANES * c:LANES * (c + 1)]
        b = proj[:, SLAB + LANES * c:SLAB + LANES * (c + 1)]
        if 4 * c < N_NORM_SLOTS:
            ss = a * a + b * b
            hi = ss.astype(bf16)
            lo = (ss - hi.astype(f32)).astype(bf16)
            tot = (jnp.dot(hi, gsum, preferred_element_type=f32)
                   + jnp.dot(lo, gsum, preferred_element_type=f32))
            inv = lax.rsqrt(tot * (1.0 / HEAD_DIM) + EPS)
            a = a * inv * ga_ref[:, LANES * c:LANES * (c + 1)]
            b = b * inv * gb_ref[:, LANES * c:LANES * (c + 1)]
        ra = a * cos - b * sin
        rb = b * cos + a * sin
        ab = jnp.concatenate([ra, rb], axis=1).astype(bf16)
        heads.append(jnp.dot(ab, perm, preferred_element_type=f32).astype(bf16))

    qa_ref[0, :, 0:256] = heads[0]
    qa_ref[0, :, 256:512] = heads[1]
    qa_ref[0, :, 512:640] = heads[2][:, 0:128]
    ka_ref[0] = heads[2][:, 128:256]
    qb_ref[0, :, 0:256] = heads[3]
    qb_ref[0, :, 256:384] = heads[4][:, 0:128]
    kb_ref[0, :, 0:128] = heads[4][:, 128:256]
    kb_ref[0, :, 128:384] = heads[5]
    qi_ref[0, :, 0:256] = heads[6]
    qi_ref[0, :, 256:512] = heads[7]
    ki_ref[0] = heads[8][:, 0:128]
    va_ref[0] = proj[:, _COL_VA:_COL_VB].astype(bf16)
    vb_ref[0] = proj[:, _COL_VB:_COL_WI].astype(bf16)
    wi_ref[0] = proj[:, _COL_WI:N_COL]


def _project(x, posf, sc, sh, g, w_perm, gain_a, gain_b):
    bsz, s, d = x.shape
    ts = min(256, s)
    gsum = jnp.asarray(_group_sum_matrix(), bf16)
    perm = jnp.asarray(_interleave_matrix(), bf16)
    inv = ROPE_THETA ** (-np.arange(HALF, dtype=np.float32) / HALF)
    invf = jnp.asarray(np.tile(inv, LANES // HALF)[None, :], f32)
    wa = N_HEADS_A * HEAD_DIM
    wb = N_HEADS_B * HEAD_DIM
    wq = N_IDX_HEADS * IDX_DIM
    const = lambda b, i: (0, 0)
    tile = lambda b, i: (b, i, 0)
    per_b = lambda b, i: (b, 0, 0)
    out_shape = [jax.ShapeDtypeStruct((bsz, s, w), dt) for w, dt in
                 ((wa, bf16), (LANES, bf16), (LANES, bf16), (wq, bf16), (LANES, bf16), (LANES, f32),
                  (wb, bf16), (wb, bf16), (wb, bf16))]
    out_specs = [pl.BlockSpec((1, ts, sh_.shape[2]), tile) for sh_ in out_shape]
    return pl.pallas_call(
        _proj_kernel,
        grid=(bsz, s // ts),
        in_specs=[pl.BlockSpec((1, ts, d), tile),
                  pl.BlockSpec((1, ts, 1), tile),
                  pl.BlockSpec((1, 1, d), per_b),
                  pl.BlockSpec((1, 1, d), per_b),
                  pl.BlockSpec((1, d), const),
                  pl.BlockSpec((d, N_COL), const),
                  pl.BlockSpec((1, N_NORM_SLOTS * HALF), const),
                  pl.BlockSpec((1, N_NORM_SLOTS * HALF), const),
                  pl.BlockSpec((LANES, LANES), const),
                  pl.BlockSpec((2 * LANES, 2 * LANES), const),
                  pl.BlockSpec((1, LANES), const)],
        out_specs=out_specs,
        out_shape=out_shape,
        compiler_params=pltpu.CompilerParams(
            dimension_semantics=("arbitrary", "arbitrary"), vmem_limit_bytes=VMEM_LIMIT),
        name="in_proj",
    )(x, posf, sc, sh, g, w_perm, gain_a, gain_b, gsum, perm, invf)


def _sparse_attn_kernel(qa_ref, ka_ref, va_ref, qi_ref, ki_ref, wi_ref, o_ref,
                        keys_ref, qis_ref, qas_ref, wrep_ref, m_ref, acc_ref,
                        *, qb, tk, n_sel, w_scale):
    i = pl.program_id(1)
    n_pairs_a = N_HEADS_A // 2
    lane = lax.broadcasted_iota(i32, (qb, LANES), 1)
    left = lane < HEAD_DIM
    mask_l = left.astype(f32).astype(bf16)
    mask_r = (1.0 - left.astype(f32)).astype(bf16)

    for c in range(N_IDX_HEADS // 2):
        ch = qi_ref[0, :, LANES * c:LANES * (c + 1)]
        qis_ref[(2 * c) * qb:(2 * c + 1) * qb, :] = ch * mask_l
        qis_ref[(2 * c + 1) * qb:(2 * c + 2) * qb, :] = ch * mask_r
    q_scale = jnp.asarray(HEAD_DIM ** -0.5, bf16)
    for c in range(n_pairs_a):
        ch = qa_ref[0, :, LANES * c:LANES * (c + 1)] * q_scale
        qas_ref[0, c * qb:(c + 1) * qb, :] = ch * mask_l
        qas_ref[1, c * qb:(c + 1) * qb, :] = ch * mask_r
    wi = wi_ref[0] * w_scale
    for h in range(N_IDX_HEADS):
        wrep_ref[h] = jnp.broadcast_to(wi[:, h:h + 1], (qb, LANES))

    n_tiles = (i * qb) // tk + 1
    reps = tk // LANES
    row = lax.broadcasted_iota(i32, (qb, tk), 0) + i * qb
    col = lax.broadcasted_iota(i32, (qb, tk), 1)

    def score_body(j, rmax):
        start = pl.multiple_of(j * tk, tk)
        kt = ki_ref[0, pl.ds(start, tk), :]
        lg = _nt_dot(qis_ref[...], kt)
        acc = jnp.zeros((qb, tk), f32)
        for h in range(N_IDX_HEADS):
            wr = jnp.concatenate([wrep_ref[h]] * reps, axis=1)
            acc = acc + jnp.maximum(lg[h * qb:(h + 1) * qb], 0.0) * wr
        acc = jnp.where(col + j * tk <= row, acc, -jnp.inf)
        keys_ref[j] = acc
        for r in range(reps):
            rmax = jnp.maximum(rmax, acc[:, LANES * r:LANES * (r + 1)])
        return rmax

    rmax = lax.fori_loop(0, n_tiles, score_body, jnp.full((qb, LANES), -jnp.inf, f32))
    rmax = jnp.broadcast_to(jnp.max(rmax, axis=1, keepdims=True), (qb, LANES))

    def key_to_float(key):
        key = jnp.maximum(key, KEY_NEG_INF)
        return lax.bitcast_convert_type(key ^ ((key >> 31) & 0x7FFFFFFF), f32)

    def count_ge(trial_f):
        def body(j, cnt):
            kt = keys_ref[j]
            for r in range(reps):
                cnt = cnt + jnp.where(kt[:, LANES * r:LANES * (r + 1)] >= trial_f, 1.0, 0.0)
            return cnt
        cnt = lax.fori_loop(0, n_tiles, body, jnp.zeros((qb, LANES), f32))
        return jnp.broadcast_to(jnp.sum(cnt, axis=1, keepdims=True), (qb, LANES))

    def search_cond(state):
        b, _, _, unresolved = state
        return (b < 32) & unresolved

    def search_body(state):
        b, cand, cnt_cand, _ = state
        trial = jnp.where(b == 0, jnp.zeros_like(cand), cand | jnp.left_shift(jnp.int32(1), 31 - b))
        trial_f = key_to_float(trial)
        reachable = jnp.max(jnp.where(trial_f <= rmax, 1.0, 0.0)) > 0.0
        tot = lax.cond(reachable, lambda: count_ge(trial_f), lambda: jnp.zeros((qb, LANES), f32))
        accept = tot >= n_sel
        cand = jnp.where(accept, trial, cand)
        cnt_cand = jnp.where(accept, tot, cnt_cand)
        unresolved = jnp.max(jnp.where(cnt_cand != n_sel, 1.0, 0.0)) > 0.0
        return b + 1, cand, cnt_cand, unresolved

    init = (jnp.int32(0), jnp.full((qb, LANES), INT_MIN, i32),
            jnp.full((qb, LANES), 0.0, f32) + (n_tiles * tk).astype(f32), jnp.bool_(True))
    _, cand, _, _ = lax.while_loop(search_cond, search_body, init)
    thr_t = jnp.concatenate([key_to_float(cand)] * reps, axis=1)

    m_ref[...] = jnp.full(m_ref.shape, NEG_BIG, f32)
    acc_ref[...] = jnp.zeros(acc_ref.shape, f32)
    lane_k = lax.broadcasted_iota(i32, (tk, LANES), 1)
    kmask_l = (lane_k < HEAD_DIM).astype(f32).astype(bf16)
    kmask_r = (lane_k >= HEAD_DIM).astype(f32).astype(bf16)

    def attn_body(j, carry):
        start = pl.multiple_of(j * tk, tk)
        sel = (keys_ref[j] >= thr_t) & (col + j * tk <= row)
        bias = jnp.where(sel, 0.0, NEG_BIG)
        bias = jnp.concatenate([bias] * n_pairs_a, axis=0)
        kk = ka_ref[0, pl.ds(start, tk), :]
        vv = va_ref[0, pl.ds(start, tk), :]
        v_ext = (vv * kmask_l + kmask_r, vv * kmask_r + kmask_l)
        for g in range(N_KV_A):
            s = _nt_dot(qas_ref[g], kk) + bias
            m_old = m_ref[g]
            m_new = jnp.maximum(m_old, jnp.max(s, axis=1, keepdims=True))
            alpha = jnp.exp(m_old - m_new)
            p = jnp.exp(s - jnp.concatenate([m_new] * reps, axis=1))
            acc_ref[g] = alpha * acc_ref[g] + jnp.dot(p.astype(bf16), v_ext[g], preferred_element_type=f32)
            m_ref[g] = m_new
        return carry

    lax.fori_loop(0, n_tiles, attn_body, 0)

    for c in range(n_pairs_a):
        rows = slice(c * qb, (c + 1) * qb)
        a0 = acc_ref[0, rows, :]
        a1 = acc_ref[1, rows, :]
        o0 = a0 / pltpu.roll(a0, HEAD_DIM, axis=1)
        o1 = a1 / pltpu.roll(a1, HEAD_DIM, axis=1)
        o_ref[0, :, LANES * c:LANES * (c + 1)] = jnp.where(left, o0, o1).astype(bf16)


def _sparse_attention(qa, ka, va, qi, ki, wi):
    bsz, s, wa = qa.shape
    qb = 128
    tk = min(512, s)
    n_sel = min(TOPK_MAX, s // 4)
    w_scale = N_IDX_HEADS ** -0.5 * IDX_DIM ** -0.5
    n_pairs_a = N_HEADS_A // 2
    tile = lambda b, i: (b, i, 0)
    per_b = lambda b, i: (b, 0, 0)
    kern = functools.partial(_sparse_attn_kernel, qb=qb, tk=tk, n_sel=float(n_sel), w_scale=w_scale)
    return pl.pallas_call(
        kern,
        grid=(bsz, s // qb),
        in_specs=[pl.BlockSpec((1, qb, wa), tile),
                  pl.BlockSpec((1, s, LANES), per_b),
                  pl.BlockSpec((1, s, LANES), per_b),
                  pl.BlockSpec((1, qb, qi.shape[2]), tile),
                  pl.BlockSpec((1, s, LANES), per_b),
                  pl.BlockSpec((1, qb, LANES), tile)],
        out_specs=pl.BlockSpec((1, qb, wa), tile),
        out_shape=jax.ShapeDtypeStruct((bsz, s, wa), bf16),
        scratch_shapes=[pltpu.VMEM((s // tk, qb, tk), f32),
                        pltpu.VMEM((N_IDX_HEADS * qb, LANES), bf16),
                        pltpu.VMEM((N_KV_A, n_pairs_a * qb, LANES), bf16),
                        pltpu.VMEM((N_IDX_HEADS, qb, LANES), f32),
                        pltpu.VMEM((N_KV_A, n_pairs_a * qb, LANES), f32),
                        pltpu.VMEM((N_KV_A, n_pairs_a * qb, LANES), f32)],
        compiler_params=pltpu.CompilerParams(
            dimension_semantics=("arbitrary", "arbitrary"), vmem_limit_bytes=VMEM_LIMIT),
        name="sparse_attn",
    )(qa, ka, va, qi, ki, wi)


def _dilated_kernel(q_ref, k_ref, v_ref, bias_ref, o_ref, qs_ref, m_ref, acc_ref, *, tq, nd):
    i = pl.program_id(2)
    lane = lax.broadcasted_iota(i32, (tq, LANES), 1)
    left = lane < HEAD_DIM
    mask_l = left.astype(f32).astype(bf16)
    mask_r = (1.0 - left.astype(f32)).astype(bf16)
    q = q_ref[0] * jnp.asarray(HEAD_DIM ** -0.5, bf16)
    qs_ref[0:tq, :] = q * mask_l
    qs_ref[tq:2 * tq, :] = q * mask_r
    m_ref[...] = jnp.full(m_ref.shape, NEG_BIG, f32)
    acc_ref[...] = jnp.zeros(acc_ref.shape, f32)
    ones = jnp.ones((tq, LANES), bf16)

    def body(j, carry):
        start = pl.multiple_of(j * tq, tq)
        kk = k_ref[0, pl.ds(start, tq), :]
        v_ext = jnp.concatenate([v_ref[0, pl.ds(start, tq), :], ones], axis=1)
        b = bias_ref[i - j]
        s = _nt_dot(qs_ref[...], kk) + jnp.concatenate([b, b], axis=0)
        m_old = m_ref[...]
        m_new = jnp.maximum(m_old, jnp.max(s, axis=1, keepdims=True))
        alpha = jnp.exp(m_old - m_new)
        p = jnp.exp(s - jnp.concatenate([m_new] * (tq // LANES), axis=1))
        pv = jnp.dot(p.astype(bf16), v_ext, preferred_element_type=f32)
        acc_ref[...] = jnp.concatenate([alpha, alpha], axis=1) * acc_ref[...] + pv
        m_ref[...] = m_new
        return carry

    lax.fori_loop(jnp.maximum(i - (nd - 1), 0), i + 1, body, 0)
    o0 = acc_ref[0:tq, 0:LANES] / acc_ref[0:tq, LANES:2 * LANES]
    o1 = acc_ref[tq:2 * tq, 0:LANES] / acc_ref[tq:2 * tq, LANES:2 * LANES]
    o_ref[0] = jnp.where(left, o0, o1).astype(bf16)


def _dilated_attention(qb_, kb_, vb_):
    bsz, s, wb = qb_.shape
    tq = min(512, s)
    bias_np, nd = _dilated_bias(tq)
    bias = jnp.asarray(bias_np)
    n_pairs = wb // LANES
    kern = functools.partial(_dilated_kernel, tq=tq, nd=nd)
    return pl.pallas_call(
        kern,
        grid=(bsz, n_pairs, s // tq),
        in_specs=[pl.BlockSpec((1, tq, LANES), lambda b, p, i: (b, i, p)),
                  pl.BlockSpec((1, s, LANES), lambda b, p, i: (b, 0, p)),
                  pl.BlockSpec((1, s, LANES), lambda b, p, i: (b, 0, p)),
                  pl.BlockSpec((nd, tq, tq), lambda b, p, i: (0, 0, 0))],
        out_specs=pl.BlockSpec((1, tq, LANES), lambda b, p, i: (b, i, p)),
        out_shape=jax.ShapeDtypeStruct((bsz, s, wb), bf16),
        scratch_shapes=[pltpu.VMEM((2 * tq, LANES), bf16),
                        pltpu.VMEM((2 * tq, LANES), f32),
                        pltpu.VMEM((2 * tq, 2 * LANES), f32)],
        compiler_params=pltpu.CompilerParams(
            dimension_semantics=("arbitrary", "arbitrary", "arbitrary"), vmem_limit_bytes=VMEM_LIMIT),
        name="dilated_attn",
    )(qb_, kb_, vb_, bias)


def _mid_kernel(oa_ref, ob_ref, x_ref, gta_ref, scf_ref, shf_ref, gtf_ref, woa_ref, wob_ref, gffn_ref,
                wsgu_ref, wsd_ref, wrt_ref, rbias_ref, triu_ref,
                base_ref, h2_ref, sel_ref, gate_ref, rank_ref, cnt_ref, carry_ref, *, n_exp, t, d_sh):
    step = pl.program_id(0)

    @pl.when(step == 0)
    def _():
        carry_ref[...] = jnp.zeros(carry_ref.shape, f32)

    mix = (jnp.dot(oa_ref[...], woa_ref[...], preferred_element_type=f32)
           + jnp.dot(ob_ref[...], wob_ref[...], preferred_element_type=f32))
    x1 = x_ref[...] + gta_ref[0] * mix
    ms = jnp.mean(x1 * x1, axis=-1, keepdims=True)
    h2 = ((x1 * lax.rsqrt(ms + EPS)) * gffn_ref[...]) * (1.0 + scf_ref[0]) + shf_ref[0]
    h2b = h2.astype(bf16)
    h2f = h2b.astype(f32)
    for a in range(h2_ref.shape[1]):
        h2_ref[:, a, :] = h2f[:, LANES * a:LANES * (a + 1)]

    gu = jnp.dot(h2b, wsgu_ref[...], preferred_element_type=f32)
    g = gu[:, :d_sh]
    u = gu[:, d_sh:]
    act = (g * jax.nn.sigmoid(g)) * u
    shared = jnp.dot(act.astype(bf16), wsd_ref[...], preferred_element_type=f32)
    base_ref[...] = x1 + gtf_ref[0] * shared

    scores = jax.nn.sigmoid(_nt_dot(wrt_ref[...], h2b))
    biased = scores + rbias_ref[...]
    per = n_exp // N_GROUPS
    neg_inf = jnp.float32(-jnp.inf)
    ri_g = lax.broadcasted_iota(i32, (per, t), 0).astype(f32)
    gs = []
    for grp in range(N_GROUPS):
        blk = biased[grp * per:(grp + 1) * per]
        m1 = jnp.max(blk, axis=0, keepdims=True)
        idx1 = jnp.min(jnp.where(blk == m1, ri_g, float(per)), axis=0, keepdims=True)
        m2 = jnp.max(jnp.where(ri_g == idx1, neg_inf, blk), axis=0, keepdims=True)
        gs.append(m1 + m2)
    masked_rows = []
    for grp in range(N_GROUPS):
        beaten = jnp.zeros((1, t), f32)
        for g2 in range(N_GROUPS):
            if g2 == grp:
                continue
            wins = gs[g2] > gs[grp]
            if g2 < grp:
                wins = wins | (gs[g2] == gs[grp])
            beaten = beaten + jnp.where(wins, 1.0, 0.0)
        keep = jnp.broadcast_to(beaten < TOPK_GROUPS, (per, t))
        masked_rows.append(jnp.where(keep, biased[grp * per:(grp + 1) * per], neg_inf))
    masked = jnp.concatenate(masked_rows, axis=0)

    ri = lax.broadcasted_iota(i32, (n_exp, t), 0).astype(f32)
    selmask = jnp.zeros((n_exp, t), f32)
    idxs, graw = [], []
    for _ in range(TOP_K):
        m = jnp.max(masked, axis=0, keepdims=True)
        idx = jnp.min(jnp.where(masked == m, ri, float(n_exp)), axis=0, keepdims=True)
        onehot = ri == idx
        graw.append(jnp.sum(jnp.where(onehot, scores, 0.0), axis=0, keepdims=True))
        masked = jnp.where(onehot, neg_inf, masked)
        selmask = jnp.where(onehot, 1.0, selmask)
        idxs.append(idx)
    den = graw[0]
    for k in range(1, TOP_K):
        den = den + graw[k]

    prefix = jnp.dot(selmask.astype(bf16), triu_ref[...], preferred_element_type=f32)
    prefix = prefix + jnp.concatenate([carry_ref[...]] * (t // LANES), axis=1)
    for k in range(TOP_K):
        rank_k = jnp.sum(jnp.where(ri == idxs[k], prefix, 0.0), axis=0, keepdims=True)
        sel_ref[k:k + 1, :] = idxs[k].astype(i32)
        rank_ref[k:k + 1, :] = rank_k.astype(i32)
        gate_ref[k:k + 1, :] = graw[k] / den * ROUTED_SCALE
    carry_ref[...] = carry_ref[...] + jnp.broadcast_to(
        jnp.sum(selmask, axis=1, keepdims=True), carry_ref.shape)
    cnt_ref[...] = carry_ref[...]


def _mid(oa, ob, x2, gta, scf, shf, gtf, woa, wob, gffn, wsgu, wsd, wrt, rbias_rep, s):
    n, d = x2.shape
    t = rbias_rep.shape[1]
    n_exp = wrt.shape[0]
    d_sh = wsd.shape[0]
    triu = jnp.asarray(np.triu(np.ones((t, t), np.float32), k=1), bf16)
    tile = lambda i: (i, 0)
    const = lambda i: (0, 0)
    per_b = lambda i: ((i * t) // s, 0, 0)
    lane_tile = lambda i: (0, i)
    out_shape = [jax.ShapeDtypeStruct((n, d), f32), jax.ShapeDtypeStruct((n, d // LANES, LANES), f32),
                 jax.ShapeDtypeStruct((TOP_K, n), i32), jax.ShapeDtypeStruct((TOP_K, n), f32),
                 jax.ShapeDtypeStruct((TOP_K, n), i32), jax.ShapeDtypeStruct((n_exp, LANES), f32)]
    out_specs = [pl.BlockSpec((t, d), tile), pl.BlockSpec((t, d // LANES, LANES), lambda i: (i, 0, 0)),
                 pl.BlockSpec((TOP_K, t), lane_tile), pl.BlockSpec((TOP_K, t), lane_tile),
                 pl.BlockSpec((TOP_K, t), lane_tile), pl.BlockSpec((n_exp, LANES), const)]
    kern = functools.partial(_mid_kernel, n_exp=n_exp, t=t, d_sh=d_sh)
    return pl.pallas_call(
        kern,
        grid=(n // t,),
        in_specs=[pl.BlockSpec((t, oa.shape[1]), tile),
                  pl.BlockSpec((t, ob.shape[1]), tile),
                  pl.BlockSpec((t, d), tile),
                  pl.BlockSpec((1, 1, d), per_b),
                  pl.BlockSpec((1, 1, d), per_b),
                  pl.BlockSpec((1, 1, d), per_b),
                  pl.BlockSpec((1, 1, d), per_b),
                  pl.BlockSpec(woa.shape, const),
                  pl.BlockSpec(wob.shape, const),
                  pl.BlockSpec((1, d), const),
                  pl.BlockSpec(wsgu.shape, const),
                  pl.BlockSpec(wsd.shape, const),
                  pl.BlockSpec(wrt.shape, const),
                  pl.BlockSpec(rbias_rep.shape, const),
                  pl.BlockSpec((t, t), const)],
        out_specs=out_specs,
        out_shape=out_shape,
        scratch_shapes=[pltpu.VMEM((n_exp, LANES), f32)],
        compiler_params=pltpu.CompilerParams(
            dimension_semantics=("arbitrary",), vmem_limit_bytes=VMEM_LIMIT),
        name="mid",
    )(oa, ob, x2, gta, scf, shf, gtf, woa, wob, gffn, wsgu, wsd, wrt, rbias_rep, triu)


def _dispatch_kernel(zb_ref, slot_ref, h_ref, xs_ref, zero_ref, sem_ref, *, td, n_zero):
    step = pl.program_id(0)

    def zero_copy(b):
        return pltpu.make_async_copy(zero_ref, xs_ref.at[pl.ds(zb_ref[b], MOE_BLOCK)], sem_ref.at[0])

    @pl.when(step == 0)
    def _():
        zero_ref[...] = jnp.zeros(zero_ref.shape, f32)

        def start(b, c):
            @pl.when(zb_ref[b] >= 0)
            def _():
                zero_copy(b).start()
            return c

        def wait(b, c):
            @pl.when(zb_ref[b] >= 0)
            def _():
                zero_copy(b).wait()
            return c

        lax.fori_loop(0, n_zero, start, 0)
        lax.fori_loop(0, n_zero, wait, 0)

    def start_rows(tok, c):
        for k in range(TOP_K):
            pltpu.make_async_copy(h_ref.at[pl.ds(tok, 1)], xs_ref.at[pl.ds(slot_ref[k, tok], 1)],
                                  sem_ref.at[1]).start(priority=k % 2)
        return c

    lax.fori_loop(0, td, start_rows, 0)
    for k in range(TOP_K):
        pltpu.make_async_copy(h_ref, xs_ref.at[pl.ds(0, td)], sem_ref.at[1]).wait()


def _dispatch(zero_start, slot, h2, n_slots):
    n, rows, lanes = h2.shape
    td = min(256, n)
    kern = functools.partial(_dispatch_kernel, td=td, n_zero=zero_start.shape[0])
    grid_spec = pltpu.PrefetchScalarGridSpec(
        num_scalar_prefetch=1,
        grid=(n // td,),
        in_specs=[pl.BlockSpec((TOP_K, td), lambda i, tail: (0, i), memory_space=pltpu.SMEM),
                  pl.BlockSpec((td, rows, lanes), lambda i, tail: (i, 0, 0))],
        out_specs=pl.BlockSpec(memory_space=pl.ANY),
        scratch_shapes=[pltpu.VMEM((MOE_BLOCK, rows, lanes), f32), pltpu.SemaphoreType.DMA((2,))],
    )
    return pl.pallas_call(
        kern,
        grid_spec=grid_spec,
        out_shape=jax.ShapeDtypeStruct((n_slots, rows, lanes), f32),
        compiler_params=pltpu.CompilerParams(
            dimension_semantics=("arbitrary",), vmem_limit_bytes=VMEM_LIMIT),
        name="dispatch",
    )(zero_start, slot, h2)


def _expert_kernel(be_ref, nu_ref, xs_ref, wg_ref, wu_ref, wd_ref, ys_ref, wgu_s, wd_s):
    i = pl.program_id(0)
    f = wd_s.shape[0]

    @pl.when(i >= nu_ref[0])
    def _():
        ys_ref[...] = jnp.zeros(ys_ref.shape, f32)

    @pl.when(i < nu_ref[0])
    def _():
        @pl.when((i == 0) | (be_ref[i] != be_ref[jnp.maximum(i - 1, 0)]))
        def _():
            wgu_s[:, 0:f] = wg_ref[0].astype(bf16)
            wgu_s[:, f:2 * f] = wu_ref[0].astype(bf16)
            wd_s[...] = wd_ref[0].astype(bf16)

        n_chunks = xs_ref.shape[1]
        xb = jnp.concatenate([xs_ref[:, a, :] for a in range(n_chunks)], axis=1).astype(bf16)
        gu = jnp.dot(xb, wgu_s[...], preferred_element_type=f32)
        g = gu[:, 0:f]
        u = gu[:, f:2 * f]
        act = (g * jax.nn.sigmoid(g)) * u
        y = jnp.dot(act.astype(bf16), wd_s[...], preferred_element_type=f32)
        for a in range(n_chunks):
            ys_ref[:, a, :] = y[:, LANES * a:LANES * (a + 1)]


def _experts(block_expert, n_used, xs, w_gate, w_up, w_down):
    n_slots, rows, lanes = xs.shape
    d = rows * lanes
    n_blocks = n_slots // MOE_BLOCK
    f = w_gate.shape[2]
    blk = lambda i, be, nu: (jnp.minimum(i, nu[0] - 1), 0, 0)
    wsel = lambda i, be, nu: (be[jnp.minimum(i, nu[0] - 1)], 0, 0)
    grid_spec = pltpu.PrefetchScalarGridSpec(
        num_scalar_prefetch=2,
        grid=(n_blocks,),
        in_specs=[pl.BlockSpec((MOE_BLOCK, rows, lanes), blk),
                  pl.BlockSpec((1, d, f), wsel),
                  pl.BlockSpec((1, d, f), wsel),
                  pl.BlockSpec((1, f, d), wsel)],
        out_specs=pl.BlockSpec((MOE_BLOCK, rows, lanes), lambda i, be, nu: (i, 0, 0)),
        scratch_shapes=[pltpu.VMEM((d, 2 * f), bf16), pltpu.VMEM((f, d), bf16)],
    )
    return pl.pallas_call(
        _expert_kernel,
        grid_spec=grid_spec,
        out_shape=jax.ShapeDtypeStruct((n_slots, rows, lanes), f32),
        compiler_params=pltpu.CompilerParams(
            dimension_semantics=("arbitrary",), vmem_limit_bytes=VMEM_LIMIT),
        name="experts",
    )(block_expert, n_used, xs, w_gate, w_up, w_down)


def _combine_kernel(slot_ref, nslot_ref, gate_ref, base_ref, gtf_ref, ys_ref, o_ref, buf_ref, sem_ref, *, tc):
    step = pl.program_id(0)
    cur = step % 2

    def start_gathers(idx_ref, b):
        def body(tok, c):
            for k in range(TOP_K):
                pltpu.make_async_copy(ys_ref.at[pl.ds(idx_ref[k, tok], 1)],
                                      buf_ref.at[b, k, pl.ds(tok, 1)], sem_ref.at[b]).start(priority=k % 2)
            return c
        lax.fori_loop(0, tc, body, 0)

    @pl.when(step == 0)
    def _():
        start_gathers(slot_ref, 0)

    @pl.when(step + 1 < pl.num_programs(0))
    def _():
        start_gathers(nslot_ref, 1 - cur)

    for k in range(TOP_K):
        pltpu.make_async_copy(ys_ref.at[pl.ds(0, tc)], buf_ref.at[cur, k], sem_ref.at[cur]).wait()

    gate = gate_ref[...]
    gtf = gtf_ref[0]
    for a in range(buf_ref.shape[3]):
        cols = slice(LANES * a, LANES * (a + 1))
        routed = buf_ref[cur, 0, :, a, :] * gate[:, 0:1]
        for k in range(1, TOP_K):
            routed = routed + buf_ref[cur, k, :, a, :] * gate[:, k:k + 1]
        o_ref[:, cols] = base_ref[:, cols] + gtf[:, cols] * routed


def _combine(slot, gate_t, base, gtf, ys, s):
    n, d = base.shape
    rows, lanes = ys.shape[1:]
    tc = min(128, n)
    n_steps = n // tc
    kern = functools.partial(_combine_kernel, tc=tc)
    return pl.pallas_call(
        kern,
        grid=(n_steps,),
        in_specs=[pl.BlockSpec((TOP_K, tc), lambda i: (0, i), memory_space=pltpu.SMEM),
                  pl.BlockSpec((TOP_K, tc), lambda i: (0, jnp.minimum(i + 1, n_steps - 1)),
                               memory_space=pltpu.SMEM),
                  pl.BlockSpec((tc, TOP_K), lambda i: (i, 0)),
                  pl.BlockSpec((tc, d), lambda i: (i, 0)),
                  pl.BlockSpec((1, 1, d), lambda i: ((i * tc) // s, 0, 0)),
                  pl.BlockSpec(memory_space=pl.ANY)],
        out_specs=pl.BlockSpec((tc, d), lambda i: (i, 0)),
        out_shape=jax.ShapeDtypeStruct((n, d), f32),
        scratch_shapes=[pltpu.VMEM((2, TOP_K, tc, rows, lanes), f32), pltpu.SemaphoreType.DMA((2,))],
        compiler_params=pltpu.CompilerParams(
            dimension_semantics=("arbitrary",), vmem_limit_bytes=VMEM_LIMIT),
        name="combine",
    )(slot, slot, gate_t, base, gtf, ys)


def _layer(x, c, posf, w_ada, b_ada, g_mix, w_in, q_norm_a, k_norm_a, q_norm_b, k_norm_b, w_out, g_ffn,
           w_router, router_bias, w_gate, w_up, w_down, ws_gate, ws_up, ws_down):
    bsz, s, d = x.shape
    n = bsz * s
    n_exp = w_router.shape[1]

    mod = _adaln(c, w_ada, b_ada)[:, None, :]
    sh_a, sc_a, gt_a, sh_f, sc_f, gt_f = jnp.split(mod, 6, axis=-1)

    w_perm = _take_runs(w_in.astype(bf16), _projection_columns(), axis=1)
    gains = {"qa": q_norm_a, "ka": k_norm_a, "qb": q_norm_b, "kb": k_norm_b}
    gain_a = jnp.concatenate([gains[kind][:HALF] for kind, _ in _SLOTS[:N_NORM_SLOTS]])[None, :].astype(f32)
    gain_b = jnp.concatenate([gains[kind][HALF:] for kind, _ in _SLOTS[:N_NORM_SLOTS]])[None, :].astype(f32)

    qa, ka, va, qi, ki, wi, qb_, kb_, vb_ = _project(x, posf, sc_a, sh_a, g_mix[None, :], w_perm, gain_a, gain_b)
    o_a = _sparse_attention(qa, ka, va, qi, ki, wi)
    o_b = _dilated_attention(qb_, kb_, vb_)

    wa = N_HEADS_A * HEAD_DIM
    rows_a = np.concatenate([np.arange(h * HEAD_DIM, (h + 1) * HEAD_DIM) for h in QA_PAIR_ORDER])
    woa = _take_runs(w_out, rows_a, axis=0).astype(bf16)
    wob = w_out[wa:].astype(bf16)
    wsgu = jnp.concatenate([ws_gate, ws_up], axis=1).astype(bf16)
    wsd = ws_down.astype(bf16)
    wrt = w_router.T.astype(bf16)
    t_mid = min(256, n)
    rbias_rep = jnp.broadcast_to(router_bias.astype(f32)[:, None], (n_exp, t_mid))

    base, h2, sel, gate, rank, cnt = _mid(
        o_a.reshape(n, wa), o_b.reshape(n, -1), x.reshape(n, d), gt_a, sc_f, sh_f, gt_f,
        woa, wob, g_ffn[None, :], wsgu, wsd, wrt, rbias_rep, s)

    counts = cnt[:, 0].astype(i32)
    padded = (counts + MOE_BLOCK - 1) // MOE_BLOCK * MOE_BLOCK
    pad_end = jnp.cumsum(padded)
    pad_start = pad_end - padded
    slot = pad_start[sel] + rank
    n_blocks = -(-(n * TOP_K) // MOE_BLOCK) + n_exp
    n_slots = n_blocks * MOE_BLOCK
    block_start = jnp.arange(n_blocks, dtype=i32) * MOE_BLOCK
    block_expert = jnp.sum((pad_end[None, :] <= block_start[:, None]).astype(i32), axis=1)
    block_expert = jnp.minimum(block_expert, n_exp - 1)
    n_used = (pad_end[-1] // MOE_BLOCK).astype(i32)[None]
    is_tail = jnp.any((block_start[:, None] == (pad_end - MOE_BLOCK)[None, :]) & (padded[None, :] > 0), axis=1)
    zero_start = jnp.where(is_tail | (block_start >= pad_end[-1]), block_start, -1).astype(i32)

    xs = _dispatch(zero_start, slot, h2, n_slots)
    ys = _experts(block_expert, n_used, xs, w_gate, w_up, w_down)
    out = _combine(slot, gate.T, base, gt_f, ys, s)
    return out.reshape(bsz, s, d)


def kernel(x, c, positions, w_ada, b_ada, g_mix, w_in, q_norm_a, k_norm_a, q_norm_b, k_norm_b, w_out, g_ffn,
           w_router, router_bias, w_gate, w_up, w_down, ws_gate, ws_up, ws_down):
    posf = positions.astype(f32)[..., None]
    for l in range(w_ada.shape[0]):
        x = _layer(x, c, posf, w_ada[l], b_ada[l], g_mix[l], w_in[l], q_norm_a[l], k_norm_a[l], q_norm_b[l],
                   k_norm_b[l], w_out[l], g_ffn[l], w_router[l], router_bias[l], w_gate[l], w_up[l], w_down[l],
                   ws_gate[l], ws_up[l], ws_down[l])
    return x
```

```python
import functools
import math

import numpy as np
import jax
import jax.numpy as jnp
from jax import lax
from jax.experimental import pallas as pl
from jax.experimental.pallas import tpu as pltpu

f32 = jnp.float32
bf16 = jnp.bfloat16
i32 = jnp.int32

HEAD_DIM = 64
HALF = HEAD_DIM // 2
N_HEADS_A = 10
N_KV_A = 2
N_HEADS_B = 6
N_IDX_HEADS = 8
IDX_DIM = 64
TOPK_MAX = 256
DILATED_PATTERNS = ((128, 1), (512, 4), (2048, 16))
ROPE_THETA = 10000.0
EPS = 1e-6
TOP_K = 8
N_GROUPS = 8
TOPK_GROUPS = 4
ROUTED_SCALE = 2.5
MOE_BLOCK = 256

LANES = 128
VMEM_LIMIT = 48 * 1024 * 1024

NEG_BIG = -1e30
KEY_NEG_INF = int(np.int32(np.uint32(0xFF800000) ^ np.uint32(0x7FFFFFFF)))
KEY_POS_INF = 0x7F800000
SEARCH_INTERP_STEPS = 16
SEARCH_MAX_STEPS = SEARCH_INTERP_STEPS + 34

_OFF_QA = 0
_OFF_KA = _OFF_QA + N_HEADS_A * HEAD_DIM
_OFF_VA = _OFF_KA + N_KV_A * HEAD_DIM
_OFF_QI = _OFF_VA + N_KV_A * HEAD_DIM
_OFF_KI = _OFF_QI + N_IDX_HEADS * IDX_DIM
_OFF_WI = _OFF_KI + IDX_DIM
_OFF_QB = _OFF_WI + N_IDX_HEADS
_OFF_KB = _OFF_QB + N_HEADS_B * HEAD_DIM
_OFF_VB = _OFF_KB + N_HEADS_B * HEAD_DIM
D_IN = _OFF_VB + N_HEADS_B * HEAD_DIM

QA_PAIR_ORDER = (0, 5, 1, 6, 2, 7, 3, 8, 4, 9)

_SLOTS = (
    [("qa", h) for h in QA_PAIR_ORDER[:8]] + [("qa", 4), ("qa", 9), ("ka", 0), ("ka", 1)]
    + [("qb", h) for h in range(6)] + [("kb", h) for h in range(6)]
    + [("qi", h) for h in range(8)] + [("ki", 0), ("ki", 0), ("pad", 0), ("pad", 0)]
)
N_NORM_SLOTS = 24
N_CHUNKS = len(_SLOTS) // 4
SLAB = N_CHUNKS * LANES
_COL_VA = 2 * SLAB
_COL_VB = _COL_VA + N_KV_A * HEAD_DIM
_COL_WI = _COL_VB + N_HEADS_B * HEAD_DIM
N_COL = _COL_WI + LANES


def _slot_offset(kind, h):
    base = {"qa": _OFF_QA, "ka": _OFF_KA, "qb": _OFF_QB, "kb": _OFF_KB, "qi": _OFF_QI, "ki": _OFF_KI}
    return base[kind] + h * HEAD_DIM


def _projection_columns():
    zero_col = D_IN
    cols_a, cols_b = [], []
    for kind, h in _SLOTS:
        if kind == "pad":
            cols_a += [zero_col] * HALF
            cols_b += [zero_col] * HALF
        else:
            off = _slot_offset(kind, h)
            cols_a += list(range(off, off + HALF))
            cols_b += list(range(off + HALF, off + HEAD_DIM))
    cols = cols_a + cols_b
    cols += list(range(_OFF_VA, _OFF_VA + N_KV_A * HEAD_DIM))
    cols += list(range(_OFF_VB, _OFF_VB + N_HEADS_B * HEAD_DIM))
    cols += list(range(_OFF_WI, _OFF_WI + N_IDX_HEADS)) + [zero_col] * (LANES - N_IDX_HEADS)
    assert len(cols) == N_COL
    return np.asarray(cols, np.int32)


def _take_runs(w, idx, axis):
    size = w.shape[axis]
    pieces, a = [], 0
    idx = [int(v) for v in idx]
    while a < len(idx):
        b = a + 1
        if idx[a] == size:
            while b < len(idx) and idx[b] == size:
                b += 1
            shape = list(w.shape)
            shape[axis] = b - a
            pieces.append(jnp.zeros(shape, w.dtype))
        else:
            while b < len(idx) and idx[b] == idx[b - 1] + 1:
                b += 1
            pieces.append(lax.slice_in_dim(w, idx[a], idx[b - 1] + 1, axis=axis))
        a = b
    return jnp.concatenate(pieces, axis=axis)


def _interleave_matrix():
    p = np.zeros((2 * LANES, 2 * LANES), np.float32)
    for head in range(4):
        for i in range(HALF):
            p[HALF * head + i, HEAD_DIM * head + i] = 1.0
            p[LANES + HALF * head + i, HEAD_DIM * head + HALF + i] = 1.0
    return p


def _group_sum_matrix():
    g = np.zeros((LANES, LANES), np.float32)
    for k in range(LANES // HALF):
        g[HALF * k:HALF * (k + 1), HALF * k:HALF * (k + 1)] = 1.0
    return g


def _dilated_bias(tq):
    max_win = max(w for w, _ in DILATED_PATTERNS)
    nd = max_win // tq + 1
    d = np.arange(nd)[:, None, None] * tq + np.arange(tq)[None, :, None] - np.arange(tq)[None, None, :]
    mult = np.zeros(d.shape, np.float64)
    for win, dil in DILATED_PATTERNS:
        mult += ((d >= 0) & (d <= win) & (d % dil == 0)).astype(np.float64)
    with np.errstate(divide="ignore"):
        bias = np.where(mult > 0, np.log(np.maximum(mult, 1.0)), NEG_BIG)
    return bias.astype(np.float32), nd


def _nt_dot(a, b):
    return lax.dot_general(a, b, (((1,), (1,)), ((), ())), preferred_element_type=f32)


def _adaln_kernel(c_ref, w_ref, b_ref, o_ref):
    c = c_ref[...]
    a = c * jax.nn.sigmoid(c)
    o_ref[...] = jnp.dot(a, w_ref[...], preferred_element_type=f32) + b_ref[...]


def _adaln(c, w_ada, b_ada):
    bsz, d = c.shape
    n = w_ada.shape[1]
    rows = -(-bsz // 8) * 8
    c_pad = jnp.zeros((rows, d), f32).at[:bsz].set(c)
    tn = 512
    out = pl.pallas_call(
        _adaln_kernel,
        grid=(n // tn,),
        in_specs=[pl.BlockSpec((rows, d), lambda j: (0, 0)),
                  pl.BlockSpec((d, tn), lambda j: (0, j)),
                  pl.BlockSpec((1, tn), lambda j: (0, j))],
        out_specs=pl.BlockSpec((rows, tn), lambda j: (0, j)),
        out_shape=jax.ShapeDtypeStruct((rows, n), f32),
        name="adaln",
    )(c_pad, w_ada, b_ada.reshape(1, n))
    return out[:bsz]


def _proj_kernel(x_ref, pos_ref, sc_ref, sh_ref, g_ref, w_ref, ga_ref, gb_ref, gsum_ref, perm_ref, invf_ref,
                 qa_ref, ka_ref, va_ref, qi_ref, ki_ref, wi_ref, qb_ref, kb_ref, vb_ref):
    x = x_ref[0]
    ms = jnp.mean(x * x, axis=-1, keepdims=True)
    h = (x * lax.rsqrt(ms + EPS)) * g_ref[...]
    h = h * (1.0 + sc_ref[0]) + sh_ref[0]
    proj = jnp.dot(h.astype(bf16), w_ref[...], preferred_element_type=f32)

    ang = pos_ref[0] * invf_ref[...]
    cos = jnp.cos(ang)
    sin = jnp.sin(ang)
    gsum = gsum_ref[...]
    perm = perm_ref[...]
    heads = []
    for c in range(N_CHUNKS):
        a = proj[:, LANES * c:LANES * (c + 1)]
        b = proj[:, SLAB + LANES * c:SLAB + LANES * (c + 1)]
        if 4 * c < N_NORM_SLOTS:
            ss = a * a + b * b
            hi = ss.astype(bf16)
            lo = (ss - hi.astype(f32)).astype(bf16)
            tot = (jnp.dot(hi, gsum, preferred_element_type=f32)
                   + jnp.dot(lo, gsum, preferred_element_type=f32))
            inv = lax.rsqrt(tot * (1.0 / HEAD_DIM) + EPS)
            a = a * inv * ga_ref[:, LANES * c:LANES * (c + 1)]
            b = b * inv * gb_ref[:, LANES * c:LANES * (c + 1)]
        ra = a * cos - b * sin
        rb = b * cos + a * sin
        ab = jnp.concatenate([ra, rb], axis=1).astype(bf16)
        heads.append(jnp.dot(ab, perm, preferred_element_type=f32).astype(bf16))

    qa_ref[0, :, 0:256] = heads[0]
    qa_ref[0, :, 256:512] = heads[1]
    qa_ref[0, :, 512:640] = heads[2][:, 0:128]
    ka_ref[0] = heads[2][:, 128:256]
    qb_ref[0, :, 0:256] = heads[3]
    qb_ref[0, :, 256:384] = heads[4][:, 0:128]
    kb_ref[0, :, 0:128] = heads[4][:, 128:256]
    kb_ref[0, :, 128:384] = heads[5]
    qi_ref[0, :, 0:256] = heads[6]
    qi_ref[0, :, 256:512] = heads[7]
    ki_ref[0] = heads[8][:, 0:128]
    va_ref[0] = proj[:, _COL_VA:_COL_VB].astype(bf16)
    vb_ref[0] = proj[:, _COL_VB:_COL_WI].astype(bf16)
    wi_ref[0] = proj[:, _COL_WI:N_COL]


def _project(x, posf, sc, sh, g, w_perm, gain_a, gain_b):
    bsz, s, d = x.shape
    ts = min(256, s)
    gsum = jnp.asarray(_group_sum_matrix(), bf16)
    perm = jnp.asarray(_interleave_matrix(), bf16)
    inv = ROPE_THETA ** (-np.arange(HALF, dtype=np.float32) / HALF)
    invf = jnp.asarray(np.tile(inv, LANES // HALF)[None, :], f32)
    wa = N_HEADS_A * HEAD_DIM
    wb = N_HEADS_B * HEAD_DIM
    wq = N_IDX_HEADS * IDX_DIM
    const = lambda b, i: (0, 0)
    tile = lambda b, i: (b, i, 0)
    per_b = lambda b, i: (b, 0, 0)
    out_shape = [jax.ShapeDtypeStruct((bsz, s, w), dt) for w, dt in
                 ((wa, bf16), (LANES, bf16), (LANES, bf16), (wq, bf16), (LANES, bf16), (LANES, f32),
                  (wb, bf16), (wb, bf16), (wb, bf16))]
    out_specs = [pl.BlockSpec((1, ts, sh_.shape[2]), tile) for sh_ in out_shape]
    return pl.pallas_call(
        _proj_kernel,
        grid=(bsz, s // ts),
        in_specs=[pl.BlockSpec((1, ts, d), tile),
                  pl.BlockSpec((1, ts, 1), tile),
                  pl.BlockSpec((1, 1, d), per_b),
                  pl.BlockSpec((1, 1, d), per_b),
                  pl.BlockSpec((1, d), const),
                  pl.BlockSpec((d, N_COL), const),
                  pl.BlockSpec((1, N_NORM_SLOTS * HALF), const),
                  pl.BlockSpec((1, N_NORM_SLOTS * HALF), const),
                  pl.BlockSpec((LANES, LANES), const),
                  pl.BlockSpec((2 * LANES, 2 * LANES), const),
                  pl.BlockSpec((1, LANES), const)],
        out_specs=out_specs,
        out_shape=out_shape,
        compiler_params=pltpu.CompilerParams(
            dimension_semantics=("arbitrary", "arbitrary"), vmem_limit_bytes=VMEM_LIMIT),
        name="in_proj",
    )(x, posf, sc, sh, g, w_perm, gain_a, gain_b, gsum, perm, invf)


def _sparse_attn_kernel(qa_ref, ka_ref, va_ref, qi_ref, ki_ref, wi_ref, triu_ref, o_ref,
                        keys_ref, qis_ref, qas_ref, wrep_ref, m_ref, acc_ref,
                        *, qb, tk, n_sel, w_scale):
    i = pl.program_id(1)
    n_pairs_a = N_HEADS_A // 2
    lane = lax.broadcasted_iota(i32, (qb, LANES), 1)
    left = lane < HEAD_DIM
    mask_l = left.astype(f32).astype(bf16)
    mask_r = (1.0 - left.astype(f32)).astype(bf16)

    for c in range(N_IDX_HEADS // 2):
        ch = qi_ref[0, :, LANES * c:LANES * (c + 1)]
        qis_ref[(2 * c) * qb:(2 * c + 1) * qb, :] = ch * mask_l
        qis_ref[(2 * c + 1) * qb:(2 * c + 2) * qb, :] = ch * mask_r
    q_scale = jnp.asarray(HEAD_DIM ** -0.5, bf16)
    for c in range(n_pairs_a):
        ch = qa_ref[0, :, LANES * c:LANES * (c + 1)] * q_scale
        qas_ref[0, c * qb:(c + 1) * qb, :] = ch * mask_l
        qas_ref[1, c * qb:(c + 1) * qb, :] = ch * mask_r
    wi = wi_ref[0] * w_scale
    for h in range(N_IDX_HEADS):
        wrep_ref[h] = jnp.broadcast_to(wi[:, h:h + 1], (qb, LANES))

    n_tiles = (i * qb) // tk + 1
    reps = tk // LANES
    row = lax.broadcasted_iota(i32, (qb, tk), 0) + i * qb
    col = lax.broadcasted_iota(i32, (qb, tk), 1)

    def score_body(j, rmax):
        start = pl.multiple_of(j * tk, tk)
        kt = ki_ref[0, pl.ds(start, tk), :]
        lg = _nt_dot(qis_ref[...], kt)
        acc = jnp.zeros((qb, tk), f32)
        for h in range(N_IDX_HEADS):
            wr = jnp.concatenate([wrep_ref[h]] * reps, axis=1)
            acc = acc + jnp.maximum(lg[h * qb:(h + 1) * qb], 0.0) * wr
        acc = jnp.where(col + j * tk <= row, acc, -jnp.inf)
        keys_ref[j] = acc
        for r in range(reps):
            rmax = jnp.maximum(rmax, acc[:, LANES * r:LANES * (r + 1)])
        return rmax

    rmax = lax.fori_loop(0, n_tiles, score_body, jnp.full((qb, LANES), -jnp.inf, f32))
    rmax = jnp.broadcast_to(jnp.max(rmax, axis=1, keepdims=True), (qb, LANES))

    def key_to_float(key):
        return lax.bitcast_convert_type(key ^ ((key >> 31) & 0x7FFFFFFF), f32)

    def float_to_key(v):
        bits = lax.bitcast_convert_type(v, i32)
        return bits ^ ((bits >> 31) & 0x7FFFFFFF)

    def count_ge(trial_f):
        def body(j, cnt):
            kt = keys_ref[j]
            for r in range(reps):
                cnt = cnt + jnp.where(kt[:, LANES * r:LANES * (r + 1)] >= trial_f, 1.0, 0.0)
            return cnt
        cnt = lax.fori_loop(0, n_tiles, body, jnp.zeros((qb, LANES), f32))
        return jnp.broadcast_to(jnp.sum(cnt, axis=1, keepdims=True), (qb, LANES))

    log_target = math.log(n_sel - 0.5)
    rmax_pad = rmax + jnp.abs(rmax) * 2.0 ** -20 + 1e-30

    def zero_body(j, cnts):
        ge, gt = cnts
        kt = keys_ref[j]
        for r in range(reps):
            ch = kt[:, LANES * r:LANES * (r + 1)]
            ge = ge + jnp.where(ch >= 0.0, 1.0, 0.0)
            gt = gt + jnp.where(ch > 0.0, 1.0, 0.0)
        return ge, gt

    zeros = jnp.zeros((qb, LANES), f32)
    ge0, gt0 = lax.fori_loop(0, n_tiles, zero_body, (zeros, zeros))
    ge0 = jnp.broadcast_to(jnp.sum(ge0, axis=1, keepdims=True), (qb, LANES))
    gt0 = jnp.broadcast_to(jnp.sum(gt0, axis=1, keepdims=True), (qb, LANES))
    total = zeros + (n_tiles * tk).astype(f32)
    above = ge0 >= n_sel
    lo_v0 = jnp.where(above, 0.0, -jnp.inf)
    lo_c0 = jnp.where(above, ge0, total)
    hi_v0 = jnp.where(above, jnp.inf, 0.0)
    hi_c0 = jnp.where(above, jnp.where(gt0 < n_sel, gt0, 0.0), ge0)
    done0 = (above & (gt0 < n_sel)) | (lo_c0 == n_sel)

    def search_cond(state):
        return (state[0] < SEARCH_MAX_STEPS) & state[-1]

    def search_body(state):
        it, lo_v, lo_c, f_lo, hi_v, hi_c, f_hi, last, done, _ = state
        lo_k = float_to_key(lo_v)
        hi_k = float_to_key(hi_v)
        hi_eff = jnp.where(hi_v == jnp.inf, rmax_pad, hi_v)
        t_int = lo_v + (hi_eff - lo_v) * (f_lo / (f_lo - f_hi))
        use_int = (lo_v > -jnp.inf) & (t_int > lo_v) & (t_int < hi_v) & (it < SEARCH_INTERP_STEPS)
        mid_k = (lo_k >> 1) + (hi_k >> 1) + (lo_k & hi_k & 1)
        t = jnp.where(use_int, t_int, key_to_float(mid_k))
        c = count_ge(t)
        f = jnp.log(jnp.maximum(c, 0.5)) - log_target
        active = done < 0.5
        is_lo = (c >= n_sel) & active
        is_hi = (c < n_sel) & active
        f_hi = jnp.where(is_lo & (last > 0.0), f_hi * 0.5, f_hi)
        f_lo = jnp.where(is_hi & (last < 0.0), f_lo * 0.5, f_lo)
        lo_v = jnp.where(is_lo, t, lo_v)
        lo_c = jnp.where(is_lo, c, lo_c)
        f_lo = jnp.where(is_lo, f, f_lo)
        hi_v = jnp.where(is_hi, t, hi_v)
        hi_c = jnp.where(is_hi, c, hi_c)
        f_hi = jnp.where(is_hi, f, f_hi)
        last = jnp.where(is_lo, 1.0, jnp.where(is_hi, -1.0, last))
        settled = (lo_c == n_sel) | (float_to_key(lo_v) + 1 >= float_to_key(hi_v))
        done = jnp.where(settled, 1.0, done)
        unresolved = jnp.min(done) < 0.5
        return it + 1, lo_v, lo_c, f_lo, hi_v, hi_c, f_hi, last, done, unresolved

    done0 = jnp.where(done0, 1.0, 0.0)
    init = (jnp.int32(0), lo_v0, lo_c0, jnp.log(lo_c0) - log_target,
            hi_v0, hi_c0, jnp.log(jnp.maximum(hi_c0, 0.5)) - log_target,
            zeros, done0, jnp.min(done0) < 0.5)
    final = lax.while_loop(search_cond, search_body, init)
    thr, lo_c, hi_c = final[1], final[2], final[5]
    thr_t = jnp.concatenate([thr] * reps, axis=1)
    tied = lo_c != n_sel
    any_tied = jnp.max(jnp.where(tied, 1.0, 0.0)) > 0.0
    quota_t = jnp.concatenate([jnp.where(tied, n_sel - hi_c, 2.0 * tk * (n_tiles + 1).astype(f32))] * reps, axis=1)

    m_ref[...] = jnp.full(m_ref.shape, NEG_BIG, f32)
    acc_ref[...] = jnp.zeros(acc_ref.shape, f32)
    lane_k = lax.broadcasted_iota(i32, (tk, LANES), 1)
    kmask_l = (lane_k < HEAD_DIM).astype(f32).astype(bf16)
    kmask_r = (lane_k >= HEAD_DIM).astype(f32).astype(bf16)

    def attn_body(j, n_ties, with_ties):
        start = pl.multiple_of(j * tk, tk)
        sc = keys_ref[j]
        if with_ties:
            eq = jnp.where(sc == thr_t, 1.0, 0.0)
            before = (jnp.dot(eq.astype(bf16), triu_ref[...], preferred_element_type=f32)
                      + jnp.concatenate([n_ties] * reps, axis=1))
            sel = (sc > thr_t) | ((sc == thr_t) & (before < quota_t))
            n_ties = n_ties + jnp.broadcast_to(jnp.sum(eq, axis=1, keepdims=True), (qb, LANES))
        else:
            sel = sc >= thr_t
        sel = sel & (col + j * tk <= row)
        bias = jnp.where(sel, 0.0, NEG_BIG)
        bias = jnp.concatenate([bias] * n_pairs_a, axis=0)
        kk = ka_ref[0, pl.ds(start, tk), :]
        vv = va_ref[0, pl.ds(start, tk), :]
        v_ext = (vv * kmask_l + kmask_r, vv * kmask_r + kmask_l)
        for g in range(N_KV_A):
            s = _nt_dot(qas_ref[g], kk) + bias
            m_old = m_ref[g]
            m_new = jnp.maximum(m_old, jnp.max(s, axis=1, keepdims=True))
            alpha = jnp.exp(m_old - m_new)
            p = jnp.exp(s - jnp.concatenate([m_new] * reps, axis=1))
            acc_ref[g] = alpha * acc_ref[g] + jnp.dot(p.astype(bf16), v_ext[g], preferred_element_type=f32)
            m_ref[g] = m_new
        return n_ties

    no_ties = jnp.zeros((qb, LANES), f32)

    @pl.when(any_tied)
    def _():
        lax.fori_loop(0, n_tiles, functools.partial(attn_body, with_ties=True), no_ties)

    @pl.when(jnp.logical_not(any_tied))
    def _():
        lax.fori_loop(0, n_tiles, functools.partial(attn_body, with_ties=False), no_ties)

    for c in range(n_pairs_a):
        rows = slice(c * qb, (c + 1) * qb)
        a0 = acc_ref[0, rows, :]
        a1 = acc_ref[1, rows, :]
        o0 = a0 / pltpu.roll(a0, HEAD_DIM, axis=1)
        o1 = a1 / pltpu.roll(a1, HEAD_DIM, axis=1)
        o_ref[0, :, LANES * c:LANES * (c + 1)] = jnp.where(left, o0, o1).astype(bf16)


def _sparse_attention(qa, ka, va, qi, ki, wi):
    bsz, s, wa = qa.shape
    qb = 128
    tk = min(512, s)
    n_sel = min(TOPK_MAX, s // 4)
    w_scale = N_IDX_HEADS ** -0.5 * IDX_DIM ** -0.5
    n_pairs_a = N_HEADS_A // 2
    tile = lambda b, i: (b, i, 0)
    per_b = lambda b, i: (b, 0, 0)
    kern = functools.partial(_sparse_attn_kernel, qb=qb, tk=tk, n_sel=float(n_sel), w_scale=w_scale)
    return pl.pallas_call(
        kern,
        grid=(bsz, s // qb),
        in_specs=[pl.BlockSpec((1, qb, wa), tile),
                  pl.BlockSpec((1, s, LANES), per_b),
                  pl.BlockSpec((1, s, LANES), per_b),
                  pl.BlockSpec((1, qb, qi.shape[2]), tile),
                  pl.BlockSpec((1, s, LANES), per_b),
                  pl.BlockSpec((1, qb, LANES), tile),
                  pl.BlockSpec((tk, tk), lambda b, i: (0, 0))],
        out_specs=pl.BlockSpec((1, qb, wa), tile),
        out_shape=jax.ShapeDtypeStruct((bsz, s, wa), bf16),
        scratch_shapes=[pltpu.VMEM((s // tk, qb, tk), f32),
                        pltpu.VMEM((N_IDX_HEADS * qb, LANES), bf16),
                        pltpu.VMEM((N_KV_A, n_pairs_a * qb, LANES), bf16),
                        pltpu.VMEM((N_IDX_HEADS, qb, LANES), f32),
                        pltpu.VMEM((N_KV_A, n_pairs_a * qb, LANES), f32),
                        pltpu.VMEM((N_KV_A, n_pairs_a * qb, LANES), f32)],
        compiler_params=pltpu.CompilerParams(
            dimension_semantics=("arbitrary", "arbitrary"), vmem_limit_bytes=VMEM_LIMIT),
        name="sparse_attn",
    )(qa, ka, va, qi, ki, wi, jnp.asarray(np.triu(np.ones((tk, tk), np.float32), k=1), bf16))


def _dilated_kernel(q_ref, k_ref, v_ref, bias_ref, o_ref, qs_ref, m_ref, acc_ref, *, tq, nd):
    i = pl.program_id(2)
    lane = lax.broadcasted_iota(i32, (tq, LANES), 1)
    left = lane < HEAD_DIM
    mask_l = left.astype(f32).astype(bf16)
    mask_r = (1.0 - left.astype(f32)).astype(bf16)
    q = q_ref[0] * jnp.asarray(HEAD_DIM ** -0.5, bf16)
    qs_ref[0:tq, :] = q * mask_l
    qs_ref[tq:2 * tq, :] = q * mask_r
    m_ref[...] = jnp.full(m_ref.shape, NEG_BIG, f32)
    acc_ref[...] = jnp.zeros(acc_ref.shape, f32)
    ones = jnp.ones((tq, LANES), bf16)

    def body(j, carry):
        start = pl.multiple_of(j * tq, tq)
        kk = k_ref[0, pl.ds(start, tq), :]
        v_ext = jnp.concatenate([v_ref[0, pl.ds(start, tq), :], ones], axis=1)
        b = bias_ref[i - j]
        s = _nt_dot(qs_ref[...], kk) + jnp.concatenate([b, b], axis=0)
        m_old = m_ref[...]
        m_new = jnp.maximum(m_old, jnp.max(s, axis=1, keepdims=True))
        alpha = jnp.exp(m_old - m_new)
        p = jnp.exp(s - jnp.concatenate([m_new] * (tq // LANES), axis=1))
        pv = jnp.dot(p.astype(bf16), v_ext, preferred_element_type=f32)
        acc_ref[...] = jnp.concatenate([alpha, alpha], axis=1) * acc_ref[...] + pv
        m_ref[...] = m_new
        return carry

    lax.fori_loop(jnp.maximum(i - (nd - 1), 0), i + 1, body, 0)
    o0 = acc_ref[0:tq, 0:LANES] / acc_ref[0:tq, LANES:2 * LANES]
    o1 = acc_ref[tq:2 * tq, 0:LANES] / acc_ref[tq:2 * tq, LANES:2 * LANES]
    o_ref[0] = jnp.where(left, o0, o1).astype(bf16)


def _dilated_attention(qb_, kb_, vb_):
    bsz, s, wb = qb_.shape
    tq = min(512, s)
    bias_np, nd = _dilated_bias(tq)
    bias = jnp.asarray(bias_np)
    n_pairs = wb // LANES
    kern = functools.partial(_dilated_kernel, tq=tq, nd=nd)
    return pl.pallas_call(
        kern,
        grid=(bsz, n_pairs, s // tq),
        in_specs=[pl.BlockSpec((1, tq, LANES), lambda b, p, i: (b, i, p)),
                  pl.BlockSpec((1, s, LANES), lambda b, p, i: (b, 0, p)),
                  pl.BlockSpec((1, s, LANES), lambda b, p, i: (b, 0, p)),
                  pl.BlockSpec((nd, tq, tq), lambda b, p, i: (0, 0, 0))],
        out_specs=pl.BlockSpec((1, tq, LANES), lambda b, p, i: (b, i, p)),
        out_shape=jax.ShapeDtypeStruct((bsz, s, wb), bf16),
        scratch_shapes=[pltpu.VMEM((2 * tq, LANES), bf16),
                        pltpu.VMEM((2 * tq, LANES), f32),
                        pltpu.VMEM((2 * tq, 2 * LANES), f32)],
        compiler_params=pltpu.CompilerParams(
            dimension_semantics=("arbitrary", "arbitrary", "arbitrary"), vmem_limit_bytes=VMEM_LIMIT),
        name="dilated_attn",
    )(qb_, kb_, vb_, bias)


def _mid_kernel(oa_ref, ob_ref, x_ref, gta_ref, scf_ref, shf_ref, gtf_ref, woa_ref, wob_ref, gffn_ref,
                wsgu_ref, wsd_ref, wrt_ref, rbias_ref, triu_ref,
                base_ref, h2_ref, sel_ref, gate_ref, rank_ref, cnt_ref, carry_ref, *, n_exp, t, d_sh):
    step = pl.program_id(0)

    @pl.when(step == 0)
    def _():
        carry_ref[...] = jnp.zeros(carry_ref.shape, f32)

    mix = (jnp.dot(oa_ref[...], woa_ref[...], preferred_element_type=f32)
           + jnp.dot(ob_ref[...], wob_ref[...], preferred_element_type=f32))
    x1 = x_ref[...] + gta_ref[0] * mix
    ms = jnp.mean(x1 * x1, axis=-1, keepdims=True)
    h2 = ((x1 * lax.rsqrt(ms + EPS)) * gffn_ref[...]) * (1.0 + scf_ref[0]) + shf_ref[0]
    h2b = h2.astype(bf16)
    h2f = h2b.astype(f32)
    n_sub = h2_ref.shape[0] // t
    for a in range(n_sub):
        h2_ref[pl.ds(a, t, stride=n_sub), :] = h2f[:, LANES * a:LANES * (a + 1)]

    gu = jnp.dot(h2b, wsgu_ref[...], preferred_element_type=f32)
    g = gu[:, :d_sh]
    u = gu[:, d_sh:]
    act = (g * jax.nn.sigmoid(g)) * u
    shared = jnp.dot(act.astype(bf16), wsd_ref[...], preferred_element_type=f32)
    base = x1 + gtf_ref[0] * shared
    for a in range(n_sub):
        base_ref[pl.ds(a, t, stride=n_sub), :] = base[:, LANES * a:LANES * (a + 1)]

    scores = jax.nn.sigmoid(_nt_dot(wrt_ref[...], h2b))
    biased = scores + rbias_ref[...]
    per = n_exp // N_GROUPS
    neg_inf = jnp.float32(-jnp.inf)
    ri_g = lax.broadcasted_iota(i32, (per, t), 0).astype(f32)
    gs = []
    for grp in range(N_GROUPS):
        blk = biased[grp * per:(grp + 1) * per]
        m1 = jnp.max(blk, axis=0, keepdims=True)
        idx1 = jnp.min(jnp.where(blk == m1, ri_g, float(per)), axis=0, keepdims=True)
        m2 = jnp.max(jnp.where(ri_g == idx1, neg_inf, blk), axis=0, keepdims=True)
        gs.append(m1 + m2)
    masked_rows = []
    for grp in range(N_GROUPS):
        beaten = jnp.zeros((1, t), f32)
        for g2 in range(N_GROUPS):
            if g2 == grp:
                continue
            wins = gs[g2] > gs[grp]
            if g2 < grp:
                wins = wins | (gs[g2] == gs[grp])
            beaten = beaten + jnp.where(wins, 1.0, 0.0)
        keep = jnp.broadcast_to(beaten < TOPK_GROUPS, (per, t))
        masked_rows.append(jnp.where(keep, biased[grp * per:(grp + 1) * per], neg_inf))
    masked = jnp.concatenate(masked_rows, axis=0)

    ri = lax.broadcasted_iota(i32, (n_exp, t), 0).astype(f32)
    selmask = jnp.zeros((n_exp, t), f32)
    idxs, graw = [], []
    for _ in range(TOP_K):
        m = jnp.max(masked, axis=0, keepdims=True)
        idx = jnp.min(jnp.where(masked == m, ri, float(n_exp)), axis=0, keepdims=True)
        onehot = ri == idx
        graw.append(jnp.sum(jnp.where(onehot, scores, 0.0), axis=0, keepdims=True))
        masked = jnp.where(onehot, neg_inf, masked)
        selmask = jnp.where(onehot, 1.0, selmask)
        idxs.append(idx)
    den = graw[0]
    for k in range(1, TOP_K):
        den = den + graw[k]

    prefix = jnp.dot(selmask.astype(bf16), triu_ref[...], preferred_element_type=f32)
    prefix = prefix + jnp.concatenate([carry_ref[...]] * (t // LANES), axis=1)
    for k in range(TOP_K):
        rank_k = jnp.sum(jnp.where(ri == idxs[k], prefix, 0.0), axis=0, keepdims=True)
        sel_ref[k:k + 1, :] = idxs[k].astype(i32)
        rank_ref[k:k + 1, :] = rank_k.astype(i32)
        gate_ref[k:k + 1, :] = graw[k] / den * ROUTED_SCALE
    carry_ref[...] = carry_ref[...] + jnp.broadcast_to(
        jnp.sum(selmask, axis=1, keepdims=True), carry_ref.shape)
    cnt_ref[...] = carry_ref[...]


def _mid(oa, ob, x2, gta, scf, shf, gtf, woa, wob, gffn, wsgu, wsd, wrt, rbias_rep, s):
    n, d = x2.shape
    t = rbias_rep.shape[1]
    n_exp = wrt.shape[0]
    d_sh = wsd.shape[0]
    triu = jnp.asarray(np.triu(np.ones((t, t), np.float32), k=1), bf16)
    tile = lambda i: (i, 0)
    const = lambda i: (0, 0)
    per_b = lambda i: ((i * t) // s, 0, 0)
    lane_tile = lambda i: (0, i)
    n_sub = d // LANES
    tok_tiles = jax.ShapeDtypeStruct((n * n_sub, LANES), f32)
    tok_spec = pl.BlockSpec((t * n_sub, LANES), tile)
    out_shape = [tok_tiles, tok_tiles,
                 jax.ShapeDtypeStruct((TOP_K, n), i32), jax.ShapeDtypeStruct((TOP_K, n), f32),
                 jax.ShapeDtypeStruct((TOP_K, n), i32), jax.ShapeDtypeStruct((n_exp, LANES), f32)]
    out_specs = [tok_spec, tok_spec,
                 pl.BlockSpec((TOP_K, t), lane_tile), pl.BlockSpec((TOP_K, t), lane_tile),
                 pl.BlockSpec((TOP_K, t), lane_tile), pl.BlockSpec((n_exp, LANES), const)]
    kern = functools.partial(_mid_kernel, n_exp=n_exp, t=t, d_sh=d_sh)
    return pl.pallas_call(
        kern,
        grid=(n // t,),
        in_specs=[pl.BlockSpec((t, oa.shape[1]), tile),
                  pl.BlockSpec((t, ob.shape[1]), tile),
                  pl.BlockSpec((t, d), tile),
                  pl.BlockSpec((1, 1, d), per_b),
                  pl.BlockSpec((1, 1, d), per_b),
                  pl.BlockSpec((1, 1, d), per_b),
                  pl.BlockSpec((1, 1, d), per_b),
                  pl.BlockSpec(woa.shape, const),
                  pl.BlockSpec(wob.shape, const),
                  pl.BlockSpec((1, d), const),
                  pl.BlockSpec(wsgu.shape, const),
                  pl.BlockSpec(wsd.shape, const),
                  pl.BlockSpec(wrt.shape, const),
                  pl.BlockSpec(rbias_rep.shape, const),
                  pl.BlockSpec((t, t), const)],
        out_specs=out_specs,
        out_shape=out_shape,
        scratch_shapes=[pltpu.VMEM((n_exp, LANES), f32)],
        compiler_params=pltpu.CompilerParams(
            dimension_semantics=("arbitrary",), vmem_limit_bytes=VMEM_LIMIT),
        name="mid",
    )(oa, ob, x2, gta, scf, shf, gtf, woa, wob, gffn, wsgu, wsd, wrt, rbias_rep, triu)


def _dispatch_kernel(zb_ref, slot_ref, h_ref, xs_ref, zero_ref, sem_ref, *, td, n_zero):
    step = pl.program_id(0)

    def zero_copy(b):
        return pltpu.make_async_copy(zero_ref, xs_ref.at[pl.ds(zb_ref[b], MOE_BLOCK)], sem_ref.at[0])

    @pl.when(step == 0)
    def _():
        zero_ref[...] = jnp.zeros(zero_ref.shape, f32)

        def start(b, c):
            @pl.when(zb_ref[b] >= 0)
            def _():
                zero_copy(b).start()
            return c

        def wait(b, c):
            @pl.when(zb_ref[b] >= 0)
            def _():
                zero_copy(b).wait()
            return c

        lax.fori_loop(0, n_zero, start, 0)
        lax.fori_loop(0, n_zero, wait, 0)

    def start_rows(tok, c):
        for k in range(TOP_K):
            pltpu.make_async_copy(h_ref.at[pl.ds(tok, 1)], xs_ref.at[pl.ds(slot_ref[k, tok], 1)],
                                  sem_ref.at[1]).start(priority=k % 2)
        return c

    lax.fori_loop(0, td, start_rows, 0)
    for k in range(TOP_K):
        pltpu.make_async_copy(h_ref, xs_ref.at[pl.ds(0, td)], sem_ref.at[1]).wait()


def _dispatch(zero_start, slot, h2, n_slots):
    n, rows, lanes = h2.shape
    td = min(256, n)
    kern = functools.partial(_dispatch_kernel, td=td, n_zero=zero_start.shape[0])
    grid_spec = pltpu.PrefetchScalarGridSpec(
        num_scalar_prefetch=1,
        grid=(n // td,),
        in_specs=[pl.BlockSpec((TOP_K, td), lambda i, tail: (0, i), memory_space=pltpu.SMEM),
                  pl.BlockSpec((td, rows, lanes), lambda i, tail: (i, 0, 0))],
        out_specs=pl.BlockSpec(memory_space=pl.ANY),
        scratch_shapes=[pltpu.VMEM((MOE_BLOCK, rows, lanes), f32), pltpu.SemaphoreType.DMA((2,))],
    )
    return pl.pallas_call(
        kern,
        grid_spec=grid_spec,
        out_shape=jax.ShapeDtypeStruct((n_slots, rows, lanes), f32),
        compiler_params=pltpu.CompilerParams(
            dimension_semantics=("arbitrary",), vmem_limit_bytes=VMEM_LIMIT),
        name="dispatch",
    )(zero_start, slot, h2)


def _expert_kernel(be_ref, nu_ref, xs_ref, wg_ref, wu_ref, wd_ref, ys_ref, wgu_s, wd_s, *, n_sub):
    i = pl.program_id(0)
    f = wd_s.shape[0]
    rows = xs_ref.shape[0] // n_sub

    @pl.when(i >= nu_ref[0])
    def _():
        ys_ref[...] = jnp.zeros(ys_ref.shape, f32)

    @pl.when(i < nu_ref[0])
    def _():
        @pl.when((i == 0) | (be_ref[i] != be_ref[jnp.maximum(i - 1, 0)]))
        def _():
            wgu_s[:, 0:f] = wg_ref[0].astype(bf16)
            wgu_s[:, f:2 * f] = wu_ref[0].astype(bf16)
            wd_s[...] = wd_ref[0].astype(bf16)

        xb = jnp.concatenate([xs_ref[pl.ds(a, rows, stride=n_sub), :] for a in range(n_sub)],
                             axis=1).astype(bf16)
        gu = jnp.dot(xb, wgu_s[...], preferred_element_type=f32)
        g = gu[:, 0:f]
        u = gu[:, f:2 * f]
        act = (g * jax.nn.sigmoid(g)) * u
        y = jnp.dot(act.astype(bf16), wd_s[...], preferred_element_type=f32)
        for a in range(n_sub):
            ys_ref[pl.ds(a, rows, stride=n_sub), :] = y[:, LANES * a:LANES * (a + 1)]


def _experts(block_expert, n_used, xs, w_gate, w_up, w_down):
    n_slots, n_sub, lanes = xs.shape
    d = n_sub * lanes
    n_blocks = n_slots // MOE_BLOCK
    f = w_gate.shape[2]
    blk = lambda i, be, nu: (jnp.minimum(i, nu[0] - 1), 0)
    wsel = lambda i, be, nu: (be[jnp.minimum(i, nu[0] - 1)], 0, 0)
    grid_spec = pltpu.PrefetchScalarGridSpec(
        num_scalar_prefetch=2,
        grid=(n_blocks,),
        in_specs=[pl.BlockSpec((MOE_BLOCK * n_sub, lanes), blk),
                  pl.BlockSpec((1, d, f), wsel),
                  pl.BlockSpec((1, d, f), wsel),
                  pl.BlockSpec((1, f, d), wsel)],
        out_specs=pl.BlockSpec((MOE_BLOCK * n_sub, lanes), lambda i, be, nu: (i, 0)),
        scratch_shapes=[pltpu.VMEM((d, 2 * f), bf16), pltpu.VMEM((f, d), bf16)],
    )
    ys = pl.pallas_call(
        functools.partial(_expert_kernel, n_sub=n_sub),
        grid_spec=grid_spec,
        out_shape=jax.ShapeDtypeStruct((n_slots * n_sub, lanes), f32),
        compiler_params=pltpu.CompilerParams(
            dimension_semantics=("arbitrary",), vmem_limit_bytes=VMEM_LIMIT),
        name="experts",
    )(block_expert, n_used, xs.reshape(n_slots * n_sub, lanes), w_gate, w_up, w_down)
    return ys.reshape(n_slots, n_sub, lanes)


def _combine_kernel(slot_ref, nslot_ref, gate_ref, base_ref, gtf_ref, ys_ref, o_ref, buf_ref, sem_ref, *, tc):
    step = pl.program_id(0)
    cur = step % 2

    def start_gathers(idx_ref, b):
        def body(tok, c):
            for k in range(TOP_K):
                pltpu.make_async_copy(ys_ref.at[pl.ds(idx_ref[k, tok], 1)],
                                      buf_ref.at[b, k, pl.ds(tok, 1)], sem_ref.at[b]).start(priority=k % 2)
            return c
        lax.fori_loop(0, tc, body, 0)

    @pl.when(step == 0)
    def _():
        start_gathers(slot_ref, 0)

    @pl.when(step + 1 < pl.num_programs(0))
    def _():
        start_gathers(nslot_ref, 1 - cur)

    for k in range(TOP_K):
        pltpu.make_async_copy(ys_ref.at[pl.ds(0, tc)], buf_ref.at[cur, k], sem_ref.at[cur]).wait()

    gtf = gtf_ref[0]

    def reduce_token(tok, c):
        routed = buf_ref[cur, 0, tok] * gate_ref[0, tok]
        for k in range(1, TOP_K):
            routed = routed + buf_ref[cur, k, tok] * gate_ref[k, tok]
        o_ref[tok] = base_ref[tok] + gtf * routed
        return c

    lax.fori_loop(0, tc, reduce_token, 0, unroll=4)


def _combine(slot, gate, base, gtf, ys, s):
    n, rows, lanes = base.shape
    tc = min(128, n)
    n_steps = n // tc
    kern = functools.partial(_combine_kernel, tc=tc)
    tok_tile = lambda i: (i, 0, 0)
    return pl.pallas_call(
        kern,
        grid=(n_steps,),
        in_specs=[pl.BlockSpec((TOP_K, tc), lambda i: (0, i), memory_space=pltpu.SMEM),
                  pl.BlockSpec((TOP_K, tc), lambda i: (0, jnp.minimum(i + 1, n_steps - 1)),
                               memory_space=pltpu.SMEM),
                  pl.BlockSpec((TOP_K, tc), lambda i: (0, i), memory_space=pltpu.SMEM),
                  pl.BlockSpec((tc, rows, lanes), tok_tile),
                  pl.BlockSpec((1, rows, lanes), lambda i: ((i * tc) // s, 0, 0)),
                  pl.BlockSpec(memory_space=pl.ANY)],
        out_specs=pl.BlockSpec((tc, rows, lanes), tok_tile),
        out_shape=jax.ShapeDtypeStruct((n, rows, lanes), f32),
        scratch_shapes=[pltpu.VMEM((2, TOP_K, tc, rows, lanes), f32), pltpu.SemaphoreType.DMA((2,))],
        compiler_params=pltpu.CompilerParams(
            dimension_semantics=("arbitrary",), vmem_limit_bytes=VMEM_LIMIT),
        name="combine",
    )(slot, slot, gate, base, gtf, ys)


def _layer(x, c, posf, w_ada, b_ada, g_mix, w_in, q_norm_a, k_norm_a, q_norm_b, k_norm_b, w_out, g_ffn,
           w_router, router_bias, w_gate, w_up, w_down, ws_gate, ws_up, ws_down):
    bsz, s, d = x.shape
    n = bsz * s
    n_exp = w_router.shape[1]

    mod = _adaln(c, w_ada, b_ada)[:, None, :]
    sh_a, sc_a, gt_a, sh_f, sc_f, gt_f = jnp.split(mod, 6, axis=-1)

    w_perm = _take_runs(w_in.astype(bf16), _projection_columns(), axis=1)
    gains = {"qa": q_norm_a, "ka": k_norm_a, "qb": q_norm_b, "kb": k_norm_b}
    gain_a = jnp.concatenate([gains[kind][:HALF] for kind, _ in _SLOTS[:N_NORM_SLOTS]])[None, :].astype(f32)
    gain_b = jnp.concatenate([gains[kind][HALF:] for kind, _ in _SLOTS[:N_NORM_SLOTS]])[None, :].astype(f32)

    qa, ka, va, qi, ki, wi, qb_, kb_, vb_ = _project(x, posf, sc_a, sh_a, g_mix[None, :], w_perm, gain_a, gain_b)
    o_a = _sparse_attention(qa, ka, va, qi, ki, wi)
    o_b = _dilated_attention(qb_, kb_, vb_)

    wa = N_HEADS_A * HEAD_DIM
    rows_a = np.concatenate([np.arange(h * HEAD_DIM, (h + 1) * HEAD_DIM) for h in QA_PAIR_ORDER])
    woa = _take_runs(w_out, rows_a, axis=0).astype(bf16)
    wob = w_out[wa:].astype(bf16)
    wsgu = jnp.concatenate([ws_gate, ws_up], axis=1).astype(bf16)
    wsd = ws_down.astype(bf16)
    wrt = w_router.T.astype(bf16)
    t_mid = min(256, n)
    rbias_rep = jnp.broadcast_to(router_bias.astype(f32)[:, None], (n_exp, t_mid))

    base, h2, sel, gate, rank, cnt = _mid(
        o_a.reshape(n, wa), o_b.reshape(n, -1), x.reshape(n, d), gt_a, sc_f, sh_f, gt_f,
        woa, wob, g_ffn[None, :], wsgu, wsd, wrt, rbias_rep, s)
    base = base.reshape(n, d // LANES, LANES)
    h2 = h2.reshape(n, d // LANES, LANES)

    counts = cnt[:, 0].astype(i32)
    padded = (counts + MOE_BLOCK - 1) // MOE_BLOCK * MOE_BLOCK
    pad_end = jnp.cumsum(padded)
    pad_start = pad_end - padded
    onehot = sel[:, :, None] == jnp.arange(n_exp, dtype=i32)[None, None, :]
    slot = jnp.sum(jnp.where(onehot, pad_start[None, None, :], 0), axis=-1) + rank
    n_blocks = -(-(n * TOP_K) // MOE_BLOCK) + n_exp
    n_slots = n_blocks * MOE_BLOCK
    block_start = jnp.arange(n_blocks, dtype=i32) * MOE_BLOCK
    block_expert = jnp.sum((pad_end[None, :] <= block_start[:, None]).astype(i32), axis=1)
    block_expert = jnp.minimum(block_expert, n_exp - 1)
    n_used = (pad_end[-1] // MOE_BLOCK).astype(i32)[None]
    is_tail = jnp.any((block_start[:, None] == (pad_end - MOE_BLOCK)[None, :]) & (padded[None, :] > 0), axis=1)
    zero_start = jnp.where(is_tail | (block_start >= pad_end[-1]), block_start, -1).astype(i32)

    xs = _dispatch(zero_start, slot, h2, n_slots)
    ys = _experts(block_expert, n_used, xs, w_gate, w_up, w_down)
    out = _combine(slot, gate, base, gt_f.reshape(bsz, d // LANES, LANES), ys, s)
    return out.reshape(bsz, s, d)


def kernel(x, c, positions, w_ada, b_ada, g_mix, w_in, q_norm_a, k_norm_a, q_norm_b, k_norm_b, w_out, g_ffn,
           w_router, router_bias, w_gate, w_up, w_down, ws_gate, ws_up, ws_down):
    posf = positions.astype(f32)[..., None]
    for l in range(w_ada.shape[0]):
        x = _layer(x, c, posf, w_ada[l], b_ada[l], g_mix[l], w_in[l], q_norm_a[l], k_norm_a[l], q_norm_b[l],
                   k_norm_b[l], w_out[l], g_ffn[l], w_router[l], router_bias[l], w_gate[l], w_up[l], w_down[l],
                   ws_gate[l], ws_up[l], ws_down[l])
    return x
```

```python
import functools
import math

import numpy as np
import jax
import jax.numpy as jnp
from jax import lax
from jax.experimental import pallas as pl
from jax.experimental.pallas import tpu as pltpu

f32 = jnp.float32
bf16 = jnp.bfloat16
i32 = jnp.int32

HEAD_DIM = 64
HALF = HEAD_DIM // 2
N_HEADS_A = 10
N_KV_A = 2
N_HEADS_B = 6
N_IDX_HEADS = 8
IDX_DIM = 64
TOPK_MAX = 256
DILATED_PATTERNS = ((128, 1), (512, 4), (2048, 16))
ROPE_THETA = 10000.0
EPS = 1e-6
TOP_K = 8
N_GROUPS = 8
TOPK_GROUPS = 4
ROUTED_SCALE = 2.5
MOE_BLOCK = 512

LANES = 128
VMEM_LIMIT = 48 * 1024 * 1024

NEG_BIG = -1e30
KEY_NEG_INF = int(np.int32(np.uint32(0xFF800000) ^ np.uint32(0x7FFFFFFF)))
KEY_POS_INF = 0x7F800000
SEARCH_INTERP_STEPS = 16
SEARCH_MAX_STEPS = SEARCH_INTERP_STEPS + 34

_OFF_QA = 0
_OFF_KA = _OFF_QA + N_HEADS_A * HEAD_DIM
_OFF_VA = _OFF_KA + N_KV_A * HEAD_DIM
_OFF_QI = _OFF_VA + N_KV_A * HEAD_DIM
_OFF_KI = _OFF_QI + N_IDX_HEADS * IDX_DIM
_OFF_WI = _OFF_KI + IDX_DIM
_OFF_QB = _OFF_WI + N_IDX_HEADS
_OFF_KB = _OFF_QB + N_HEADS_B * HEAD_DIM
_OFF_VB = _OFF_KB + N_HEADS_B * HEAD_DIM
D_IN = _OFF_VB + N_HEADS_B * HEAD_DIM

QA_PAIR_ORDER = (0, 5, 1, 6, 2, 7, 3, 8, 4, 9)

_SLOTS = (
    [("qa", h) for h in QA_PAIR_ORDER[:8]] + [("qa", 4), ("qa", 9), ("ka", 0), ("ka", 1)]
    + [("qb", h) for h in range(6)] + [("kb", h) for h in range(6)]
    + [("qi", h) for h in range(8)] + [("ki", 0), ("ki", 0), ("pad", 0), ("pad", 0)]
)
N_NORM_SLOTS = 24
N_CHUNKS = len(_SLOTS) // 4
SLAB = N_CHUNKS * LANES
_COL_VA = 2 * SLAB
_COL_VB = _COL_VA + N_KV_A * HEAD_DIM
_COL_WI = _COL_VB + N_HEADS_B * HEAD_DIM
N_COL = _COL_WI + LANES


def _slot_offset(kind, h):
    base = {"qa": _OFF_QA, "ka": _OFF_KA, "qb": _OFF_QB, "kb": _OFF_KB, "qi": _OFF_QI, "ki": _OFF_KI}
    return base[kind] + h * HEAD_DIM


def _projection_columns():
    zero_col = D_IN
    cols_a, cols_b = [], []
    for kind, h in _SLOTS:
        if kind == "pad":
            cols_a += [zero_col] * HALF
            cols_b += [zero_col] * HALF
        else:
            off = _slot_offset(kind, h)
            cols_a += list(range(off, off + HALF))
            cols_b += list(range(off + HALF, off + HEAD_DIM))
    cols = cols_a + cols_b
    cols += list(range(_OFF_VA, _OFF_VA + N_KV_A * HEAD_DIM))
    cols += list(range(_OFF_VB, _OFF_VB + N_HEADS_B * HEAD_DIM))
    cols += list(range(_OFF_WI, _OFF_WI + N_IDX_HEADS)) + [zero_col] * (LANES - N_IDX_HEADS)
    assert len(cols) == N_COL
    return np.asarray(cols, np.int32)


def _take_runs(w, idx, axis):
    size = w.shape[axis]
    pieces, a = [], 0
    idx = [int(v) for v in idx]
    while a < len(idx):
        b = a + 1
        if idx[a] == size:
            while b < len(idx) and idx[b] == size:
                b += 1
            shape = list(w.shape)
            shape[axis] = b - a
            pieces.append(jnp.zeros(shape, w.dtype))
        else:
            while b < len(idx) and idx[b] == idx[b - 1] + 1:
                b += 1
            pieces.append(lax.slice_in_dim(w, idx[a], idx[b - 1] + 1, axis=axis))
        a = b
    return jnp.concatenate(pieces, axis=axis)


def _interleave_matrix():
    p = np.zeros((2 * LANES, 2 * LANES), np.float32)
    for head in range(4):
        for i in range(HALF):
            p[HALF * head + i, HEAD_DIM * head + i] = 1.0
            p[LANES + HALF * head + i, HEAD_DIM * head + HALF + i] = 1.0
    return p


def _group_sum_matrix():
    g = np.zeros((LANES, LANES), np.float32)
    for k in range(LANES // HALF):
        g[HALF * k:HALF * (k + 1), HALF * k:HALF * (k + 1)] = 1.0
    return g


def _dilated_bias(tq):
    max_win = max(w for w, _ in DILATED_PATTERNS)
    nd = max_win // tq + 1
    d = np.arange(nd)[:, None, None] * tq + np.arange(tq)[None, :, None] - np.arange(tq)[None, None, :]
    mult = np.zeros(d.shape, np.float64)
    for win, dil in DILATED_PATTERNS:
        mult += ((d >= 0) & (d <= win) & (d % dil == 0)).astype(np.float64)
    with np.errstate(divide="ignore"):
        bias = np.where(mult > 0, np.log(np.maximum(mult, 1.0)), NEG_BIG)
    return bias.astype(np.float32), nd


def _nt_dot(a, b):
    return lax.dot_general(a, b, (((1,), (1,)), ((), ())), preferred_element_type=f32)


def _adaln_kernel(c_ref, w_ref, b_ref, o_ref):
    c = c_ref[...]
    a = c * jax.nn.sigmoid(c)
    o_ref[...] = jnp.dot(a, w_ref[...], preferred_element_type=f32) + b_ref[...]


def _adaln(c, w_ada, b_ada):
    bsz, d = c.shape
    n = w_ada.shape[1]
    rows = -(-bsz // 8) * 8
    c_pad = jnp.zeros((rows, d), f32).at[:bsz].set(c)
    tn = 512
    out = pl.pallas_call(
        _adaln_kernel,
        grid=(n // tn,),
        in_specs=[pl.BlockSpec((rows, d), lambda j: (0, 0)),
                  pl.BlockSpec((d, tn), lambda j: (0, j)),
                  pl.BlockSpec((1, tn), lambda j: (0, j))],
        out_specs=pl.BlockSpec((rows, tn), lambda j: (0, j)),
        out_shape=jax.ShapeDtypeStruct((rows, n), f32),
        name="adaln",
    )(c_pad, w_ada, b_ada.reshape(1, n))
    return out[:bsz]


def _proj_kernel(x_ref, pos_ref, sc_ref, sh_ref, g_ref, w_ref, ga_ref, gb_ref, gsum_ref, perm_ref, invf_ref,
                 qa_ref, ka_ref, va_ref, qi_ref, ki_ref, wi_ref, qb_ref, kb_ref, vb_ref):
    x = x_ref[0]
    ms = jnp.mean(x * x, axis=-1, keepdims=True)
    h = (x * lax.rsqrt(ms + EPS)) * g_ref[...]
    h = h * (1.0 + sc_ref[0]) + sh_ref[0]
    proj = jnp.dot(h.astype(bf16), w_ref[...], preferred_element_type=f32)

    ang = pos_ref[0] * invf_ref[...]
    cos = jnp.cos(ang)
    sin = jnp.sin(ang)
    gsum = gsum_ref[...]
    perm = perm_ref[...]
    heads = []
    for c in range(N_CHUNKS):
        a = proj[:, LANES * c:LANES * (c + 1)]
        b = proj[:, SLAB + LANES * c:SLAB + LANES * (c + 1)]
        if 4 * c < N_NORM_SLOTS:
            ss = a * a + b * b
            hi = ss.astype(bf16)
            lo = (ss - hi.astype(f32)).astype(bf16)
            tot = (jnp.dot(hi, gsum, preferred_element_type=f32)
                   + jnp.dot(lo, gsum, preferred_element_type=f32))
            inv = lax.rsqrt(tot * (1.0 / HEAD_DIM) + EPS)
            a = a * inv * ga_ref[:, LANES * c:LANES * (c + 1)]
            b = b * inv * gb_ref[:, LANES * c:LANES * (c + 1)]
        ra = a * cos - b * sin
        rb = b * cos + a * sin
        ab = jnp.concatenate([ra, rb], axis=1).astype(bf16)
        heads.append(jnp.dot(ab, perm, preferred_element_type=f32).astype(bf16))

    qa_ref[0, :, 0:256] = heads[0]
    qa_ref[0, :, 256:512] = heads[1]
    qa_ref[0, :, 512:640] = heads[2][:, 0:128]
    ka_ref[0] = heads[2][:, 128:256]
    qb_ref[0, :, 0:256] = heads[3]
    qb_ref[0, :, 256:384] = heads[4][:, 0:128]
    kb_ref[0, :, 0:128] = heads[4][:, 128:256]
    kb_ref[0, :, 128:384] = heads[5]
    qi_ref[0, :, 0:256] = heads[6]
    qi_ref[0, :, 256:512] = heads[7]
    ki_ref[0] = heads[8][:, 0:128]
    va_ref[0] = proj[:, _COL_VA:_COL_VB].astype(bf16)
    vb_ref[0] = proj[:, _COL_VB:_COL_WI].astype(bf16)
    wi_ref[0] = proj[:, _COL_WI:N_COL]


def _project(x, posf, sc, sh, g, w_perm, gain_a, gain_b):
    bsz, s, d = x.shape
    ts = min(256, s)
    gsum = jnp.asarray(_group_sum_matrix(), bf16)
    perm = jnp.asarray(_interleave_matrix(), bf16)
    inv = ROPE_THETA ** (-np.arange(HALF, dtype=np.float32) / HALF)
    invf = jnp.asarray(np.tile(inv, LANES // HALF)[None, :], f32)
    wa = N_HEADS_A * HEAD_DIM
    wb = N_HEADS_B * HEAD_DIM
    wq = N_IDX_HEADS * IDX_DIM
    const = lambda b, i: (0, 0)
    tile = lambda b, i: (b, i, 0)
    per_b = lambda b, i: (b, 0, 0)
    out_shape = [jax.ShapeDtypeStruct((bsz, s, w), dt) for w, dt in
                 ((wa, bf16), (LANES, bf16), (LANES, bf16), (wq, bf16), (LANES, bf16), (LANES, f32),
                  (wb, bf16), (wb, bf16), (wb, bf16))]
    out_specs = [pl.BlockSpec((1, ts, sh_.shape[2]), tile) for sh_ in out_shape]
    return pl.pallas_call(
        _proj_kernel,
        grid=(bsz, s // ts),
        in_specs=[pl.BlockSpec((1, ts, d), tile),
                  pl.BlockSpec((1, ts, 1), tile),
                  pl.BlockSpec((1, 1, d), per_b),
                  pl.BlockSpec((1, 1, d), per_b),
                  pl.BlockSpec((1, d), const),
                  pl.BlockSpec((d, N_COL), const),
                  pl.BlockSpec((1, N_NORM_SLOTS * HALF), const),
                  pl.BlockSpec((1, N_NORM_SLOTS * HALF), const),
                  pl.BlockSpec((LANES, LANES), const),
                  pl.BlockSpec((2 * LANES, 2 * LANES), const),
                  pl.BlockSpec((1, LANES), const)],
        out_specs=out_specs,
        out_shape=out_shape,
        compiler_params=pltpu.CompilerParams(
            dimension_semantics=("arbitrary", "arbitrary"), vmem_limit_bytes=VMEM_LIMIT),
        name="in_proj",
    )(x, posf, sc, sh, g, w_perm, gain_a, gain_b, gsum, perm, invf)


def _sparse_attn_kernel(qa_ref, ka_ref, va_ref, qi_ref, ki_ref, wi_ref, triu_ref, o_ref,
                        keys_ref, qis_ref, qas_ref, wrep_ref, m_ref, acc_ref,
                        *, qb, tk, n_sel, w_scale):
    i = pl.program_id(1)
    n_pairs_a = N_HEADS_A // 2
    lane = lax.broadcasted_iota(i32, (qb, LANES), 1)
    left = lane < HEAD_DIM
    mask_l = left.astype(f32).astype(bf16)
    mask_r = (1.0 - left.astype(f32)).astype(bf16)

    for c in range(N_IDX_HEADS // 2):
        ch = qi_ref[0, :, LANES * c:LANES * (c + 1)]
        qis_ref[(2 * c) * qb:(2 * c + 1) * qb, :] = ch * mask_l
        qis_ref[(2 * c + 1) * qb:(2 * c + 2) * qb, :] = ch * mask_r
    q_scale = jnp.asarray(HEAD_DIM ** -0.5, bf16)
    for c in range(n_pairs_a):
        ch = qa_ref[0, :, LANES * c:LANES * (c + 1)] * q_scale
        qas_ref[0, c * qb:(c + 1) * qb, :] = ch * mask_l
        qas_ref[1, c * qb:(c + 1) * qb, :] = ch * mask_r
    wi = wi_ref[0] * w_scale
    for h in range(N_IDX_HEADS):
        wrep_ref[h] = jnp.broadcast_to(wi[:, h:h + 1], (qb, LANES))

    n_tiles = (i * qb) // tk + 1
    reps = tk // LANES
    row = lax.broadcasted_iota(i32, (qb, tk), 0) + i * qb
    col = lax.broadcasted_iota(i32, (qb, tk), 1)

    def score_body(j, rmax):
        start = pl.multiple_of(j * tk, tk)
        kt = ki_ref[0, pl.ds(start, tk), :]
        lg = _nt_dot(qis_ref[...], kt)
        acc = jnp.zeros((qb, tk), f32)
        for h in range(N_IDX_HEADS):
            wr = jnp.concatenate([wrep_ref[h]] * reps, axis=1)
            acc = acc + jnp.maximum(lg[h * qb:(h + 1) * qb], 0.0) * wr
        acc = jnp.where(col + j * tk <= row, acc, -jnp.inf)
        keys_ref[j] = acc
        for r in range(reps):
            rmax = jnp.maximum(rmax, acc[:, LANES * r:LANES * (r + 1)])
        return rmax

    rmax = lax.fori_loop(0, n_tiles, score_body, jnp.full((qb, LANES), -jnp.inf, f32))
    rmax = jnp.broadcast_to(jnp.max(rmax, axis=1, keepdims=True), (qb, LANES))

    def key_to_float(key):
        return lax.bitcast_convert_type(key ^ ((key >> 31) & 0x7FFFFFFF), f32)

    def float_to_key(v):
        bits = lax.bitcast_convert_type(v, i32)
        return bits ^ ((bits >> 31) & 0x7FFFFFFF)

    sweep_rows = min(qb, LANES)
    ones8 = jnp.ones((8, LANES), bf16)

    def to_rows(x_lane):
        return jnp.transpose(jnp.broadcast_to(x_lane[0:1, :], (LANES, qb)))

    def to_lanes(cnt):
        return _nt_dot(ones8, cnt.astype(bf16))

    def count_ge(trial_lane):
        trial_rows = to_rows(trial_lane)
        parts = []
        for r0 in range(0, qb, sweep_rows):
            trial_r = trial_rows[r0:r0 + sweep_rows]

            def body(j, cnt, r0=r0, trial_r=trial_r):
                for r in range(reps):
                    ch = keys_ref[j, r0:r0 + sweep_rows, LANES * r:LANES * (r + 1)]
                    cnt = cnt + jnp.where(ch >= trial_r, 1.0, 0.0)
                return cnt
            parts.append(to_lanes(lax.fori_loop(0, n_tiles, body, jnp.zeros((sweep_rows, LANES), f32))))
        return jnp.concatenate(parts, axis=1)

    log_target = math.log(n_sel - 0.5)
    rmax = jnp.transpose(rmax)[0:8, :]
    rmax_pad = rmax + jnp.abs(rmax) * 2.0 ** -20 + 1e-30

    def zero_counts(r0):
        def body(j, cnts):
            ge, gt = cnts
            for r in range(reps):
                ch = keys_ref[j, r0:r0 + sweep_rows, LANES * r:LANES * (r + 1)]
                ge = ge + jnp.where(ch >= 0.0, 1.0, 0.0)
                gt = gt + jnp.where(ch > 0.0, 1.0, 0.0)
            return ge, gt
        z = jnp.zeros((sweep_rows, LANES), f32)
        ge, gt = lax.fori_loop(0, n_tiles, body, (z, z))
        return to_lanes(ge), to_lanes(gt)

    zero_parts = [zero_counts(r0) for r0 in range(0, qb, sweep_rows)]
    ge0 = jnp.concatenate([p[0] for p in zero_parts], axis=1)
    gt0 = jnp.concatenate([p[1] for p in zero_parts], axis=1)
    zeros = jnp.zeros((8, qb), f32)
    total = zeros + (n_tiles * tk).astype(f32)
    above = ge0 >= n_sel
    lo_v0 = jnp.where(above, 0.0, -jnp.inf)
    lo_c0 = jnp.where(above, ge0, total)
    hi_v0 = jnp.where(above, jnp.inf, 0.0)
    hi_c0 = jnp.where(above, jnp.where(gt0 < n_sel, gt0, 0.0), ge0)
    done0 = (above & (gt0 < n_sel)) | (lo_c0 == n_sel)

    def search_cond(state):
        return (state[0] < SEARCH_MAX_STEPS) & state[-1]

    def search_body(state):
        it, lo_v, lo_c, f_lo, hi_v, hi_c, f_hi, last, done, _ = state
        lo_k = float_to_key(lo_v)
        hi_k = float_to_key(hi_v)
        hi_eff = jnp.where(hi_v == jnp.inf, rmax_pad, hi_v)
        t_int = lo_v + (hi_eff - lo_v) * (f_lo / (f_lo - f_hi))
        use_int = (lo_v > -jnp.inf) & (t_int > lo_v) & (t_int < hi_v) & (it < SEARCH_INTERP_STEPS)
        mid_k = (lo_k >> 1) + (hi_k >> 1) + (lo_k & hi_k & 1)
        t = jnp.where(use_int, t_int, key_to_float(mid_k))
        c = count_ge(t)
        f = jnp.log(jnp.maximum(c, 0.5)) - log_target
        active = done < 0.5
        is_lo = (c >= n_sel) & active
        is_hi = (c < n_sel) & active
        f_hi = jnp.where(is_lo & (last > 0.0), f_hi * 0.5, f_hi)
        f_lo = jnp.where(is_hi & (last < 0.0), f_lo * 0.5, f_lo)
        lo_v = jnp.where(is_lo, t, lo_v)
        lo_c = jnp.where(is_lo, c, lo_c)
        f_lo = jnp.where(is_lo, f, f_lo)
        hi_v = jnp.where(is_hi, t, hi_v)
        hi_c = jnp.where(is_hi, c, hi_c)
        f_hi = jnp.where(is_hi, f, f_hi)
        last = jnp.where(is_lo, 1.0, jnp.where(is_hi, -1.0, last))
        settled = (lo_c == n_sel) | (float_to_key(lo_v) + 1 >= float_to_key(hi_v))
        done = jnp.where(settled, 1.0, done)
        unresolved = jnp.min(done) < 0.5
        return it + 1, lo_v, lo_c, f_lo, hi_v, hi_c, f_hi, last, done, unresolved

    done0 = jnp.where(done0, 1.0, 0.0)
    init = (jnp.int32(0), lo_v0, lo_c0, jnp.log(lo_c0) - log_target,
            hi_v0, hi_c0, jnp.log(jnp.maximum(hi_c0, 0.5)) - log_target,
            zeros, done0, jnp.min(done0) < 0.5)
    final = lax.while_loop(search_cond, search_body, init)
    thr, lo_c, hi_c = final[1], final[2], final[5]
    thr_t = jnp.concatenate([to_rows(thr)] * reps, axis=1)
    tied = lo_c != n_sel
    any_tied = jnp.max(jnp.where(tied, 1.0, 0.0)) > 0.0
    quota = to_rows(jnp.where(tied, n_sel - hi_c, 2.0 * tk * (n_tiles + 1).astype(f32)))
    quota_t = jnp.concatenate([quota] * reps, axis=1)

    m_ref[...] = jnp.full(m_ref.shape, NEG_BIG, f32)
    acc_ref[...] = jnp.zeros(acc_ref.shape, f32)
    lane_k = lax.broadcasted_iota(i32, (tk, LANES), 1)
    kmask_l = (lane_k < HEAD_DIM).astype(f32).astype(bf16)
    kmask_r = (lane_k >= HEAD_DIM).astype(f32).astype(bf16)

    def attn_body(j, n_ties, with_ties):
        start = pl.multiple_of(j * tk, tk)
        sc = keys_ref[j]
        if with_ties:
            eq = jnp.where(sc == thr_t, 1.0, 0.0)
            before = (jnp.dot(eq.astype(bf16), triu_ref[...], preferred_element_type=f32)
                      + jnp.concatenate([n_ties] * reps, axis=1))
            sel = (sc > thr_t) | ((sc == thr_t) & (before < quota_t))
            n_ties = n_ties + jnp.broadcast_to(jnp.sum(eq, axis=1, keepdims=True), (qb, LANES))
        else:
            sel = sc >= thr_t
        sel = sel & (col + j * tk <= row)
        bias = jnp.where(sel, 0.0, NEG_BIG)
        bias = jnp.concatenate([bias] * n_pairs_a, axis=0)
        kk = ka_ref[0, pl.ds(start, tk), :]
        vv = va_ref[0, pl.ds(start, tk), :]
        v_ext = (vv * kmask_l + kmask_r, vv * kmask_r + kmask_l)
        for g in range(N_KV_A):
            s = _nt_dot(qas_ref[g], kk) + bias
            m_old = m_ref[g]
            m_new = jnp.maximum(m_old, jnp.max(s, axis=1, keepdims=True))
            alpha = jnp.exp(m_old - m_new)
            p = jnp.exp(s - jnp.concatenate([m_new] * reps, axis=1))
            acc_ref[g] = alpha * acc_ref[g] + jnp.dot(p.astype(bf16), v_ext[g], preferred_element_type=f32)
            m_ref[g] = m_new
        return n_ties

    no_ties = jnp.zeros((qb, LANES), f32)

    @pl.when(any_tied)
    def _():
        lax.fori_loop(0, n_tiles, functools.partial(attn_body, with_ties=True), no_ties)

    @pl.when(jnp.logical_not(any_tied))
    def _():
        lax.fori_loop(0, n_tiles, functools.partial(attn_body, with_ties=False), no_ties)

    for c in range(n_pairs_a):
        rows = slice(c * qb, (c + 1) * qb)
        a0 = acc_ref[0, rows, :]
        a1 = acc_ref[1, rows, :]
        o0 = a0 / pltpu.roll(a0, HEAD_DIM, axis=1)
        o1 = a1 / pltpu.roll(a1, HEAD_DIM, axis=1)
        o_ref[0, :, LANES * c:LANES * (c + 1)] = jnp.where(left, o0, o1).astype(bf16)


def _sparse_attention(qa, ka, va, qi, ki, wi):
    bsz, s, wa = qa.shape
    qb = min(256, s)
    tk = min(512, s)
    n_sel = min(TOPK_MAX, s // 4)
    w_scale = N_IDX_HEADS ** -0.5 * IDX_DIM ** -0.5
    n_pairs_a = N_HEADS_A // 2
    tile = lambda b, i: (b, i, 0)
    per_b = lambda b, i: (b, 0, 0)
    kern = functools.partial(_sparse_attn_kernel, qb=qb, tk=tk, n_sel=float(n_sel), w_scale=w_scale)
    return pl.pallas_call(
        kern,
        grid=(bsz, s // qb),
        in_specs=[pl.BlockSpec((1, qb, wa), tile),
                  pl.BlockSpec((1, s, LANES), per_b),
                  pl.BlockSpec((1, s, LANES), per_b),
                  pl.BlockSpec((1, qb, qi.shape[2]), tile),
                  pl.BlockSpec((1, s, LANES), per_b),
                  pl.BlockSpec((1, qb, LANES), tile),
                  pl.BlockSpec((tk, tk), lambda b, i: (0, 0))],
        out_specs=pl.BlockSpec((1, qb, wa), tile),
        out_shape=jax.ShapeDtypeStruct((bsz, s, wa), bf16),
        scratch_shapes=[pltpu.VMEM((s // tk, qb, tk), f32),
                        pltpu.VMEM((N_IDX_HEADS * qb, LANES), bf16),
                        pltpu.VMEM((N_KV_A, n_pairs_a * qb, LANES), bf16),
                        pltpu.VMEM((N_IDX_HEADS, qb, LANES), f32),
                        pltpu.VMEM((N_KV_A, n_pairs_a * qb, LANES), f32),
                        pltpu.VMEM((N_KV_A, n_pairs_a * qb, LANES), f32)],
        compiler_params=pltpu.CompilerParams(
            dimension_semantics=("arbitrary", "arbitrary"), vmem_limit_bytes=VMEM_LIMIT),
        name="sparse_attn",
    )(qa, ka, va, qi, ki, wi, jnp.asarray(np.triu(np.ones((tk, tk), np.float32), k=1), bf16))


def _dilated_kernel(q_ref, k_ref, v_ref, bias_ref, o_ref, qs_ref, m_ref, acc_ref, *, tq, nd):
    i = pl.program_id(2)
    lane = lax.broadcasted_iota(i32, (tq, LANES), 1)
    left = lane < HEAD_DIM
    mask_l = left.astype(f32).astype(bf16)
    mask_r = (1.0 - left.astype(f32)).astype(bf16)
    q = q_ref[0] * jnp.asarray(HEAD_DIM ** -0.5, bf16)
    qs_ref[0:tq, :] = q * mask_l
    qs_ref[tq:2 * tq, :] = q * mask_r
    m_ref[...] = jnp.full(m_ref.shape, NEG_BIG, f32)
    acc_ref[...] = jnp.zeros(acc_ref.shape, f32)
    ones = jnp.ones((tq, LANES), bf16)

    def body(j, carry):
        start = pl.multiple_of(j * tq, tq)
        kk = k_ref[0, pl.ds(start, tq), :]
        v_ext = jnp.concatenate([v_ref[0, pl.ds(start, tq), :], ones], axis=1)
        b = bias_ref[i - j]
        s = _nt_dot(qs_ref[...], kk) + jnp.concatenate([b, b], axis=0)
        m_old = m_ref[...]
        m_new = jnp.maximum(m_old, jnp.max(s, axis=1, keepdims=True))
        alpha = jnp.exp(m_old - m_new)
        p = jnp.exp(s - jnp.concatenate([m_new] * (tq // LANES), axis=1))
        pv = jnp.dot(p.astype(bf16), v_ext, preferred_element_type=f32)
        acc_ref[...] = jnp.concatenate([alpha, alpha], axis=1) * acc_ref[...] + pv
        m_ref[...] = m_new
        return carry

    lax.fori_loop(jnp.maximum(i - (nd - 1), 0), i + 1, body, 0)
    o0 = acc_ref[0:tq, 0:LANES] / acc_ref[0:tq, LANES:2 * LANES]
    o1 = acc_ref[tq:2 * tq, 0:LANES] / acc_ref[tq:2 * tq, LANES:2 * LANES]
    o_ref[0] = jnp.where(left, o0, o1).astype(bf16)


def _dilated_attention(qb_, kb_, vb_):
    bsz, s, wb = qb_.shape
    tq = min(512, s)
    bias_np, nd = _dilated_bias(tq)
    bias = jnp.asarray(bias_np)
    n_pairs = wb // LANES
    kern = functools.partial(_dilated_kernel, tq=tq, nd=nd)
    return pl.pallas_call(
        kern,
        grid=(bsz, n_pairs, s // tq),
        in_specs=[pl.BlockSpec((1, tq, LANES), lambda b, p, i: (b, i, p)),
                  pl.BlockSpec((1, s, LANES), lambda b, p, i: (b, 0, p)),
                  pl.BlockSpec((1, s, LANES), lambda b, p, i: (b, 0, p)),
                  pl.BlockSpec((nd, tq, tq), lambda b, p, i: (0, 0, 0))],
        out_specs=pl.BlockSpec((1, tq, LANES), lambda b, p, i: (b, i, p)),
        out_shape=jax.ShapeDtypeStruct((bsz, s, wb), bf16),
        scratch_shapes=[pltpu.VMEM((2 * tq, LANES), bf16),
                        pltpu.VMEM((2 * tq, LANES), f32),
                        pltpu.VMEM((2 * tq, 2 * LANES), f32)],
        compiler_params=pltpu.CompilerParams(
            dimension_semantics=("arbitrary", "arbitrary", "arbitrary"), vmem_limit_bytes=VMEM_LIMIT),
        name="dilated_attn",
    )(qb_, kb_, vb_, bias)


def _mid_kernel(oa_ref, ob_ref, x_ref, gta_ref, scf_ref, shf_ref, gtf_ref, woa_ref, wob_ref, gffn_ref,
                wsgu_ref, wsd_ref, wrt_ref, rbias_ref, triu_ref,
                base_ref, h2_ref, sel_ref, gate_ref, rank_ref, cnt_ref, carry_ref, *, n_exp, t, d_sh):
    step = pl.program_id(0)

    @pl.when(step == 0)
    def _():
        carry_ref[...] = jnp.zeros(carry_ref.shape, f32)

    mix = (jnp.dot(oa_ref[...], woa_ref[...], preferred_element_type=f32)
           + jnp.dot(ob_ref[...], wob_ref[...], preferred_element_type=f32))
    x1 = x_ref[...] + gta_ref[0] * mix
    ms = jnp.mean(x1 * x1, axis=-1, keepdims=True)
    h2 = ((x1 * lax.rsqrt(ms + EPS)) * gffn_ref[...]) * (1.0 + scf_ref[0]) + shf_ref[0]
    h2b = h2.astype(bf16)
    h2f = h2b.astype(f32)
    n_sub = h2_ref.shape[0] // t
    for a in range(n_sub):
        h2_ref[pl.ds(a, t, stride=n_sub), :] = h2f[:, LANES * a:LANES * (a + 1)]

    gu = jnp.dot(h2b, wsgu_ref[...], preferred_element_type=f32)
    g = gu[:, :d_sh]
    u = gu[:, d_sh:]
    act = (g * jax.nn.sigmoid(g)) * u
    shared = jnp.dot(act.astype(bf16), wsd_ref[...], preferred_element_type=f32)
    base = x1 + gtf_ref[0] * shared
    for a in range(n_sub):
        base_ref[pl.ds(a, t, stride=n_sub), :] = base[:, LANES * a:LANES * (a + 1)]

    scores = jax.nn.sigmoid(_nt_dot(wrt_ref[...], h2b))
    biased = scores + rbias_ref[...]
    per = n_exp // N_GROUPS
    neg_inf = jnp.float32(-jnp.inf)
    ri_g = lax.broadcasted_iota(i32, (per, t), 0).astype(f32)
    gs = []
    for grp in range(N_GROUPS):
        blk = biased[grp * per:(grp + 1) * per]
        m1 = jnp.max(blk, axis=0, keepdims=True)
        idx1 = jnp.min(jnp.where(blk == m1, ri_g, float(per)), axis=0, keepdims=True)
        m2 = jnp.max(jnp.where(ri_g == idx1, neg_inf, blk), axis=0, keepdims=True)
        gs.append(m1 + m2)
    masked_rows = []
    for grp in range(N_GROUPS):
        beaten = jnp.zeros((1, t), f32)
        for g2 in range(N_GROUPS):
            if g2 == grp:
                continue
            wins = gs[g2] > gs[grp]
            if g2 < grp:
                wins = wins | (gs[g2] == gs[grp])
            beaten = beaten + jnp.where(wins, 1.0, 0.0)
        keep = jnp.broadcast_to(beaten < TOPK_GROUPS, (per, t))
        masked_rows.append(jnp.where(keep, biased[grp * per:(grp + 1) * per], neg_inf))
    masked = jnp.concatenate(masked_rows, axis=0)

    ri = lax.broadcasted_iota(i32, (n_exp, t), 0).astype(f32)
    selmask = jnp.zeros((n_exp, t), f32)
    idxs, graw = [], []
    for _ in range(TOP_K):
        m = jnp.max(masked, axis=0, keepdims=True)
        idx = jnp.min(jnp.where(masked == m, ri, float(n_exp)), axis=0, keepdims=True)
        onehot = ri == idx
        graw.append(jnp.sum(jnp.where(onehot, scores, 0.0), axis=0, keepdims=True))
        masked = jnp.where(onehot, neg_inf, masked)
        selmask = jnp.where(onehot, 1.0, selmask)
        idxs.append(idx)
    den = graw[0]
    for k in range(1, TOP_K):
        den = den + graw[k]

    prefix = jnp.dot(selmask.astype(bf16), triu_ref[...], preferred_element_type=f32)
    prefix = prefix + jnp.concatenate([carry_ref[...]] * (t // LANES), axis=1)
    for k in range(TOP_K):
        rank_k = jnp.sum(jnp.where(ri == idxs[k], prefix, 0.0), axis=0, keepdims=True)
        sel_ref[k:k + 1, :] = idxs[k].astype(i32)
        rank_ref[k:k + 1, :] = rank_k.astype(i32)
        gate_ref[k:k + 1, :] = graw[k] / den * ROUTED_SCALE
    carry_ref[...] = carry_ref[...] + jnp.broadcast_to(
        jnp.sum(selmask, axis=1, keepdims=True), carry_ref.shape)
    cnt_ref[...] = carry_ref[...]


def _mid(oa, ob, x2, gta, scf, shf, gtf, woa, wob, gffn, wsgu, wsd, wrt, rbias_rep, s):
    n, d = x2.shape
    t = rbias_rep.shape[1]
    n_exp = wrt.shape[0]
    d_sh = wsd.shape[0]
    triu = jnp.asarray(np.triu(np.ones((t, t), np.float32), k=1), bf16)
    tile = lambda i: (i, 0)
    const = lambda i: (0, 0)
    per_b = lambda i: ((i * t) // s, 0, 0)
    lane_tile = lambda i: (0, i)
    n_sub = d // LANES
    tok_tiles = jax.ShapeDtypeStruct((n * n_sub, LANES), f32)
    tok_spec = pl.BlockSpec((t * n_sub, LANES), tile)
    out_shape = [tok_tiles, tok_tiles,
                 jax.ShapeDtypeStruct((TOP_K, n), i32), jax.ShapeDtypeStruct((TOP_K, n), f32),
                 jax.ShapeDtypeStruct((TOP_K, n), i32), jax.ShapeDtypeStruct((n_exp, LANES), f32)]
    out_specs = [tok_spec, tok_spec,
                 pl.BlockSpec((TOP_K, t), lane_tile), pl.BlockSpec((TOP_K, t), lane_tile),
                 pl.BlockSpec((TOP_K, t), lane_tile), pl.BlockSpec((n_exp, LANES), const)]
    kern = functools.partial(_mid_kernel, n_exp=n_exp, t=t, d_sh=d_sh)
    return pl.pallas_call(
        kern,
        grid=(n // t,),
        in_specs=[pl.BlockSpec((t, oa.shape[1]), tile),
                  pl.BlockSpec((t, ob.shape[1]), tile),
                  pl.BlockSpec((t, d), tile),
                  pl.BlockSpec((1, 1, d), per_b),
                  pl.BlockSpec((1, 1, d), per_b),
                  pl.BlockSpec((1, 1, d), per_b),
                  pl.BlockSpec((1, 1, d), per_b),
                  pl.BlockSpec(woa.shape, const),
                  pl.BlockSpec(wob.shape, const),
                  pl.BlockSpec((1, d), const),
                  pl.BlockSpec(wsgu.shape, const),
                  pl.BlockSpec(wsd.shape, const),
                  pl.BlockSpec(wrt.shape, const),
                  pl.BlockSpec(rbias_rep.shape, const),
                  pl.BlockSpec((t, t), const)],
        out_specs=out_specs,
        out_shape=out_shape,
        scratch_shapes=[pltpu.VMEM((n_exp, LANES), f32)],
        compiler_params=pltpu.CompilerParams(
            dimension_semantics=("arbitrary",), vmem_limit_bytes=VMEM_LIMIT),
        name="mid",
    )(oa, ob, x2, gta, scf, shf, gtf, woa, wob, gffn, wsgu, wsd, wrt, rbias_rep, triu)


def _dispatch_kernel(zb_ref, slot_ref, h_ref, xs_ref, zero_ref, sem_ref, *, td, n_zero):
    step = pl.program_id(0)

    def zero_copy(b):
        return pltpu.make_async_copy(zero_ref, xs_ref.at[pl.ds(zb_ref[b], MOE_BLOCK)], sem_ref.at[0])

    @pl.when(step == 0)
    def _():
        zero_ref[...] = jnp.zeros(zero_ref.shape, f32)

        def start(b, c):
            @pl.when(zb_ref[b] >= 0)
            def _():
                zero_copy(b).start()
            return c

        def wait(b, c):
            @pl.when(zb_ref[b] >= 0)
            def _():
                zero_copy(b).wait()
            return c

        lax.fori_loop(0, n_zero, start, 0)
        lax.fori_loop(0, n_zero, wait, 0)

    def start_rows(tok, c):
        for k in range(TOP_K):
            pltpu.make_async_copy(h_ref.at[pl.ds(tok, 1)], xs_ref.at[pl.ds(slot_ref[k, tok], 1)],
                                  sem_ref.at[1]).start(priority=k % 2)
        return c

    lax.fori_loop(0, td, start_rows, 0)
    for k in range(TOP_K):
        pltpu.make_async_copy(h_ref, xs_ref.at[pl.ds(0, td)], sem_ref.at[1]).wait()


def _dispatch(zero_start, slot, h2, n_slots):
    n, rows, lanes = h2.shape
    td = min(256, n)
    kern = functools.partial(_dispatch_kernel, td=td, n_zero=zero_start.shape[0])
    grid_spec = pltpu.PrefetchScalarGridSpec(
        num_scalar_prefetch=1,
        grid=(n // td,),
        in_specs=[pl.BlockSpec((TOP_K, td), lambda i, tail: (0, i), memory_space=pltpu.SMEM),
                  pl.BlockSpec((td, rows, lanes), lambda i, tail: (i, 0, 0))],
        out_specs=pl.BlockSpec(memory_space=pl.ANY),
        scratch_shapes=[pltpu.VMEM((MOE_BLOCK, rows, lanes), f32), pltpu.SemaphoreType.DMA((2,))],
    )
    return pl.pallas_call(
        kern,
        grid_spec=grid_spec,
        out_shape=jax.ShapeDtypeStruct((n_slots, rows, lanes), f32),
        compiler_params=pltpu.CompilerParams(
            dimension_semantics=("arbitrary",), vmem_limit_bytes=VMEM_LIMIT),
        name="dispatch",
    )(zero_start, slot, h2)


def _expert_kernel(be_ref, nu_ref, xs_ref, wg_ref, wu_ref, wd_ref, ys_ref, wgu_s, wd_s, *, n_sub):
    i = pl.program_id(0)
    f = wd_s.shape[0]
    rows = xs_ref.shape[0] // n_sub

    @pl.when(i >= nu_ref[0])
    def _():
        ys_ref[...] = jnp.zeros(ys_ref.shape, f32)

    @pl.when(i < nu_ref[0])
    def _():
        @pl.when((i == 0) | (be_ref[i] != be_ref[jnp.maximum(i - 1, 0)]))
        def _():
            wgu_s[:, 0:f] = wg_ref[0].astype(bf16)
            wgu_s[:, f:2 * f] = wu_ref[0].astype(bf16)
            wd_s[...] = wd_ref[0].astype(bf16)

        xb = jnp.concatenate([xs_ref[pl.ds(a, rows, stride=n_sub), :] for a in range(n_sub)],
                             axis=1).astype(bf16)
        gu = jnp.dot(xb, wgu_s[...], preferred_element_type=f32)
        g = gu[:, 0:f]
        u = gu[:, f:2 * f]
        act = (g * jax.nn.sigmoid(g)) * u
        y = jnp.dot(act.astype(bf16), wd_s[...], preferred_element_type=f32)
        for a in range(n_sub):
            ys_ref[pl.ds(a, rows, stride=n_sub), :] = y[:, LANES * a:LANES * (a + 1)]


def _experts(block_expert, n_used, xs, w_gate, w_up, w_down):
    n_slots, n_sub, lanes = xs.shape
    d = n_sub * lanes
    n_blocks = n_slots // MOE_BLOCK
    f = w_gate.shape[2]
    blk = lambda i, be, nu: (jnp.minimum(i, nu[0] - 1), 0)
    wsel = lambda i, be, nu: (be[jnp.minimum(i, nu[0] - 1)], 0, 0)
    grid_spec = pltpu.PrefetchScalarGridSpec(
        num_scalar_prefetch=2,
        grid=(n_blocks,),
        in_specs=[pl.BlockSpec((MOE_BLOCK * n_sub, lanes), blk),
                  pl.BlockSpec((1, d, f), wsel),
                  pl.BlockSpec((1, d, f), wsel),
                  pl.BlockSpec((1, f, d), wsel)],
        out_specs=pl.BlockSpec((MOE_BLOCK * n_sub, lanes), lambda i, be, nu: (i, 0)),
        scratch_shapes=[pltpu.VMEM((d, 2 * f), bf16), pltpu.VMEM((f, d), bf16)],
    )
    ys = pl.pallas_call(
        functools.partial(_expert_kernel, n_sub=n_sub),
        grid_spec=grid_spec,
        out_shape=jax.ShapeDtypeStruct((n_slots * n_sub, lanes), f32),
        compiler_params=pltpu.CompilerParams(
            dimension_semantics=("arbitrary",), vmem_limit_bytes=VMEM_LIMIT),
        name="experts",
    )(block_expert, n_used, xs.reshape(n_slots * n_sub, lanes), w_gate, w_up, w_down)
    return ys.reshape(n_slots, n_sub, lanes)


def _combine_kernel(slot_ref, nslot_ref, gate_ref, base_ref, gtf_ref, ys_ref, o_ref, buf_ref, sem_ref, *, tc):
    step = pl.program_id(0)
    cur = step % 2

    def start_gathers(idx_ref, b):
        def body(tok, c):
            for k in range(TOP_K):
                pltpu.make_async_copy(ys_ref.at[pl.ds(idx_ref[k, tok], 1)],
                                      buf_ref.at[b, k, pl.ds(tok, 1)], sem_ref.at[b]).start(priority=k % 2)
            return c
        lax.fori_loop(0, tc, body, 0)

    @pl.when(step == 0)
    def _():
        start_gathers(slot_ref, 0)

    @pl.when(step + 1 < pl.num_programs(0))
    def _():
        start_gathers(nslot_ref, 1 - cur)

    for k in range(TOP_K):
        pltpu.make_async_copy(ys_ref.at[pl.ds(0, tc)], buf_ref.at[cur, k], sem_ref.at[cur]).wait()

    gtf = gtf_ref[0]

    def reduce_token(tok, c):
        routed = buf_ref[cur, 0, tok] * gate_ref[0, tok]
        for k in range(1, TOP_K):
            routed = routed + buf_ref[cur, k, tok] * gate_ref[k, tok]
        o_ref[tok] = base_ref[tok] + gtf * routed
        return c

    lax.fori_loop(0, tc, reduce_token, 0, unroll=4)


def _combine(slot, gate, base, gtf, ys, s):
    n, rows, lanes = base.shape
    tc = min(128, n)
    n_steps = n // tc
    kern = functools.partial(_combine_kernel, tc=tc)
    tok_tile = lambda i: (i, 0, 0)
    return pl.pallas_call(
        kern,
        grid=(n_steps,),
        in_specs=[pl.BlockSpec((TOP_K, tc), lambda i: (0, i), memory_space=pltpu.SMEM),
                  pl.BlockSpec((TOP_K, tc), lambda i: (0, jnp.minimum(i + 1, n_steps - 1)),
                               memory_space=pltpu.SMEM),
                  pl.BlockSpec((TOP_K, tc), lambda i: (0, i), memory_space=pltpu.SMEM),
                  pl.BlockSpec((tc, rows, lanes), tok_tile),
                  pl.BlockSpec((1, rows, lanes), lambda i: ((i * tc) // s, 0, 0)),
                  pl.BlockSpec(memory_space=pl.ANY)],
        out_specs=pl.BlockSpec((tc, rows, lanes), tok_tile),
        out_shape=jax.ShapeDtypeStruct((n, rows, lanes), f32),
        scratch_shapes=[pltpu.VMEM((2, TOP_K, tc, rows, lanes), f32), pltpu.SemaphoreType.DMA((2,))],
        compiler_params=pltpu.CompilerParams(
            dimension_semantics=("arbitrary",), vmem_limit_bytes=VMEM_LIMIT),
        name="combine",
    )(slot, slot, gate, base, gtf, ys)


def _layer(x, c, posf, w_ada, b_ada, g_mix, w_in, q_norm_a, k_norm_a, q_norm_b, k_norm_b, w_out, g_ffn,
           w_router, router_bias, w_gate, w_up, w_down, ws_gate, ws_up, ws_down):
    bsz, s, d = x.shape
    n = bsz * s
    n_exp = w_router.shape[1]

    mod = _adaln(c, w_ada, b_ada)[:, None, :]
    sh_a, sc_a, gt_a, sh_f, sc_f, gt_f = jnp.split(mod, 6, axis=-1)

    w_perm = _take_runs(w_in.astype(bf16), _projection_columns(), axis=1)
    gains = {"qa": q_norm_a, "ka": k_norm_a, "qb": q_norm_b, "kb": k_norm_b}
    gain_a = jnp.concatenate([gains[kind][:HALF] for kind, _ in _SLOTS[:N_NORM_SLOTS]])[None, :].astype(f32)
    gain_b = jnp.concatenate([gains[kind][HALF:] for kind, _ in _SLOTS[:N_NORM_SLOTS]])[None, :].astype(f32)

    qa, ka, va, qi, ki, wi, qb_, kb_, vb_ = _project(x, posf, sc_a, sh_a, g_mix[None, :], w_perm, gain_a, gain_b)
    o_a = _sparse_attention(qa, ka, va, qi, ki, wi)
    o_b = _dilated_attention(qb_, kb_, vb_)

    wa = N_HEADS_A * HEAD_DIM
    rows_a = np.concatenate([np.arange(h * HEAD_DIM, (h + 1) * HEAD_DIM) for h in QA_PAIR_ORDER])
    woa = _take_runs(w_out, rows_a, axis=0).astype(bf16)
    wob = w_out[wa:].astype(bf16)
    wsgu = jnp.concatenate([ws_gate, ws_up], axis=1).astype(bf16)
    wsd = ws_down.astype(bf16)
    wrt = w_router.T.astype(bf16)
    t_mid = min(256, n)
    rbias_rep = jnp.broadcast_to(router_bias.astype(f32)[:, None], (n_exp, t_mid))

    base, h2, sel, gate, rank, cnt = _mid(
        o_a.reshape(n, wa), o_b.reshape(n, -1), x.reshape(n, d), gt_a, sc_f, sh_f, gt_f,
        woa, wob, g_ffn[None, :], wsgu, wsd, wrt, rbias_rep, s)
    base = base.reshape(n, d // LANES, LANES)
    h2 = h2.reshape(n, d // LANES, LANES)

    counts = cnt[:, 0].astype(i32)
    padded = (counts + MOE_BLOCK - 1) // MOE_BLOCK * MOE_BLOCK
    pad_end = jnp.cumsum(padded)
    pad_start = pad_end - padded
    onehot = sel[:, :, None] == jnp.arange(n_exp, dtype=i32)[None, None, :]
    slot = jnp.sum(jnp.where(onehot, pad_start[None, None, :], 0), axis=-1) + rank
    n_blocks = -(-(n * TOP_K) // MOE_BLOCK) + n_exp
    n_slots = n_blocks * MOE_BLOCK
    block_start = jnp.arange(n_blocks, dtype=i32) * MOE_BLOCK
    block_expert = jnp.sum((pad_end[None, :] <= block_start[:, None]).astype(i32), axis=1)
    block_expert = jnp.minimum(block_expert, n_exp - 1)
    n_used = (pad_end[-1] // MOE_BLOCK).astype(i32)[None]
    is_tail = jnp.any((block_start[:, None] == (pad_end - MOE_BLOCK)[None, :]) & (padded[None, :] > 0), axis=1)
    zero_start = jnp.where(is_tail | (block_start >= pad_end[-1]), block_start, -1).astype(i32)

    xs = _dispatch(zero_start, slot, h2, n_slots)
    ys = _experts(block_expert, n_used, xs, w_gate, w_up, w_down)
    out = _combine(slot, gate, base, gt_f.reshape(bsz, d // LANES, LANES), ys, s)
    return out.reshape(bsz, s, d)


def kernel(x, c, positions, w_ada, b_ada, g_mix, w_in, q_norm_a, k_norm_a, q_norm_b, k_norm_b, w_out, g_ffn,
           w_router, router_bias, w_gate, w_up, w_down, ws_gate, ws_up, ws_down):
    posf = positions.astype(f32)[..., None]
    for l in range(w_ada.shape[0]):
        x = _layer(x, c, posf, w_ada[l], b_ada[l], g_mix[l], w_in[l], q_norm_a[l], k_norm_a[l], q_norm_b[l],
                   k_norm_b[l], w_out[l], g_ffn[l], w_router[l], router_bias[l], w_gate[l], w_up[l], w_down[l],
                   ws_gate[l], ws_up[l], ws_down[l])
    return x
```

```python
import functools
import math

import numpy as np
import jax
import jax.numpy as jnp
from jax import lax
from jax.experimental import pallas as pl
from jax.experimental.pallas import tpu as pltpu

f32 = jnp.float32
bf16 = jnp.bfloat16
i32 = jnp.int32

HEAD_DIM = 64
HALF = HEAD_DIM // 2
N_HEADS_A = 10
N_KV_A = 2
N_HEADS_B = 6
N_IDX_HEADS = 8
IDX_DIM = 64
TOPK_MAX = 256
DILATED_PATTERNS = ((128, 1), (512, 4), (2048, 16))
ROPE_THETA = 10000.0
EPS = 1e-6
TOP_K = 8
N_GROUPS = 8
TOPK_GROUPS = 4
ROUTED_SCALE = 2.5
MOE_BLOCK = 512
ZERO_ROWS = 64

LANES = 128
VMEM_LIMIT = 48 * 1024 * 1024

NEG_BIG = -1e30
KEY_NEG_INF = int(np.int32(np.uint32(0xFF800000) ^ np.uint32(0x7FFFFFFF)))
KEY_POS_INF = 0x7F800000
SEARCH_INTERP_STEPS = 16
SEARCH_MAX_STEPS = SEARCH_INTERP_STEPS + 34

_OFF_QA = 0
_OFF_KA = _OFF_QA + N_HEADS_A * HEAD_DIM
_OFF_VA = _OFF_KA + N_KV_A * HEAD_DIM
_OFF_QI = _OFF_VA + N_KV_A * HEAD_DIM
_OFF_KI = _OFF_QI + N_IDX_HEADS * IDX_DIM
_OFF_WI = _OFF_KI + IDX_DIM
_OFF_QB = _OFF_WI + N_IDX_HEADS
_OFF_KB = _OFF_QB + N_HEADS_B * HEAD_DIM
_OFF_VB = _OFF_KB + N_HEADS_B * HEAD_DIM
D_IN = _OFF_VB + N_HEADS_B * HEAD_DIM

QA_PAIR_ORDER = (0, 5, 1, 6, 2, 7, 3, 8, 4, 9)

_SLOTS = (
    [("qa", h) for h in QA_PAIR_ORDER[:8]] + [("qa", 4), ("qa", 9), ("ka", 0), ("ka", 1)]
    + [("qb", h) for h in range(6)] + [("kb", h) for h in range(6)]
    + [("qi", h) for h in range(8)] + [("ki", 0), ("ki", 0), ("pad", 0), ("pad", 0)]
)
N_NORM_SLOTS = 24
N_CHUNKS = len(_SLOTS) // 4
SLAB = N_CHUNKS * LANES
_COL_VA = 2 * SLAB
_COL_VB = _COL_VA + N_KV_A * HEAD_DIM
_COL_WI = _COL_VB + N_HEADS_B * HEAD_DIM
N_COL = _COL_WI + LANES


def _slot_offset(kind, h):
    base = {"qa": _OFF_QA, "ka": _OFF_KA, "qb": _OFF_QB, "kb": _OFF_KB, "qi": _OFF_QI, "ki": _OFF_KI}
    return base[kind] + h * HEAD_DIM


def _projection_columns():
    zero_col = D_IN
    cols_a, cols_b = [], []
    for kind, h in _SLOTS:
        if kind == "pad":
            cols_a += [zero_col] * HALF
            cols_b += [zero_col] * HALF
        else:
            off = _slot_offset(kind, h)
            cols_a += list(range(off, off + HALF))
            cols_b += list(range(off + HALF, off + HEAD_DIM))
    cols = cols_a + cols_b
    cols += list(range(_OFF_VA, _OFF_VA + N_KV_A * HEAD_DIM))
    cols += list(range(_OFF_VB, _OFF_VB + N_HEADS_B * HEAD_DIM))
    cols += list(range(_OFF_WI, _OFF_WI + N_IDX_HEADS)) + [zero_col] * (LANES - N_IDX_HEADS)
    assert len(cols) == N_COL
    return np.asarray(cols, np.int32)


def _take_runs(w, idx, axis):
    size = w.shape[axis]
    pieces, a = [], 0
    idx = [int(v) for v in idx]
    while a < len(idx):
        b = a + 1
        if idx[a] == size:
            while b < len(idx) and idx[b] == size:
                b += 1
            shape = list(w.shape)
            shape[axis] = b - a
            pieces.append(jnp.zeros(shape, w.dtype))
        else:
            while b < len(idx) and idx[b] == idx[b - 1] + 1:
                b += 1
            pieces.append(lax.slice_in_dim(w, idx[a], idx[b - 1] + 1, axis=axis))
        a = b
    return jnp.concatenate(pieces, axis=axis)


def _interleave_matrix():
    p = np.zeros((2 * LANES, 2 * LANES), np.float32)
    for head in range(4):
        for i in range(HALF):
            p[HALF * head + i, HEAD_DIM * head + i] = 1.0
            p[LANES + HALF * head + i, HEAD_DIM * head + HALF + i] = 1.0
    return p


def _group_sum_matrix():
    g = np.zeros((LANES, LANES), np.float32)
    for k in range(LANES // HALF):
        g[HALF * k:HALF * (k + 1), HALF * k:HALF * (k + 1)] = 1.0
    return g


def _dilated_bias(tq):
    max_win = max(w for w, _ in DILATED_PATTERNS)
    nd = max_win // tq + 1
    d = np.arange(nd)[:, None, None] * tq + np.arange(tq)[None, :, None] - np.arange(tq)[None, None, :]
    mult = np.zeros(d.shape, np.float64)
    for win, dil in DILATED_PATTERNS:
        mult += ((d >= 0) & (d <= win) & (d % dil == 0)).astype(np.float64)
    with np.errstate(divide="ignore"):
        bias = np.where(mult > 0, np.log(np.maximum(mult, 1.0)), NEG_BIG)
    return bias.astype(np.float32), nd


def _nt_dot(a, b):
    return lax.dot_general(a, b, (((1,), (1,)), ((), ())), preferred_element_type=f32)


def _adaln_kernel(c_ref, w_ref, b_ref, o_ref):
    c = c_ref[...]
    a = c * jax.nn.sigmoid(c)
    o_ref[...] = jnp.dot(a, w_ref[...], preferred_element_type=f32) + b_ref[...]


def _adaln(c, w_ada, b_ada):
    bsz, d = c.shape
    n = w_ada.shape[1]
    rows = -(-bsz // 8) * 8
    c_pad = jnp.zeros((rows, d), f32).at[:bsz].set(c)
    tn = 512
    out = pl.pallas_call(
        _adaln_kernel,
        grid=(n // tn,),
        in_specs=[pl.BlockSpec((rows, d), lambda j: (0, 0)),
                  pl.BlockSpec((d, tn), lambda j: (0, j)),
                  pl.BlockSpec((1, tn), lambda j: (0, j))],
        out_specs=pl.BlockSpec((rows, tn), lambda j: (0, j)),
        out_shape=jax.ShapeDtypeStruct((rows, n), f32),
        name="adaln",
    )(c_pad, w_ada, b_ada.reshape(1, n))
    return out[:bsz]


def _proj_kernel(x_ref, pos_ref, sc_ref, sh_ref, g_ref, w_ref, ga_ref, gb_ref, gsum_ref, perm_ref, invf_ref,
                 qa_ref, ka_ref, va_ref, qi_ref, ki_ref, wi_ref, qb_ref, kb_ref, vb_ref):
    x = x_ref[0]
    ms = jnp.mean(x * x, axis=-1, keepdims=True)
    h = (x * lax.rsqrt(ms + EPS)) * g_ref[...]
    h = h * (1.0 + sc_ref[0]) + sh_ref[0]
    proj = jnp.dot(h.astype(bf16), w_ref[...], preferred_element_type=f32)

    ang = pos_ref[0] * invf_ref[...]
    cos = jnp.cos(ang)
    sin = jnp.sin(ang)
    gsum = gsum_ref[...]
    perm = perm_ref[...]
    heads = []
    for c in range(N_CHUNKS):
        a = proj[:, LANES * c:LANES * (c + 1)]
        b = proj[:, SLAB + LANES * c:SLAB + LANES * (c + 1)]
        if 4 * c < N_NORM_SLOTS:
            ss = a * a + b * b
            hi = ss.astype(bf16)
            lo = (ss - hi.astype(f32)).astype(bf16)
            tot = (jnp.dot(hi, gsum, preferred_element_type=f32)
                   + jnp.dot(lo, gsum, preferred_element_type=f32))
            inv = lax.rsqrt(tot * (1.0 / HEAD_DIM) + EPS)
            a = a * inv * ga_ref[:, LANES * c:LANES * (c + 1)]
            b = b * inv * gb_ref[:, LANES * c:LANES * (c + 1)]
        ra = a * cos - b * sin
        rb = b * cos + a * sin
        ab = jnp.concatenate([ra, rb], axis=1).astype(bf16)
        heads.append(jnp.dot(ab, perm, preferred_element_type=f32).astype(bf16))

    qa_ref[0, :, 0:256] = heads[0]
    qa_ref[0, :, 256:512] = heads[1]
    qa_ref[0, :, 512:640] = heads[2][:, 0:128]
    ka_ref[0] = heads[2][:, 128:256]
    qb_ref[0, :, 0:256] = heads[3]
    qb_ref[0, :, 256:384] = heads[4][:, 0:128]
    kb_ref[0, :, 0:128] = heads[4][:, 128:256]
    kb_ref[0, :, 128:384] = heads[5]
    qi_ref[0, :, 0:256] = heads[6]
    qi_ref[0, :, 256:512] = heads[7]
    ki_ref[0] = heads[8][:, 0:128]
    va_ref[0] = proj[:, _COL_VA:_COL_VB].astype(bf16)
    vb_ref[0] = proj[:, _COL_VB:_COL_WI].astype(bf16)
    wi_ref[0] = proj[:, _COL_WI:N_COL]


def _project(x, posf, sc, sh, g, w_perm, gain_a, gain_b):
    bsz, s, d = x.shape
    ts = min(256, s)
    gsum = jnp.asarray(_group_sum_matrix(), bf16)
    perm = jnp.asarray(_interleave_matrix(), bf16)
    inv = ROPE_THETA ** (-np.arange(HALF, dtype=np.float32) / HALF)
    invf = jnp.asarray(np.tile(inv, LANES // HALF)[None, :], f32)
    wa = N_HEADS_A * HEAD_DIM
    wb = N_HEADS_B * HEAD_DIM
    wq = N_IDX_HEADS * IDX_DIM
    const = lambda b, i: (0, 0)
    tile = lambda b, i: (b, i, 0)
    per_b = lambda b, i: (b, 0, 0)
    out_shape = [jax.ShapeDtypeStruct((bsz, s, w), dt) for w, dt in
                 ((wa, bf16), (LANES, bf16), (LANES, bf16), (wq, bf16), (LANES, bf16), (LANES, f32),
                  (wb, bf16), (wb, bf16), (wb, bf16))]
    out_specs = [pl.BlockSpec((1, ts, sh_.shape[2]), tile) for sh_ in out_shape]
    return pl.pallas_call(
        _proj_kernel,
        grid=(bsz, s // ts),
        in_specs=[pl.BlockSpec((1, ts, d), tile),
                  pl.BlockSpec((1, ts, 1), tile),
                  pl.BlockSpec((1, 1, d), per_b),
                  pl.BlockSpec((1, 1, d), per_b),
                  pl.BlockSpec((1, d), const),
                  pl.BlockSpec((d, N_COL), const),
                  pl.BlockSpec((1, N_NORM_SLOTS * HALF), const),
                  pl.BlockSpec((1, N_NORM_SLOTS * HALF), const),
                  pl.BlockSpec((LANES, LANES), const),
                  pl.BlockSpec((2 * LANES, 2 * LANES), const),
                  pl.BlockSpec((1, LANES), const)],
        out_specs=out_specs,
        out_shape=out_shape,
        compiler_params=pltpu.CompilerParams(
            dimension_semantics=("arbitrary", "arbitrary"), vmem_limit_bytes=VMEM_LIMIT),
        name="in_proj",
    )(x, posf, sc, sh, g, w_perm, gain_a, gain_b, gsum, perm, invf)


def _sparse_attn_kernel(qa_ref, ka_ref, va_ref, qi_ref, ki_ref, wi_ref, triu_ref, o_ref,
                        keys_ref, qis_ref, qas_ref, wrep_ref, m_ref, acc_ref,
                        *, qb, tk, n_sel, w_scale):
    i = pl.program_id(1)
    n_pairs_a = N_HEADS_A // 2
    lane = lax.broadcasted_iota(i32, (qb, LANES), 1)
    left = lane < HEAD_DIM
    mask_l = left.astype(f32).astype(bf16)
    mask_r = (1.0 - left.astype(f32)).astype(bf16)

    for c in range(N_IDX_HEADS // 2):
        ch = qi_ref[0, :, LANES * c:LANES * (c + 1)]
        qis_ref[(2 * c) * qb:(2 * c + 1) * qb, :] = ch * mask_l
        qis_ref[(2 * c + 1) * qb:(2 * c + 2) * qb, :] = ch * mask_r
    q_scale = jnp.asarray(HEAD_DIM ** -0.5, bf16)
    for c in range(n_pairs_a):
        ch = qa_ref[0, :, LANES * c:LANES * (c + 1)] * q_scale
        qas_ref[0, c * qb:(c + 1) * qb, :] = ch * mask_l
        qas_ref[1, c * qb:(c + 1) * qb, :] = ch * mask_r
    wi = wi_ref[0] * w_scale
    for h in range(N_IDX_HEADS):
        wrep_ref[h] = jnp.broadcast_to(wi[:, h:h + 1], (qb, LANES))

    n_tiles = (i * qb) // tk + 1
    reps = tk // LANES
    row = lax.broadcasted_iota(i32, (qb, tk), 0) + i * qb
    col = lax.broadcasted_iota(i32, (qb, tk), 1)

    def score_body(j, rmax):
        start = pl.multiple_of(j * tk, tk)
        kt = ki_ref[0, pl.ds(start, tk), :]
        lg = _nt_dot(qis_ref[...], kt)
        acc = jnp.zeros((qb, tk), f32)
        for h in range(N_IDX_HEADS):
            wr = jnp.concatenate([wrep_ref[h]] * reps, axis=1)
            acc = acc + jnp.maximum(lg[h * qb:(h + 1) * qb], 0.0) * wr
        acc = jnp.where(col + j * tk <= row, acc, -jnp.inf)
        keys_ref[j] = acc
        for r in range(reps):
            rmax = jnp.maximum(rmax, acc[:, LANES * r:LANES * (r + 1)])
        return rmax

    rmax = lax.fori_loop(0, n_tiles, score_body, jnp.full((qb, LANES), -jnp.inf, f32))
    rmax = jnp.broadcast_to(jnp.max(rmax, axis=1, keepdims=True), (qb, LANES))

    def key_to_float(key):
        return lax.bitcast_convert_type(key ^ ((key >> 31) & 0x7FFFFFFF), f32)

    def float_to_key(v):
        bits = lax.bitcast_convert_type(v, i32)
        return bits ^ ((bits >> 31) & 0x7FFFFFFF)

    sweep_rows = min(qb, LANES)
    ones8 = jnp.ones((8, LANES), bf16)

    def to_rows(x_lane):
        return jnp.transpose(jnp.broadcast_to(x_lane[0:1, :], (LANES, qb)))

    def to_lanes(cnt):
        return _nt_dot(ones8, cnt.astype(bf16))

    def count_ge(trial_lane):
        trial_rows = to_rows(trial_lane)
        parts = []
        for r0 in range(0, qb, sweep_rows):
            trial_r = trial_rows[r0:r0 + sweep_rows]

            def body(j, cnt, r0=r0, trial_r=trial_r):
                for r in range(reps):
                    ch = keys_ref[j, r0:r0 + sweep_rows, LANES * r:LANES * (r + 1)]
                    cnt = cnt + jnp.where(ch >= trial_r, 1.0, 0.0)
                return cnt
            parts.append(lax.fori_loop(0, n_tiles, body, jnp.zeros((sweep_rows, LANES), f32)))
        return to_lanes(jnp.concatenate(parts, axis=0))

    log_target = math.log(n_sel - 0.5)
    rmax = jnp.transpose(rmax)[0:8, :]
    rmax_pad = rmax + jnp.abs(rmax) * 2.0 ** -20 + 1e-30

    def zero_counts(r0):
        def body(j, cnts):
            ge, gt = cnts
            for r in range(reps):
                ch = keys_ref[j, r0:r0 + sweep_rows, LANES * r:LANES * (r + 1)]
                ge = ge + jnp.where(ch >= 0.0, 1.0, 0.0)
                gt = gt + jnp.where(ch > 0.0, 1.0, 0.0)
            return ge, gt
        z = jnp.zeros((sweep_rows, LANES), f32)
        ge, gt = lax.fori_loop(0, n_tiles, body, (z, z))
        return to_lanes(ge), to_lanes(gt)

    zero_parts = [zero_counts(r0) for r0 in range(0, qb, sweep_rows)]
    ge0 = jnp.concatenate([p[0] for p in zero_parts], axis=1)
    gt0 = jnp.concatenate([p[1] for p in zero_parts], axis=1)
    zeros = jnp.zeros((8, qb), f32)
    total = zeros + (n_tiles * tk).astype(f32)
    above = ge0 >= n_sel
    lo_v0 = jnp.where(above, 0.0, -jnp.inf)
    lo_c0 = jnp.where(above, ge0, total)
    hi_v0 = jnp.where(above, jnp.inf, 0.0)
    hi_c0 = jnp.where(above, jnp.where(gt0 < n_sel, gt0, 0.0), ge0)
    done0 = (above & (gt0 < n_sel)) | (lo_c0 == n_sel)

    def search_cond(state):
        return (state[0] < SEARCH_MAX_STEPS) & state[-1]

    def search_body(state):
        it, lo_v, lo_c, f_lo, hi_v, hi_c, f_hi, last, done, _ = state
        lo_k = float_to_key(lo_v)
        hi_k = float_to_key(hi_v)
        hi_eff = jnp.where(hi_v == jnp.inf, rmax_pad, hi_v)
        t_int = lo_v + (hi_eff - lo_v) * (f_lo / (f_lo - f_hi))
        use_int = (lo_v > -jnp.inf) & (t_int > lo_v) & (t_int < hi_v) & (it < SEARCH_INTERP_STEPS)
        mid_k = (lo_k >> 1) + (hi_k >> 1) + (lo_k & hi_k & 1)
        t = jnp.where(use_int, t_int, key_to_float(mid_k))
        c = count_ge(t)
        f = jnp.log(jnp.maximum(c, 0.5)) - log_target
        active = done < 0.5
        is_lo = (c >= n_sel) & active
        is_hi = (c < n_sel) & active
        f_hi = jnp.where(is_lo & (last > 0.0), f_hi * 0.5, f_hi)
        f_lo = jnp.where(is_hi & (last < 0.0), f_lo * 0.5, f_lo)
        lo_v = jnp.where(is_lo, t, lo_v)
        lo_c = jnp.where(is_lo, c, lo_c)
        f_lo = jnp.where(is_lo, f, f_lo)
        hi_v = jnp.where(is_hi, t, hi_v)
        hi_c = jnp.where(is_hi, c, hi_c)
        f_hi = jnp.where(is_hi, f, f_hi)
        last = jnp.where(is_lo, 1.0, jnp.where(is_hi, -1.0, last))
        settled = (lo_c == n_sel) | (float_to_key(lo_v) + 1 >= float_to_key(hi_v))
        done = jnp.where(settled, 1.0, done)
        unresolved = jnp.min(done) < 0.5
        return it + 1, lo_v, lo_c, f_lo, hi_v, hi_c, f_hi, last, done, unresolved

    done0 = jnp.where(done0, 1.0, 0.0)
    init = (jnp.int32(0), lo_v0, lo_c0, jnp.log(lo_c0) - log_target,
            hi_v0, hi_c0, jnp.log(jnp.maximum(hi_c0, 0.5)) - log_target,
            zeros, done0, jnp.min(done0) < 0.5)
    final = lax.while_loop(search_cond, search_body, init)
    thr, lo_c, hi_c = final[1], final[2], final[5]
    thr_t = jnp.concatenate([to_rows(thr)] * reps, axis=1)
    tied = lo_c != n_sel
    any_tied = jnp.max(jnp.where(tied, 1.0, 0.0)) > 0.0
    quota = to_rows(jnp.where(tied, n_sel - hi_c, 2.0 * tk * (n_tiles + 1).astype(f32)))
    quota_t = jnp.concatenate([quota] * reps, axis=1)

    m_ref[...] = jnp.full(m_ref.shape, NEG_BIG, f32)
    acc_ref[...] = jnp.zeros(acc_ref.shape, f32)
    lane_k = lax.broadcasted_iota(i32, (tk, LANES), 1)
    kmask_l = (lane_k < HEAD_DIM).astype(f32).astype(bf16)
    kmask_r = (lane_k >= HEAD_DIM).astype(f32).astype(bf16)

    def attn_body(j, n_ties, with_ties):
        start = pl.multiple_of(j * tk, tk)
        sc = keys_ref[j]
        if with_ties:
            eq = jnp.where(sc == thr_t, 1.0, 0.0)
            before = (jnp.dot(eq.astype(bf16), triu_ref[...], preferred_element_type=f32)
                      + jnp.concatenate([n_ties] * reps, axis=1))
            sel = (sc > thr_t) | ((sc == thr_t) & (before < quota_t))
            n_ties = n_ties + jnp.broadcast_to(jnp.sum(eq, axis=1, keepdims=True), (qb, LANES))
        else:
            sel = sc >= thr_t
        sel = sel & (col + j * tk <= row)
        bias = jnp.where(sel, 0.0, NEG_BIG)
        bias = jnp.concatenate([bias] * n_pairs_a, axis=0)
        kk = ka_ref[0, pl.ds(start, tk), :]
        vv = va_ref[0, pl.ds(start, tk), :]
        v_ext = (vv * kmask_l + kmask_r, vv * kmask_r + kmask_l)
        for g in range(N_KV_A):
            s = _nt_dot(qas_ref[g], kk) + bias
            m_old = m_ref[g]
            m_new = jnp.maximum(m_old, jnp.max(s, axis=1, keepdims=True))
            alpha = jnp.exp(m_old - m_new)
            p = jnp.exp(s - jnp.concatenate([m_new] * reps, axis=1))
            acc_ref[g] = alpha * acc_ref[g] + jnp.dot(p.astype(bf16), v_ext[g], preferred_element_type=f32)
            m_ref[g] = m_new
        return n_ties

    no_ties = jnp.zeros((qb, LANES), f32)

    @pl.when(any_tied)
    def _():
        lax.fori_loop(0, n_tiles, functools.partial(attn_body, with_ties=True), no_ties)

    @pl.when(jnp.logical_not(any_tied))
    def _():
        lax.fori_loop(0, n_tiles, functools.partial(attn_body, with_ties=False), no_ties)

    for c in range(n_pairs_a):
        rows = slice(c * qb, (c + 1) * qb)
        a0 = acc_ref[0, rows, :]
        a1 = acc_ref[1, rows, :]
        o0 = a0 / pltpu.roll(a0, HEAD_DIM, axis=1)
        o1 = a1 / pltpu.roll(a1, HEAD_DIM, axis=1)
        o_ref[0, :, LANES * c:LANES * (c + 1)] = jnp.where(left, o0, o1).astype(bf16)


def _sparse_attention(qa, ka, va, qi, ki, wi):
    bsz, s, wa = qa.shape
    qb = min(256, s)
    tk = min(512, s)
    n_sel = min(TOPK_MAX, s // 4)
    w_scale = N_IDX_HEADS ** -0.5 * IDX_DIM ** -0.5
    n_pairs_a = N_HEADS_A // 2
    tile = lambda b, i: (b, i, 0)
    per_b = lambda b, i: (b, 0, 0)
    kern = functools.partial(_sparse_attn_kernel, qb=qb, tk=tk, n_sel=float(n_sel), w_scale=w_scale)
    return pl.pallas_call(
        kern,
        grid=(bsz, s // qb),
        in_specs=[pl.BlockSpec((1, qb, wa), tile),
                  pl.BlockSpec((1, s, LANES), per_b),
                  pl.BlockSpec((1, s, LANES), per_b),
                  pl.BlockSpec((1, qb, qi.shape[2]), tile),
                  pl.BlockSpec((1, s, LANES), per_b),
                  pl.BlockSpec((1, qb, LANES), tile),
                  pl.BlockSpec((tk, tk), lambda b, i: (0, 0))],
        out_specs=pl.BlockSpec((1, qb, wa), tile),
        out_shape=jax.ShapeDtypeStruct((bsz, s, wa), bf16),
        scratch_shapes=[pltpu.VMEM((s // tk, qb, tk), f32),
                        pltpu.VMEM((N_IDX_HEADS * qb, LANES), bf16),
                        pltpu.VMEM((N_KV_A, n_pairs_a * qb, LANES), bf16),
                        pltpu.VMEM((N_IDX_HEADS, qb, LANES), f32),
                        pltpu.VMEM((N_KV_A, n_pairs_a * qb, LANES), f32),
                        pltpu.VMEM((N_KV_A, n_pairs_a * qb, LANES), f32)],
        compiler_params=pltpu.CompilerParams(
            dimension_semantics=("arbitrary", "arbitrary"), vmem_limit_bytes=VMEM_LIMIT),
        name="sparse_attn",
    )(qa, ka, va, qi, ki, wi, jnp.asarray(np.triu(np.ones((tk, tk), np.float32), k=1), bf16))


def _dilated_kernel(q_ref, k_ref, v_ref, bias_ref, o_ref, qs_ref, m_ref, acc_ref, *, tq, nd):
    i = pl.program_id(2)
    lane = lax.broadcasted_iota(i32, (tq, LANES), 1)
    left = lane < HEAD_DIM
    mask_l = left.astype(f32).astype(bf16)
    mask_r = (1.0 - left.astype(f32)).astype(bf16)
    q = q_ref[0] * jnp.asarray(HEAD_DIM ** -0.5, bf16)
    qs_ref[0:tq, :] = q * mask_l
    qs_ref[tq:2 * tq, :] = q * mask_r
    m_ref[...] = jnp.full(m_ref.shape, NEG_BIG, f32)
    acc_ref[...] = jnp.zeros(acc_ref.shape, f32)
    ones = jnp.ones((tq, LANES), bf16)

    def body(j, carry):
        start = pl.multiple_of(j * tq, tq)
        kk = k_ref[0, pl.ds(start, tq), :]
        v_ext = jnp.concatenate([v_ref[0, pl.ds(start, tq), :], ones], axis=1)
        b = bias_ref[i - j]
        s = _nt_dot(qs_ref[...], kk) + jnp.concatenate([b, b], axis=0)
        m_old = m_ref[...]
        m_new = jnp.maximum(m_old, jnp.max(s, axis=1, keepdims=True))
        alpha = jnp.exp(m_old - m_new)
        p = jnp.exp(s - jnp.concatenate([m_new] * (tq // LANES), axis=1))
        pv = jnp.dot(p.astype(bf16), v_ext, preferred_element_type=f32)
        acc_ref[...] = jnp.concatenate([alpha, alpha], axis=1) * acc_ref[...] + pv
        m_ref[...] = m_new
        return carry

    lax.fori_loop(jnp.maximum(i - (nd - 1), 0), i + 1, body, 0)
    o0 = acc_ref[0:tq, 0:LANES] / acc_ref[0:tq, LANES:2 * LANES]
    o1 = acc_ref[tq:2 * tq, 0:LANES] / acc_ref[tq:2 * tq, LANES:2 * LANES]
    o_ref[0] = jnp.where(left, o0, o1).astype(bf16)


def _dilated_attention(qb_, kb_, vb_):
    bsz, s, wb = qb_.shape
    tq = min(512, s)
    bias_np, nd = _dilated_bias(tq)
    bias = jnp.asarray(bias_np)
    n_pairs = wb // LANES
    kern = functools.partial(_dilated_kernel, tq=tq, nd=nd)
    return pl.pallas_call(
        kern,
        grid=(bsz, n_pairs, s // tq),
        in_specs=[pl.BlockSpec((1, tq, LANES), lambda b, p, i: (b, i, p)),
                  pl.BlockSpec((1, s, LANES), lambda b, p, i: (b, 0, p)),
                  pl.BlockSpec((1, s, LANES), lambda b, p, i: (b, 0, p)),
                  pl.BlockSpec((nd, tq, tq), lambda b, p, i: (0, 0, 0))],
        out_specs=pl.BlockSpec((1, tq, LANES), lambda b, p, i: (b, i, p)),
        out_shape=jax.ShapeDtypeStruct((bsz, s, wb), bf16),
        scratch_shapes=[pltpu.VMEM((2 * tq, LANES), bf16),
                        pltpu.VMEM((2 * tq, LANES), f32),
                        pltpu.VMEM((2 * tq, 2 * LANES), f32)],
        compiler_params=pltpu.CompilerParams(
            dimension_semantics=("arbitrary", "arbitrary", "arbitrary"), vmem_limit_bytes=VMEM_LIMIT),
        name="dilated_attn",
    )(qb_, kb_, vb_, bias)


def _mid_kernel(oa_ref, ob_ref, x_ref, gta_ref, scf_ref, shf_ref, gtf_ref, woa_ref, wob_ref, gffn_ref,
                wsgu_ref, wsd_ref, wrt_ref, rbias_ref, triu_ref,
                base_ref, h2_ref, sel_ref, gate_ref, rank_ref, cnt_ref, carry_ref, *, n_exp, t, d_sh):
    step = pl.program_id(0)

    @pl.when(step == 0)
    def _():
        carry_ref[...] = jnp.zeros(carry_ref.shape, f32)

    mix = (jnp.dot(oa_ref[...], woa_ref[...], preferred_element_type=f32)
           + jnp.dot(ob_ref[...], wob_ref[...], preferred_element_type=f32))
    x1 = x_ref[...] + gta_ref[0] * mix
    ms = jnp.mean(x1 * x1, axis=-1, keepdims=True)
    h2 = ((x1 * lax.rsqrt(ms + EPS)) * gffn_ref[...]) * (1.0 + scf_ref[0]) + shf_ref[0]
    h2b = h2.astype(bf16)
    h2f = h2b.astype(f32)
    n_sub = h2_ref.shape[0] // t
    for a in range(n_sub):
        h2_ref[pl.ds(a, t, stride=n_sub), :] = h2f[:, LANES * a:LANES * (a + 1)]

    gu = jnp.dot(h2b, wsgu_ref[...], preferred_element_type=f32)
    g = gu[:, :d_sh]
    u = gu[:, d_sh:]
    act = (g * jax.nn.sigmoid(g)) * u
    shared = jnp.dot(act.astype(bf16), wsd_ref[...], preferred_element_type=f32)
    base = x1 + gtf_ref[0] * shared
    for a in range(n_sub):
        base_ref[pl.ds(a, t, stride=n_sub), :] = base[:, LANES * a:LANES * (a + 1)]

    scores = jax.nn.sigmoid(_nt_dot(wrt_ref[...], h2b))
    biased = scores + rbias_ref[...]
    per = n_exp // N_GROUPS
    neg_inf = jnp.float32(-jnp.inf)
    ri_g = lax.broadcasted_iota(i32, (per, t), 0).astype(f32)
    gs = []
    for grp in range(N_GROUPS):
        blk = biased[grp * per:(grp + 1) * per]
        m1 = jnp.max(blk, axis=0, keepdims=True)
        idx1 = jnp.min(jnp.where(blk == m1, ri_g, float(per)), axis=0, keepdims=True)
        m2 = jnp.max(jnp.where(ri_g == idx1, neg_inf, blk), axis=0, keepdims=True)
        gs.append(m1 + m2)
    masked_rows = []
    for grp in range(N_GROUPS):
        beaten = jnp.zeros((1, t), f32)
        for g2 in range(N_GROUPS):
            if g2 == grp:
                continue
            wins = gs[g2] > gs[grp]
            if g2 < grp:
                wins = wins | (gs[g2] == gs[grp])
            beaten = beaten + jnp.where(wins, 1.0, 0.0)
        keep = jnp.broadcast_to(beaten < TOPK_GROUPS, (per, t))
        masked_rows.append(jnp.where(keep, biased[grp * per:(grp + 1) * per], neg_inf))
    masked = jnp.concatenate(masked_rows, axis=0)

    ri = lax.broadcasted_iota(i32, (n_exp, t), 0).astype(f32)
    selmask = jnp.zeros((n_exp, t), f32)
    idxs, graw = [], []
    for _ in range(TOP_K):
        m = jnp.max(masked, axis=0, keepdims=True)
        idx = jnp.min(jnp.where(masked == m, ri, float(n_exp)), axis=0, keepdims=True)
        onehot = ri == idx
        graw.append(jnp.sum(jnp.where(onehot, scores, 0.0), axis=0, keepdims=True))
        masked = jnp.where(onehot, neg_inf, masked)
        selmask = jnp.where(onehot, 1.0, selmask)
        idxs.append(idx)
    den = graw[0]
    for k in range(1, TOP_K):
        den = den + graw[k]

    prefix = jnp.dot(selmask.astype(bf16), triu_ref[...], preferred_element_type=f32)
    prefix = prefix + jnp.concatenate([carry_ref[...]] * (t // LANES), axis=1)
    for k in range(TOP_K):
        rank_k = jnp.sum(jnp.where(ri == idxs[k], prefix, 0.0), axis=0, keepdims=True)
        sel_ref[k:k + 1, :] = idxs[k].astype(i32)
        rank_ref[k:k + 1, :] = rank_k.astype(i32)
        gate_ref[k:k + 1, :] = graw[k] / den * ROUTED_SCALE
    carry_ref[...] = carry_ref[...] + jnp.broadcast_to(
        jnp.sum(selmask, axis=1, keepdims=True), carry_ref.shape)
    cnt_ref[...] = carry_ref[...]


def _mid(oa, ob, x2, gta, scf, shf, gtf, woa, wob, gffn, wsgu, wsd, wrt, rbias_rep, s):
    n, d = x2.shape
    t = rbias_rep.shape[1]
    n_exp = wrt.shape[0]
    d_sh = wsd.shape[0]
    triu = jnp.asarray(np.triu(np.ones((t, t), np.float32), k=1), bf16)
    tile = lambda i: (i, 0)
    const = lambda i: (0, 0)
    per_b = lambda i: ((i * t) // s, 0, 0)
    lane_tile = lambda i: (0, i)
    n_sub = d // LANES
    tok_tiles = jax.ShapeDtypeStruct((n * n_sub, LANES), f32)
    tok_spec = pl.BlockSpec((t * n_sub, LANES), tile)
    out_shape = [tok_tiles, tok_tiles,
                 jax.ShapeDtypeStruct((TOP_K, n), i32), jax.ShapeDtypeStruct((TOP_K, n), f32),
                 jax.ShapeDtypeStruct((TOP_K, n), i32), jax.ShapeDtypeStruct((n_exp, LANES), f32)]
    out_specs = [tok_spec, tok_spec,
                 pl.BlockSpec((TOP_K, t), lane_tile), pl.BlockSpec((TOP_K, t), lane_tile),
                 pl.BlockSpec((TOP_K, t), lane_tile), pl.BlockSpec((n_exp, LANES), const)]
    kern = functools.partial(_mid_kernel, n_exp=n_exp, t=t, d_sh=d_sh)
    return pl.pallas_call(
        kern,
        grid=(n // t,),
        in_specs=[pl.BlockSpec((t, oa.shape[1]), tile),
                  pl.BlockSpec((t, ob.shape[1]), tile),
                  pl.BlockSpec((t, d), tile),
                  pl.BlockSpec((1, 1, d), per_b),
                  pl.BlockSpec((1, 1, d), per_b),
                  pl.BlockSpec((1, 1, d), per_b),
                  pl.BlockSpec((1, 1, d), per_b),
                  pl.BlockSpec(woa.shape, const),
                  pl.BlockSpec(wob.shape, const),
                  pl.BlockSpec((1, d), const),
                  pl.BlockSpec(wsgu.shape, const),
                  pl.BlockSpec(wsd.shape, const),
                  pl.BlockSpec(wrt.shape, const),
                  pl.BlockSpec(rbias_rep.shape, const),
                  pl.BlockSpec((t, t), const)],
        out_specs=out_specs,
        out_shape=out_shape,
        scratch_shapes=[pltpu.VMEM((n_exp, LANES), f32)],
        compiler_params=pltpu.CompilerParams(
            dimension_semantics=("arbitrary",), vmem_limit_bytes=VMEM_LIMIT),
        name="mid",
    )(oa, ob, x2, gta, scf, shf, gtf, woa, wob, gffn, wsgu, wsd, wrt, rbias_rep, triu)


def _dispatch_kernel(zb_ref, slot_ref, h_ref, xs_ref, zero_ref, sem_ref, *, td, n_zero):
    step = pl.program_id(0)

    def zero_copy(b):
        return pltpu.make_async_copy(zero_ref, xs_ref.at[pl.ds(zb_ref[b], ZERO_ROWS)], sem_ref.at[0])

    @pl.when(step == 0)
    def _():
        zero_ref[...] = jnp.zeros(zero_ref.shape, f32)

        def start(b, c):
            @pl.when(zb_ref[b] >= 0)
            def _():
                zero_copy(b).start()
            return c

        def wait(b, c):
            @pl.when(zb_ref[b] >= 0)
            def _():
                zero_copy(b).wait()
            return c

        lax.fori_loop(0, n_zero, start, 0)
        lax.fori_loop(0, n_zero, wait, 0)

    def start_rows(tok, c):
        for k in range(TOP_K):
            pltpu.make_async_copy(h_ref.at[pl.ds(tok, 1)], xs_ref.at[pl.ds(slot_ref[k, tok], 1)],
                                  sem_ref.at[1]).start(priority=k % 2)
        return c

    lax.fori_loop(0, td, start_rows, 0)
    for k in range(TOP_K):
        pltpu.make_async_copy(h_ref, xs_ref.at[pl.ds(0, td)], sem_ref.at[1]).wait()


def _dispatch(zero_start, slot, h2, n_slots):
    n, rows, lanes = h2.shape
    td = min(256, n)
    kern = functools.partial(_dispatch_kernel, td=td, n_zero=zero_start.shape[0])
    grid_spec = pltpu.PrefetchScalarGridSpec(
        num_scalar_prefetch=1,
        grid=(n // td,),
        in_specs=[pl.BlockSpec((TOP_K, td), lambda i, tail: (0, i), memory_space=pltpu.SMEM),
                  pl.BlockSpec((td, rows, lanes), lambda i, tail: (i, 0, 0))],
        out_specs=pl.BlockSpec(memory_space=pl.ANY),
        scratch_shapes=[pltpu.VMEM((ZERO_ROWS, rows, lanes), f32), pltpu.SemaphoreType.DMA((2,))],
    )
    return pl.pallas_call(
        kern,
        grid_spec=grid_spec,
        out_shape=jax.ShapeDtypeStruct((n_slots, rows, lanes), f32),
        compiler_params=pltpu.CompilerParams(
            dimension_semantics=("arbitrary",), vmem_limit_bytes=VMEM_LIMIT),
        name="dispatch",
    )(zero_start, slot, h2)


def _expert_kernel(be_ref, nu_ref, xs_ref, wg_ref, wu_ref, wd_ref, ys_ref, wgu_s, wd_s, *, n_sub):
    i = pl.program_id(0)
    f = wd_s.shape[0]
    rows = xs_ref.shape[0] // n_sub

    @pl.when(i >= nu_ref[0])
    def _():
        ys_ref[...] = jnp.zeros(ys_ref.shape, f32)

    @pl.when(i < nu_ref[0])
    def _():
        @pl.when((i == 0) | (be_ref[i] != be_ref[jnp.maximum(i - 1, 0)]))
        def _():
            wgu_s[:, 0:f] = wg_ref[0].astype(bf16)
            wgu_s[:, f:2 * f] = wu_ref[0].astype(bf16)
            wd_s[...] = wd_ref[0].astype(bf16)

        xb = jnp.concatenate([xs_ref[pl.ds(a, rows, stride=n_sub), :] for a in range(n_sub)],
                             axis=1).astype(bf16)
        gu = jnp.dot(xb, wgu_s[...], preferred_element_type=f32)
        g = gu[:, 0:f]
        u = gu[:, f:2 * f]
        act = (g * jax.nn.sigmoid(g)) * u
        y = jnp.dot(act.astype(bf16), wd_s[...], preferred_element_type=f32)
        for a in range(n_sub):
            ys_ref[pl.ds(a, rows, stride=n_sub), :] = y[:, LANES * a:LANES * (a + 1)]


def _experts(block_expert, n_used, xs, w_gate, w_up, w_down):
    n_slots, n_sub, lanes = xs.shape
    d = n_sub * lanes
    n_blocks = n_slots // MOE_BLOCK
    f = w_gate.shape[2]
    blk = lambda i, be, nu: (jnp.minimum(i, nu[0] - 1), 0)
    wsel = lambda i, be, nu: (be[jnp.minimum(i, nu[0] - 1)], 0, 0)
    grid_spec = pltpu.PrefetchScalarGridSpec(
        num_scalar_prefetch=2,
        grid=(n_blocks,),
        in_specs=[pl.BlockSpec((MOE_BLOCK * n_sub, lanes), blk),
                  pl.BlockSpec((1, d, f), wsel),
                  pl.BlockSpec((1, d, f), wsel),
                  pl.BlockSpec((1, f, d), wsel)],
        out_specs=pl.BlockSpec((MOE_BLOCK * n_sub, lanes), lambda i, be, nu: (i, 0)),
        scratch_shapes=[pltpu.VMEM((d, 2 * f), bf16), pltpu.VMEM((f, d), bf16)],
    )
    ys = pl.pallas_call(
        functools.partial(_expert_kernel, n_sub=n_sub),
        grid_spec=grid_spec,
        out_shape=jax.ShapeDtypeStruct((n_slots * n_sub, lanes), f32),
        compiler_params=pltpu.CompilerParams(
            dimension_semantics=("arbitrary",), vmem_limit_bytes=VMEM_LIMIT),
        name="experts",
    )(block_expert, n_used, xs.reshape(n_slots * n_sub, lanes), w_gate, w_up, w_down)
    return ys.reshape(n_slots, n_sub, lanes)


def _combine_kernel(slot_ref, nslot_ref, gate_ref, base_ref, gtf_ref, ys_ref, o_ref, buf_ref, sem_ref, *, tc):
    step = pl.program_id(0)
    cur = step % 2

    def start_gathers(idx_ref, b):
        def body(tok, c):
            for k in range(TOP_K):
                pltpu.make_async_copy(ys_ref.at[pl.ds(idx_ref[k, tok], 1)],
                                      buf_ref.at[b, k, pl.ds(tok, 1)], sem_ref.at[b]).start(priority=k % 2)
            return c
        lax.fori_loop(0, tc, body, 0)

    @pl.when(step == 0)
    def _():
        start_gathers(slot_ref, 0)

    @pl.when(step + 1 < pl.num_programs(0))
    def _():
        start_gathers(nslot_ref, 1 - cur)

    for k in range(TOP_K):
        pltpu.make_async_copy(ys_ref.at[pl.ds(0, tc)], buf_ref.at[cur, k], sem_ref.at[cur]).wait()

    gtf = gtf_ref[0]

    def reduce_token(tok, c):
        routed = buf_ref[cur, 0, tok] * gate_ref[0, tok]
        for k in range(1, TOP_K):
            routed = routed + buf_ref[cur, k, tok] * gate_ref[k, tok]
        o_ref[tok] = base_ref[tok] + gtf * routed
        return c

    lax.fori_loop(0, tc, reduce_token, 0, unroll=4)


def _combine(slot, gate, base, gtf, ys, s):
    n, rows, lanes = base.shape
    tc = min(128, n)
    n_steps = n // tc
    kern = functools.partial(_combine_kernel, tc=tc)
    tok_tile = lambda i: (i, 0, 0)
    return pl.pallas_call(
        kern,
        grid=(n_steps,),
        in_specs=[pl.BlockSpec((TOP_K, tc), lambda i: (0, i), memory_space=pltpu.SMEM),
                  pl.BlockSpec((TOP_K, tc), lambda i: (0, jnp.minimum(i + 1, n_steps - 1)),
                               memory_space=pltpu.SMEM),
                  pl.BlockSpec((TOP_K, tc), lambda i: (0, i), memory_space=pltpu.SMEM),
                  pl.BlockSpec((tc, rows, lanes), tok_tile),
                  pl.BlockSpec((1, rows, lanes), lambda i: ((i * tc) // s, 0, 0)),
                  pl.BlockSpec(memory_space=pl.ANY)],
        out_specs=pl.BlockSpec((tc, rows, lanes), tok_tile),
        out_shape=jax.ShapeDtypeStruct((n, rows, lanes), f32),
        scratch_shapes=[pltpu.VMEM((2, TOP_K, tc, rows, lanes), f32), pltpu.SemaphoreType.DMA((2,))],
        compiler_params=pltpu.CompilerParams(
            dimension_semantics=("arbitrary",), vmem_limit_bytes=VMEM_LIMIT),
        name="combine",
    )(slot, slot, gate, base, gtf, ys)


def _layer(x, c, posf, w_ada, b_ada, g_mix, w_in, q_norm_a, k_norm_a, q_norm_b, k_norm_b, w_out, g_ffn,
           w_router, router_bias, w_gate, w_up, w_down, ws_gate, ws_up, ws_down):
    bsz, s, d = x.shape
    n = bsz * s
    n_exp = w_router.shape[1]

    mod = _adaln(c, w_ada, b_ada)[:, None, :]
    sh_a, sc_a, gt_a, sh_f, sc_f, gt_f = jnp.split(mod, 6, axis=-1)

    w_perm = _take_runs(w_in.astype(bf16), _projection_columns(), axis=1)
    gains = {"qa": q_norm_a, "ka": k_norm_a, "qb": q_norm_b, "kb": k_norm_b}
    gain_a = jnp.concatenate([gains[kind][:HALF] for kind, _ in _SLOTS[:N_NORM_SLOTS]])[None, :].astype(f32)
    gain_b = jnp.concatenate([gains[kind][HALF:] for kind, _ in _SLOTS[:N_NORM_SLOTS]])[None, :].astype(f32)

    qa, ka, va, qi, ki, wi, qb_, kb_, vb_ = _project(x, posf, sc_a, sh_a, g_mix[None, :], w_perm, gain_a, gain_b)
    o_a = _sparse_attention(qa, ka, va, qi, ki, wi)
    o_b = _dilated_attention(qb_, kb_, vb_)

    wa = N_HEADS_A * HEAD_DIM
    rows_a = np.concatenate([np.arange(h * HEAD_DIM, (h + 1) * HEAD_DIM) for h in QA_PAIR_ORDER])
    woa = _take_runs(w_out, rows_a, axis=0).astype(bf16)
    wob = w_out[wa:].astype(bf16)
    wsgu = jnp.concatenate([ws_gate, ws_up], axis=1).astype(bf16)
    wsd = ws_down.astype(bf16)
    wrt = w_router.T.astype(bf16)
    t_mid = min(256, n)
    rbias_rep = jnp.broadcast_to(router_bias.astype(f32)[:, None], (n_exp, t_mid))

    base, h2, sel, gate, rank, cnt = _mid(
        o_a.reshape(n, wa), o_b.reshape(n, -1), x.reshape(n, d), gt_a, sc_f, sh_f, gt_f,
        woa, wob, g_ffn[None, :], wsgu, wsd, wrt, rbias_rep, s)
    base = base.reshape(n, d // LANES, LANES)
    h2 = h2.reshape(n, d // LANES, LANES)

    counts = cnt[:, 0].astype(i32)
    padded = (counts + MOE_BLOCK - 1) // MOE_BLOCK * MOE_BLOCK
    pad_end = jnp.cumsum(padded)
    pad_start = pad_end - padded
    onehot = sel[:, :, None] == jnp.arange(n_exp, dtype=i32)[None, None, :]
    slot = jnp.sum(jnp.where(onehot, pad_start[None, None, :], 0), axis=-1) + rank
    n_blocks = -(-(n * TOP_K) // MOE_BLOCK) + n_exp
    n_slots = n_blocks * MOE_BLOCK
    block_start = jnp.arange(n_blocks, dtype=i32) * MOE_BLOCK
    block_expert = jnp.sum((pad_end[None, :] <= block_start[:, None]).astype(i32), axis=1)
    block_expert = jnp.minimum(block_expert, n_exp - 1)
    n_used = (pad_end[-1] // MOE_BLOCK).astype(i32)[None]
    unit_start = jnp.arange(n_slots // ZERO_ROWS, dtype=i32) * ZERO_ROWS
    unit_expert = jnp.sum((pad_end[None, :] <= unit_start[:, None]).astype(i32), axis=1)
    real_end = pad_start + counts
    unit_real_end = jnp.sum(jnp.where(unit_expert[:, None] == jnp.arange(n_exp, dtype=i32)[None, :],
                                      real_end[None, :], 0), axis=1)
    has_pad = (unit_start + ZERO_ROWS > unit_real_end) | (unit_start >= pad_end[-1])
    zero_start = jnp.where(has_pad, unit_start, -1).astype(i32)

    xs = _dispatch(zero_start, slot, h2, n_slots)
    ys = _experts(block_expert, n_used, xs, w_gate, w_up, w_down)
    out = _combine(slot, gate, base, gt_f.reshape(bsz, d // LANES, LANES), ys, s)
    return out.reshape(bsz, s, d)


def kernel(x, c, positions, w_ada, b_ada, g_mix, w_in, q_norm_a, k_norm_a, q_norm_b, k_norm_b, w_out, g_ffn,
           w_router, router_bias, w_gate, w_up, w_down, ws_gate, ws_up, ws_down):
    posf = positions.astype(f32)[..., None]
    for l in range(w_ada.shape[0]):
        x = _layer(x, c, posf, w_ada[l], b_ada[l], g_mix[l], w_in[l], q_norm_a[l], k_norm_a[l], q_norm_b[l],
                   k_norm_b[l], w_out[l], g_ffn[l], w_router[l], router_bias[l], w_gate[l], w_up[l], w_down[l],
                   ws_gate[l], ws_up[l], ws_down[l])
    return x
```

```python
import functools
import math

import numpy as np
import jax
import jax.numpy as jnp
from jax import lax
from jax.experimental import pallas as pl
from jax.experimental.pallas import tpu as pltpu

f32 = jnp.float32
bf16 = jnp.bfloat16
i32 = jnp.int32

HEAD_DIM = 64
HALF = HEAD_DIM // 2
N_HEADS_A = 10
N_KV_A = 2
N_HEADS_B = 6
N_IDX_HEADS = 8
IDX_DIM = 64
TOPK_MAX = 256
DILATED_PATTERNS = ((128, 1), (512, 4), (2048, 16))
ROPE_THETA = 10000.0
EPS = 1e-6
TOP_K = 8
N_GROUPS = 8
TOPK_GROUPS = 4
ROUTED_SCALE = 2.5
MOE_BLOCK = 512
ZERO_ROWS = 64

LANES = 128
VMEM_LIMIT = 48 * 1024 * 1024

NEG_BIG = -1e30
KEY_NEG_INF = int(np.int32(np.uint32(0xFF800000) ^ np.uint32(0x7FFFFFFF)))
KEY_POS_INF = 0x7F800000
SEARCH_INTERP_STEPS = 16
SEARCH_MAX_STEPS = SEARCH_INTERP_STEPS + 34

_OFF_QA = 0
_OFF_KA = _OFF_QA + N_HEADS_A * HEAD_DIM
_OFF_VA = _OFF_KA + N_KV_A * HEAD_DIM
_OFF_QI = _OFF_VA + N_KV_A * HEAD_DIM
_OFF_KI = _OFF_QI + N_IDX_HEADS * IDX_DIM
_OFF_WI = _OFF_KI + IDX_DIM
_OFF_QB = _OFF_WI + N_IDX_HEADS
_OFF_KB = _OFF_QB + N_HEADS_B * HEAD_DIM
_OFF_VB = _OFF_KB + N_HEADS_B * HEAD_DIM
D_IN = _OFF_VB + N_HEADS_B * HEAD_DIM

QA_PAIR_ORDER = (0, 5, 1, 6, 2, 7, 3, 8, 4, 9)

_SLOTS = (
    [("qa", h) for h in QA_PAIR_ORDER[:8]] + [("qa", 4), ("qa", 9), ("ka", 0), ("ka", 1)]
    + [("qb", h) for h in range(6)] + [("kb", h) for h in range(6)]
    + [("qi", h) for h in range(8)] + [("ki", 0), ("ki", 0), ("pad", 0), ("pad", 0)]
)
N_NORM_SLOTS = 24
N_CHUNKS = len(_SLOTS) // 4
SLAB = N_CHUNKS * LANES
_COL_VA = 2 * SLAB
_COL_VB = _COL_VA + N_KV_A * HEAD_DIM
_COL_WI = _COL_VB + N_HEADS_B * HEAD_DIM
N_COL = _COL_WI + LANES


def _slot_offset(kind, h):
    base = {"qa": _OFF_QA, "ka": _OFF_KA, "qb": _OFF_QB, "kb": _OFF_KB, "qi": _OFF_QI, "ki": _OFF_KI}
    return base[kind] + h * HEAD_DIM


def _projection_columns():
    zero_col = D_IN
    cols_a, cols_b = [], []
    for kind, h in _SLOTS:
        if kind == "pad":
            cols_a += [zero_col] * HALF
            cols_b += [zero_col] * HALF
        else:
            off = _slot_offset(kind, h)
            cols_a += list(range(off, off + HALF))
            cols_b += list(range(off + HALF, off + HEAD_DIM))
    cols = cols_a + cols_b
    cols += list(range(_OFF_VA, _OFF_VA + N_KV_A * HEAD_DIM))
    cols += list(range(_OFF_VB, _OFF_VB + N_HEADS_B * HEAD_DIM))
    cols += list(range(_OFF_WI, _OFF_WI + N_IDX_HEADS)) + [zero_col] * (LANES - N_IDX_HEADS)
    assert len(cols) == N_COL
    return np.asarray(cols, np.int32)


def _take_runs(w, idx, axis):
    size = w.shape[axis]
    pieces, a = [], 0
    idx = [int(v) for v in idx]
    while a < len(idx):
        b = a + 1
        if idx[a] == size:
            while b < len(idx) and idx[b] == size:
                b += 1
            shape = list(w.shape)
            shape[axis] = b - a
            pieces.append(jnp.zeros(shape, w.dtype))
        else:
            while b < len(idx) and idx[b] == idx[b - 1] + 1:
                b += 1
            pieces.append(lax.slice_in_dim(w, idx[a], idx[b - 1] + 1, axis=axis))
        a = b
    return jnp.concatenate(pieces, axis=axis)


def _interleave_matrix():
    p = np.zeros((2 * LANES, 2 * LANES), np.float32)
    for head in range(4):
        for i in range(HALF):
            p[HALF * head + i, HEAD_DIM * head + i] = 1.0
            p[LANES + HALF * head + i, HEAD_DIM * head + HALF + i] = 1.0
    return p


def _group_sum_matrix():
    g = np.zeros((LANES, LANES), np.float32)
    for k in range(LANES // HALF):
        g[HALF * k:HALF * (k + 1), HALF * k:HALF * (k + 1)] = 1.0
    return g


def _dilated_bias(tq):
    max_win = max(w for w, _ in DILATED_PATTERNS)
    nd = max_win // tq + 1
    d = np.arange(nd)[:, None, None] * tq + np.arange(tq)[None, :, None] - np.arange(tq)[None, None, :]
    mult = np.zeros(d.shape, np.float64)
    for win, dil in DILATED_PATTERNS:
        mult += ((d >= 0) & (d <= win) & (d % dil == 0)).astype(np.float64)
    with np.errstate(divide="ignore"):
        bias = np.where(mult > 0, np.log(np.maximum(mult, 1.0)), NEG_BIG)
    return bias.astype(np.float32), nd


def _nt_dot(a, b):
    return lax.dot_general(a, b, (((1,), (1,)), ((), ())), preferred_element_type=f32)


def _adaln_kernel(c_ref, w_ref, b_ref, o_ref):
    c = c_ref[...]
    a = c * jax.nn.sigmoid(c)
    o_ref[...] = jnp.dot(a, w_ref[...], preferred_element_type=f32) + b_ref[...]


def _adaln(c, w_ada, b_ada):
    bsz, d = c.shape
    n = w_ada.shape[1]
    rows = -(-bsz // 8) * 8
    c_pad = jnp.zeros((rows, d), f32).at[:bsz].set(c)
    tn = 512
    out = pl.pallas_call(
        _adaln_kernel,
        grid=(n // tn,),
        in_specs=[pl.BlockSpec((rows, d), lambda j: (0, 0)),
                  pl.BlockSpec((d, tn), lambda j: (0, j)),
                  pl.BlockSpec((1, tn), lambda j: (0, j))],
        out_specs=pl.BlockSpec((rows, tn), lambda j: (0, j)),
        out_shape=jax.ShapeDtypeStruct((rows, n), f32),
        name="adaln",
    )(c_pad, w_ada, b_ada.reshape(1, n))
    return out[:bsz]


def _proj_kernel(x_ref, pos_ref, sc_ref, sh_ref, g_ref, w_ref, ga_ref, gb_ref, gsum_ref, perm_ref, invf_ref,
                 qa_ref, ka_ref, va_ref, qi_ref, ki_ref, wi_ref, qb_ref, kb_ref, vb_ref):
    x = x_ref[0]
    ms = jnp.mean(x * x, axis=-1, keepdims=True)
    h = (x * lax.rsqrt(ms + EPS)) * g_ref[...]
    h = h * (1.0 + sc_ref[0]) + sh_ref[0]
    proj = jnp.dot(h.astype(bf16), w_ref[...], preferred_element_type=f32)

    ang = pos_ref[0] * invf_ref[...]
    cos = jnp.cos(ang)
    sin = jnp.sin(ang)
    gsum = gsum_ref[...]
    perm = perm_ref[...]
    heads = []
    for c in range(N_CHUNKS):
        a = proj[:, LANES * c:LANES * (c + 1)]
        b = proj[:, SLAB + LANES * c:SLAB + LANES * (c + 1)]
        if 4 * c < N_NORM_SLOTS:
            ss = a * a + b * b
            hi = ss.astype(bf16)
            lo = (ss - hi.astype(f32)).astype(bf16)
            tot = (jnp.dot(hi, gsum, preferred_element_type=f32)
                   + jnp.dot(lo, gsum, preferred_element_type=f32))
            inv = lax.rsqrt(tot * (1.0 / HEAD_DIM) + EPS)
            a = a * inv * ga_ref[:, LANES * c:LANES * (c + 1)]
            b = b * inv * gb_ref[:, LANES * c:LANES * (c + 1)]
        ra = a * cos - b * sin
        rb = b * cos + a * sin
        ab = jnp.concatenate([ra, rb], axis=1).astype(bf16)
        heads.append(jnp.dot(ab, perm, preferred_element_type=f32).astype(bf16))

    qa_ref[0, :, 0:256] = heads[0]
    qa_ref[0, :, 256:512] = heads[1]
    qa_ref[0, :, 512:640] = heads[2][:, 0:128]
    ka_ref[0] = heads[2][:, 128:256]
    qb_ref[0, :, 0:256] = heads[3]
    qb_ref[0, :, 256:384] = heads[4][:, 0:128]
    kb_ref[0, :, 0:128] = heads[4][:, 128:256]
    kb_ref[0, :, 128:384] = heads[5]
    qi_ref[0, :, 0:256] = heads[6]
    qi_ref[0, :, 256:512] = heads[7]
    ki_ref[0] = heads[8][:, 0:128]
    va_ref[0] = proj[:, _COL_VA:_COL_VB].astype(bf16)
    vb_ref[0] = proj[:, _COL_VB:_COL_WI].astype(bf16)
    wi_ref[0] = proj[:, _COL_WI:N_COL]


def _project(x, posf, sc, sh, g, w_perm, gain_a, gain_b):
    bsz, s, d = x.shape
    ts = min(256, s)
    gsum = jnp.asarray(_group_sum_matrix(), bf16)
    perm = jnp.asarray(_interleave_matrix(), bf16)
    inv = ROPE_THETA ** (-jnp.arange(HALF, dtype=f32) / HALF)
    invf = jnp.tile(inv, LANES // HALF)[None, :]
    wa = N_HEADS_A * HEAD_DIM
    wb = N_HEADS_B * HEAD_DIM
    wq = N_IDX_HEADS * IDX_DIM
    const = lambda b, i: (0, 0)
    tile = lambda b, i: (b, i, 0)
    per_b = lambda b, i: (b, 0, 0)
    out_shape = [jax.ShapeDtypeStruct((bsz, s, w), dt) for w, dt in
                 ((wa, bf16), (LANES, bf16), (LANES, bf16), (wq, bf16), (LANES, bf16), (LANES, f32),
                  (wb, bf16), (wb, bf16), (wb, bf16))]
    out_specs = [pl.BlockSpec((1, ts, sh_.shape[2]), tile) for sh_ in out_shape]
    return pl.pallas_call(
        _proj_kernel,
        grid=(bsz, s // ts),
        in_specs=[pl.BlockSpec((1, ts, d), tile),
                  pl.BlockSpec((1, ts, 1), tile),
                  pl.BlockSpec((1, 1, d), per_b),
                  pl.BlockSpec((1, 1, d), per_b),
                  pl.BlockSpec((1, d), const),
                  pl.BlockSpec((d, N_COL), const),
                  pl.BlockSpec((1, N_NORM_SLOTS * HALF), const),
                  pl.BlockSpec((1, N_NORM_SLOTS * HALF), const),
                  pl.BlockSpec((LANES, LANES), const),
                  pl.BlockSpec((2 * LANES, 2 * LANES), const),
                  pl.BlockSpec((1, LANES), const)],
        out_specs=out_specs,
        out_shape=out_shape,
        compiler_params=pltpu.CompilerParams(
            dimension_semantics=("arbitrary", "arbitrary"), vmem_limit_bytes=VMEM_LIMIT),
        name="in_proj",
    )(x, posf, sc, sh, g, w_perm, gain_a, gain_b, gsum, perm, invf)


def _sparse_attn_kernel(qa_ref, ka_ref, va_ref, qi_ref, ki_ref, wi_ref, triu_ref, o_ref,
                        keys_ref, qis_ref, qas_ref, wrep_ref, m_ref, acc_ref,
                        *, qb, tk, n_sel, w_scale):
    i = pl.program_id(1)
    n_pairs_a = N_HEADS_A // 2
    lane = lax.broadcasted_iota(i32, (qb, LANES), 1)
    left = lane < HEAD_DIM
    mask_l = left.astype(f32).astype(bf16)
    mask_r = (1.0 - left.astype(f32)).astype(bf16)

    for c in range(N_IDX_HEADS // 2):
        ch = qi_ref[0, :, LANES * c:LANES * (c + 1)]
        qis_ref[(2 * c) * qb:(2 * c + 1) * qb, :] = ch * mask_l
        qis_ref[(2 * c + 1) * qb:(2 * c + 2) * qb, :] = ch * mask_r
    q_scale = jnp.asarray(HEAD_DIM ** -0.5, bf16)
    for c in range(n_pairs_a):
        ch = qa_ref[0, :, LANES * c:LANES * (c + 1)] * q_scale
        qas_ref[0, c * qb:(c + 1) * qb, :] = ch * mask_l
        qas_ref[1, c * qb:(c + 1) * qb, :] = ch * mask_r
    wi = wi_ref[0] * w_scale
    for h in range(N_IDX_HEADS):
        wrep_ref[h] = jnp.broadcast_to(wi[:, h:h + 1], (qb, LANES))

    n_tiles = (i * qb) // tk + 1
    reps = tk // LANES
    row = lax.broadcasted_iota(i32, (qb, tk), 0) + i * qb
    col = lax.broadcasted_iota(i32, (qb, tk), 1)

    def score_body(j, rmax):
        start = pl.multiple_of(j * tk, tk)
        kt = ki_ref[0, pl.ds(start, tk), :]
        lg = _nt_dot(qis_ref[...], kt)
        acc = jnp.zeros((qb, tk), f32)
        for h in range(N_IDX_HEADS):
            wr = jnp.concatenate([wrep_ref[h]] * reps, axis=1)
            acc = acc + jnp.maximum(lg[h * qb:(h + 1) * qb], 0.0) * wr
        acc = jnp.where(col + j * tk <= row, acc, -jnp.inf)
        keys_ref[j] = acc
        for r in range(reps):
            rmax = jnp.maximum(rmax, acc[:, LANES * r:LANES * (r + 1)])
        return rmax

    rmax = lax.fori_loop(0, n_tiles, score_body, jnp.full((qb, LANES), -jnp.inf, f32))
    rmax = jnp.broadcast_to(jnp.max(rmax, axis=1, keepdims=True), (qb, LANES))

    def key_to_float(key):
        return lax.bitcast_convert_type(key ^ ((key >> 31) & 0x7FFFFFFF), f32)

    def float_to_key(v):
        bits = lax.bitcast_convert_type(v, i32)
        return bits ^ ((bits >> 31) & 0x7FFFFFFF)

    sweep_rows = min(qb, LANES)
    ones8 = jnp.ones((8, LANES), bf16)

    def to_rows(x_lane):
        return jnp.transpose(jnp.broadcast_to(x_lane[0:1, :], (LANES, qb)))

    def to_lanes(cnt):
        return _nt_dot(ones8, cnt.astype(bf16))

    def count_ge(trial_lane):
        trial_rows = to_rows(trial_lane)
        parts = []
        for r0 in range(0, qb, sweep_rows):
            trial_r = trial_rows[r0:r0 + sweep_rows]

            def body(j, cnt, r0=r0, trial_r=trial_r):
                for r in range(reps):
                    ch = keys_ref[j, r0:r0 + sweep_rows, LANES * r:LANES * (r + 1)]
                    cnt = cnt + jnp.where(ch >= trial_r, 1.0, 0.0)
                return cnt
            parts.append(lax.fori_loop(0, n_tiles, body, jnp.zeros((sweep_rows, LANES), f32)))
        return to_lanes(jnp.concatenate(parts, axis=0))

    log_target = math.log(n_sel - 0.5)
    rmax = jnp.transpose(rmax)[0:8, :]
    rmax_pad = rmax + jnp.abs(rmax) * 2.0 ** -20 + 1e-30

    def zero_counts(r0):
        def body(j, cnts):
            ge, gt = cnts
            for r in range(reps):
                ch = keys_ref[j, r0:r0 + sweep_rows, LANES * r:LANES * (r + 1)]
                ge = ge + jnp.where(ch >= 0.0, 1.0, 0.0)
                gt = gt + jnp.where(ch > 0.0, 1.0, 0.0)
            return ge, gt
        z = jnp.zeros((sweep_rows, LANES), f32)
        ge, gt = lax.fori_loop(0, n_tiles, body, (z, z))
        return to_lanes(ge), to_lanes(gt)

    zero_parts = [zero_counts(r0) for r0 in range(0, qb, sweep_rows)]
    ge0 = jnp.concatenate([p[0] for p in zero_parts], axis=1)
    gt0 = jnp.concatenate([p[1] for p in zero_parts], axis=1)
    zeros = jnp.zeros((8, qb), f32)
    total = zeros + (n_tiles * tk).astype(f32)
    above = ge0 >= n_sel
    lo_v0 = jnp.where(above, 0.0, -jnp.inf)
    lo_c0 = jnp.where(above, ge0, total)
    hi_v0 = jnp.where(above, jnp.inf, 0.0)
    hi_c0 = jnp.where(above, jnp.where(gt0 < n_sel, gt0, 0.0), ge0)
    done0 = (above & (gt0 < n_sel)) | (lo_c0 == n_sel)

    def search_cond(state):
        return (state[0] < SEARCH_MAX_STEPS) & state[-1]

    def search_body(state):
        it, lo_v, lo_c, f_lo, hi_v, hi_c, f_hi, last, done, _ = state
        lo_k = float_to_key(lo_v)
        hi_k = float_to_key(hi_v)
        hi_eff = jnp.where(hi_v == jnp.inf, rmax_pad, hi_v)
        t_int = lo_v + (hi_eff - lo_v) * (f_lo / (f_lo - f_hi))
        t_int = key_to_float(float_to_key(t_int))
        use_int = (lo_v > -jnp.inf) & (t_int > lo_v) & (t_int < hi_v) & (it < SEARCH_INTERP_STEPS)
        mid_k = (lo_k >> 1) + (hi_k >> 1) + (lo_k & hi_k & 1)
        t = jnp.where(use_int, t_int, key_to_float(mid_k))
        c = count_ge(t)
        f = jnp.log(jnp.maximum(c, 0.5)) - log_target
        active = done < 0.5
        is_lo = (c >= n_sel) & active
        is_hi = (c < n_sel) & active
        f_hi = jnp.where(is_lo & (last > 0.0), f_hi * 0.5, f_hi)
        f_lo = jnp.where(is_hi & (last < 0.0), f_lo * 0.5, f_lo)
        lo_v = jnp.where(is_lo, t, lo_v)
        lo_c = jnp.where(is_lo, c, lo_c)
        f_lo = jnp.where(is_lo, f, f_lo)
        hi_v = jnp.where(is_hi, t, hi_v)
        hi_c = jnp.where(is_hi, c, hi_c)
        f_hi = jnp.where(is_hi, f, f_hi)
        last = jnp.where(is_lo, 1.0, jnp.where(is_hi, -1.0, last))
        settled = (lo_c == n_sel) | (float_to_key(lo_v) + 1 >= float_to_key(hi_v))
        done = jnp.where(settled, 1.0, done)
        unresolved = jnp.min(done) < 0.5
        return it + 1, lo_v, lo_c, f_lo, hi_v, hi_c, f_hi, last, done, unresolved

    done0 = jnp.where(done0, 1.0, 0.0)
    init = (jnp.int32(0), lo_v0, lo_c0, jnp.log(lo_c0) - log_target,
            hi_v0, hi_c0, jnp.log(jnp.maximum(hi_c0, 0.5)) - log_target,
            zeros, done0, jnp.min(done0) < 0.5)
    final = lax.while_loop(search_cond, search_body, init)
    thr, lo_c, hi_c = final[1], final[2], final[5]
    thr_t = jnp.concatenate([to_rows(thr)] * reps, axis=1)
    tied = lo_c != n_sel
    any_tied = jnp.max(jnp.where(tied, 1.0, 0.0)) > 0.0
    quota = to_rows(jnp.where(tied, n_sel - hi_c, 2.0 * tk * (n_tiles + 1).astype(f32)))
    quota_t = jnp.concatenate([quota] * reps, axis=1)

    m_ref[...] = jnp.full(m_ref.shape, NEG_BIG, f32)
    acc_ref[...] = jnp.zeros(acc_ref.shape, f32)
    lane_k = lax.broadcasted_iota(i32, (tk, LANES), 1)
    kmask_l = (lane_k < HEAD_DIM).astype(f32).astype(bf16)
    kmask_r = (lane_k >= HEAD_DIM).astype(f32).astype(bf16)

    def attn_body(j, n_ties, with_ties):
        start = pl.multiple_of(j * tk, tk)
        sc = keys_ref[j]
        if with_ties:
            eq = jnp.where(sc == thr_t, 1.0, 0.0)
            before = (jnp.dot(eq.astype(bf16), triu_ref[...], preferred_element_type=f32)
                      + jnp.concatenate([n_ties] * reps, axis=1))
            sel = (sc > thr_t) | ((sc == thr_t) & (before < quota_t))
            n_ties = n_ties + jnp.broadcast_to(jnp.sum(eq, axis=1, keepdims=True), (qb, LANES))
        else:
            sel = sc >= thr_t
        sel = sel & (col + j * tk <= row)
        bias = jnp.where(sel, 0.0, NEG_BIG)
        bias = jnp.concatenate([bias] * n_pairs_a, axis=0)
        kk = ka_ref[0, pl.ds(start, tk), :]
        vv = va_ref[0, pl.ds(start, tk), :]
        v_ext = (vv * kmask_l + kmask_r, vv * kmask_r + kmask_l)
        for g in range(N_KV_A):
            s = _nt_dot(qas_ref[g], kk) + bias
            m_old = m_ref[g]
            m_new = jnp.maximum(m_old, jnp.max(s, axis=1, keepdims=True))
            alpha = jnp.exp(m_old - m_new)
            p = jnp.exp(s - jnp.concatenate([m_new] * reps, axis=1))
            acc_ref[g] = alpha * acc_ref[g] + jnp.dot(p.astype(bf16), v_ext[g], preferred_element_type=f32)
            m_ref[g] = m_new
        return n_ties

    no_ties = jnp.zeros((qb, LANES), f32)

    @pl.when(any_tied)
    def _():
        lax.fori_loop(0, n_tiles, functools.partial(attn_body, with_ties=True), no_ties)

    @pl.when(jnp.logical_not(any_tied))
    def _():
        lax.fori_loop(0, n_tiles, functools.partial(attn_body, with_ties=False), no_ties)

    for c in range(n_pairs_a):
        rows = slice(c * qb, (c + 1) * qb)
        a0 = acc_ref[0, rows, :]
        a1 = acc_ref[1, rows, :]
        o0 = a0 / pltpu.roll(a0, HEAD_DIM, axis=1)
        o1 = a1 / pltpu.roll(a1, HEAD_DIM, axis=1)
        o_ref[0, :, LANES * c:LANES * (c + 1)] = jnp.where(left, o0, o1).astype(bf16)


def _sparse_attention(qa, ka, va, qi, ki, wi):
    bsz, s, wa = qa.shape
    qb = min(256, s)
    tk = min(512, s)
    n_sel = min(TOPK_MAX, s // 4)
    w_scale = N_IDX_HEADS ** -0.5 * IDX_DIM ** -0.5
    n_pairs_a = N_HEADS_A // 2
    tile = lambda b, i: (b, i, 0)
    per_b = lambda b, i: (b, 0, 0)
    kern = functools.partial(_sparse_attn_kernel, qb=qb, tk=tk, n_sel=float(n_sel), w_scale=w_scale)
    return pl.pallas_call(
        kern,
        grid=(bsz, s // qb),
        in_specs=[pl.BlockSpec((1, qb, wa), tile),
                  pl.BlockSpec((1, s, LANES), per_b),
                  pl.BlockSpec((1, s, LANES), per_b),
                  pl.BlockSpec((1, qb, qi.shape[2]), tile),
                  pl.BlockSpec((1, s, LANES), per_b),
                  pl.BlockSpec((1, qb, LANES), tile),
                  pl.BlockSpec((tk, tk), lambda b, i: (0, 0))],
        out_specs=pl.BlockSpec((1, qb, wa), tile),
        out_shape=jax.ShapeDtypeStruct((bsz, s, wa), bf16),
        scratch_shapes=[pltpu.VMEM((s // tk, qb, tk), f32),
                        pltpu.VMEM((N_IDX_HEADS * qb, LANES), bf16),
                        pltpu.VMEM((N_KV_A, n_pairs_a * qb, LANES), bf16),
                        pltpu.VMEM((N_IDX_HEADS, qb, LANES), f32),
                        pltpu.VMEM((N_KV_A, n_pairs_a * qb, LANES), f32),
                        pltpu.VMEM((N_KV_A, n_pairs_a * qb, LANES), f32)],
        compiler_params=pltpu.CompilerParams(
            dimension_semantics=("arbitrary", "arbitrary"), vmem_limit_bytes=VMEM_LIMIT),
        name="sparse_attn",
    )(qa, ka, va, qi, ki, wi, jnp.asarray(np.triu(np.ones((tk, tk), np.float32), k=1), bf16))


def _dilated_kernel(q_ref, k_ref, v_ref, bias_ref, o_ref, qs_ref, m_ref, acc_ref, *, tq, nd):
    i = pl.program_id(2)
    lane = lax.broadcasted_iota(i32, (tq, LANES), 1)
    left = lane < HEAD_DIM
    mask_l = left.astype(f32).astype(bf16)
    mask_r = (1.0 - left.astype(f32)).astype(bf16)
    q = q_ref[0] * jnp.asarray(HEAD_DIM ** -0.5, bf16)
    qs_ref[0:tq, :] = q * mask_l
    qs_ref[tq:2 * tq, :] = q * mask_r
    m_ref[...] = jnp.full(m_ref.shape, NEG_BIG, f32)
    acc_ref[...] = jnp.zeros(acc_ref.shape, f32)
    ones = jnp.ones((tq, LANES), bf16)

    def body(j, carry):
        start = pl.multiple_of(j * tq, tq)
        kk = k_ref[0, pl.ds(start, tq), :]
        v_ext = jnp.concatenate([v_ref[0, pl.ds(start, tq), :], ones], axis=1)
        b = bias_ref[i - j]
        s = _nt_dot(qs_ref[...], kk) + jnp.concatenate([b, b], axis=0)
        m_old = m_ref[...]
        m_new = jnp.maximum(m_old, jnp.max(s, axis=1, keepdims=True))
        alpha = jnp.exp(m_old - m_new)
        p = jnp.exp(s - jnp.concatenate([m_new] * (tq // LANES), axis=1))
        pv = jnp.dot(p.astype(bf16), v_ext, preferred_element_type=f32)
        acc_ref[...] = jnp.concatenate([alpha, alpha], axis=1) * acc_ref[...] + pv
        m_ref[...] = m_new
        return carry

    lax.fori_loop(jnp.maximum(i - (nd - 1), 0), i + 1, body, 0)
    o0 = acc_ref[0:tq, 0:LANES] / acc_ref[0:tq, LANES:2 * LANES]
    o1 = acc_ref[tq:2 * tq, 0:LANES] / acc_ref[tq:2 * tq, LANES:2 * LANES]
    o_ref[0] = jnp.where(left, o0, o1).astype(bf16)


def _dilated_attention(qb_, kb_, vb_):
    bsz, s, wb = qb_.shape
    tq = min(512, s)
    bias_np, nd = _dilated_bias(tq)
    bias = jnp.asarray(bias_np)
    n_pairs = wb // LANES
    kern = functools.partial(_dilated_kernel, tq=tq, nd=nd)
    return pl.pallas_call(
        kern,
        grid=(bsz, n_pairs, s // tq),
        in_specs=[pl.BlockSpec((1, tq, LANES), lambda b, p, i: (b, i, p)),
                  pl.BlockSpec((1, s, LANES), lambda b, p, i: (b, 0, p)),
                  pl.BlockSpec((1, s, LANES), lambda b, p, i: (b, 0, p)),
                  pl.BlockSpec((nd, tq, tq), lambda b, p, i: (0, 0, 0))],
        out_specs=pl.BlockSpec((1, tq, LANES), lambda b, p, i: (b, i, p)),
        out_shape=jax.ShapeDtypeStruct((bsz, s, wb), bf16),
        scratch_shapes=[pltpu.VMEM((2 * tq, LANES), bf16),
                        pltpu.VMEM((2 * tq, LANES), f32),
                        pltpu.VMEM((2 * tq, 2 * LANES), f32)],
        compiler_params=pltpu.CompilerParams(
            dimension_semantics=("arbitrary", "arbitrary", "arbitrary"), vmem_limit_bytes=VMEM_LIMIT),
        name="dilated_attn",
    )(qb_, kb_, vb_, bias)


def _mid_kernel(oa_ref, ob_ref, x_ref, gta_ref, scf_ref, shf_ref, gtf_ref, woa_ref, wob_ref, gffn_ref,
                wsgu_ref, wsd_ref, wrt_ref, rbias_ref, triu_ref,
                base_ref, h2_ref, sel_ref, gate_ref, rank_ref, cnt_ref, carry_ref, *, n_exp, t, d_sh):
    step = pl.program_id(0)

    @pl.when(step == 0)
    def _():
        carry_ref[...] = jnp.zeros(carry_ref.shape, f32)

    mix = (jnp.dot(oa_ref[...], woa_ref[...], preferred_element_type=f32)
           + jnp.dot(ob_ref[...], wob_ref[...], preferred_element_type=f32))
    x1 = x_ref[...] + gta_ref[0] * mix
    ms = jnp.mean(x1 * x1, axis=-1, keepdims=True)
    h2 = ((x1 * lax.rsqrt(ms + EPS)) * gffn_ref[...]) * (1.0 + scf_ref[0]) + shf_ref[0]
    h2b = h2.astype(bf16)
    h2f = h2b.astype(f32)
    n_sub = h2_ref.shape[0] // t
    for a in range(n_sub):
        h2_ref[pl.ds(a, t, stride=n_sub), :] = h2f[:, LANES * a:LANES * (a + 1)]

    gu = jnp.dot(h2b, wsgu_ref[...], preferred_element_type=f32)
    g = gu[:, :d_sh]
    u = gu[:, d_sh:]
    act = (g * jax.nn.sigmoid(g)) * u
    shared = jnp.dot(act.astype(bf16), wsd_ref[...], preferred_element_type=f32)
    base = x1 + gtf_ref[0] * shared
    for a in range(n_sub):
        base_ref[pl.ds(a, t, stride=n_sub), :] = base[:, LANES * a:LANES * (a + 1)]

    scores = jax.nn.sigmoid(_nt_dot(wrt_ref[...], h2b))
    biased = scores + rbias_ref[...]
    per = n_exp // N_GROUPS
    neg_inf = jnp.float32(-jnp.inf)
    ri_g = lax.broadcasted_iota(i32, (per, t), 0).astype(f32)
    gs = []
    for grp in range(N_GROUPS):
        blk = biased[grp * per:(grp + 1) * per]
        m1 = jnp.max(blk, axis=0, keepdims=True)
        idx1 = jnp.min(jnp.where(blk == m1, ri_g, float(per)), axis=0, keepdims=True)
        m2 = jnp.max(jnp.where(ri_g == idx1, neg_inf, blk), axis=0, keepdims=True)
        gs.append(m1 + m2)
    masked_rows = []
    for grp in range(N_GROUPS):
        beaten = jnp.zeros((1, t), f32)
        for g2 in range(N_GROUPS):
            if g2 == grp:
                continue
            wins = gs[g2] > gs[grp]
            if g2 < grp:
                wins = wins | (gs[g2] == gs[grp])
            beaten = beaten + jnp.where(wins, 1.0, 0.0)
        keep = jnp.broadcast_to(beaten < TOPK_GROUPS, (per, t))
        masked_rows.append(jnp.where(keep, biased[grp * per:(grp + 1) * per], neg_inf))
    masked = jnp.concatenate(masked_rows, axis=0)

    ri = lax.broadcasted_iota(i32, (n_exp, t), 0).astype(f32)
    selmask = jnp.zeros((n_exp, t), f32)
    idxs, graw = [], []
    for _ in range(TOP_K):
        m = jnp.max(masked, axis=0, keepdims=True)
        idx = jnp.min(jnp.where(masked == m, ri, float(n_exp)), axis=0, keepdims=True)
        onehot = ri == idx
        graw.append(jnp.sum(jnp.where(onehot, scores, 0.0), axis=0, keepdims=True))
        masked = jnp.where(onehot, neg_inf, masked)
        selmask = jnp.where(onehot, 1.0, selmask)
        idxs.append(idx)
    den = graw[0]
    for k in range(1, TOP_K):
        den = den + graw[k]

    prefix = jnp.dot(selmask.astype(bf16), triu_ref[...], preferred_element_type=f32)
    prefix = prefix + jnp.concatenate([carry_ref[...]] * (t // LANES), axis=1)
    for k in range(TOP_K):
        rank_k = jnp.sum(jnp.where(ri == idxs[k], prefix, 0.0), axis=0, keepdims=True)
        sel_ref[k:k + 1, :] = idxs[k].astype(i32)
        rank_ref[k:k + 1, :] = rank_k.astype(i32)
        gate_ref[k:k + 1, :] = graw[k] / den * ROUTED_SCALE
    carry_ref[...] = carry_ref[...] + jnp.broadcast_to(
        jnp.sum(selmask, axis=1, keepdims=True), carry_ref.shape)
    cnt_ref[...] = carry_ref[...]


def _mid(oa, ob, x2, gta, scf, shf, gtf, woa, wob, gffn, wsgu, wsd, wrt, rbias_rep, s):
    n, d = x2.shape
    t = rbias_rep.shape[1]
    n_exp = wrt.shape[0]
    d_sh = wsd.shape[0]
    triu = jnp.asarray(np.triu(np.ones((t, t), np.float32), k=1), bf16)
    tile = lambda i: (i, 0)
    const = lambda i: (0, 0)
    per_b = lambda i: ((i * t) // s, 0, 0)
    lane_tile = lambda i: (0, i)
    n_sub = d // LANES
    tok_tiles = jax.ShapeDtypeStruct((n * n_sub, LANES), f32)
    tok_spec = pl.BlockSpec((t * n_sub, LANES), tile)
    out_shape = [tok_tiles, tok_tiles,
                 jax.ShapeDtypeStruct((TOP_K, n), i32), jax.ShapeDtypeStruct((TOP_K, n), f32),
                 jax.ShapeDtypeStruct((TOP_K, n), i32), jax.ShapeDtypeStruct((n_exp, LANES), f32)]
    out_specs = [tok_spec, tok_spec,
                 pl.BlockSpec((TOP_K, t), lane_tile), pl.BlockSpec((TOP_K, t), lane_tile),
                 pl.BlockSpec((TOP_K, t), lane_tile), pl.BlockSpec((n_exp, LANES), const)]
    kern = functools.partial(_mid_kernel, n_exp=n_exp, t=t, d_sh=d_sh)
    return pl.pallas_call(
        kern,
        grid=(n // t,),
        in_specs=[pl.BlockSpec((t, oa.shape[1]), tile),
                  pl.BlockSpec((t, ob.shape[1]), tile),
                  pl.BlockSpec((t, d), tile),
                  pl.BlockSpec((1, 1, d), per_b),
                  pl.BlockSpec((1, 1, d), per_b),
                  pl.BlockSpec((1, 1, d), per_b),
                  pl.BlockSpec((1, 1, d), per_b),
                  pl.BlockSpec(woa.shape, const),
                  pl.BlockSpec(wob.shape, const),
                  pl.BlockSpec((1, d), const),
                  pl.BlockSpec(wsgu.shape, const),
                  pl.BlockSpec(wsd.shape, const),
                  pl.BlockSpec(wrt.shape, const),
                  pl.BlockSpec(rbias_rep.shape, const),
                  pl.BlockSpec((t, t), const)],
        out_specs=out_specs,
        out_shape=out_shape,
        scratch_shapes=[pltpu.VMEM((n_exp, LANES), f32)],
        compiler_params=pltpu.CompilerParams(
            dimension_semantics=("arbitrary",), vmem_limit_bytes=VMEM_LIMIT),
        name="mid",
    )(oa, ob, x2, gta, scf, shf, gtf, woa, wob, gffn, wsgu, wsd, wrt, rbias_rep, triu)


def _dispatch_kernel(zb_ref, slot_ref, h_ref, xs_ref, zero_ref, sem_ref, *, td, n_zero):
    step = pl.program_id(0)

    def zero_copy(b):
        return pltpu.make_async_copy(zero_ref, xs_ref.at[pl.ds(zb_ref[b], ZERO_ROWS)], sem_ref.at[0])

    @pl.when(step == 0)
    def _():
        zero_ref[...] = jnp.zeros(zero_ref.shape, f32)

        def start(b, c):
            @pl.when(zb_ref[b] >= 0)
            def _():
                zero_copy(b).start()
            return c

        def wait(b, c):
            @pl.when(zb_ref[b] >= 0)
            def _():
                zero_copy(b).wait()
            return c

        lax.fori_loop(0, n_zero, start, 0)
        lax.fori_loop(0, n_zero, wait, 0)

    def start_rows(tok, c):
        for k in range(TOP_K):
            pltpu.make_async_copy(h_ref.at[pl.ds(tok, 1)], xs_ref.at[pl.ds(slot_ref[k, tok], 1)],
                                  sem_ref.at[1]).start(priority=k % 2)
        return c

    lax.fori_loop(0, td, start_rows, 0)
    for k in range(TOP_K):
        pltpu.make_async_copy(h_ref, xs_ref.at[pl.ds(0, td)], sem_ref.at[1]).wait()


def _dispatch(zero_start, slot, h2, n_slots):
    n, rows, lanes = h2.shape
    td = min(256, n)
    kern = functools.partial(_dispatch_kernel, td=td, n_zero=zero_start.shape[0])
    grid_spec = pltpu.PrefetchScalarGridSpec(
        num_scalar_prefetch=1,
        grid=(n // td,),
        in_specs=[pl.BlockSpec((TOP_K, td), lambda i, tail: (0, i), memory_space=pltpu.SMEM),
                  pl.BlockSpec((td, rows, lanes), lambda i, tail: (i, 0, 0))],
        out_specs=pl.BlockSpec(memory_space=pl.ANY),
        scratch_shapes=[pltpu.VMEM((ZERO_ROWS, rows, lanes), f32), pltpu.SemaphoreType.DMA((2,))],
    )
    return pl.pallas_call(
        kern,
        grid_spec=grid_spec,
        out_shape=jax.ShapeDtypeStruct((n_slots, rows, lanes), f32),
        compiler_params=pltpu.CompilerParams(
            dimension_semantics=("arbitrary",), vmem_limit_bytes=VMEM_LIMIT),
        name="dispatch",
    )(zero_start, slot, h2)


def _expert_kernel(be_ref, nu_ref, xs_ref, wg_ref, wu_ref, wd_ref, ys_ref, wgu_s, wd_s, *, n_sub):
    i = pl.program_id(0)
    f = wd_s.shape[0]
    rows = xs_ref.shape[0] // n_sub

    @pl.when(i >= nu_ref[0])
    def _():
        ys_ref[...] = jnp.zeros(ys_ref.shape, f32)

    @pl.when(i < nu_ref[0])
    def _():
        @pl.when((i == 0) | (be_ref[i] != be_ref[jnp.maximum(i - 1, 0)]))
        def _():
            wgu_s[:, 0:f] = wg_ref[0].astype(bf16)
            wgu_s[:, f:2 * f] = wu_ref[0].astype(bf16)
            wd_s[...] = wd_ref[0].astype(bf16)

        xb = jnp.concatenate([xs_ref[pl.ds(a, rows, stride=n_sub), :] for a in range(n_sub)],
                             axis=1).astype(bf16)
        gu = jnp.dot(xb, wgu_s[...], preferred_element_type=f32)
        g = gu[:, 0:f]
        u = gu[:, f:2 * f]
        act = (g * jax.nn.sigmoid(g)) * u
        y = jnp.dot(act.astype(bf16), wd_s[...], preferred_element_type=f32)
        for a in range(n_sub):
            ys_ref[pl.ds(a, rows, stride=n_sub), :] = y[:, LANES * a:LANES * (a + 1)]


def _experts(block_expert, n_used, xs, w_gate, w_up, w_down):
    n_slots, n_sub, lanes = xs.shape
    d = n_sub * lanes
    n_blocks = n_slots // MOE_BLOCK
    f = w_gate.shape[2]
    blk = lambda i, be, nu: (jnp.minimum(i, nu[0] - 1), 0)
    wsel = lambda i, be, nu: (be[jnp.minimum(i, nu[0] - 1)], 0, 0)
    grid_spec = pltpu.PrefetchScalarGridSpec(
        num_scalar_prefetch=2,
        grid=(n_blocks,),
        in_specs=[pl.BlockSpec((MOE_BLOCK * n_sub, lanes), blk),
                  pl.BlockSpec((1, d, f), wsel),
                  pl.BlockSpec((1, d, f), wsel),
                  pl.BlockSpec((1, f, d), wsel)],
        out_specs=pl.BlockSpec((MOE_BLOCK * n_sub, lanes), lambda i, be, nu: (i, 0)),
        scratch_shapes=[pltpu.VMEM((d, 2 * f), bf16), pltpu.VMEM((f, d), bf16)],
    )
    ys = pl.pallas_call(
        functools.partial(_expert_kernel, n_sub=n_sub),
        grid_spec=grid_spec,
        out_shape=jax.ShapeDtypeStruct((n_slots * n_sub, lanes), f32),
        compiler_params=pltpu.CompilerParams(
            dimension_semantics=("arbitrary",), vmem_limit_bytes=VMEM_LIMIT),
        name="experts",
    )(block_expert, n_used, xs.reshape(n_slots * n_sub, lanes), w_gate, w_up, w_down)
    return ys.reshape(n_slots, n_sub, lanes)


def _combine_kernel(slot_ref, nslot_ref, gate_ref, base_ref, gtf_ref, ys_ref, o_ref, buf_ref, res_ref, sem_ref,
                    *, tc):
    step = pl.program_id(0)
    cur = step % 2

    def start_gathers(idx_ref, b):
        def body(tok, c):
            for k in range(TOP_K):
                pltpu.make_async_copy(ys_ref.at[pl.ds(idx_ref[k, tok], 1)],
                                      buf_ref.at[b, k, pl.ds(tok, 1)], sem_ref.at[b]).start(priority=k % 2)
            return c
        lax.fori_loop(0, tc, body, 0)

    @pl.when(step == 0)
    def _():
        start_gathers(slot_ref, 0)

    @pl.when(step + 1 < pl.num_programs(0))
    def _():
        start_gathers(nslot_ref, 1 - cur)

    for k in range(TOP_K):
        pltpu.make_async_copy(ys_ref.at[pl.ds(0, tc)], buf_ref.at[cur, k], sem_ref.at[cur]).wait()

    gtf = gtf_ref[0]

    def reduce_token(tok, c):
        routed = buf_ref[cur, 0, tok] * gate_ref[0, tok]
        for k in range(1, TOP_K):
            routed = routed + buf_ref[cur, k, tok] * gate_ref[k, tok]
        res_ref[pl.ds(pl.multiple_of(tok * n_sub, n_sub), n_sub), :] = base_ref[tok] + gtf * routed
        return c

    n_sub = buf_ref.shape[3]
    lax.fori_loop(0, tc, reduce_token, 0, unroll=4)
    for a in range(n_sub):
        o_ref[:, LANES * a:LANES * (a + 1)] = res_ref[pl.ds(a, tc, stride=n_sub), :]


def _combine(slot, gate, base, gtf, ys, s):
    n, rows, lanes = base.shape
    tc = min(128, n)
    n_steps = n // tc
    kern = functools.partial(_combine_kernel, tc=tc)
    tok_tile = lambda i: (i, 0, 0)
    return pl.pallas_call(
        kern,
        grid=(n_steps,),
        in_specs=[pl.BlockSpec((TOP_K, tc), lambda i: (0, i), memory_space=pltpu.SMEM),
                  pl.BlockSpec((TOP_K, tc), lambda i: (0, jnp.minimum(i + 1, n_steps - 1)),
                               memory_space=pltpu.SMEM),
                  pl.BlockSpec((TOP_K, tc), lambda i: (0, i), memory_space=pltpu.SMEM),
                  pl.BlockSpec((tc, rows, lanes), tok_tile),
                  pl.BlockSpec((1, rows, lanes), lambda i: ((i * tc) // s, 0, 0)),
                  pl.BlockSpec(memory_space=pl.ANY)],
        out_specs=pl.BlockSpec((tc, rows * lanes), lambda i: (i, 0)),
        out_shape=jax.ShapeDtypeStruct((n, rows * lanes), f32),
        scratch_shapes=[pltpu.VMEM((2, TOP_K, tc, rows, lanes), f32), pltpu.VMEM((tc * rows, lanes), f32),
                        pltpu.SemaphoreType.DMA((2,))],
        compiler_params=pltpu.CompilerParams(
            dimension_semantics=("arbitrary",), vmem_limit_bytes=VMEM_LIMIT),
        name="combine",
    )(slot, slot, gate, base, gtf, ys)


def _layer(x, c, posf, w_ada, b_ada, g_mix, w_in, q_norm_a, k_norm_a, q_norm_b, k_norm_b, w_out, g_ffn,
           w_router, router_bias, w_gate, w_up, w_down, ws_gate, ws_up, ws_down):
    bsz, s, d = x.shape
    n = bsz * s
    n_exp = w_router.shape[1]

    mod = _adaln(c, w_ada, b_ada)[:, None, :]
    sh_a, sc_a, gt_a, sh_f, sc_f, gt_f = (mod[..., k * d:(k + 1) * d] for k in range(6))

    w_perm = _take_runs(w_in.astype(bf16), _projection_columns(), axis=1)
    gains = {"qa": q_norm_a, "ka": k_norm_a, "qb": q_norm_b, "kb": k_norm_b}
    gain_a = jnp.concatenate([gains[kind][:HALF] for kind, _ in _SLOTS[:N_NORM_SLOTS]])[None, :].astype(f32)
    gain_b = jnp.concatenate([gains[kind][HALF:] for kind, _ in _SLOTS[:N_NORM_SLOTS]])[None, :].astype(f32)

    qa, ka, va, qi, ki, wi, qb_, kb_, vb_ = _project(x, posf, sc_a, sh_a, g_mix[None, :], w_perm, gain_a, gain_b)
    o_a = _sparse_attention(qa, ka, va, qi, ki, wi)
    o_b = _dilated_attention(qb_, kb_, vb_)

    wa = N_HEADS_A * HEAD_DIM
    rows_a = np.concatenate([np.arange(h * HEAD_DIM, (h + 1) * HEAD_DIM) for h in QA_PAIR_ORDER])
    woa = _take_runs(w_out, rows_a, axis=0).astype(bf16)
    wob = w_out[wa:].astype(bf16)
    wsgu = jnp.concatenate([ws_gate, ws_up], axis=1).astype(bf16)
    wsd = ws_down.astype(bf16)
    wrt = w_router.T.astype(bf16)
    t_mid = min(256, n)
    rbias_rep = jnp.broadcast_to(router_bias.astype(f32)[:, None], (n_exp, t_mid))

    base, h2, sel, gate, rank, cnt = _mid(
        o_a.reshape(n, wa), o_b.reshape(n, -1), x.reshape(n, d), gt_a, sc_f, sh_f, gt_f,
        woa, wob, g_ffn[None, :], wsgu, wsd, wrt, rbias_rep, s)
    base = base.reshape(n, d // LANES, LANES)
    h2 = h2.reshape(n, d // LANES, LANES)

    counts = cnt[:, 0].astype(i32)
    padded = (counts + MOE_BLOCK - 1) // MOE_BLOCK * MOE_BLOCK
    pad_end = jnp.cumsum(padded)
    pad_start = pad_end - padded
    onehot = sel[:, :, None] == jnp.arange(n_exp, dtype=i32)[None, None, :]
    slot = jnp.sum(jnp.where(onehot, pad_start[None, None, :], 0), axis=-1) + rank
    n_blocks = -(-(n * TOP_K) // MOE_BLOCK) + n_exp
    n_slots = n_blocks * MOE_BLOCK
    block_start = jnp.arange(n_blocks, dtype=i32) * MOE_BLOCK
    block_expert = jnp.sum((pad_end[None, :] <= block_start[:, None]).astype(i32), axis=1)
    block_expert = jnp.minimum(block_expert, n_exp - 1)
    n_used = (pad_end[-1] // MOE_BLOCK).astype(i32)[None]
    unit_start = jnp.arange(n_slots // ZERO_ROWS, dtype=i32) * ZERO_ROWS
    unit_expert = jnp.sum((pad_end[None, :] <= unit_start[:, None]).astype(i32), axis=1)
    real_end = pad_start + counts
    unit_real_end = jnp.sum(jnp.where(unit_expert[:, None] == jnp.arange(n_exp, dtype=i32)[None, :],
                                      real_end[None, :], 0), axis=1)
    has_pad = (unit_start + ZERO_ROWS > unit_real_end) | (unit_start >= pad_end[-1])
    zero_start = jnp.where(has_pad, unit_start, -1).astype(i32)

    xs = _dispatch(zero_start, slot, h2, n_slots)
    ys = _experts(block_expert, n_used, xs, w_gate, w_up, w_down)
    out = _combine(slot, gate, base, gt_f.reshape(bsz, d // LANES, LANES), ys, s)
    return out.reshape(bsz, s, d)


def kernel(x, c, positions, w_ada, b_ada, g_mix, w_in, q_norm_a, k_norm_a, q_norm_b, k_norm_b, w_out, g_ffn,
           w_router, router_bias, w_gate, w_up, w_down, ws_gate, ws_up, ws_down):
    posf = positions.astype(f32)[..., None]
    for l in range(w_ada.shape[0]):
        x = _layer(x, c, posf, w_ada[l], b_ada[l], g_mix[l], w_in[l], q_norm_a[l], k_norm_a[l], q_norm_b[l],
                   k_norm_b[l], w_out[l], g_ffn[l], w_router[l], router_bias[l], w_gate[l], w_up[l], w_down[l],
                   ws_gate[l], ws_up[l], ws_down[l])
    return x
```

```python
import functools
import math

import numpy as np
import jax
import jax.numpy as jnp
from jax import lax
from jax.experimental import pallas as pl
from jax.experimental.pallas import tpu as pltpu

f32 = jnp.float32
bf16 = jnp.bfloat16
i32 = jnp.int32

HEAD_DIM = 64
HALF = HEAD_DIM // 2
N_HEADS_A = 10
N_KV_A = 2
N_HEADS_B = 6
N_IDX_HEADS = 8
IDX_DIM = 64
TOPK_MAX = 256
DILATED_PATTERNS = ((128, 1), (512, 4), (2048, 16))
ROPE_THETA = 10000.0
EPS = 1e-6
TOP_K = 8
N_GROUPS = 8
TOPK_GROUPS = 4
ROUTED_SCALE = 2.5
MOE_BLOCK = 512
ZERO_ROWS = 64

LANES = 128
VMEM_LIMIT = 48 * 1024 * 1024

NEG_BIG = -1e30
KEY_NEG_INF = int(np.int32(np.uint32(0xFF800000) ^ np.uint32(0x7FFFFFFF)))
KEY_POS_INF = 0x7F800000
SEARCH_INTERP_STEPS = 16
SEARCH_MAX_STEPS = SEARCH_INTERP_STEPS + 34

_OFF_QA = 0
_OFF_KA = _OFF_QA + N_HEADS_A * HEAD_DIM
_OFF_VA = _OFF_KA + N_KV_A * HEAD_DIM
_OFF_QI = _OFF_VA + N_KV_A * HEAD_DIM
_OFF_KI = _OFF_QI + N_IDX_HEADS * IDX_DIM
_OFF_WI = _OFF_KI + IDX_DIM
_OFF_QB = _OFF_WI + N_IDX_HEADS
_OFF_KB = _OFF_QB + N_HEADS_B * HEAD_DIM
_OFF_VB = _OFF_KB + N_HEADS_B * HEAD_DIM
D_IN = _OFF_VB + N_HEADS_B * HEAD_DIM

QA_PAIR_ORDER = (0, 5, 1, 6, 2, 7, 3, 8, 4, 9)

_SLOTS = (
    [("qa", h) for h in QA_PAIR_ORDER[:8]] + [("qa", 4), ("qa", 9), ("ka", 0), ("ka", 1)]
    + [("qb", h) for h in range(6)] + [("kb", h) for h in range(6)]
    + [("qi", h) for h in range(8)] + [("ki", 0), ("ki", 0), ("pad", 0), ("pad", 0)]
)
N_NORM_SLOTS = 24
N_CHUNKS = len(_SLOTS) // 4
SLAB = N_CHUNKS * LANES
_COL_VA = 2 * SLAB
_COL_VB = _COL_VA + N_KV_A * HEAD_DIM
_COL_WI = _COL_VB + N_HEADS_B * HEAD_DIM
N_COL = _COL_WI + LANES


def _slot_offset(kind, h):
    base = {"qa": _OFF_QA, "ka": _OFF_KA, "qb": _OFF_QB, "kb": _OFF_KB, "qi": _OFF_QI, "ki": _OFF_KI}
    return base[kind] + h * HEAD_DIM


def _projection_columns():
    zero_col = D_IN
    cols_a, cols_b = [], []
    for kind, h in _SLOTS:
        if kind == "pad":
            cols_a += [zero_col] * HALF
            cols_b += [zero_col] * HALF
        else:
            off = _slot_offset(kind, h)
            cols_a += list(range(off, off + HALF))
            cols_b += list(range(off + HALF, off + HEAD_DIM))
    cols = cols_a + cols_b
    cols += list(range(_OFF_VA, _OFF_VA + N_KV_A * HEAD_DIM))
    cols += list(range(_OFF_VB, _OFF_VB + N_HEADS_B * HEAD_DIM))
    cols += list(range(_OFF_WI, _OFF_WI + N_IDX_HEADS)) + [zero_col] * (LANES - N_IDX_HEADS)
    assert len(cols) == N_COL
    return np.asarray(cols, np.int32)


def _take_runs(w, idx, axis):
    size = w.shape[axis]
    pieces, a = [], 0
    idx = [int(v) for v in idx]
    while a < len(idx):
        b = a + 1
        if idx[a] == size:
            while b < len(idx) and idx[b] == size:
                b += 1
            shape = list(w.shape)
            shape[axis] = b - a
            pieces.append(jnp.zeros(shape, w.dtype))
        else:
            while b < len(idx) and idx[b] == idx[b - 1] + 1:
                b += 1
            pieces.append(lax.slice_in_dim(w, idx[a], idx[b - 1] + 1, axis=axis))
        a = b
    return jnp.concatenate(pieces, axis=axis)


def _interleave_matrix():
    p = np.zeros((2 * LANES, 2 * LANES), np.float32)
    for head in range(4):
        for i in range(HALF):
            p[HALF * head + i, HEAD_DIM * head + i] = 1.0
            p[LANES + HALF * head + i, HEAD_DIM * head + HALF + i] = 1.0
    return p


def _group_sum_matrix():
    g = np.zeros((LANES, LANES), np.float32)
    for k in range(LANES // HALF):
        g[HALF * k:HALF * (k + 1), HALF * k:HALF * (k + 1)] = 1.0
    return g


def _dilated_bias(tq):
    max_win = max(w for w, _ in DILATED_PATTERNS)
    nd = max_win // tq + 1
    d = np.arange(nd)[:, None, None] * tq + np.arange(tq)[None, :, None] - np.arange(tq)[None, None, :]
    mult = np.zeros(d.shape, np.float64)
    for win, dil in DILATED_PATTERNS:
        mult += ((d >= 0) & (d <= win) & (d % dil == 0)).astype(np.float64)
    with np.errstate(divide="ignore"):
        bias = np.where(mult > 0, np.log(np.maximum(mult, 1.0)), NEG_BIG)
    return bias.astype(np.float32), nd


def _nt_dot(a, b):
    return lax.dot_general(a, b, (((1,), (1,)), ((), ())), preferred_element_type=f32)


def _adaln_kernel(c_ref, w_ref, b_ref, o_ref):
    c = c_ref[...]
    a = c * jax.nn.sigmoid(c)
    o_ref[...] = jnp.dot(a, w_ref[...], preferred_element_type=f32) + b_ref[...]


def _adaln(c, w_ada, b_ada):
    bsz, d = c.shape
    n = w_ada.shape[1]
    rows = -(-bsz // 8) * 8
    c_pad = jnp.zeros((rows, d), f32).at[:bsz].set(c)
    tn = 512
    out = pl.pallas_call(
        _adaln_kernel,
        grid=(n // tn,),
        in_specs=[pl.BlockSpec((rows, d), lambda j: (0, 0)),
                  pl.BlockSpec((d, tn), lambda j: (0, j)),
                  pl.BlockSpec((1, tn), lambda j: (0, j))],
        out_specs=pl.BlockSpec((rows, tn), lambda j: (0, j)),
        out_shape=jax.ShapeDtypeStruct((rows, n), f32),
        name="adaln",
    )(c_pad, w_ada, b_ada.reshape(1, n))
    return out[:bsz]


def _proj_kernel(x_ref, pos_ref, sc_ref, sh_ref, g_ref, w_ref, ga_ref, gb_ref, gsum_ref, perm_ref, invf_ref,
                 qa_ref, ka_ref, va_ref, qi_ref, ki_ref, wi_ref, qb_ref, kb_ref, vb_ref):
    x = x_ref[0]
    ms = jnp.mean(x * x, axis=-1, keepdims=True)
    h = (x * lax.rsqrt(ms + EPS)) * g_ref[...]
    h = h * (1.0 + sc_ref[0]) + sh_ref[0]
    proj = jnp.dot(h.astype(bf16), w_ref[...], preferred_element_type=f32)

    ang = pos_ref[0] * invf_ref[...]
    cos = jnp.cos(ang)
    sin = jnp.sin(ang)
    gsum = gsum_ref[...]
    perm = perm_ref[...]
    heads = []
    for c in range(N_CHUNKS):
        a = proj[:, LANES * c:LANES * (c + 1)]
        b = proj[:, SLAB + LANES * c:SLAB + LANES * (c + 1)]
        if 4 * c < N_NORM_SLOTS:
            ss = a * a + b * b
            hi = ss.astype(bf16)
            lo = (ss - hi.astype(f32)).astype(bf16)
            tot = (jnp.dot(hi, gsum, preferred_element_type=f32)
                   + jnp.dot(lo, gsum, preferred_element_type=f32))
            inv = lax.rsqrt(tot * (1.0 / HEAD_DIM) + EPS)
            a = a * inv * ga_ref[:, LANES * c:LANES * (c + 1)]
            b = b * inv * gb_ref[:, LANES * c:LANES * (c + 1)]
        ra = a * cos - b * sin
        rb = b * cos + a * sin
        ab = jnp.concatenate([ra, rb], axis=1).astype(bf16)
        heads.append(jnp.dot(ab, perm, preferred_element_type=f32).astype(bf16))

    qa_ref[0, :, 0:256] = heads[0]
    qa_ref[0, :, 256:512] = heads[1]
    qa_ref[0, :, 512:640] = heads[2][:, 0:128]
    ka_ref[0] = heads[2][:, 128:256]
    qb_ref[0, :, 0:256] = heads[3]
    qb_ref[0, :, 256:384] = heads[4][:, 0:128]
    kb_ref[0, :, 0:128] = heads[4][:, 128:256]
    kb_ref[0, :, 128:384] = heads[5]
    qi_ref[0, :, 0:256] = heads[6]
    qi_ref[0, :, 256:512] = heads[7]
    ki_ref[0] = heads[8][:, 0:128]
    va_ref[0] = proj[:, _COL_VA:_COL_VB].astype(bf16)
    vb_ref[0] = proj[:, _COL_VB:_COL_WI].astype(bf16)
    wi_ref[0] = proj[:, _COL_WI:N_COL]


def _project(x, posf, sc, sh, g, w_perm, gain_a, gain_b):
    bsz, s, d = x.shape
    ts = min(512, s)
    gsum = jnp.asarray(_group_sum_matrix(), bf16)
    perm = jnp.asarray(_interleave_matrix(), bf16)
    inv = ROPE_THETA ** (-jnp.arange(HALF, dtype=f32) / HALF)
    invf = jnp.tile(inv, LANES // HALF)[None, :]
    wa = N_HEADS_A * HEAD_DIM
    wb = N_HEADS_B * HEAD_DIM
    wq = N_IDX_HEADS * IDX_DIM
    const = lambda b, i: (0, 0)
    tile = lambda b, i: (b, i, 0)
    per_b = lambda b, i: (b, 0, 0)
    out_shape = [jax.ShapeDtypeStruct((bsz, s, w), dt) for w, dt in
                 ((wa, bf16), (LANES, bf16), (LANES, bf16), (wq, bf16), (LANES, bf16), (LANES, f32),
                  (wb, bf16), (wb, bf16), (wb, bf16))]
    out_specs = [pl.BlockSpec((1, ts, sh_.shape[2]), tile) for sh_ in out_shape]
    return pl.pallas_call(
        _proj_kernel,
        grid=(bsz, s // ts),
        in_specs=[pl.BlockSpec((1, ts, d), tile),
                  pl.BlockSpec((1, ts, 1), tile),
                  pl.BlockSpec((1, 1, d), per_b),
                  pl.BlockSpec((1, 1, d), per_b),
                  pl.BlockSpec((1, d), const),
                  pl.BlockSpec((d, N_COL), const),
                  pl.BlockSpec((1, N_NORM_SLOTS * HALF), const),
                  pl.BlockSpec((1, N_NORM_SLOTS * HALF), const),
                  pl.BlockSpec((LANES, LANES), const),
                  pl.BlockSpec((2 * LANES, 2 * LANES), const),
                  pl.BlockSpec((1, LANES), const)],
        out_specs=out_specs,
        out_shape=out_shape,
        compiler_params=pltpu.CompilerParams(
            dimension_semantics=("arbitrary", "arbitrary"), vmem_limit_bytes=VMEM_LIMIT),
        name="in_proj",
    )(x, posf, sc, sh, g, w_perm, gain_a, gain_b, gsum, perm, invf)


def _sparse_attn_kernel(qa_ref, ka_ref, va_ref, qi_ref, ki_ref, wi_ref, triu_ref, o_ref,
                        keys_ref, qis_ref, qas_ref, wrep_ref, m_ref, acc_ref,
                        *, qb, tk, n_sel, w_scale):
    i = pl.program_id(1)
    n_pairs_a = N_HEADS_A // 2
    lane = lax.broadcasted_iota(i32, (qb, LANES), 1)
    left = lane < HEAD_DIM
    mask_l = left.astype(f32).astype(bf16)
    mask_r = (1.0 - left.astype(f32)).astype(bf16)

    for c in range(N_IDX_HEADS // 2):
        ch = qi_ref[0, :, LANES * c:LANES * (c + 1)]
        qis_ref[(2 * c) * qb:(2 * c + 1) * qb, :] = ch * mask_l
        qis_ref[(2 * c + 1) * qb:(2 * c + 2) * qb, :] = ch * mask_r
    q_scale = jnp.asarray(HEAD_DIM ** -0.5, bf16)
    for c in range(n_pairs_a):
        ch = qa_ref[0, :, LANES * c:LANES * (c + 1)] * q_scale
        qas_ref[0, c * qb:(c + 1) * qb, :] = ch * mask_l
        qas_ref[1, c * qb:(c + 1) * qb, :] = ch * mask_r
    wi = wi_ref[0] * w_scale
    for h in range(N_IDX_HEADS):
        wrep_ref[h] = jnp.broadcast_to(wi[:, h:h + 1], (qb, LANES))

    n_tiles = (i * qb) // tk + 1
    reps = tk // LANES
    row = lax.broadcasted_iota(i32, (qb, tk), 0) + i * qb
    col = lax.broadcasted_iota(i32, (qb, tk), 1)

    def score_body(j, rmax):
        start = pl.multiple_of(j * tk, tk)
        kt = ki_ref[0, pl.ds(start, tk), :]
        lg = _nt_dot(qis_ref[...], kt)
        acc = jnp.zeros((qb, tk), f32)
        for h in range(N_IDX_HEADS):
            wr = jnp.concatenate([wrep_ref[h]] * reps, axis=1)
            acc = acc + jnp.maximum(lg[h * qb:(h + 1) * qb], 0.0) * wr
        acc = jnp.where(col + j * tk <= row, acc, -jnp.inf)
        keys_ref[j] = acc
        for r in range(reps):
            rmax = jnp.maximum(rmax, acc[:, LANES * r:LANES * (r + 1)])
        return rmax

    rmax = lax.fori_loop(0, n_tiles, score_body, jnp.full((qb, LANES), -jnp.inf, f32))
    rmax = jnp.broadcast_to(jnp.max(rmax, axis=1, keepdims=True), (qb, LANES))

    def key_to_float(key):
        return lax.bitcast_convert_type(key ^ ((key >> 31) & 0x7FFFFFFF), f32)

    def float_to_key(v):
        bits = lax.bitcast_convert_type(v, i32)
        return bits ^ ((bits >> 31) & 0x7FFFFFFF)

    sweep_rows = min(qb, LANES)
    ones8 = jnp.ones((8, LANES), bf16)

    def to_rows(x_lane):
        return jnp.transpose(jnp.broadcast_to(x_lane[0:1, :], (LANES, qb)))

    def to_lanes(cnt):
        return _nt_dot(ones8, cnt.astype(bf16))

    def count_ge(trial_lane):
        trial_rows = to_rows(trial_lane)
        parts = []
        for r0 in range(0, qb, sweep_rows):
            trial_r = trial_rows[r0:r0 + sweep_rows]

            def body(j, cnt, r0=r0, trial_r=trial_r):
                for r in range(reps):
                    ch = keys_ref[j, r0:r0 + sweep_rows, LANES * r:LANES * (r + 1)]
                    cnt = cnt + jnp.where(ch >= trial_r, 1.0, 0.0)
                return cnt
            parts.append(lax.fori_loop(0, n_tiles, body, jnp.zeros((sweep_rows, LANES), f32)))
        return to_lanes(jnp.concatenate(parts, axis=0))

    log_target = math.log(n_sel - 0.5)
    rmax = jnp.transpose(rmax)[0:8, :]
    rmax_pad = rmax + jnp.abs(rmax) * 2.0 ** -20 + 1e-30

    def zero_counts(r0):
        def body(j, cnts):
            ge, gt = cnts
            for r in range(reps):
                ch = keys_ref[j, r0:r0 + sweep_rows, LANES * r:LANES * (r + 1)]
                ge = ge + jnp.where(ch >= 0.0, 1.0, 0.0)
                gt = gt + jnp.where(ch > 0.0, 1.0, 0.0)
            return ge, gt
        z = jnp.zeros((sweep_rows, LANES), f32)
        ge, gt = lax.fori_loop(0, n_tiles, body, (z, z))
        return to_lanes(ge), to_lanes(gt)

    zero_parts = [zero_counts(r0) for r0 in range(0, qb, sweep_rows)]
    ge0 = jnp.concatenate([p[0] for p in zero_parts], axis=1)
    gt0 = jnp.concatenate([p[1] for p in zero_parts], axis=1)
    zeros = jnp.zeros((8, qb), f32)
    total = zeros + (n_tiles * tk).astype(f32)
    above = ge0 >= n_sel
    lo_v0 = jnp.where(above, 0.0, -jnp.inf)
    lo_c0 = jnp.where(above, ge0, total)
    hi_v0 = jnp.where(above, jnp.inf, 0.0)
    hi_c0 = jnp.where(above, jnp.where(gt0 < n_sel, gt0, 0.0), ge0)
    done0 = (above & (gt0 < n_sel)) | (lo_c0 == n_sel)

    def search_cond(state):
        return (state[0] < SEARCH_MAX_STEPS) & state[-1]

    def search_body(state):
        it, lo_v, lo_c, f_lo, hi_v, hi_c, f_hi, last, done, _ = state
        lo_k = float_to_key(lo_v)
        hi_k = float_to_key(hi_v)
        hi_eff = jnp.where(hi_v == jnp.inf, rmax_pad, hi_v)
        t_int = lo_v + (hi_eff - lo_v) * (f_lo / (f_lo - f_hi))
        t_int = key_to_float(float_to_key(t_int))
        use_int = (lo_v > -jnp.inf) & (t_int > lo_v) & (t_int < hi_v) & (it < SEARCH_INTERP_STEPS)
        mid_k = (lo_k >> 1) + (hi_k >> 1) + (lo_k & hi_k & 1)
        t = jnp.where(use_int, t_int, key_to_float(mid_k))
        c = count_ge(t)
        f = jnp.log(jnp.maximum(c, 0.5)) - log_target
        active = done < 0.5
        is_lo = (c >= n_sel) & active
        is_hi = (c < n_sel) & active
        f_hi = jnp.where(is_lo & (last > 0.0), f_hi * 0.5, f_hi)
        f_lo = jnp.where(is_hi & (last < 0.0), f_lo * 0.5, f_lo)
        lo_v = jnp.where(is_lo, t, lo_v)
        lo_c = jnp.where(is_lo, c, lo_c)
        f_lo = jnp.where(is_lo, f, f_lo)
        hi_v = jnp.where(is_hi, t, hi_v)
        hi_c = jnp.where(is_hi, c, hi_c)
        f_hi = jnp.where(is_hi, f, f_hi)
        last = jnp.where(is_lo, 1.0, jnp.where(is_hi, -1.0, last))
        settled = (lo_c == n_sel) | (float_to_key(lo_v) + 1 >= float_to_key(hi_v))
        done = jnp.where(settled, 1.0, done)
        unresolved = jnp.min(done) < 0.5
        return it + 1, lo_v, lo_c, f_lo, hi_v, hi_c, f_hi, last, done, unresolved

    done0 = jnp.where(done0, 1.0, 0.0)
    init = (jnp.int32(0), lo_v0, lo_c0, jnp.log(lo_c0) - log_target,
            hi_v0, hi_c0, jnp.log(jnp.maximum(hi_c0, 0.5)) - log_target,
            zeros, done0, jnp.min(done0) < 0.5)
    final = lax.while_loop(search_cond, search_body, init)
    thr, lo_c, hi_c = final[1], final[2], final[5]
    thr_t = jnp.concatenate([to_rows(thr)] * reps, axis=1)
    tied = lo_c != n_sel
    any_tied = jnp.max(jnp.where(tied, 1.0, 0.0)) > 0.0
    quota = to_rows(jnp.where(tied, n_sel - hi_c, 2.0 * tk * (n_tiles + 1).astype(f32)))
    quota_t = jnp.concatenate([quota] * reps, axis=1)

    m_ref[...] = jnp.full(m_ref.shape, NEG_BIG, f32)
    acc_ref[...] = jnp.zeros(acc_ref.shape, f32)
    lane_k = lax.broadcasted_iota(i32, (tk, LANES), 1)
    kmask_l = (lane_k < HEAD_DIM).astype(f32).astype(bf16)
    kmask_r = (lane_k >= HEAD_DIM).astype(f32).astype(bf16)

    def attn_body(j, n_ties, with_ties):
        start = pl.multiple_of(j * tk, tk)
        sc = keys_ref[j]
        if with_ties:
            eq = jnp.where(sc == thr_t, 1.0, 0.0)
            before = (jnp.dot(eq.astype(bf16), triu_ref[...], preferred_element_type=f32)
                      + jnp.concatenate([n_ties] * reps, axis=1))
            sel = (sc > thr_t) | ((sc == thr_t) & (before < quota_t))
            n_ties = n_ties + jnp.broadcast_to(jnp.sum(eq, axis=1, keepdims=True), (qb, LANES))
        else:
            sel = sc >= thr_t
        sel = sel & (col + j * tk <= row)
        bias = jnp.where(sel, 0.0, NEG_BIG)
        bias = jnp.concatenate([bias] * n_pairs_a, axis=0)
        kk = ka_ref[0, pl.ds(start, tk), :]
        vv = va_ref[0, pl.ds(start, tk), :]
        v_ext = (vv * kmask_l + kmask_r, vv * kmask_r + kmask_l)
        for g in range(N_KV_A):
            s = _nt_dot(qas_ref[g], kk) + bias
            m_old = m_ref[g]
            m_new = jnp.maximum(m_old, jnp.max(s, axis=1, keepdims=True))
            alpha = jnp.exp(m_old - m_new)
            p = jnp.exp(s - jnp.concatenate([m_new] * reps, axis=1))
            acc_ref[g] = alpha * acc_ref[g] + jnp.dot(p.astype(bf16), v_ext[g], preferred_element_type=f32)
            m_ref[g] = m_new
        return n_ties

    no_ties = jnp.zeros((qb, LANES), f32)

    @pl.when(any_tied)
    def _():
        lax.fori_loop(0, n_tiles, functools.partial(attn_body, with_ties=True), no_ties)

    @pl.when(jnp.logical_not(any_tied))
    def _():
        lax.fori_loop(0, n_tiles, functools.partial(attn_body, with_ties=False), no_ties)

    for c in range(n_pairs_a):
        rows = slice(c * qb, (c + 1) * qb)
        a0 = acc_ref[0, rows, :]
        a1 = acc_ref[1, rows, :]
        o0 = a0 / pltpu.roll(a0, HEAD_DIM, axis=1)
        o1 = a1 / pltpu.roll(a1, HEAD_DIM, axis=1)
        o_ref[0, :, LANES * c:LANES * (c + 1)] = jnp.where(left, o0, o1).astype(bf16)


def _sparse_attention(qa, ka, va, qi, ki, wi):
    bsz, s, wa = qa.shape
    qb = min(256, s)
    tk = min(512, s)
    n_sel = min(TOPK_MAX, s // 4)
    w_scale = N_IDX_HEADS ** -0.5 * IDX_DIM ** -0.5
    n_pairs_a = N_HEADS_A // 2
    tile = lambda b, i: (b, i, 0)
    per_b = lambda b, i: (b, 0, 0)
    kern = functools.partial(_sparse_attn_kernel, qb=qb, tk=tk, n_sel=float(n_sel), w_scale=w_scale)
    return pl.pallas_call(
        kern,
        grid=(bsz, s // qb),
        in_specs=[pl.BlockSpec((1, qb, wa), tile),
                  pl.BlockSpec((1, s, LANES), per_b),
                  pl.BlockSpec((1, s, LANES), per_b),
                  pl.BlockSpec((1, qb, qi.shape[2]), tile),
                  pl.BlockSpec((1, s, LANES), per_b),
                  pl.BlockSpec((1, qb, LANES), tile),
                  pl.BlockSpec((tk, tk), lambda b, i: (0, 0))],
        out_specs=pl.BlockSpec((1, qb, wa), tile),
        out_shape=jax.ShapeDtypeStruct((bsz, s, wa), bf16),
        scratch_shapes=[pltpu.VMEM((s // tk, qb, tk), f32),
                        pltpu.VMEM((N_IDX_HEADS * qb, LANES), bf16),
                        pltpu.VMEM((N_KV_A, n_pairs_a * qb, LANES), bf16),
                        pltpu.VMEM((N_IDX_HEADS, qb, LANES), f32),
                        pltpu.VMEM((N_KV_A, n_pairs_a * qb, LANES), f32),
                        pltpu.VMEM((N_KV_A, n_pairs_a * qb, LANES), f32)],
        compiler_params=pltpu.CompilerParams(
            dimension_semantics=("arbitrary", "arbitrary"), vmem_limit_bytes=VMEM_LIMIT),
        name="sparse_attn",
    )(qa, ka, va, qi, ki, wi, jnp.asarray(np.triu(np.ones((tk, tk), np.float32), k=1), bf16))


def _dilated_kernel(q_ref, k_ref, v_ref, bias_ref, o_ref, qs_ref, m_ref, acc_ref, *, tq, nd):
    i = pl.program_id(2)
    lane = lax.broadcasted_iota(i32, (tq, LANES), 1)
    left = lane < HEAD_DIM
    mask_l = left.astype(f32).astype(bf16)
    mask_r = (1.0 - left.astype(f32)).astype(bf16)
    q = q_ref[0] * jnp.asarray(HEAD_DIM ** -0.5, bf16)
    qs_ref[0:tq, :] = q * mask_l
    qs_ref[tq:2 * tq, :] = q * mask_r
    m_ref[...] = jnp.full(m_ref.shape, NEG_BIG, f32)
    acc_ref[...] = jnp.zeros(acc_ref.shape, f32)
    ones = jnp.ones((tq, LANES), bf16)

    def body(j, carry):
        start = pl.multiple_of(j * tq, tq)
        kk = k_ref[0, pl.ds(start, tq), :]
        v_ext = jnp.concatenate([v_ref[0, pl.ds(start, tq), :], ones], axis=1)
        b = bias_ref[i - j]
        s = _nt_dot(qs_ref[...], kk) + jnp.concatenate([b, b], axis=0)
        m_old = m_ref[...]
        m_new = jnp.maximum(m_old, jnp.max(s, axis=1, keepdims=True))
        alpha = jnp.exp(m_old - m_new)
        p = jnp.exp(s - jnp.concatenate([m_new] * (tq // LANES), axis=1))
        pv = jnp.dot(p.astype(bf16), v_ext, preferred_element_type=f32)
        acc_ref[...] = jnp.concatenate([alpha, alpha], axis=1) * acc_ref[...] + pv
        m_ref[...] = m_new
        return carry

    lax.fori_loop(jnp.maximum(i - (nd - 1), 0), i + 1, body, 0)
    o0 = acc_ref[0:tq, 0:LANES] / acc_ref[0:tq, LANES:2 * LANES]
    o1 = acc_ref[tq:2 * tq, 0:LANES] / acc_ref[tq:2 * tq, LANES:2 * LANES]
    o_ref[0] = jnp.where(left, o0, o1).astype(bf16)


def _dilated_attention(qb_, kb_, vb_):
    bsz, s, wb = qb_.shape
    tq = min(512, s)
    bias_np, nd = _dilated_bias(tq)
    bias = jnp.asarray(bias_np)
    n_pairs = wb // LANES
    kern = functools.partial(_dilated_kernel, tq=tq, nd=nd)
    return pl.pallas_call(
        kern,
        grid=(bsz, n_pairs, s // tq),
        in_specs=[pl.BlockSpec((1, tq, LANES), lambda b, p, i: (b, i, p)),
                  pl.BlockSpec((1, s, LANES), lambda b, p, i: (b, 0, p)),
                  pl.BlockSpec((1, s, LANES), lambda b, p, i: (b, 0, p)),
                  pl.BlockSpec((nd, tq, tq), lambda b, p, i: (0, 0, 0))],
        out_specs=pl.BlockSpec((1, tq, LANES), lambda b, p, i: (b, i, p)),
        out_shape=jax.ShapeDtypeStruct((bsz, s, wb), bf16),
        scratch_shapes=[pltpu.VMEM((2 * tq, LANES), bf16),
                        pltpu.VMEM((2 * tq, LANES), f32),
                        pltpu.VMEM((2 * tq, 2 * LANES), f32)],
        compiler_params=pltpu.CompilerParams(
            dimension_semantics=("arbitrary", "arbitrary", "arbitrary"), vmem_limit_bytes=VMEM_LIMIT),
        name="dilated_attn",
    )(qb_, kb_, vb_, bias)


def _mid_kernel(oa_ref, ob_ref, x_ref, gta_ref, scf_ref, shf_ref, gtf_ref, woa_ref, wob_ref, gffn_ref,
                wsgu_ref, wsd_ref, wrt_ref, rbias_ref, triu_ref,
                base_ref, h2_ref, sel_ref, gate_ref, rank_ref, cnt_ref, carry_ref, *, n_exp, t, d_sh):
    step = pl.program_id(0)

    @pl.when(step == 0)
    def _():
        carry_ref[...] = jnp.zeros(carry_ref.shape, f32)

    mix = (jnp.dot(oa_ref[...], woa_ref[...], preferred_element_type=f32)
           + jnp.dot(ob_ref[...], wob_ref[...], preferred_element_type=f32))
    x1 = x_ref[...] + gta_ref[0] * mix
    ms = jnp.mean(x1 * x1, axis=-1, keepdims=True)
    h2 = ((x1 * lax.rsqrt(ms + EPS)) * gffn_ref[...]) * (1.0 + scf_ref[0]) + shf_ref[0]
    h2b = h2.astype(bf16)
    h2f = h2b.astype(f32)
    n_sub = h2_ref.shape[0] // t
    for a in range(n_sub):
        h2_ref[pl.ds(a, t, stride=n_sub), :] = h2f[:, LANES * a:LANES * (a + 1)]

    gu = jnp.dot(h2b, wsgu_ref[...], preferred_element_type=f32)
    g = gu[:, :d_sh]
    u = gu[:, d_sh:]
    act = (g * jax.nn.sigmoid(g)) * u
    shared = jnp.dot(act.astype(bf16), wsd_ref[...], preferred_element_type=f32)
    base = x1 + gtf_ref[0] * shared
    for a in range(n_sub):
        base_ref[pl.ds(a, t, stride=n_sub), :] = base[:, LANES * a:LANES * (a + 1)]

    scores = jax.nn.sigmoid(_nt_dot(wrt_ref[...], h2b))
    biased = scores + rbias_ref[...]
    per = n_exp // N_GROUPS
    neg_inf = jnp.float32(-jnp.inf)
    ri_g = lax.broadcasted_iota(i32, (per, t), 0).astype(f32)
    gs = []
    for grp in range(N_GROUPS):
        blk = biased[grp * per:(grp + 1) * per]
        m1 = jnp.max(blk, axis=0, keepdims=True)
        idx1 = jnp.min(jnp.where(blk == m1, ri_g, float(per)), axis=0, keepdims=True)
        m2 = jnp.max(jnp.where(ri_g == idx1, neg_inf, blk), axis=0, keepdims=True)
        gs.append(m1 + m2)
    masked_rows = []
    for grp in range(N_GROUPS):
        beaten = jnp.zeros((1, t), f32)
        for g2 in range(N_GROUPS):
            if g2 == grp:
                continue
            wins = gs[g2] > gs[grp]
            if g2 < grp:
                wins = wins | (gs[g2] == gs[grp])
            beaten = beaten + jnp.where(wins, 1.0, 0.0)
        keep = jnp.broadcast_to(beaten < TOPK_GROUPS, (per, t))
        masked_rows.append(jnp.where(keep, biased[grp * per:(grp + 1) * per], neg_inf))
    masked = jnp.concatenate(masked_rows, axis=0)

    ri = lax.broadcasted_iota(i32, (n_exp, t), 0).astype(f32)
    selmask = jnp.zeros((n_exp, t), f32)
    idxs, graw = [], []
    for _ in range(TOP_K):
        m = jnp.max(masked, axis=0, keepdims=True)
        idx = jnp.min(jnp.where(masked == m, ri, float(n_exp)), axis=0, keepdims=True)
        onehot = ri == idx
        graw.append(jnp.sum(jnp.where(onehot, scores, 0.0), axis=0, keepdims=True))
        masked = jnp.where(onehot, neg_inf, masked)
        selmask = jnp.where(onehot, 1.0, selmask)
        idxs.append(idx)
    den = graw[0]
    for k in range(1, TOP_K):
        den = den + graw[k]

    prefix = jnp.dot(selmask.astype(bf16), triu_ref[...], preferred_element_type=f32)
    prefix = prefix + jnp.concatenate([carry_ref[...]] * (t // LANES), axis=1)
    for k in range(TOP_K):
        rank_k = jnp.sum(jnp.where(ri == idxs[k], prefix, 0.0), axis=0, keepdims=True)
        sel_ref[k:k + 1, :] = idxs[k].astype(i32)
        rank_ref[k:k + 1, :] = rank_k.astype(i32)
        gate_ref[k:k + 1, :] = graw[k] / den * ROUTED_SCALE
    carry_ref[...] = carry_ref[...] + jnp.broadcast_to(
        jnp.sum(selmask, axis=1, keepdims=True), carry_ref.shape)
    cnt_ref[...] = carry_ref[...]


def _mid(oa, ob, x2, gta, scf, shf, gtf, woa, wob, gffn, wsgu, wsd, wrt, rbias_rep, s):
    n, d = x2.shape
    t = rbias_rep.shape[1]
    n_exp = wrt.shape[0]
    d_sh = wsd.shape[0]
    triu = jnp.asarray(np.triu(np.ones((t, t), np.float32), k=1), bf16)
    tile = lambda i: (i, 0)
    const = lambda i: (0, 0)
    per_b = lambda i: ((i * t) // s, 0, 0)
    lane_tile = lambda i: (0, i)
    n_sub = d // LANES
    tok_tiles = jax.ShapeDtypeStruct((n * n_sub, LANES), f32)
    tok_spec = pl.BlockSpec((t * n_sub, LANES), tile)
    out_shape = [tok_tiles, tok_tiles,
                 jax.ShapeDtypeStruct((TOP_K, n), i32), jax.ShapeDtypeStruct((TOP_K, n), f32),
                 jax.ShapeDtypeStruct((TOP_K, n), i32), jax.ShapeDtypeStruct((n_exp, LANES), f32)]
    out_specs = [tok_spec, tok_spec,
                 pl.BlockSpec((TOP_K, t), lane_tile), pl.BlockSpec((TOP_K, t), lane_tile),
                 pl.BlockSpec((TOP_K, t), lane_tile), pl.BlockSpec((n_exp, LANES), const)]
    kern = functools.partial(_mid_kernel, n_exp=n_exp, t=t, d_sh=d_sh)
    return pl.pallas_call(
        kern,
        grid=(n // t,),
        in_specs=[pl.BlockSpec((t, oa.shape[1]), tile),
                  pl.BlockSpec((t, ob.shape[1]), tile),
                  pl.BlockSpec((t, d), tile),
                  pl.BlockSpec((1, 1, d), per_b),
                  pl.BlockSpec((1, 1, d), per_b),
                  pl.BlockSpec((1, 1, d), per_b),
                  pl.BlockSpec((1, 1, d), per_b),
                  pl.BlockSpec(woa.shape, const),
                  pl.BlockSpec(wob.shape, const),
                  pl.BlockSpec((1, d), const),
                  pl.BlockSpec(wsgu.shape, const),
                  pl.BlockSpec(wsd.shape, const),
                  pl.BlockSpec(wrt.shape, const),
                  pl.BlockSpec(rbias_rep.shape, const),
                  pl.BlockSpec((t, t), const)],
        out_specs=out_specs,
        out_shape=out_shape,
        scratch_shapes=[pltpu.VMEM((n_exp, LANES), f32)],
        compiler_params=pltpu.CompilerParams(
            dimension_semantics=("arbitrary",), vmem_limit_bytes=VMEM_LIMIT),
        name="mid",
    )(oa, ob, x2, gta, scf, shf, gtf, woa, wob, gffn, wsgu, wsd, wrt, rbias_rep, triu)


def _dispatch_kernel(zb_ref, slot_ref, h_ref, xs_ref, zero_ref, sem_ref, *, td, n_zero):
    step = pl.program_id(0)

    def zero_copy(b):
        return pltpu.make_async_copy(zero_ref, xs_ref.at[pl.ds(zb_ref[b], ZERO_ROWS)], sem_ref.at[0])

    @pl.when(step == 0)
    def _():
        zero_ref[...] = jnp.zeros(zero_ref.shape, f32)

        def start(b, c):
            @pl.when(zb_ref[b] >= 0)
            def _():
                zero_copy(b).start()
            return c

        def wait(b, c):
            @pl.when(zb_ref[b] >= 0)
            def _():
                zero_copy(b).wait()
            return c

        lax.fori_loop(0, n_zero, start, 0)
        lax.fori_loop(0, n_zero, wait, 0)

    def start_rows(tok, c):
        for k in range(TOP_K):
            pltpu.make_async_copy(h_ref.at[pl.ds(tok, 1)], xs_ref.at[pl.ds(slot_ref[k, tok], 1)],
                                  sem_ref.at[1]).start(priority=k % 2)
        return c

    lax.fori_loop(0, td, start_rows, 0)
    for k in range(TOP_K):
        pltpu.make_async_copy(h_ref, xs_ref.at[pl.ds(0, td)], sem_ref.at[1]).wait()


def _dispatch(zero_start, slot, h2, n_slots):
    n, rows, lanes = h2.shape
    td = min(256, n)
    kern = functools.partial(_dispatch_kernel, td=td, n_zero=zero_start.shape[0])
    grid_spec = pltpu.PrefetchScalarGridSpec(
        num_scalar_prefetch=1,
        grid=(n // td,),
        in_specs=[pl.BlockSpec((TOP_K, td), lambda i, tail: (0, i), memory_space=pltpu.SMEM),
                  pl.BlockSpec((td, rows, lanes), lambda i, tail: (i, 0, 0))],
        out_specs=pl.BlockSpec(memory_space=pl.ANY),
        scratch_shapes=[pltpu.VMEM((ZERO_ROWS, rows, lanes), f32), pltpu.SemaphoreType.DMA((2,))],
    )
    return pl.pallas_call(
        kern,
        grid_spec=grid_spec,
        out_shape=jax.ShapeDtypeStruct((n_slots, rows, lanes), f32),
        compiler_params=pltpu.CompilerParams(
            dimension_semantics=("arbitrary",), vmem_limit_bytes=VMEM_LIMIT),
        name="dispatch",
    )(zero_start, slot, h2)


def _expert_kernel(be_ref, nu_ref, xs_ref, wg_ref, wu_ref, wd_ref, ys_ref, wgu_s, wd_s, *, n_sub):
    i = pl.program_id(0)
    f = wd_s.shape[0]
    rows = xs_ref.shape[0] // n_sub

    @pl.when(i >= nu_ref[0])
    def _():
        ys_ref[...] = jnp.zeros(ys_ref.shape, f32)

    @pl.when(i < nu_ref[0])
    def _():
        @pl.when((i == 0) | (be_ref[i] != be_ref[jnp.maximum(i - 1, 0)]))
        def _():
            wgu_s[:, 0:f] = wg_ref[0].astype(bf16)
            wgu_s[:, f:2 * f] = wu_ref[0].astype(bf16)
            wd_s[...] = wd_ref[0].astype(bf16)

        xb = jnp.concatenate([xs_ref[pl.ds(a, rows, stride=n_sub), :] for a in range(n_sub)],
                             axis=1).astype(bf16)
        gu = jnp.dot(xb, wgu_s[...], preferred_element_type=f32)
        g = gu[:, 0:f]
        u = gu[:, f:2 * f]
        act = (g * jax.nn.sigmoid(g)) * u
        y = jnp.dot(act.astype(bf16), wd_s[...], preferred_element_type=f32)
        for a in range(n_sub):
            ys_ref[pl.ds(a, rows, stride=n_sub), :] = y[:, LANES * a:LANES * (a + 1)]


def _experts(block_expert, n_used, xs, w_gate, w_up, w_down):
    n_slots, n_sub, lanes = xs.shape
    d = n_sub * lanes
    n_blocks = n_slots // MOE_BLOCK
    f = w_gate.shape[2]
    blk = lambda i, be, nu: (jnp.minimum(i, nu[0] - 1), 0)
    wsel = lambda i, be, nu: (be[jnp.minimum(i, nu[0] - 1)], 0, 0)
    grid_spec = pltpu.PrefetchScalarGridSpec(
        num_scalar_prefetch=2,
        grid=(n_blocks,),
        in_specs=[pl.BlockSpec((MOE_BLOCK * n_sub, lanes), blk),
                  pl.BlockSpec((1, d, f), wsel),
                  pl.BlockSpec((1, d, f), wsel),
                  pl.BlockSpec((1, f, d), wsel)],
        out_specs=pl.BlockSpec((MOE_BLOCK * n_sub, lanes), lambda i, be, nu: (i, 0)),
        scratch_shapes=[pltpu.VMEM((d, 2 * f), bf16), pltpu.VMEM((f, d), bf16)],
    )
    ys = pl.pallas_call(
        functools.partial(_expert_kernel, n_sub=n_sub),
        grid_spec=grid_spec,
        out_shape=jax.ShapeDtypeStruct((n_slots * n_sub, lanes), f32),
        compiler_params=pltpu.CompilerParams(
            dimension_semantics=("arbitrary",), vmem_limit_bytes=VMEM_LIMIT),
        name="experts",
    )(block_expert, n_used, xs.reshape(n_slots * n_sub, lanes), w_gate, w_up, w_down)
    return ys.reshape(n_slots, n_sub, lanes)


def _combine_kernel(slot_ref, nslot_ref, gate_ref, base_ref, gtf_ref, ys_ref, o_ref, buf_ref, res_ref, sem_ref,
                    *, tc):
    step = pl.program_id(0)
    cur = step % 2

    def start_gathers(idx_ref, b):
        def body(tok, c):
            for k in range(TOP_K):
                pltpu.make_async_copy(ys_ref.at[pl.ds(idx_ref[k, tok], 1)],
                                      buf_ref.at[b, k, pl.ds(tok, 1)], sem_ref.at[b]).start(priority=k % 2)
            return c
        lax.fori_loop(0, tc, body, 0)

    @pl.when(step == 0)
    def _():
        start_gathers(slot_ref, 0)

    @pl.when(step + 1 < pl.num_programs(0))
    def _():
        start_gathers(nslot_ref, 1 - cur)

    for k in range(TOP_K):
        pltpu.make_async_copy(ys_ref.at[pl.ds(0, tc)], buf_ref.at[cur, k], sem_ref.at[cur]).wait()

    gtf = gtf_ref[0]

    def reduce_token(tok, c):
        routed = buf_ref[cur, 0, tok] * gate_ref[0, tok]
        for k in range(1, TOP_K):
            routed = routed + buf_ref[cur, k, tok] * gate_ref[k, tok]
        res_ref[pl.ds(pl.multiple_of(tok * n_sub, n_sub), n_sub), :] = base_ref[tok] + gtf * routed
        return c

    n_sub = buf_ref.shape[3]
    lax.fori_loop(0, tc, reduce_token, 0, unroll=4)
    for a in range(n_sub):
        o_ref[:, LANES * a:LANES * (a + 1)] = res_ref[pl.ds(a, tc, stride=n_sub), :]


def _combine(slot, gate, base, gtf, ys, s):
    n, rows, lanes = base.shape
    tc = min(128, n)
    n_steps = n // tc
    kern = functools.partial(_combine_kernel, tc=tc)
    tok_tile = lambda i: (i, 0, 0)
    return pl.pallas_call(
        kern,
        grid=(n_steps,),
        in_specs=[pl.BlockSpec((TOP_K, tc), lambda i: (0, i), memory_space=pltpu.SMEM),
                  pl.BlockSpec((TOP_K, tc), lambda i: (0, jnp.minimum(i + 1, n_steps - 1)),
                               memory_space=pltpu.SMEM),
                  pl.BlockSpec((TOP_K, tc), lambda i: (0, i), memory_space=pltpu.SMEM),
                  pl.BlockSpec((tc, rows, lanes), tok_tile),
                  pl.BlockSpec((1, rows, lanes), lambda i: ((i * tc) // s, 0, 0)),
                  pl.BlockSpec(memory_space=pl.ANY)],
        out_specs=pl.BlockSpec((tc, rows * lanes), lambda i: (i, 0)),
        out_shape=jax.ShapeDtypeStruct((n, rows * lanes), f32),
        scratch_shapes=[pltpu.VMEM((2, TOP_K, tc, rows, lanes), f32), pltpu.VMEM((tc * rows, lanes), f32),
                        pltpu.SemaphoreType.DMA((2,))],
        compiler_params=pltpu.CompilerParams(
            dimension_semantics=("arbitrary",), vmem_limit_bytes=VMEM_LIMIT),
        name="combine",
    )(slot, slot, gate, base, gtf, ys)


def _layer(x, c, posf, w_ada, b_ada, g_mix, w_in, q_norm_a, k_norm_a, q_norm_b, k_norm_b, w_out, g_ffn,
           w_router, router_bias, w_gate, w_up, w_down, ws_gate, ws_up, ws_down):
    bsz, s, d = x.shape
    n = bsz * s
    n_exp = w_router.shape[1]

    mod = _adaln(c, w_ada, b_ada)[:, None, :]
    sh_a, sc_a, gt_a, sh_f, sc_f, gt_f = (mod[..., k * d:(k + 1) * d] for k in range(6))

    w_perm = _take_runs(w_in.astype(bf16), _projection_columns(), axis=1)
    gains = {"qa": q_norm_a, "ka": k_norm_a, "qb": q_norm_b, "kb": k_norm_b}
    gain_a = jnp.concatenate([gains[kind][:HALF] for kind, _ in _SLOTS[:N_NORM_SLOTS]])[None, :].astype(f32)
    gain_b = jnp.concatenate([gains[kind][HALF:] for kind, _ in _SLOTS[:N_NORM_SLOTS]])[None, :].astype(f32)

    qa, ka, va, qi, ki, wi, qb_, kb_, vb_ = _project(x, posf, sc_a, sh_a, g_mix[None, :], w_perm, gain_a, gain_b)
    o_a = _sparse_attention(qa, ka, va, qi, ki, wi)
    o_b = _dilated_attention(qb_, kb_, vb_)

    wa = N_HEADS_A * HEAD_DIM
    rows_a = np.concatenate([np.arange(h * HEAD_DIM, (h + 1) * HEAD_DIM) for h in QA_PAIR_ORDER])
    woa = _take_runs(w_out, rows_a, axis=0).astype(bf16)
    wob = w_out[wa:].astype(bf16)
    wsgu = jnp.concatenate([ws_gate, ws_up], axis=1).astype(bf16)
    wsd = ws_down.astype(bf16)
    wrt = w_router.T.astype(bf16)
    t_mid = min(512, n)
    rbias_rep = jnp.broadcast_to(router_bias.astype(f32)[:, None], (n_exp, t_mid))

    base, h2, sel, gate, rank, cnt = _mid(
        o_a.reshape(n, wa), o_b.reshape(n, -1), x.reshape(n, d), gt_a, sc_f, sh_f, gt_f,
        woa, wob, g_ffn[None, :], wsgu, wsd, wrt, rbias_rep, s)
    base = base.reshape(n, d // LANES, LANES)
    h2 = h2.reshape(n, d // LANES, LANES)

    counts = cnt[:, 0].astype(i32)
    padded = (counts + MOE_BLOCK - 1) // MOE_BLOCK * MOE_BLOCK
    pad_end = jnp.cumsum(padded)
    pad_start = pad_end - padded
    onehot = sel[:, :, None] == jnp.arange(n_exp, dtype=i32)[None, None, :]
    slot = jnp.sum(jnp.where(onehot, pad_start[None, None, :], 0), axis=-1) + rank
    n_blocks = -(-(n * TOP_K) // MOE_BLOCK) + n_exp
    n_slots = n_blocks * MOE_BLOCK
    block_start = jnp.arange(n_blocks, dtype=i32) * MOE_BLOCK
    block_expert = jnp.sum((pad_end[None, :] <= block_start[:, None]).astype(i32), axis=1)
    block_expert = jnp.minimum(block_expert, n_exp - 1)
    n_used = (pad_end[-1] // MOE_BLOCK).astype(i32)[None]
    unit_start = jnp.arange(n_slots // ZERO_ROWS, dtype=i32) * ZERO_ROWS
    unit_expert = jnp.sum((pad_end[None, :] <= unit_start[:, None]).astype(i32), axis=1)
    real_end = pad_start + counts
    unit_real_end = jnp.sum(jnp.where(unit_expert[:, None] == jnp.arange(n_exp, dtype=i32)[None, :],
                                      real_end[None, :], 0), axis=1)
    has_pad = (unit_start + ZERO_ROWS > unit_real_end) | (unit_start >= pad_end[-1])
    zero_start = jnp.where(has_pad, unit_start, -1).astype(i32)

    xs = _dispatch(zero_start, slot, h2, n_slots)
    ys = _experts(block_expert, n_used, xs, w_gate, w_up, w_down)
    out = _combine(slot, gate, base, gt_f.reshape(bsz, d // LANES, LANES), ys, s)
    return out.reshape(bsz, s, d)


def kernel(x, c, positions, w_ada, b_ada, g_mix, w_in, q_norm_a, k_norm_a, q_norm_b, k_norm_b, w_out, g_ffn,
           w_router, router_bias, w_gate, w_up, w_down, ws_gate, ws_up, ws_down):
    posf = positions.astype(f32)[..., None]
    for l in range(w_ada.shape[0]):
        x = _layer(x, c, posf, w_ada[l], b_ada[l], g_mix[l], w_in[l], q_norm_a[l], k_norm_a[l], q_norm_b[l],
                   k_norm_b[l], w_out[l], g_ffn[l], w_router[l], router_bias[l], w_gate[l], w_up[l], w_down[l],
                   ws_gate[l], ws_up[l], ws_down[l])
    return x
```

```python
import functools
import math

import numpy as np
import jax
import jax.numpy as jnp
from jax import lax
from jax.experimental import pallas as pl
from jax.experimental.pallas import tpu as pltpu

f32 = jnp.float32
bf16 = jnp.bfloat16
i32 = jnp.int32

HEAD_DIM = 64
HALF = HEAD_DIM // 2
N_HEADS_A = 10
N_KV_A = 2
N_HEADS_B = 6
N_IDX_HEADS = 8
IDX_DIM = 64
TOPK_MAX = 256
DILATED_PATTERNS = ((128, 1), (512, 4), (2048, 16))
ROPE_THETA = 10000.0
EPS = 1e-6
TOP_K = 8
N_GROUPS = 8
TOPK_GROUPS = 4
ROUTED_SCALE = 2.5
MOE_BLOCK = 512
ZERO_ROWS = 64

LANES = 128
VMEM_LIMIT = 48 * 1024 * 1024

NEG_BIG = -1e30
KEY_NEG_INF = int(np.int32(np.uint32(0xFF800000) ^ np.uint32(0x7FFFFFFF)))
KEY_POS_INF = 0x7F800000
SEARCH_INTERP_STEPS = 16
SEARCH_MAX_STEPS = SEARCH_INTERP_STEPS + 34

_OFF_QA = 0
_OFF_KA = _OFF_QA + N_HEADS_A * HEAD_DIM
_OFF_VA = _OFF_KA + N_KV_A * HEAD_DIM
_OFF_QI = _OFF_VA + N_KV_A * HEAD_DIM
_OFF_KI = _OFF_QI + N_IDX_HEADS * IDX_DIM
_OFF_WI = _OFF_KI + IDX_DIM
_OFF_QB = _OFF_WI + N_IDX_HEADS
_OFF_KB = _OFF_QB + N_HEADS_B * HEAD_DIM
_OFF_VB = _OFF_KB + N_HEADS_B * HEAD_DIM
D_IN = _OFF_VB + N_HEADS_B * HEAD_DIM

QA_PAIR_ORDER = (0, 5, 1, 6, 2, 7, 3, 8, 4, 9)

_SLOTS = (
    [("qa", h) for h in QA_PAIR_ORDER[:8]] + [("qa", 4), ("qa", 9), ("ka", 0), ("ka", 1)]
    + [("qb", h) for h in range(6)] + [("kb", h) for h in range(6)]
    + [("qi", h) for h in range(8)] + [("ki", 0), ("ki", 0), ("pad", 0), ("pad", 0)]
)
N_NORM_SLOTS = 24
N_CHUNKS = len(_SLOTS) // 4
SLAB = N_CHUNKS * LANES
_COL_VA = 2 * SLAB
_COL_VB = _COL_VA + N_KV_A * HEAD_DIM
_COL_WI = _COL_VB + N_HEADS_B * HEAD_DIM
N_COL = _COL_WI + LANES


def _slot_offset(kind, h):
    base = {"qa": _OFF_QA, "ka": _OFF_KA, "qb": _OFF_QB, "kb": _OFF_KB, "qi": _OFF_QI, "ki": _OFF_KI}
    return base[kind] + h * HEAD_DIM


def _projection_columns():
    zero_col = D_IN
    cols_a, cols_b = [], []
    for kind, h in _SLOTS:
        if kind == "pad":
            cols_a += [zero_col] * HALF
            cols_b += [zero_col] * HALF
        else:
            off = _slot_offset(kind, h)
            cols_a += list(range(off, off + HALF))
            cols_b += list(range(off + HALF, off + HEAD_DIM))
    cols = cols_a + cols_b
    cols += list(range(_OFF_VA, _OFF_VA + N_KV_A * HEAD_DIM))
    cols += list(range(_OFF_VB, _OFF_VB + N_HEADS_B * HEAD_DIM))
    cols += list(range(_OFF_WI, _OFF_WI + N_IDX_HEADS)) + [zero_col] * (LANES - N_IDX_HEADS)
    assert len(cols) == N_COL
    return np.asarray(cols, np.int32)


def _take_runs(w, idx, axis):
    size = w.shape[axis]
    pieces, a = [], 0
    idx = [int(v) for v in idx]
    while a < len(idx):
        b = a + 1
        if idx[a] == size:
            while b < len(idx) and idx[b] == size:
                b += 1
            shape = list(w.shape)
            shape[axis] = b - a
            pieces.append(jnp.zeros(shape, w.dtype))
        else:
            while b < len(idx) and idx[b] == idx[b - 1] + 1:
                b += 1
            pieces.append(lax.slice_in_dim(w, idx[a], idx[b - 1] + 1, axis=axis))
        a = b
    return jnp.concatenate(pieces, axis=axis)


def _interleave_matrix():
    p = np.zeros((2 * LANES, 2 * LANES), np.float32)
    for head in range(4):
        for i in range(HALF):
            p[HALF * head + i, HEAD_DIM * head + i] = 1.0
            p[LANES + HALF * head + i, HEAD_DIM * head + HALF + i] = 1.0
    return p


def _group_sum_matrix():
    g = np.zeros((LANES, LANES), np.float32)
    for k in range(LANES // HALF):
        g[HALF * k:HALF * (k + 1), HALF * k:HALF * (k + 1)] = 1.0
    return g


def _dilated_bias(tq):
    max_win = max(w for w, _ in DILATED_PATTERNS)
    nd = max_win // tq + 1
    d = np.arange(nd)[:, None, None] * tq + np.arange(tq)[None, :, None] - np.arange(tq)[None, None, :]
    mult = np.zeros(d.shape, np.float64)
    for win, dil in DILATED_PATTERNS:
        mult += ((d >= 0) & (d <= win) & (d % dil == 0)).astype(np.float64)
    with np.errstate(divide="ignore"):
        bias = np.where(mult > 0, np.log(np.maximum(mult, 1.0)), NEG_BIG)
    return bias.astype(np.float32), nd


def _nt_dot(a, b):
    return lax.dot_general(a, b, (((1,), (1,)), ((), ())), preferred_element_type=f32)


def _adaln_kernel(c_ref, w_ref, b_ref, o_ref):
    c = c_ref[...]
    a = c * jax.nn.sigmoid(c)
    o_ref[...] = jnp.dot(a, w_ref[...], preferred_element_type=f32) + b_ref[...]


def _adaln(c, w_ada, b_ada):
    bsz, d = c.shape
    n = w_ada.shape[1]
    rows = -(-bsz // 8) * 8
    c_pad = jnp.zeros((rows, d), f32).at[:bsz].set(c)
    tn = 512
    out = pl.pallas_call(
        _adaln_kernel,
        grid=(n // tn,),
        in_specs=[pl.BlockSpec((rows, d), lambda j: (0, 0)),
                  pl.BlockSpec((d, tn), lambda j: (0, j)),
                  pl.BlockSpec((1, tn), lambda j: (0, j))],
        out_specs=pl.BlockSpec((rows, tn), lambda j: (0, j)),
        out_shape=jax.ShapeDtypeStruct((rows, n), f32),
        name="adaln",
    )(c_pad, w_ada, b_ada.reshape(1, n))
    return out[:bsz]


def _proj_kernel(x_ref, pos_ref, sc_ref, sh_ref, g_ref, w_ref, ga_ref, gb_ref, gsum_ref, perm_ref, invf_ref,
                 qa_ref, ka_ref, va_ref, qi_ref, ki_ref, wi_ref, qb_ref, kb_ref, vb_ref):
    x = x_ref[0]
    ms = jnp.mean(x * x, axis=-1, keepdims=True)
    h = (x * lax.rsqrt(ms + EPS)) * g_ref[...]
    h = h * (1.0 + sc_ref[0]) + sh_ref[0]
    proj = jnp.dot(h.astype(bf16), w_ref[...], preferred_element_type=f32)

    ang = pos_ref[0] * invf_ref[...]
    cos = jnp.cos(ang)
    sin = jnp.sin(ang)
    gsum = gsum_ref[...]
    perm = perm_ref[...]
    heads = []
    for c in range(N_CHUNKS):
        a = proj[:, LANES * c:LANES * (c + 1)]
        b = proj[:, SLAB + LANES * c:SLAB + LANES * (c + 1)]
        if 4 * c < N_NORM_SLOTS:
            ss = a * a + b * b
            hi = ss.astype(bf16)
            lo = (ss - hi.astype(f32)).astype(bf16)
            tot = (jnp.dot(hi, gsum, preferred_element_type=f32)
                   + jnp.dot(lo, gsum, preferred_element_type=f32))
            inv = lax.rsqrt(tot * (1.0 / HEAD_DIM) + EPS)
            a = a * inv * ga_ref[:, LANES * c:LANES * (c + 1)]
            b = b * inv * gb_ref[:, LANES * c:LANES * (c + 1)]
        ra = a * cos - b * sin
        rb = b * cos + a * sin
        ab = jnp.concatenate([ra, rb], axis=1).astype(bf16)
        heads.append(jnp.dot(ab, perm, preferred_element_type=f32).astype(bf16))

    qa_ref[0, :, 0:256] = heads[0]
    qa_ref[0, :, 256:512] = heads[1]
    qa_ref[0, :, 512:640] = heads[2][:, 0:128]
    ka_ref[0] = heads[2][:, 128:256]
    qb_ref[0, :, 0:256] = heads[3]
    qb_ref[0, :, 256:384] = heads[4][:, 0:128]
    kb_ref[0, :, 0:128] = heads[4][:, 128:256]
    kb_ref[0, :, 128:384] = heads[5]
    qi_ref[0, :, 0:256] = heads[6]
    qi_ref[0, :, 256:512] = heads[7]
    ki_ref[0] = heads[8][:, 0:128]
    va_ref[0] = proj[:, _COL_VA:_COL_VB].astype(bf16)
    vb_ref[0] = proj[:, _COL_VB:_COL_WI].astype(bf16)
    wi_ref[0] = proj[:, _COL_WI:N_COL]


def _project(x, posf, sc, sh, g, w_perm, gain_a, gain_b):
    bsz, s, d = x.shape
    ts = min(512, s)
    gsum = jnp.asarray(_group_sum_matrix(), bf16)
    perm = jnp.asarray(_interleave_matrix(), bf16)
    inv = ROPE_THETA ** (-jnp.arange(HALF, dtype=f32) / HALF)
    invf = jnp.tile(inv, LANES // HALF)[None, :]
    wa = N_HEADS_A * HEAD_DIM
    wb = N_HEADS_B * HEAD_DIM
    wq = N_IDX_HEADS * IDX_DIM
    const = lambda b, i: (0, 0)
    tile = lambda b, i: (b, i, 0)
    per_b = lambda b, i: (b, 0, 0)
    out_shape = [jax.ShapeDtypeStruct((bsz, s, w), dt) for w, dt in
                 ((wa, bf16), (LANES, bf16), (LANES, bf16), (wq, bf16), (LANES, bf16), (LANES, f32),
                  (wb, bf16), (wb, bf16), (wb, bf16))]
    out_specs = [pl.BlockSpec((1, ts, sh_.shape[2]), tile) for sh_ in out_shape]
    return pl.pallas_call(
        _proj_kernel,
        grid=(bsz, s // ts),
        in_specs=[pl.BlockSpec((1, ts, d), tile),
                  pl.BlockSpec((1, ts, 1), tile),
                  pl.BlockSpec((1, 1, d), per_b),
                  pl.BlockSpec((1, 1, d), per_b),
                  pl.BlockSpec((1, d), const),
                  pl.BlockSpec((d, N_COL), const),
                  pl.BlockSpec((1, N_NORM_SLOTS * HALF), const),
                  pl.BlockSpec((1, N_NORM_SLOTS * HALF), const),
                  pl.BlockSpec((LANES, LANES), const),
                  pl.BlockSpec((2 * LANES, 2 * LANES), const),
                  pl.BlockSpec((1, LANES), const)],
        out_specs=out_specs,
        out_shape=out_shape,
        compiler_params=pltpu.CompilerParams(
            dimension_semantics=("arbitrary", "arbitrary"), vmem_limit_bytes=VMEM_LIMIT),
        name="in_proj",
    )(x, posf, sc, sh, g, w_perm, gain_a, gain_b, gsum, perm, invf)


def _sparse_attn_kernel(qa_ref, ka_ref, va_ref, qi_ref, ki_ref, wi_ref, triu_ref, o_ref,
                        keys_ref, qis_ref, qas_ref, wrep_ref, m_ref, acc_ref,
                        *, qb, tk, n_sel, w_scale):
    i = pl.program_id(1)
    n_pairs_a = N_HEADS_A // 2
    lane = lax.broadcasted_iota(i32, (qb, LANES), 1)
    left = lane < HEAD_DIM
    mask_l = left.astype(f32).astype(bf16)
    mask_r = (1.0 - left.astype(f32)).astype(bf16)

    for c in range(N_IDX_HEADS // 2):
        ch = qi_ref[0, :, LANES * c:LANES * (c + 1)]
        qis_ref[(2 * c) * qb:(2 * c + 1) * qb, :] = ch * mask_l
        qis_ref[(2 * c + 1) * qb:(2 * c + 2) * qb, :] = ch * mask_r
    q_scale = jnp.asarray(HEAD_DIM ** -0.5, bf16)
    for c in range(n_pairs_a):
        ch = qa_ref[0, :, LANES * c:LANES * (c + 1)] * q_scale
        qas_ref[0, c * qb:(c + 1) * qb, :] = ch * mask_l
        qas_ref[1, c * qb:(c + 1) * qb, :] = ch * mask_r
    wi = wi_ref[0] * w_scale
    for h in range(N_IDX_HEADS):
        wrep_ref[h] = jnp.broadcast_to(wi[:, h:h + 1], (qb, LANES))

    n_tiles = (i * qb) // tk + 1
    reps = tk // LANES
    row = lax.broadcasted_iota(i32, (qb, tk), 0) + i * qb
    col = lax.broadcasted_iota(i32, (qb, tk), 1)

    def score_body(j, rmax):
        start = pl.multiple_of(j * tk, tk)
        kt = ki_ref[0, pl.ds(start, tk), :]
        lg = _nt_dot(qis_ref[...], kt)
        acc = jnp.zeros((qb, tk), f32)
        for h in range(N_IDX_HEADS):
            wr = jnp.concatenate([wrep_ref[h]] * reps, axis=1)
            acc = acc + jnp.maximum(lg[h * qb:(h + 1) * qb], 0.0) * wr
        acc = jnp.where(col + j * tk <= row, acc, -jnp.inf)
        keys_ref[j] = acc
        for r in range(reps):
            rmax = jnp.maximum(rmax, acc[:, LANES * r:LANES * (r + 1)])
        return rmax

    rmax = lax.fori_loop(0, n_tiles, score_body, jnp.full((qb, LANES), -jnp.inf, f32))
    rmax = jnp.broadcast_to(jnp.max(rmax, axis=1, keepdims=True), (qb, LANES))

    def key_to_float(key):
        return lax.bitcast_convert_type(key ^ ((key >> 31) & 0x7FFFFFFF), f32)

    def float_to_key(v):
        bits = lax.bitcast_convert_type(v, i32)
        return bits ^ ((bits >> 31) & 0x7FFFFFFF)

    sweep_rows = min(qb, LANES)
    ones8 = jnp.ones((8, LANES), bf16)

    def to_rows(x_lane):
        return jnp.transpose(jnp.broadcast_to(x_lane[0:1, :], (LANES, qb)))

    def to_lanes(cnt):
        return _nt_dot(ones8, cnt.astype(bf16))

    def count_ge(trial_lane):
        trial_rows = to_rows(trial_lane)
        parts = []
        for r0 in range(0, qb, sweep_rows):
            trial_r = trial_rows[r0:r0 + sweep_rows]

            def body(j, cnt, r0=r0, trial_r=trial_r):
                for r in range(reps):
                    ch = keys_ref[j, r0:r0 + sweep_rows, LANES * r:LANES * (r + 1)]
                    cnt = cnt + jnp.where(ch >= trial_r, 1.0, 0.0)
                return cnt
            parts.append(lax.fori_loop(0, n_tiles, body, jnp.zeros((sweep_rows, LANES), f32)))
        return to_lanes(jnp.concatenate(parts, axis=0))

    log_target = math.log(n_sel - 0.5)
    rmax = jnp.transpose(rmax)[0:8, :]
    rmax_pad = rmax + jnp.abs(rmax) * 2.0 ** -20 + 1e-30

    def zero_counts(r0):
        def body(j, cnts):
            ge, gt = cnts
            for r in range(reps):
                ch = keys_ref[j, r0:r0 + sweep_rows, LANES * r:LANES * (r + 1)]
                ge = ge + jnp.where(ch >= 0.0, 1.0, 0.0)
                gt = gt + jnp.where(ch > 0.0, 1.0, 0.0)
            return ge, gt
        z = jnp.zeros((sweep_rows, LANES), f32)
        ge, gt = lax.fori_loop(0, n_tiles, body, (z, z))
        return to_lanes(ge), to_lanes(gt)

    zero_parts = [zero_counts(r0) for r0 in range(0, qb, sweep_rows)]
    ge0 = jnp.concatenate([p[0] for p in zero_parts], axis=1)
    gt0 = jnp.concatenate([p[1] for p in zero_parts], axis=1)
    zeros = jnp.zeros((8, qb), f32)
    total = zeros + (n_tiles * tk).astype(f32)
    above = ge0 >= n_sel
    lo_v0 = jnp.where(above, 0.0, -jnp.inf)
    lo_c0 = jnp.where(above, ge0, total)
    hi_v0 = jnp.where(above, jnp.inf, 0.0)
    hi_c0 = jnp.where(above, jnp.where(gt0 < n_sel, gt0, 0.0), ge0)
    done0 = (above & (gt0 < n_sel)) | (lo_c0 == n_sel)

    def search_cond(state):
        return (state[0] < SEARCH_MAX_STEPS) & state[-1]

    def search_body(state):
        it, lo_v, lo_c, f_lo, hi_v, hi_c, f_hi, last, done, _ = state
        lo_k = float_to_key(lo_v)
        hi_k = float_to_key(hi_v)
        hi_eff = jnp.where(hi_v == jnp.inf, rmax_pad, hi_v)
        t_int = lo_v + (hi_eff - lo_v) * (f_lo / (f_lo - f_hi))
        t_int = key_to_float(float_to_key(t_int))
        use_int = (lo_v > -jnp.inf) & (t_int > lo_v) & (t_int < hi_v) & (it < SEARCH_INTERP_STEPS)
        mid_k = (lo_k >> 1) + (hi_k >> 1) + (lo_k & hi_k & 1)
        t = jnp.where(use_int, t_int, key_to_float(mid_k))
        c = count_ge(t)
        f = jnp.log(jnp.maximum(c, 0.5)) - log_target
        active = done < 0.5
        is_lo = (c >= n_sel) & active
        is_hi = (c < n_sel) & active
        f_hi = jnp.where(is_lo & (last > 0.0), f_hi * 0.5, f_hi)
        f_lo = jnp.where(is_hi & (last < 0.0), f_lo * 0.5, f_lo)
        lo_v = jnp.where(is_lo, t, lo_v)
        lo_c = jnp.where(is_lo, c, lo_c)
        f_lo = jnp.where(is_lo, f, f_lo)
        hi_v = jnp.where(is_hi, t, hi_v)
        hi_c = jnp.where(is_hi, c, hi_c)
        f_hi = jnp.where(is_hi, f, f_hi)
        last = jnp.where(is_lo, 1.0, jnp.where(is_hi, -1.0, last))
        settled = (lo_c == n_sel) | (float_to_key(lo_v) + 1 >= float_to_key(hi_v))
        done = jnp.where(settled, 1.0, done)
        unresolved = jnp.min(done) < 0.5
        return it + 1, lo_v, lo_c, f_lo, hi_v, hi_c, f_hi, last, done, unresolved

    done0 = jnp.where(done0, 1.0, 0.0)
    init = (jnp.int32(0), lo_v0, lo_c0, jnp.log(lo_c0) - log_target,
            hi_v0, hi_c0, jnp.log(jnp.maximum(hi_c0, 0.5)) - log_target,
            zeros, done0, jnp.min(done0) < 0.5)
    final = lax.while_loop(search_cond, search_body, init)
    thr, lo_c, hi_c = final[1], final[2], final[5]
    thr_t = jnp.concatenate([to_rows(thr)] * reps, axis=1)
    tied = lo_c != n_sel
    any_tied = jnp.max(jnp.where(tied, 1.0, 0.0)) > 0.0
    quota = to_rows(jnp.where(tied, n_sel - hi_c, 2.0 * tk * (n_tiles + 1).astype(f32)))
    quota_t = jnp.concatenate([quota] * reps, axis=1)

    m_ref[...] = jnp.full(m_ref.shape, NEG_BIG, f32)
    acc_ref[...] = jnp.zeros(acc_ref.shape, f32)
    lane_k = lax.broadcasted_iota(i32, (tk, LANES), 1)
    kmask_l = (lane_k < HEAD_DIM).astype(f32).astype(bf16)
    kmask_r = (lane_k >= HEAD_DIM).astype(f32).astype(bf16)

    def attn_body(j, n_ties, with_ties):
        start = pl.multiple_of(j * tk, tk)
        sc = keys_ref[j]
        if with_ties:
            eq = jnp.where(sc == thr_t, 1.0, 0.0)
            before = (jnp.dot(eq.astype(bf16), triu_ref[...], preferred_element_type=f32)
                      + jnp.concatenate([n_ties] * reps, axis=1))
            sel = (sc > thr_t) | ((sc == thr_t) & (before < quota_t))
            n_ties = n_ties + jnp.broadcast_to(jnp.sum(eq, axis=1, keepdims=True), (qb, LANES))
        else:
            sel = sc >= thr_t
        sel = sel & (col + j * tk <= row)
        bias = jnp.where(sel, 0.0, NEG_BIG)
        bias = jnp.concatenate([bias] * n_pairs_a, axis=0)
        kk = ka_ref[0, pl.ds(start, tk), :]
        vv = va_ref[0, pl.ds(start, tk), :]
        v_ext = (vv * kmask_l + kmask_r, vv * kmask_r + kmask_l)
        for g in range(N_KV_A):
            s = _nt_dot(qas_ref[g], kk) + bias
            m_old = m_ref[g]
            m_new = jnp.maximum(m_old, jnp.max(s, axis=1, keepdims=True))
            alpha = jnp.exp(m_old - m_new)
            p = jnp.exp(s - jnp.concatenate([m_new] * reps, axis=1))
            acc_ref[g] = alpha * acc_ref[g] + jnp.dot(p.astype(bf16), v_ext[g], preferred_element_type=f32)
            m_ref[g] = m_new
        return n_ties

    no_ties = jnp.zeros((qb, LANES), f32)

    @pl.when(any_tied)
    def _():
        lax.fori_loop(0, n_tiles, functools.partial(attn_body, with_ties=True), no_ties)

    @pl.when(jnp.logical_not(any_tied))
    def _():
        lax.fori_loop(0, n_tiles, functools.partial(attn_body, with_ties=False), no_ties)

    for c in range(n_pairs_a):
        rows = slice(c * qb, (c + 1) * qb)
        a0 = acc_ref[0, rows, :]
        a1 = acc_ref[1, rows, :]
        o0 = a0 / pltpu.roll(a0, HEAD_DIM, axis=1)
        o1 = a1 / pltpu.roll(a1, HEAD_DIM, axis=1)
        o_ref[0, :, LANES * c:LANES * (c + 1)] = jnp.where(left, o0, o1).astype(bf16)


def _sparse_attention(qa, ka, va, qi, ki, wi):
    bsz, s, wa = qa.shape
    qb = min(256, s)
    tk = min(512, s)
    n_sel = min(TOPK_MAX, s // 4)
    w_scale = N_IDX_HEADS ** -0.5 * IDX_DIM ** -0.5
    n_pairs_a = N_HEADS_A // 2
    tile = lambda b, i: (b, i, 0)
    per_b = lambda b, i: (b, 0, 0)
    kern = functools.partial(_sparse_attn_kernel, qb=qb, tk=tk, n_sel=float(n_sel), w_scale=w_scale)
    return pl.pallas_call(
        kern,
        grid=(bsz, s // qb),
        in_specs=[pl.BlockSpec((1, qb, wa), tile),
                  pl.BlockSpec((1, s, LANES), per_b),
                  pl.BlockSpec((1, s, LANES), per_b),
                  pl.BlockSpec((1, qb, qi.shape[2]), tile),
                  pl.BlockSpec((1, s, LANES), per_b),
                  pl.BlockSpec((1, qb, LANES), tile),
                  pl.BlockSpec((tk, tk), lambda b, i: (0, 0))],
        out_specs=pl.BlockSpec((1, qb, wa), tile),
        out_shape=jax.ShapeDtypeStruct((bsz, s, wa), bf16),
        scratch_shapes=[pltpu.VMEM((s // tk, qb, tk), f32),
                        pltpu.VMEM((N_IDX_HEADS * qb, LANES), bf16),
                        pltpu.VMEM((N_KV_A, n_pairs_a * qb, LANES), bf16),
                        pltpu.VMEM((N_IDX_HEADS, qb, LANES), f32),
                        pltpu.VMEM((N_KV_A, n_pairs_a * qb, LANES), f32),
                        pltpu.VMEM((N_KV_A, n_pairs_a * qb, LANES), f32)],
        compiler_params=pltpu.CompilerParams(
            dimension_semantics=("arbitrary", "arbitrary"), vmem_limit_bytes=VMEM_LIMIT),
        name="sparse_attn",
    )(qa, ka, va, qi, ki, wi, jnp.asarray(np.triu(np.ones((tk, tk), np.float32), k=1), bf16))


def _dilated_kernel(q_ref, k_ref, v_ref, bias_ref, o_ref, qs_ref, m_ref, acc_ref, *, tq, nd):
    i = pl.program_id(2)
    lane = lax.broadcasted_iota(i32, (tq, LANES), 1)
    left = lane < HEAD_DIM
    mask_l = left.astype(f32).astype(bf16)
    mask_r = (1.0 - left.astype(f32)).astype(bf16)
    q = q_ref[0] * jnp.asarray(HEAD_DIM ** -0.5, bf16)
    qs_ref[0:tq, :] = q * mask_l
    qs_ref[tq:2 * tq, :] = q * mask_r
    ones = jnp.ones((tq, LANES), bf16)

    for d in range(nd):
        j = i - d
        start = pl.multiple_of(jnp.maximum(j, 0) * tq, tq)
        kk = k_ref[0, pl.ds(start, tq), :]
        v_ext = jnp.concatenate([v_ref[0, pl.ds(start, tq), :], ones], axis=1)
        b = bias_ref[d] + jnp.where(j >= 0, 0.0, NEG_BIG)
        s = _nt_dot(qs_ref[...], kk) + jnp.concatenate([b, b], axis=0)
        m_d = jnp.broadcast_to(jnp.max(s, axis=1, keepdims=True), (2 * tq, LANES))
        p = jnp.exp(s - jnp.concatenate([m_d] * (tq // LANES), axis=1))
        acc_ref[d] = jnp.dot(p.astype(bf16), v_ext, preferred_element_type=f32)
        m_ref[d] = m_d

    m = m_ref[0]
    for d in range(1, nd):
        m = jnp.maximum(m, m_ref[d])
    acc = jnp.zeros((2 * tq, 2 * LANES), f32)
    for d in range(nd):
        w = jnp.exp(m_ref[d] - m)
        acc = acc + jnp.concatenate([w, w], axis=1) * acc_ref[d]
    o0 = acc[0:tq, 0:LANES] / acc[0:tq, LANES:2 * LANES]
    o1 = acc[tq:2 * tq, 0:LANES] / acc[tq:2 * tq, LANES:2 * LANES]
    o_ref[0] = jnp.where(left, o0, o1).astype(bf16)


def _dilated_attention(qb_, kb_, vb_):
    bsz, s, wb = qb_.shape
    tq = min(512, s)
    bias_np, nd = _dilated_bias(tq)
    bias = jnp.asarray(bias_np)
    n_pairs = wb // LANES
    kern = functools.partial(_dilated_kernel, tq=tq, nd=nd)
    return pl.pallas_call(
        kern,
        grid=(bsz, n_pairs, s // tq),
        in_specs=[pl.BlockSpec((1, tq, LANES), lambda b, p, i: (b, i, p)),
                  pl.BlockSpec((1, s, LANES), lambda b, p, i: (b, 0, p)),
                  pl.BlockSpec((1, s, LANES), lambda b, p, i: (b, 0, p)),
                  pl.BlockSpec((nd, tq, tq), lambda b, p, i: (0, 0, 0))],
        out_specs=pl.BlockSpec((1, tq, LANES), lambda b, p, i: (b, i, p)),
        out_shape=jax.ShapeDtypeStruct((bsz, s, wb), bf16),
        scratch_shapes=[pltpu.VMEM((2 * tq, LANES), bf16),
                        pltpu.VMEM((nd, 2 * tq, LANES), f32),
                        pltpu.VMEM((nd, 2 * tq, 2 * LANES), f32)],
        compiler_params=pltpu.CompilerParams(
            dimension_semantics=("arbitrary", "arbitrary", "arbitrary"), vmem_limit_bytes=VMEM_LIMIT),
        name="dilated_attn",
    )(qb_, kb_, vb_, bias)


def _mid_kernel(oa_ref, ob_ref, x_ref, gta_ref, scf_ref, shf_ref, gtf_ref, woa_ref, wob_ref, gffn_ref,
                wsgu_ref, wsd_ref, wrt_ref, rbias_ref, triu_ref,
                base_ref, h2_ref, sel_ref, gate_ref, rank_ref, cnt_ref, carry_ref, *, n_exp, t, d_sh):
    step = pl.program_id(0)

    @pl.when(step == 0)
    def _():
        carry_ref[...] = jnp.zeros(carry_ref.shape, f32)

    mix = (jnp.dot(oa_ref[...], woa_ref[...], preferred_element_type=f32)
           + jnp.dot(ob_ref[...], wob_ref[...], preferred_element_type=f32))
    x1 = x_ref[...] + gta_ref[0] * mix
    ms = jnp.mean(x1 * x1, axis=-1, keepdims=True)
    h2 = ((x1 * lax.rsqrt(ms + EPS)) * gffn_ref[...]) * (1.0 + scf_ref[0]) + shf_ref[0]
    h2b = h2.astype(bf16)
    h2f = h2b.astype(f32)
    n_sub = h2_ref.shape[0] // t
    for a in range(n_sub):
        h2_ref[pl.ds(a, t, stride=n_sub), :] = h2f[:, LANES * a:LANES * (a + 1)]

    gu = jnp.dot(h2b, wsgu_ref[...], preferred_element_type=f32)
    g = gu[:, :d_sh]
    u = gu[:, d_sh:]
    act = (g * jax.nn.sigmoid(g)) * u
    shared = jnp.dot(act.astype(bf16), wsd_ref[...], preferred_element_type=f32)
    base = x1 + gtf_ref[0] * shared
    for a in range(n_sub):
        base_ref[pl.ds(a, t, stride=n_sub), :] = base[:, LANES * a:LANES * (a + 1)]

    scores = jax.nn.sigmoid(_nt_dot(wrt_ref[...], h2b))
    biased = scores + rbias_ref[...]
    per = n_exp // N_GROUPS
    neg_inf = jnp.float32(-jnp.inf)
    ri_g = lax.broadcasted_iota(i32, (per, t), 0).astype(f32)
    gs = []
    for grp in range(N_GROUPS):
        blk = biased[grp * per:(grp + 1) * per]
        m1 = jnp.max(blk, axis=0, keepdims=True)
        idx1 = jnp.min(jnp.where(blk == m1, ri_g, float(per)), axis=0, keepdims=True)
        m2 = jnp.max(jnp.where(ri_g == idx1, neg_inf, blk), axis=0, keepdims=True)
        gs.append(m1 + m2)
    masked_rows = []
    for grp in range(N_GROUPS):
        beaten = jnp.zeros((1, t), f32)
        for g2 in range(N_GROUPS):
            if g2 == grp:
                continue
            wins = gs[g2] > gs[grp]
            if g2 < grp:
                wins = wins | (gs[g2] == gs[grp])
            beaten = beaten + jnp.where(wins, 1.0, 0.0)
        keep = jnp.broadcast_to(beaten < TOPK_GROUPS, (per, t))
        masked_rows.append(jnp.where(keep, biased[grp * per:(grp + 1) * per], neg_inf))
    masked = jnp.concatenate(masked_rows, axis=0)

    ri = lax.broadcasted_iota(i32, (n_exp, t), 0).astype(f32)
    selmask = jnp.zeros((n_exp, t), f32)
    idxs, graw = [], []
    for _ in range(TOP_K):
        m = jnp.max(masked, axis=0, keepdims=True)
        idx = jnp.min(jnp.where(masked == m, ri, float(n_exp)), axis=0, keepdims=True)
        onehot = ri == idx
        graw.append(jnp.sum(jnp.where(onehot, scores, 0.0), axis=0, keepdims=True))
        masked = jnp.where(onehot, neg_inf, masked)
        selmask = jnp.where(onehot, 1.0, selmask)
        idxs.append(idx)
    den = graw[0]
    for k in range(1, TOP_K):
        den = den + graw[k]

    prefix = jnp.dot(selmask.astype(bf16), triu_ref[...], preferred_element_type=f32)
    prefix = prefix + jnp.concatenate([carry_ref[...]] * (t // LANES), axis=1)
    for k in range(TOP_K):
        rank_k = jnp.sum(jnp.where(ri == idxs[k], prefix, 0.0), axis=0, keepdims=True)
        sel_ref[k:k + 1, :] = idxs[k].astype(i32)
        rank_ref[k:k + 1, :] = rank_k.astype(i32)
        gate_ref[k:k + 1, :] = graw[k] / den * ROUTED_SCALE
    carry_ref[...] = carry_ref[...] + jnp.broadcast_to(
        jnp.sum(selmask, axis=1, keepdims=True), carry_ref.shape)
    cnt_ref[...] = carry_ref[...]


def _mid(oa, ob, x2, gta, scf, shf, gtf, woa, wob, gffn, wsgu, wsd, wrt, rbias_rep, s):
    n, d = x2.shape
    t = rbias_rep.shape[1]
    n_exp = wrt.shape[0]
    d_sh = wsd.shape[0]
    triu = jnp.asarray(np.triu(np.ones((t, t), np.float32), k=1), bf16)
    tile = lambda i: (i, 0)
    const = lambda i: (0, 0)
    per_b = lambda i: ((i * t) // s, 0, 0)
    lane_tile = lambda i: (0, i)
    n_sub = d // LANES
    tok_tiles = jax.ShapeDtypeStruct((n * n_sub, LANES), f32)
    tok_spec = pl.BlockSpec((t * n_sub, LANES), tile)
    out_shape = [tok_tiles, tok_tiles,
                 jax.ShapeDtypeStruct((TOP_K, n), i32), jax.ShapeDtypeStruct((TOP_K, n), f32),
                 jax.ShapeDtypeStruct((TOP_K, n), i32), jax.ShapeDtypeStruct((n_exp, LANES), f32)]
    out_specs = [tok_spec, tok_spec,
                 pl.BlockSpec((TOP_K, t), lane_tile), pl.BlockSpec((TOP_K, t), lane_tile),
                 pl.BlockSpec((TOP_K, t), lane_tile), pl.BlockSpec((n_exp, LANES), const)]
    kern = functools.partial(_mid_kernel, n_exp=n_exp, t=t, d_sh=d_sh)
    return pl.pallas_call(
        kern,
        grid=(n // t,),
        in_specs=[pl.BlockSpec((t, oa.shape[1]), tile),
                  pl.BlockSpec((t, ob.shape[1]), tile),
                  pl.BlockSpec((t, d), tile),
                  pl.BlockSpec((1, 1, d), per_b),
                  pl.BlockSpec((1, 1, d), per_b),
                  pl.BlockSpec((1, 1, d), per_b),
                  pl.BlockSpec((1, 1, d), per_b),
                  pl.BlockSpec(woa.shape, const),
                  pl.BlockSpec(wob.shape, const),
                  pl.BlockSpec((1, d), const),
                  pl.BlockSpec(wsgu.shape, const),
                  pl.BlockSpec(wsd.shape, const),
                  pl.BlockSpec(wrt.shape, const),
                  pl.BlockSpec(rbias_rep.shape, const),
                  pl.BlockSpec((t, t), const)],
        out_specs=out_specs,
        out_shape=out_shape,
        scratch_shapes=[pltpu.VMEM((n_exp, LANES), f32)],
        compiler_params=pltpu.CompilerParams(
            dimension_semantics=("arbitrary",), vmem_limit_bytes=VMEM_LIMIT),
        name="mid",
    )(oa, ob, x2, gta, scf, shf, gtf, woa, wob, gffn, wsgu, wsd, wrt, rbias_rep, triu)


def _dispatch_kernel(zb_ref, slot_ref, h_ref, xs_ref, zero_ref, sem_ref, *, td, n_zero):
    step = pl.program_id(0)

    def zero_copy(b):
        return pltpu.make_async_copy(zero_ref, xs_ref.at[pl.ds(zb_ref[b], ZERO_ROWS)], sem_ref.at[0])

    @pl.when(step == 0)
    def _():
        zero_ref[...] = jnp.zeros(zero_ref.shape, f32)

        def start(b, c):
            @pl.when(zb_ref[b] >= 0)
            def _():
                zero_copy(b).start()
            return c

        def wait(b, c):
            @pl.when(zb_ref[b] >= 0)
            def _():
                zero_copy(b).wait()
            return c

        lax.fori_loop(0, n_zero, start, 0)
        lax.fori_loop(0, n_zero, wait, 0)

    def start_rows(tok, c):
        for k in range(TOP_K):
            pltpu.make_async_copy(h_ref.at[pl.ds(tok, 1)], xs_ref.at[pl.ds(slot_ref[k, tok], 1)],
                                  sem_ref.at[1]).start(priority=k % 2)
        return c

    lax.fori_loop(0, td, start_rows, 0)
    for k in range(TOP_K):
        pltpu.make_async_copy(h_ref, xs_ref.at[pl.ds(0, td)], sem_ref.at[1]).wait()


def _dispatch(zero_start, slot, h2, n_slots):
    n, rows, lanes = h2.shape
    td = min(256, n)
    kern = functools.partial(_dispatch_kernel, td=td, n_zero=zero_start.shape[0])
    grid_spec = pltpu.PrefetchScalarGridSpec(
        num_scalar_prefetch=1,
        grid=(n // td,),
        in_specs=[pl.BlockSpec((TOP_K, td), lambda i, tail: (0, i), memory_space=pltpu.SMEM),
                  pl.BlockSpec((td, rows, lanes), lambda i, tail: (i, 0, 0))],
        out_specs=pl.BlockSpec(memory_space=pl.ANY),
        scratch_shapes=[pltpu.VMEM((ZERO_ROWS, rows, lanes), f32), pltpu.SemaphoreType.DMA((2,))],
    )
    return pl.pallas_call(
        kern,
        grid_spec=grid_spec,
        out_shape=jax.ShapeDtypeStruct((n_slots, rows, lanes), f32),
        compiler_params=pltpu.CompilerParams(
            dimension_semantics=("arbitrary",), vmem_limit_bytes=VMEM_LIMIT),
        name="dispatch",
    )(zero_start, slot, h2)


def _expert_kernel(be_ref, nu_ref, xs_ref, wg_ref, wu_ref, wd_ref, ys_ref, wgu_s, wd_s, *, n_sub):
    i = pl.program_id(0)
    f = wd_s.shape[0]
    rows = xs_ref.shape[0] // n_sub

    @pl.when(i >= nu_ref[0])
    def _():
        ys_ref[...] = jnp.zeros(ys_ref.shape, f32)

    @pl.when(i < nu_ref[0])
    def _():
        @pl.when((i == 0) | (be_ref[i] != be_ref[jnp.maximum(i - 1, 0)]))
        def _():
            wgu_s[:, 0:f] = wg_ref[0].astype(bf16)
            wgu_s[:, f:2 * f] = wu_ref[0].astype(bf16)
            wd_s[...] = wd_ref[0].astype(bf16)

        xb = jnp.concatenate([xs_ref[pl.ds(a, rows, stride=n_sub), :] for a in range(n_sub)],
                             axis=1).astype(bf16)
        gu = jnp.dot(xb, wgu_s[...], preferred_element_type=f32)
        g = gu[:, 0:f]
        u = gu[:, f:2 * f]
        act = (g * jax.nn.sigmoid(g)) * u
        y = jnp.dot(act.astype(bf16), wd_s[...], preferred_element_type=f32)
        for a in range(n_sub):
            ys_ref[pl.ds(a, rows, stride=n_sub), :] = y[:, LANES * a:LANES * (a + 1)]


def _experts(block_expert, n_used, xs, w_gate, w_up, w_down):
    n_slots, n_sub, lanes = xs.shape
    d = n_sub * lanes
    n_blocks = n_slots // MOE_BLOCK
    f = w_gate.shape[2]
    blk = lambda i, be, nu: (jnp.minimum(i, nu[0] - 1), 0)
    wsel = lambda i, be, nu: (be[jnp.minimum(i, nu[0] - 1)], 0, 0)
    grid_spec = pltpu.PrefetchScalarGridSpec(
        num_scalar_prefetch=2,
        grid=(n_blocks,),
        in_specs=[pl.BlockSpec((MOE_BLOCK * n_sub, lanes), blk),
                  pl.BlockSpec((1, d, f), wsel),
                  pl.BlockSpec((1, d, f), wsel),
                  pl.BlockSpec((1, f, d), wsel)],
        out_specs=pl.BlockSpec((MOE_BLOCK * n_sub, lanes), lambda i, be, nu: (i, 0)),
        scratch_shapes=[pltpu.VMEM((d, 2 * f), bf16), pltpu.VMEM((f, d), bf16)],
    )
    ys = pl.pallas_call(
        functools.partial(_expert_kernel, n_sub=n_sub),
        grid_spec=grid_spec,
        out_shape=jax.ShapeDtypeStruct((n_slots * n_sub, lanes), f32),
        compiler_params=pltpu.CompilerParams(
            dimension_semantics=("arbitrary",), vmem_limit_bytes=VMEM_LIMIT),
        name="experts",
    )(block_expert, n_used, xs.reshape(n_slots * n_sub, lanes), w_gate, w_up, w_down)
    return ys.reshape(n_slots, n_sub, lanes)


def _combine_kernel(slot_ref, nslot_ref, gate_ref, base_ref, gtf_ref, ys_ref, o_ref, buf_ref, res_ref, sem_ref,
                    *, tc):
    step = pl.program_id(0)
    cur = step % 2

    def start_gathers(idx_ref, b):
        def body(tok, c):
            for k in range(TOP_K):
                pltpu.make_async_copy(ys_ref.at[pl.ds(idx_ref[k, tok], 1)],
                                      buf_ref.at[b, k, pl.ds(tok, 1)], sem_ref.at[b]).start(priority=k % 2)
            return c
        lax.fori_loop(0, tc, body, 0)

    @pl.when(step == 0)
    def _():
        start_gathers(slot_ref, 0)

    @pl.when(step + 1 < pl.num_programs(0))
    def _():
        start_gathers(nslot_ref, 1 - cur)

    for k in range(TOP_K):
        pltpu.make_async_copy(ys_ref.at[pl.ds(0, tc)], buf_ref.at[cur, k], sem_ref.at[cur]).wait()

    gtf = gtf_ref[0]

    def reduce_token(tok, c):
        routed = buf_ref[cur, 0, tok] * gate_ref[0, tok]
        for k in range(1, TOP_K):
            routed = routed + buf_ref[cur, k, tok] * gate_ref[k, tok]
        res_ref[pl.ds(pl.multiple_of(tok * n_sub, n_sub), n_sub), :] = base_ref[tok] + gtf * routed
        return c

    n_sub = buf_ref.shape[3]
    lax.fori_loop(0, tc, reduce_token, 0, unroll=4)
    for a in range(n_sub):
        o_ref[:, LANES * a:LANES * (a + 1)] = res_ref[pl.ds(a, tc, stride=n_sub), :]


def _combine(slot, gate, base, gtf, ys, s):
    n, rows, lanes = base.shape
    tc = min(128, n)
    n_steps = n // tc
    kern = functools.partial(_combine_kernel, tc=tc)
    tok_tile = lambda i: (i, 0, 0)
    return pl.pallas_call(
        kern,
        grid=(n_steps,),
        in_specs=[pl.BlockSpec((TOP_K, tc), lambda i: (0, i), memory_space=pltpu.SMEM),
                  pl.BlockSpec((TOP_K, tc), lambda i: (0, jnp.minimum(i + 1, n_steps - 1)),
                               memory_space=pltpu.SMEM),
                  pl.BlockSpec((TOP_K, tc), lambda i: (0, i), memory_space=pltpu.SMEM),
                  pl.BlockSpec((tc, rows, lanes), tok_tile),
                  pl.BlockSpec((1, rows, lanes), lambda i: ((i * tc) // s, 0, 0)),
                  pl.BlockSpec(memory_space=pl.ANY)],
        out_specs=pl.BlockSpec((tc, rows * lanes), lambda i: (i, 0)),
        out_shape=jax.ShapeDtypeStruct((n, rows * lanes), f32),
        scratch_shapes=[pltpu.VMEM((2, TOP_K, tc, rows, lanes), f32), pltpu.VMEM((tc * rows, lanes), f32),
                        pltpu.SemaphoreType.DMA((2,))],
        compiler_params=pltpu.CompilerParams(
            dimension_semantics=("arbitrary",), vmem_limit_bytes=VMEM_LIMIT),
        name="combine",
    )(slot, slot, gate, base, gtf, ys)


def _layer(x, c, posf, w_ada, b_ada, g_mix, w_in, q_norm_a, k_norm_a, q_norm_b, k_norm_b, w_out, g_ffn,
           w_router, router_bias, w_gate, w_up, w_down, ws_gate, ws_up, ws_down):
    bsz, s, d = x.shape
    n = bsz * s
    n_exp = w_router.shape[1]

    mod = _adaln(c, w_ada, b_ada)[:, None, :]
    sh_a, sc_a, gt_a, sh_f, sc_f, gt_f = (mod[..., k * d:(k + 1) * d] for k in range(6))

    w_perm = _take_runs(w_in.astype(bf16), _projection_columns(), axis=1)
    gains = {"qa": q_norm_a, "ka": k_norm_a, "qb": q_norm_b, "kb": k_norm_b}
    gain_a = jnp.concatenate([gains[kind][:HALF] for kind, _ in _SLOTS[:N_NORM_SLOTS]])[None, :].astype(f32)
    gain_b = jnp.concatenate([gains[kind][HALF:] for kind, _ in _SLOTS[:N_NORM_SLOTS]])[None, :].astype(f32)

    qa, ka, va, qi, ki, wi, qb_, kb_, vb_ = _project(x, posf, sc_a, sh_a, g_mix[None, :], w_perm, gain_a, gain_b)
    o_a = _sparse_attention(qa, ka, va, qi, ki, wi)
    o_b = _dilated_attention(qb_, kb_, vb_)

    wa = N_HEADS_A * HEAD_DIM
    rows_a = np.concatenate([np.arange(h * HEAD_DIM, (h + 1) * HEAD_DIM) for h in QA_PAIR_ORDER])
    woa = _take_runs(w_out, rows_a, axis=0).astype(bf16)
    wob = w_out[wa:].astype(bf16)
    wsgu = jnp.concatenate([ws_gate, ws_up], axis=1).astype(bf16)
    wsd = ws_down.astype(bf16)
    wrt = w_router.T.astype(bf16)
    t_mid = min(512, n)
    rbias_rep = jnp.broadcast_to(router_bias.astype(f32)[:, None], (n_exp, t_mid))

    base, h2, sel, gate, rank, cnt = _mid(
        o_a.reshape(n, wa), o_b.reshape(n, -1), x.reshape(n, d), gt_a, sc_f, sh_f, gt_f,
        woa, wob, g_ffn[None, :], wsgu, wsd, wrt, rbias_rep, s)
    base = base.reshape(n, d // LANES, LANES)
    h2 = h2.reshape(n, d // LANES, LANES)

    counts = cnt[:, 0].astype(i32)
    padded = (counts + MOE_BLOCK - 1) // MOE_BLOCK * MOE_BLOCK
    pad_end = jnp.cumsum(padded)
    pad_start = pad_end - padded
    onehot = sel[:, :, None] == jnp.arange(n_exp, dtype=i32)[None, None, :]
    slot = jnp.sum(jnp.where(onehot, pad_start[None, None, :], 0), axis=-1) + rank
    n_blocks = -(-(n * TOP_K) // MOE_BLOCK) + n_exp
    n_slots = n_blocks * MOE_BLOCK
    block_start = jnp.arange(n_blocks, dtype=i32) * MOE_BLOCK
    block_expert = jnp.sum((pad_end[None, :] <= block_start[:, None]).astype(i32), axis=1)
    block_expert = jnp.minimum(block_expert, n_exp - 1)
    n_used = (pad_end[-1] // MOE_BLOCK).astype(i32)[None]
    unit_start = jnp.arange(n_slots // ZERO_ROWS, dtype=i32) * ZERO_ROWS
    unit_expert = jnp.sum((pad_end[None, :] <= unit_start[:, None]).astype(i32), axis=1)
    real_end = pad_start + counts
    unit_real_end = jnp.sum(jnp.where(unit_expert[:, None] == jnp.arange(n_exp, dtype=i32)[None, :],
                                      real_end[None, :], 0), axis=1)
    has_pad = (unit_start + ZERO_ROWS > unit_real_end) | (unit_start >= pad_end[-1])
    zero_start = jnp.where(has_pad, unit_start, -1).astype(i32)

    xs = _dispatch(zero_start, slot, h2, n_slots)
    ys = _experts(block_expert, n_used, xs, w_gate, w_up, w_down)
    out = _combine(slot, gate, base, gt_f.reshape(bsz, d // LANES, LANES), ys, s)
    return out.reshape(bsz, s, d)


def kernel(x, c, positions, w_ada, b_ada, g_mix, w_in, q_norm_a, k_norm_a, q_norm_b, k_norm_b, w_out, g_ffn,
           w_router, router_bias, w_gate, w_up, w_down, ws_gate, ws_up, ws_down):
    posf = positions.astype(f32)[..., None]
    for l in range(w_ada.shape[0]):
        x = _layer(x, c, posf, w_ada[l], b_ada[l], g_mix[l], w_in[l], q_norm_a[l], k_norm_a[l], q_norm_b[l],
                   k_norm_b[l], w_out[l], g_ffn[l], w_router[l], router_bias[l], w_gate[l], w_up[l], w_down[l],
                   ws_gate[l], ws_up[l], ws_down[l])
    return x
```

```python
import functools
import math

import numpy as np
import jax
import jax.numpy as jnp
from jax import lax
from jax.experimental import pallas as pl
from jax.experimental.pallas import tpu as pltpu

f32 = jnp.float32
bf16 = jnp.bfloat16
i32 = jnp.int32

HEAD_DIM = 64
HALF = HEAD_DIM // 2
N_HEADS_A = 10
N_KV_A = 2
N_HEADS_B = 6
N_IDX_HEADS = 8
IDX_DIM = 64
TOPK_MAX = 256
DILATED_PATTERNS = ((128, 1), (512, 4), (2048, 16))
ROPE_THETA = 10000.0
EPS = 1e-6
TOP_K = 8
N_GROUPS = 8
TOPK_GROUPS = 4
ROUTED_SCALE = 2.5
MOE_BLOCK = 512
ZERO_ROWS = 64

LANES = 128
SUBLANES = 8
VMEM_LIMIT = 48 * 1024 * 1024

ADALN_COLS = 512
PROJ_ROWS = 512
ATTN_QUERY_ROWS = 256
ATTN_KEY_TILE = 512
DILATED_TILE = 512
MID_ROWS = 512
DISPATCH_ROWS = 512
COMBINE_ROWS = 256

NEG_BIG = -1e30
KEY_NEG_INF = int(np.int32(np.uint32(0xFF800000) ^ np.uint32(0x7FFFFFFF)))
KEY_POS_INF = 0x7F800000
SEARCH_INTERP_STEPS = 16
SEARCH_MAX_STEPS = SEARCH_INTERP_STEPS + 34

_OFF_QA = 0
_OFF_KA = _OFF_QA + N_HEADS_A * HEAD_DIM
_OFF_VA = _OFF_KA + N_KV_A * HEAD_DIM
_OFF_QI = _OFF_VA + N_KV_A * HEAD_DIM
_OFF_KI = _OFF_QI + N_IDX_HEADS * IDX_DIM
_OFF_WI = _OFF_KI + IDX_DIM
_OFF_QB = _OFF_WI + N_IDX_HEADS
_OFF_KB = _OFF_QB + N_HEADS_B * HEAD_DIM
_OFF_VB = _OFF_KB + N_HEADS_B * HEAD_DIM
D_IN = _OFF_VB + N_HEADS_B * HEAD_DIM

QA_PAIR_ORDER = (0, 5, 1, 6, 2, 7, 3, 8, 4, 9)

_SLOTS = (
    [("qa", h) for h in QA_PAIR_ORDER[:8]] + [("qa", 4), ("qa", 9), ("ka", 0), ("ka", 1)]
    + [("qb", h) for h in range(6)] + [("kb", h) for h in range(6)]
    + [("qi", h) for h in range(8)] + [("ki", 0), ("ki", 0), ("pad", 0), ("pad", 0)]
)
N_NORM_SLOTS = 24
N_CHUNKS = len(_SLOTS) // 4
SLAB = N_CHUNKS * LANES
_COL_VA = 2 * SLAB
_COL_VB = _COL_VA + N_KV_A * HEAD_DIM
_COL_WI = _COL_VB + N_HEADS_B * HEAD_DIM
N_COL = _COL_WI + LANES


def _slot_offset(kind, h):
    base = {"qa": _OFF_QA, "ka": _OFF_KA, "qb": _OFF_QB, "kb": _OFF_KB, "qi": _OFF_QI, "ki": _OFF_KI}
    return base[kind] + h * HEAD_DIM


def _projection_columns():
    zero_col = D_IN
    cols_a, cols_b = [], []
    for kind, h in _SLOTS:
        if kind == "pad":
            cols_a += [zero_col] * HALF
            cols_b += [zero_col] * HALF
        else:
            off = _slot_offset(kind, h)
            cols_a += list(range(off, off + HALF))
            cols_b += list(range(off + HALF, off + HEAD_DIM))
    cols = cols_a + cols_b
    cols += list(range(_OFF_VA, _OFF_VA + N_KV_A * HEAD_DIM))
    cols += list(range(_OFF_VB, _OFF_VB + N_HEADS_B * HEAD_DIM))
    cols += list(range(_OFF_WI, _OFF_WI + N_IDX_HEADS)) + [zero_col] * (LANES - N_IDX_HEADS)
    assert len(cols) == N_COL
    return np.asarray(cols, np.int32)


def _take_runs(w, idx, axis):
    size = w.shape[axis]
    pieces, a = [], 0
    idx = [int(v) for v in idx]
    while a < len(idx):
        b = a + 1
        if idx[a] == size:
            while b < len(idx) and idx[b] == size:
                b += 1
            shape = list(w.shape)
            shape[axis] = b - a
            pieces.append(jnp.zeros(shape, w.dtype))
        else:
            while b < len(idx) and idx[b] == idx[b - 1] + 1:
                b += 1
            pieces.append(lax.slice_in_dim(w, idx[a], idx[b - 1] + 1, axis=axis))
        a = b
    return jnp.concatenate(pieces, axis=axis)


def _interleave_matrix():
    p = np.zeros((2 * LANES, 2 * LANES), np.float32)
    for head in range(4):
        for i in range(HALF):
            p[HALF * head + i, HEAD_DIM * head + i] = 1.0
            p[LANES + HALF * head + i, HEAD_DIM * head + HALF + i] = 1.0
    return p


def _group_sum_matrix():
    g = np.zeros((LANES, LANES), np.float32)
    for k in range(LANES // HALF):
        g[HALF * k:HALF * (k + 1), HALF * k:HALF * (k + 1)] = 1.0
    return g


def _dilated_bias(tq):
    max_win = max(w for w, _ in DILATED_PATTERNS)
    nd = max_win // tq + 1
    d = np.arange(nd)[:, None, None] * tq + np.arange(tq)[None, :, None] - np.arange(tq)[None, None, :]
    mult = np.zeros(d.shape, np.float64)
    for win, dil in DILATED_PATTERNS:
        mult += ((d >= 0) & (d <= win) & (d % dil == 0)).astype(np.float64)
    with np.errstate(divide="ignore"):
        bias = np.where(mult > 0, np.log(np.maximum(mult, 1.0)), NEG_BIG)
    return bias.astype(np.float32), nd


def _nt_dot(a, b):
    return lax.dot_general(a, b, (((1,), (1,)), ((), ())), preferred_element_type=f32)


def _adaln_kernel(c_ref, w_ref, b_ref, o_ref):
    c = c_ref[...]
    a = c * jax.nn.sigmoid(c)
    o_ref[...] = jnp.dot(a, w_ref[...], preferred_element_type=f32) + b_ref[...]


def _adaln(c, w_ada, b_ada):
    bsz, d = c.shape
    n = w_ada.shape[1]
    rows = -(-bsz // 8) * 8
    c_pad = jnp.zeros((rows, d), f32).at[:bsz].set(c)
    tn = min(ADALN_COLS, n)
    out = pl.pallas_call(
        _adaln_kernel,
        grid=(n // tn,),
        in_specs=[pl.BlockSpec((rows, d), lambda j: (0, 0)),
                  pl.BlockSpec((d, tn), lambda j: (0, j)),
                  pl.BlockSpec((1, tn), lambda j: (0, j))],
        out_specs=pl.BlockSpec((rows, tn), lambda j: (0, j)),
        out_shape=jax.ShapeDtypeStruct((rows, n), f32),
        name="adaln",
    )(c_pad, w_ada, b_ada.reshape(1, n))
    return out[:bsz]


def _proj_kernel(x_ref, pos_ref, sc_ref, sh_ref, g_ref, w_ref, ga_ref, gb_ref, gsum_ref, perm_ref, invf_ref,
                 qa_ref, ka_ref, va_ref, qi_ref, ki_ref, wi_ref, qb_ref, kb_ref, vb_ref):
    x = x_ref[0]
    ms = jnp.mean(x * x, axis=-1, keepdims=True)
    h = (x * lax.rsqrt(ms + EPS)) * g_ref[...]
    h = h * (1.0 + sc_ref[0]) + sh_ref[0]
    proj = jnp.dot(h.astype(bf16), w_ref[...], preferred_element_type=f32)

    ang = pos_ref[0] * invf_ref[...]
    cos = jnp.cos(ang)
    sin = jnp.sin(ang)
    gsum = gsum_ref[...]
    perm = perm_ref[...]
    heads = []
    for c in range(N_CHUNKS):
        a = proj[:, LANES * c:LANES * (c + 1)]
        b = proj[:, SLAB + LANES * c:SLAB + LANES * (c + 1)]
        if 4 * c < N_NORM_SLOTS:
            ss = a * a + b * b
            hi = ss.astype(bf16)
            lo = (ss - hi.astype(f32)).astype(bf16)
            tot = (jnp.dot(hi, gsum, preferred_element_type=f32)
                   + jnp.dot(lo, gsum, preferred_element_type=f32))
            inv = lax.rsqrt(tot * (1.0 / HEAD_DIM) + EPS)
            a = a * inv * ga_ref[:, LANES * c:LANES * (c + 1)]
            b = b * inv * gb_ref[:, LANES * c:LANES * (c + 1)]
        ra = a * cos - b * sin
        rb = b * cos + a * sin
        ab = jnp.concatenate([ra, rb], axis=1).astype(bf16)
        heads.append(jnp.dot(ab, perm, preferred_element_type=f32).astype(bf16))

    pair = LANES
    qa_ref[0, :, 0:2 * pair] = heads[0]
    qa_ref[0, :, 2 * pair:4 * pair] = heads[1]
    qa_ref[0, :, 4 * pair:5 * pair] = heads[2][:, 0:pair]
    ka_ref[0] = heads[2][:, pair:2 * pair]
    qb_ref[0, :, 0:2 * pair] = heads[3]
    qb_ref[0, :, 2 * pair:3 * pair] = heads[4][:, 0:pair]
    kb_ref[0, :, 0:pair] = heads[4][:, pair:2 * pair]
    kb_ref[0, :, pair:3 * pair] = heads[5]
    qi_ref[0, :, 0:2 * pair] = heads[6]
    qi_ref[0, :, 2 * pair:4 * pair] = heads[7]
    ki_ref[0] = heads[8][:, 0:pair]
    va_ref[0] = proj[:, _COL_VA:_COL_VB].astype(bf16)
    vb_ref[0] = proj[:, _COL_VB:_COL_WI].astype(bf16)
    wi_ref[0] = proj[:, _COL_WI:N_COL]


def _project(x, posf, sc, sh, g, w_perm, gain_a, gain_b):
    bsz, s, d = x.shape
    ts = min(PROJ_ROWS, s)
    gsum = jnp.asarray(_group_sum_matrix(), bf16)
    perm = jnp.asarray(_interleave_matrix(), bf16)
    inv = ROPE_THETA ** (-jnp.arange(HALF, dtype=f32) / HALF)
    invf = jnp.tile(inv, LANES // HALF)[None, :]
    wa = N_HEADS_A * HEAD_DIM
    wb = N_HEADS_B * HEAD_DIM
    wq = N_IDX_HEADS * IDX_DIM
    const = lambda b, i: (0, 0)
    tile = lambda b, i: (b, i, 0)
    per_b = lambda b, i: (b, 0, 0)
    out_shape = [jax.ShapeDtypeStruct((bsz, s, w), dt) for w, dt in
                 ((wa, bf16), (LANES, bf16), (LANES, bf16), (wq, bf16), (LANES, bf16), (LANES, f32),
                  (wb, bf16), (wb, bf16), (wb, bf16))]
    out_specs = [pl.BlockSpec((1, ts, sh_.shape[2]), tile) for sh_ in out_shape]
    return pl.pallas_call(
        _proj_kernel,
        grid=(bsz, s // ts),
        in_specs=[pl.BlockSpec((1, ts, d), tile),
                  pl.BlockSpec((1, ts, 1), tile),
                  pl.BlockSpec((1, 1, d), per_b),
                  pl.BlockSpec((1, 1, d), per_b),
                  pl.BlockSpec((1, d), const),
                  pl.BlockSpec((d, N_COL), const),
                  pl.BlockSpec((1, N_NORM_SLOTS * HALF), const),
                  pl.BlockSpec((1, N_NORM_SLOTS * HALF), const),
                  pl.BlockSpec((LANES, LANES), const),
                  pl.BlockSpec((2 * LANES, 2 * LANES), const),
                  pl.BlockSpec((1, LANES), const)],
        out_specs=out_specs,
        out_shape=out_shape,
        compiler_params=pltpu.CompilerParams(
            dimension_semantics=("arbitrary", "arbitrary"), vmem_limit_bytes=VMEM_LIMIT),
        name="in_proj",
    )(x, posf, sc, sh, g, w_perm, gain_a, gain_b, gsum, perm, invf)


def _sparse_attn_kernel(qa_ref, ka_ref, va_ref, qi_ref, ki_ref, wi_ref, triu_ref, o_ref,
                        score_ref, qis_ref, qas_ref, wrep_ref, m_ref, acc_ref,
                        *, qb, tk, n_sel, w_scale):
    i = pl.program_id(1)
    n_pairs_a = N_HEADS_A // 2
    lane = lax.broadcasted_iota(i32, (qb, LANES), 1)
    left = lane < HEAD_DIM
    mask_l = left.astype(f32).astype(bf16)
    mask_r = (1.0 - left.astype(f32)).astype(bf16)

    for c in range(N_IDX_HEADS // 2):
        ch = qi_ref[0, :, LANES * c:LANES * (c + 1)]
        qis_ref[(2 * c) * qb:(2 * c + 1) * qb, :] = ch * mask_l
        qis_ref[(2 * c + 1) * qb:(2 * c + 2) * qb, :] = ch * mask_r
    q_scale = jnp.asarray(HEAD_DIM ** -0.5, bf16)
    for c in range(n_pairs_a):
        ch = qa_ref[0, :, LANES * c:LANES * (c + 1)] * q_scale
        qas_ref[0, c * qb:(c + 1) * qb, :] = ch * mask_l
        qas_ref[1, c * qb:(c + 1) * qb, :] = ch * mask_r
    wi = wi_ref[0] * w_scale
    for h in range(N_IDX_HEADS):
        wrep_ref[h] = jnp.broadcast_to(wi[:, h:h + 1], (qb, LANES))

    n_tiles = (i * qb) // tk + 1
    reps = tk // LANES
    row = lax.broadcasted_iota(i32, (qb, tk), 0) + i * qb
    col = lax.broadcasted_iota(i32, (qb, tk), 1)

    def score_body(j, rmax):
        start = pl.multiple_of(j * tk, tk)
        kt = ki_ref[0, pl.ds(start, tk), :]
        lg = _nt_dot(qis_ref[...], kt)
        acc = jnp.zeros((qb, tk), f32)
        for h in range(N_IDX_HEADS):
            wr = jnp.concatenate([wrep_ref[h]] * reps, axis=1)
            acc = acc + jnp.maximum(lg[h * qb:(h + 1) * qb], 0.0) * wr
        acc = jnp.where(col + j * tk <= row, acc, -jnp.inf)
        score_ref[j] = acc
        for r in range(reps):
            rmax = jnp.maximum(rmax, acc[:, LANES * r:LANES * (r + 1)])
        return rmax

    rmax = lax.fori_loop(0, n_tiles, score_body, jnp.full((qb, LANES), -jnp.inf, f32))
    rmax = jnp.broadcast_to(jnp.max(rmax, axis=1, keepdims=True), (qb, LANES))

    def key_to_float(key):
        return lax.bitcast_convert_type(key ^ ((key >> 31) & 0x7FFFFFFF), f32)

    def float_to_key(v):
        bits = lax.bitcast_convert_type(v, i32)
        return bits ^ ((bits >> 31) & 0x7FFFFFFF)

    sweep_rows = min(qb, LANES)
    ones_rows = jnp.ones((SUBLANES, LANES), bf16)

    def to_rows(x_lane):
        return jnp.transpose(jnp.broadcast_to(x_lane[0:1, :], (LANES, qb)))

    def to_lanes(cnt):
        return _nt_dot(ones_rows, cnt.astype(bf16))

    def count_ge(trial_lane):
        trial_rows = to_rows(trial_lane)
        parts = []
        for r0 in range(0, qb, sweep_rows):
            trial_r = trial_rows[r0:r0 + sweep_rows]

            def body(j, cnt, r0=r0, trial_r=trial_r):
                for r in range(reps):
                    ch = score_ref[j, r0:r0 + sweep_rows, LANES * r:LANES * (r + 1)]
                    cnt = cnt + jnp.where(ch >= trial_r, 1.0, 0.0)
                return cnt
            parts.append(lax.fori_loop(0, n_tiles, body, jnp.zeros((sweep_rows, LANES), f32)))
        return to_lanes(jnp.concatenate(parts, axis=0))

    log_target = math.log(n_sel - 0.5)
    rmax = jnp.transpose(rmax)[0:SUBLANES, :]
    rmax_pad = rmax + jnp.abs(rmax) * 2.0 ** -20 + 1e-30

    def zero_counts(r0):
        def body(j, cnts):
            ge, gt = cnts
            for r in range(reps):
                ch = score_ref[j, r0:r0 + sweep_rows, LANES * r:LANES * (r + 1)]
                ge = ge + jnp.where(ch >= 0.0, 1.0, 0.0)
                gt = gt + jnp.where(ch > 0.0, 1.0, 0.0)
            return ge, gt
        z = jnp.zeros((sweep_rows, LANES), f32)
        ge, gt = lax.fori_loop(0, n_tiles, body, (z, z))
        return to_lanes(ge), to_lanes(gt)

    zero_parts = [zero_counts(r0) for r0 in range(0, qb, sweep_rows)]
    ge0 = jnp.concatenate([p[0] for p in zero_parts], axis=1)
    gt0 = jnp.concatenate([p[1] for p in zero_parts], axis=1)
    zeros = jnp.zeros((SUBLANES, qb), f32)
    total = zeros + (n_tiles * tk).astype(f32)
    above = ge0 >= n_sel
    lo_v0 = jnp.where(above, 0.0, -jnp.inf)
    lo_c0 = jnp.where(above, ge0, total)
    hi_v0 = jnp.where(above, jnp.inf, 0.0)
    hi_c0 = jnp.where(above, jnp.where(gt0 < n_sel, gt0, 0.0), ge0)
    done0 = (above & (gt0 < n_sel)) | (lo_c0 == n_sel)

    def search_cond(state):
        return (state[0] < SEARCH_MAX_STEPS) & state[-1]

    def search_body(state):
        it, lo_v, lo_c, f_lo, hi_v, hi_c, f_hi, last, done, _ = state
        lo_k = float_to_key(lo_v)
        hi_k = float_to_key(hi_v)
        hi_eff = jnp.where(hi_v == jnp.inf, rmax_pad, hi_v)
        t_int = lo_v + (hi_eff - lo_v) * (f_lo / (f_lo - f_hi))
        t_int = key_to_float(float_to_key(t_int))
        use_int = (lo_v > -jnp.inf) & (t_int > lo_v) & (t_int < hi_v) & (it < SEARCH_INTERP_STEPS)
        mid_k = (lo_k >> 1) + (hi_k >> 1) + (lo_k & hi_k & 1)
        t = jnp.where(use_int, t_int, key_to_float(mid_k))
        c = count_ge(t)
        f = jnp.log(jnp.maximum(c, 0.5)) - log_target
        active = done < 0.5
        is_lo = (c >= n_sel) & active
        is_hi = (c < n_sel) & active
        f_hi = jnp.where(is_lo & (last > 0.0), f_hi * 0.5, f_hi)
        f_lo = jnp.where(is_hi & (last < 0.0), f_lo * 0.5, f_lo)
        lo_v = jnp.where(is_lo, t, lo_v)
        lo_c = jnp.where(is_lo, c, lo_c)
        f_lo = jnp.where(is_lo, f, f_lo)
        hi_v = jnp.where(is_hi, t, hi_v)
        hi_c = jnp.where(is_hi, c, hi_c)
        f_hi = jnp.where(is_hi, f, f_hi)
        last = jnp.where(is_lo, 1.0, jnp.where(is_hi, -1.0, last))
        settled = (lo_c == n_sel) | (float_to_key(lo_v) + 1 >= float_to_key(hi_v))
        done = jnp.where(settled, 1.0, done)
        unresolved = jnp.min(done) < 0.5
        return it + 1, lo_v, lo_c, f_lo, hi_v, hi_c, f_hi, last, done, unresolved

    done0 = jnp.where(done0, 1.0, 0.0)
    init = (jnp.int32(0), lo_v0, lo_c0, jnp.log(lo_c0) - log_target,
            hi_v0, hi_c0, jnp.log(jnp.maximum(hi_c0, 0.5)) - log_target,
            zeros, done0, jnp.min(done0) < 0.5)
    final = lax.while_loop(search_cond, search_body, init)
    thr, lo_c, hi_c = final[1], final[2], final[5]
    thr_t = jnp.concatenate([to_rows(thr)] * reps, axis=1)
    tied = lo_c != n_sel
    any_tied = jnp.max(jnp.where(tied, 1.0, 0.0)) > 0.0
    quota = to_rows(jnp.where(tied, n_sel - hi_c, 2.0 * tk * (n_tiles + 1).astype(f32)))
    quota_t = jnp.concatenate([quota] * reps, axis=1)

    m_ref[...] = jnp.full(m_ref.shape, NEG_BIG, f32)
    acc_ref[...] = jnp.zeros(acc_ref.shape, f32)
    lane_k = lax.broadcasted_iota(i32, (tk, LANES), 1)
    kmask_l = (lane_k < HEAD_DIM).astype(f32).astype(bf16)
    kmask_r = (lane_k >= HEAD_DIM).astype(f32).astype(bf16)

    def attn_body(j, n_ties, with_ties):
        start = pl.multiple_of(j * tk, tk)
        sc = score_ref[j]
        if with_ties:
            eq = jnp.where(sc == thr_t, 1.0, 0.0)
            before = (jnp.dot(eq.astype(bf16), triu_ref[...], preferred_element_type=f32)
                      + jnp.concatenate([n_ties] * reps, axis=1))
            sel = (sc > thr_t) | ((sc == thr_t) & (before < quota_t))
            n_ties = n_ties + jnp.broadcast_to(jnp.sum(eq, axis=1, keepdims=True), (qb, LANES))
        else:
            sel = sc >= thr_t
        sel = sel & (col + j * tk <= row)
        bias = jnp.where(sel, 0.0, NEG_BIG)
        bias = jnp.concatenate([bias] * n_pairs_a, axis=0)
        kk = ka_ref[0, pl.ds(start, tk), :]
        vv = va_ref[0, pl.ds(start, tk), :]
        v_ext = (vv * kmask_l + kmask_r, vv * kmask_r + kmask_l)
        for g in range(N_KV_A):
            s = _nt_dot(qas_ref[g], kk) + bias
            m_old = m_ref[g]
            m_new = jnp.maximum(m_old, jnp.max(s, axis=1, keepdims=True))
            alpha = jnp.exp(m_old - m_new)
            p = jnp.exp(s - jnp.concatenate([m_new] * reps, axis=1))
            acc_ref[g] = alpha * acc_ref[g] + jnp.dot(p.astype(bf16), v_ext[g], preferred_element_type=f32)
            m_ref[g] = m_new
        return n_ties

    no_ties = jnp.zeros((qb, LANES), f32)

    @pl.when(any_tied)
    def _():
        lax.fori_loop(0, n_tiles, functools.partial(attn_body, with_ties=True), no_ties)

    @pl.when(jnp.logical_not(any_tied))
    def _():
        lax.fori_loop(0, n_tiles, functools.partial(attn_body, with_ties=False), no_ties)

    for c in range(n_pairs_a):
        rows = slice(c * qb, (c + 1) * qb)
        a0 = acc_ref[0, rows, :]
        a1 = acc_ref[1, rows, :]
        o0 = a0 / pltpu.roll(a0, HEAD_DIM, axis=1)
        o1 = a1 / pltpu.roll(a1, HEAD_DIM, axis=1)
        o_ref[0, :, LANES * c:LANES * (c + 1)] = jnp.where(left, o0, o1).astype(bf16)


def _sparse_attention(qa, ka, va, qi, ki, wi):
    bsz, s, wa = qa.shape
    qb = min(ATTN_QUERY_ROWS, s)
    tk = min(ATTN_KEY_TILE, s)
    n_sel = min(TOPK_MAX, s // 4)
    w_scale = N_IDX_HEADS ** -0.5 * IDX_DIM ** -0.5
    n_pairs_a = N_HEADS_A // 2
    tile = lambda b, i: (b, i, 0)
    per_b = lambda b, i: (b, 0, 0)
    kern = functools.partial(_sparse_attn_kernel, qb=qb, tk=tk, n_sel=float(n_sel), w_scale=w_scale)
    return pl.pallas_call(
        kern,
        grid=(bsz, s // qb),
        in_specs=[pl.BlockSpec((1, qb, wa), tile),
                  pl.BlockSpec((1, s, LANES), per_b),
                  pl.BlockSpec((1, s, LANES), per_b),
                  pl.BlockSpec((1, qb, qi.shape[2]), tile),
                  pl.BlockSpec((1, s, LANES), per_b),
                  pl.BlockSpec((1, qb, LANES), tile),
                  pl.BlockSpec((tk, tk), lambda b, i: (0, 0))],
        out_specs=pl.BlockSpec((1, qb, wa), tile),
        out_shape=jax.ShapeDtypeStruct((bsz, s, wa), bf16),
        scratch_shapes=[pltpu.VMEM((s // tk, qb, tk), f32),
                        pltpu.VMEM((N_IDX_HEADS * qb, LANES), bf16),
                        pltpu.VMEM((N_KV_A, n_pairs_a * qb, LANES), bf16),
                        pltpu.VMEM((N_IDX_HEADS, qb, LANES), f32),
                        pltpu.VMEM((N_KV_A, n_pairs_a * qb, LANES), f32),
                        pltpu.VMEM((N_KV_A, n_pairs_a * qb, LANES), f32)],
        compiler_params=pltpu.CompilerParams(
            dimension_semantics=("arbitrary", "arbitrary"), vmem_limit_bytes=VMEM_LIMIT),
        name="sparse_attn",
    )(qa, ka, va, qi, ki, wi, jnp.asarray(np.triu(np.ones((tk, tk), np.float32), k=1), bf16))


def _dilated_kernel(q_ref, k_ref, v_ref, bias_ref, o_ref, qs_ref, m_ref, acc_ref, *, tq, nd):
    i = pl.program_id(2)
    lane = lax.broadcasted_iota(i32, (tq, LANES), 1)
    left = lane < HEAD_DIM
    mask_l = left.astype(f32).astype(bf16)
    mask_r = (1.0 - left.astype(f32)).astype(bf16)
    q = q_ref[0] * jnp.asarray(HEAD_DIM ** -0.5, bf16)
    qs_ref[0:tq, :] = q * mask_l
    qs_ref[tq:2 * tq, :] = q * mask_r
    ones = jnp.ones((tq, LANES), bf16)

    for d in range(nd):
        j = i - d
        start = pl.multiple_of(jnp.maximum(j, 0) * tq, tq)
        kk = k_ref[0, pl.ds(start, tq), :]
        v_ext = jnp.concatenate([v_ref[0, pl.ds(start, tq), :], ones], axis=1)
        b = bias_ref[d] + jnp.where(j >= 0, 0.0, NEG_BIG)
        s = _nt_dot(qs_ref[...], kk) + jnp.concatenate([b, b], axis=0)
        m_d = jnp.broadcast_to(jnp.max(s, axis=1, keepdims=True), (2 * tq, LANES))
        p = jnp.exp(s - jnp.concatenate([m_d] * (tq // LANES), axis=1))
        acc_ref[d] = jnp.dot(p.astype(bf16), v_ext, preferred_element_type=f32)
        m_ref[d] = m_d

    m = m_ref[0]
    for d in range(1, nd):
        m = jnp.maximum(m, m_ref[d])
    acc = jnp.zeros((2 * tq, 2 * LANES), f32)
    for d in range(nd):
        w = jnp.exp(m_ref[d] - m)
        acc = acc + jnp.concatenate([w, w], axis=1) * acc_ref[d]
    o0 = acc[0:tq, 0:LANES] / acc[0:tq, LANES:2 * LANES]
    o1 = acc[tq:2 * tq, 0:LANES] / acc[tq:2 * tq, LANES:2 * LANES]
    o_ref[0] = jnp.where(left, o0, o1).astype(bf16)


def _dilated_attention(qb_, kb_, vb_):
    bsz, s, wb = qb_.shape
    tq = min(DILATED_TILE, s)
    bias_np, nd = _dilated_bias(tq)
    bias = jnp.asarray(bias_np)
    n_pairs = wb // LANES
    kern = functools.partial(_dilated_kernel, tq=tq, nd=nd)
    return pl.pallas_call(
        kern,
        grid=(bsz, n_pairs, s // tq),
        in_specs=[pl.BlockSpec((1, tq, LANES), lambda b, p, i: (b, i, p)),
                  pl.BlockSpec((1, s, LANES), lambda b, p, i: (b, 0, p)),
                  pl.BlockSpec((1, s, LANES), lambda b, p, i: (b, 0, p)),
                  pl.BlockSpec((nd, tq, tq), lambda b, p, i: (0, 0, 0))],
        out_specs=pl.BlockSpec((1, tq, LANES), lambda b, p, i: (b, i, p)),
        out_shape=jax.ShapeDtypeStruct((bsz, s, wb), bf16),
        scratch_shapes=[pltpu.VMEM((2 * tq, LANES), bf16),
                        pltpu.VMEM((nd, 2 * tq, LANES), f32),
                        pltpu.VMEM((nd, 2 * tq, 2 * LANES), f32)],
        compiler_params=pltpu.CompilerParams(
            dimension_semantics=("arbitrary", "arbitrary", "arbitrary"), vmem_limit_bytes=VMEM_LIMIT),
        name="dilated_attn",
    )(qb_, kb_, vb_, bias)


def _mid_kernel(oa_ref, ob_ref, x_ref, gta_ref, scf_ref, shf_ref, gtf_ref, woa_ref, wob_ref, gffn_ref,
                wsgu_ref, wsd_ref, wrt_ref, rbias_ref, triu_ref,
                base_ref, h2_ref, sel_ref, gate_ref, rank_ref, cnt_ref, carry_ref, *, n_exp, t, d_sh):
    step = pl.program_id(0)

    @pl.when(step == 0)
    def _():
        carry_ref[...] = jnp.zeros(carry_ref.shape, f32)

    mix = (jnp.dot(oa_ref[...], woa_ref[...], preferred_element_type=f32)
           + jnp.dot(ob_ref[...], wob_ref[...], preferred_element_type=f32))
    x1 = x_ref[...] + gta_ref[0] * mix
    ms = jnp.mean(x1 * x1, axis=-1, keepdims=True)
    h2 = ((x1 * lax.rsqrt(ms + EPS)) * gffn_ref[...]) * (1.0 + scf_ref[0]) + shf_ref[0]
    h2b = h2.astype(bf16)
    h2f = h2b.astype(f32)
    n_sub = h2_ref.shape[0] // t
    for a in range(n_sub):
        h2_ref[pl.ds(a, t, stride=n_sub), :] = h2f[:, LANES * a:LANES * (a + 1)]

    gu = jnp.dot(h2b, wsgu_ref[...], preferred_element_type=f32)
    g = gu[:, :d_sh]
    u = gu[:, d_sh:]
    act = (g * jax.nn.sigmoid(g)) * u
    shared = jnp.dot(act.astype(bf16), wsd_ref[...], preferred_element_type=f32)
    base = x1 + gtf_ref[0] * shared
    for a in range(n_sub):
        base_ref[pl.ds(a, t, stride=n_sub), :] = base[:, LANES * a:LANES * (a + 1)]

    scores = jax.nn.sigmoid(_nt_dot(wrt_ref[...], h2b))
    biased = scores + rbias_ref[...]
    per = n_exp // N_GROUPS
    neg_inf = jnp.float32(-jnp.inf)
    ri_g = lax.broadcasted_iota(i32, (per, t), 0).astype(f32)
    gs = []
    for grp in range(N_GROUPS):
        blk = biased[grp * per:(grp + 1) * per]
        m1 = jnp.max(blk, axis=0, keepdims=True)
        idx1 = jnp.min(jnp.where(blk == m1, ri_g, float(per)), axis=0, keepdims=True)
        m2 = jnp.max(jnp.where(ri_g == idx1, neg_inf, blk), axis=0, keepdims=True)
        gs.append(m1 + m2)
    masked_rows = []
    for grp in range(N_GROUPS):
        beaten = jnp.zeros((1, t), f32)
        for g2 in range(N_GROUPS):
            if g2 == grp:
                continue
            wins = gs[g2] > gs[grp]
            if g2 < grp:
                wins = wins | (gs[g2] == gs[grp])
            beaten = beaten + jnp.where(wins, 1.0, 0.0)
        keep = jnp.broadcast_to(beaten < TOPK_GROUPS, (per, t))
        masked_rows.append(jnp.where(keep, biased[grp * per:(grp + 1) * per], neg_inf))
    masked = jnp.concatenate(masked_rows, axis=0)

    ri = lax.broadcasted_iota(i32, (n_exp, t), 0).astype(f32)
    selmask = jnp.zeros((n_exp, t), f32)
    idxs, graw = [], []
    for _ in range(TOP_K):
        m = jnp.max(masked, axis=0, keepdims=True)
        idx = jnp.min(jnp.where(masked == m, ri, float(n_exp)), axis=0, keepdims=True)
        onehot = ri == idx
        graw.append(jnp.sum(jnp.where(onehot, scores, 0.0), axis=0, keepdims=True))
        masked = jnp.where(onehot, neg_inf, masked)
        selmask = jnp.where(onehot, 1.0, selmask)
        idxs.append(idx)
    den = graw[0]
    for k in range(1, TOP_K):
        den = den + graw[k]

    prefix = jnp.dot(selmask.astype(bf16), triu_ref[...], preferred_element_type=f32)
    prefix = prefix + jnp.concatenate([carry_ref[...]] * (t // LANES), axis=1)
    for k in range(TOP_K):
        rank_k = jnp.sum(jnp.where(ri == idxs[k], prefix, 0.0), axis=0, keepdims=True)
        sel_ref[k:k + 1, :] = idxs[k].astype(i32)
        rank_ref[k:k + 1, :] = rank_k.astype(i32)
        gate_ref[k:k + 1, :] = graw[k] / den * ROUTED_SCALE
    carry_ref[...] = carry_ref[...] + jnp.broadcast_to(
        jnp.sum(selmask, axis=1, keepdims=True), carry_ref.shape)
    cnt_ref[...] = carry_ref[...]


def _mid(oa, ob, x2, gta, scf, shf, gtf, woa, wob, gffn, wsgu, wsd, wrt, rbias_rep, s):
    n, d = x2.shape
    t = rbias_rep.shape[1]
    n_exp = wrt.shape[0]
    d_sh = wsd.shape[0]
    triu = jnp.asarray(np.triu(np.ones((t, t), np.float32), k=1), bf16)
    tile = lambda i: (i, 0)
    const = lambda i: (0, 0)
    per_b = lambda i: ((i * t) // s, 0, 0)
    lane_tile = lambda i: (0, i)
    n_sub = d // LANES
    tok_tiles = jax.ShapeDtypeStruct((n * n_sub, LANES), f32)
    tok_spec = pl.BlockSpec((t * n_sub, LANES), tile)
    out_shape = [tok_tiles, tok_tiles,
                 jax.ShapeDtypeStruct((TOP_K, n), i32), jax.ShapeDtypeStruct((TOP_K, n), f32),
                 jax.ShapeDtypeStruct((TOP_K, n), i32), jax.ShapeDtypeStruct((n_exp, LANES), f32)]
    out_specs = [tok_spec, tok_spec,
                 pl.BlockSpec((TOP_K, t), lane_tile), pl.BlockSpec((TOP_K, t), lane_tile),
                 pl.BlockSpec((TOP_K, t), lane_tile), pl.BlockSpec((n_exp, LANES), const)]
    kern = functools.partial(_mid_kernel, n_exp=n_exp, t=t, d_sh=d_sh)
    return pl.pallas_call(
        kern,
        grid=(n // t,),
        in_specs=[pl.BlockSpec((t, oa.shape[1]), tile),
                  pl.BlockSpec((t, ob.shape[1]), tile),
                  pl.BlockSpec((t, d), tile),
                  pl.BlockSpec((1, 1, d), per_b),
                  pl.BlockSpec((1, 1, d), per_b),
                  pl.BlockSpec((1, 1, d), per_b),
                  pl.BlockSpec((1, 1, d), per_b),
                  pl.BlockSpec(woa.shape, const),
                  pl.BlockSpec(wob.shape, const),
                  pl.BlockSpec((1, d), const),
                  pl.BlockSpec(wsgu.shape, const),
                  pl.BlockSpec(wsd.shape, const),
                  pl.BlockSpec(wrt.shape, const),
                  pl.BlockSpec(rbias_rep.shape, const),
                  pl.BlockSpec((t, t), const)],
        out_specs=out_specs,
        out_shape=out_shape,
        scratch_shapes=[pltpu.VMEM((n_exp, LANES), f32)],
        compiler_params=pltpu.CompilerParams(
            dimension_semantics=("arbitrary",), vmem_limit_bytes=VMEM_LIMIT),
        name="mid",
    )(oa, ob, x2, gta, scf, shf, gtf, woa, wob, gffn, wsgu, wsd, wrt, rbias_rep, triu)


def _dispatch_kernel(zb_ref, slot_ref, h_ref, xs_ref, zero_ref, sem_ref, *, td, n_zero):
    step = pl.program_id(0)

    def zero_copy(b):
        return pltpu.make_async_copy(zero_ref, xs_ref.at[pl.ds(zb_ref[b], ZERO_ROWS)], sem_ref.at[0])

    @pl.when(step == 0)
    def _():
        zero_ref[...] = jnp.zeros(zero_ref.shape, f32)

        def start(b, c):
            @pl.when(zb_ref[b] >= 0)
            def _():
                zero_copy(b).start()
            return c

        def wait(b, c):
            @pl.when(zb_ref[b] >= 0)
            def _():
                zero_copy(b).wait()
            return c

        lax.fori_loop(0, n_zero, start, 0)
        lax.fori_loop(0, n_zero, wait, 0)

    def start_rows(tok, c):
        for k in range(TOP_K):
            pltpu.make_async_copy(h_ref.at[pl.ds(tok, 1)], xs_ref.at[pl.ds(slot_ref[k, tok], 1)],
                                  sem_ref.at[1]).start(priority=k % 2)
        return c

    lax.fori_loop(0, td, start_rows, 0)
    for k in range(TOP_K):
        pltpu.make_async_copy(h_ref, xs_ref.at[pl.ds(0, td)], sem_ref.at[1]).wait()


def _dispatch(zero_start, slot, h2, n_slots):
    n, rows, lanes = h2.shape
    td = min(DISPATCH_ROWS, n)
    kern = functools.partial(_dispatch_kernel, td=td, n_zero=zero_start.shape[0])
    grid_spec = pltpu.PrefetchScalarGridSpec(
        num_scalar_prefetch=1,
        grid=(n // td,),
        in_specs=[pl.BlockSpec((TOP_K, td), lambda i, tail: (0, i), memory_space=pltpu.SMEM),
                  pl.BlockSpec((td, rows, lanes), lambda i, tail: (i, 0, 0))],
        out_specs=pl.BlockSpec(memory_space=pl.ANY),
        scratch_shapes=[pltpu.VMEM((ZERO_ROWS, rows, lanes), f32), pltpu.SemaphoreType.DMA((2,))],
    )
    return pl.pallas_call(
        kern,
        grid_spec=grid_spec,
        out_shape=jax.ShapeDtypeStruct((n_slots, rows, lanes), f32),
        compiler_params=pltpu.CompilerParams(
            dimension_semantics=("arbitrary",), vmem_limit_bytes=VMEM_LIMIT),
        name="dispatch",
    )(zero_start, slot, h2)


def _expert_kernel(be_ref, nu_ref, xs_ref, wg_ref, wu_ref, wd_ref, ys_ref, wgu_s, wd_s, *, n_sub):
    i = pl.program_id(0)
    f = wd_s.shape[0]
    rows = xs_ref.shape[0] // n_sub

    @pl.when(i >= nu_ref[0])
    def _():
        ys_ref[...] = jnp.zeros(ys_ref.shape, f32)

    @pl.when(i < nu_ref[0])
    def _():
        @pl.when((i == 0) | (be_ref[i] != be_ref[jnp.maximum(i - 1, 0)]))
        def _():
            wgu_s[:, 0:f] = wg_ref[0].astype(bf16)
            wgu_s[:, f:2 * f] = wu_ref[0].astype(bf16)
            wd_s[...] = wd_ref[0].astype(bf16)

        xb = jnp.concatenate([xs_ref[pl.ds(a, rows, stride=n_sub), :] for a in range(n_sub)],
                             axis=1).astype(bf16)
        gu = jnp.dot(xb, wgu_s[...], preferred_element_type=f32)
        g = gu[:, 0:f]
        u = gu[:, f:2 * f]
        act = (g * jax.nn.sigmoid(g)) * u
        y = jnp.dot(act.astype(bf16), wd_s[...], preferred_element_type=f32)
        for a in range(n_sub):
            ys_ref[pl.ds(a, rows, stride=n_sub), :] = y[:, LANES * a:LANES * (a + 1)]


def _experts(block_expert, n_used, xs, w_gate, w_up, w_down):
    n_slots, n_sub, lanes = xs.shape
    d = n_sub * lanes
    n_blocks = n_slots // MOE_BLOCK
    f = w_gate.shape[2]
    blk = lambda i, be, nu: (jnp.minimum(i, nu[0] - 1), 0)
    wsel = lambda i, be, nu: (be[jnp.minimum(i, nu[0] - 1)], 0, 0)
    grid_spec = pltpu.PrefetchScalarGridSpec(
        num_scalar_prefetch=2,
        grid=(n_blocks,),
        in_specs=[pl.BlockSpec((MOE_BLOCK * n_sub, lanes), blk),
                  pl.BlockSpec((1, d, f), wsel),
                  pl.BlockSpec((1, d, f), wsel),
                  pl.BlockSpec((1, f, d), wsel)],
        out_specs=pl.BlockSpec((MOE_BLOCK * n_sub, lanes), lambda i, be, nu: (i, 0)),
        scratch_shapes=[pltpu.VMEM((d, 2 * f), bf16), pltpu.VMEM((f, d), bf16)],
    )
    ys = pl.pallas_call(
        functools.partial(_expert_kernel, n_sub=n_sub),
        grid_spec=grid_spec,
        out_shape=jax.ShapeDtypeStruct((n_slots * n_sub, lanes), f32),
        compiler_params=pltpu.CompilerParams(
            dimension_semantics=("arbitrary",), vmem_limit_bytes=VMEM_LIMIT),
        name="experts",
    )(block_expert, n_used, xs.reshape(n_slots * n_sub, lanes), w_gate, w_up, w_down)
    return ys.reshape(n_slots, n_sub, lanes)


def _combine_kernel(slot_ref, nslot_ref, gate_ref, base_ref, gtf_ref, ys_ref, o_ref, buf_ref, res_ref, sem_ref,
                    *, tc):
    step = pl.program_id(0)
    cur = step % 2

    def start_gathers(idx_ref, b):
        def body(tok, c):
            for k in range(TOP_K):
                pltpu.make_async_copy(ys_ref.at[pl.ds(idx_ref[k, tok], 1)],
                                      buf_ref.at[b, k, pl.ds(tok, 1)], sem_ref.at[b]).start(priority=k % 2)
            return c
        lax.fori_loop(0, tc, body, 0)

    @pl.when(step == 0)
    def _():
        start_gathers(slot_ref, 0)

    @pl.when(step + 1 < pl.num_programs(0))
    def _():
        start_gathers(nslot_ref, 1 - cur)

    for k in range(TOP_K):
        pltpu.make_async_copy(ys_ref.at[pl.ds(0, tc)], buf_ref.at[cur, k], sem_ref.at[cur]).wait()

    gtf = gtf_ref[0]

    def reduce_token(tok, c):
        routed = buf_ref[cur, 0, tok] * gate_ref[0, tok]
        for k in range(1, TOP_K):
            routed = routed + buf_ref[cur, k, tok] * gate_ref[k, tok]
        res_ref[pl.ds(pl.multiple_of(tok * n_sub, n_sub), n_sub), :] = base_ref[tok] + gtf * routed
        return c

    n_sub = buf_ref.shape[3]
    lax.fori_loop(0, tc, reduce_token, 0, unroll=4)
    for a in range(n_sub):
        o_ref[:, LANES * a:LANES * (a + 1)] = res_ref[pl.ds(a, tc, stride=n_sub), :]


def _combine(slot, gate, base, gtf, ys, s):
    n, rows, lanes = base.shape
    tc = min(COMBINE_ROWS, n)
    n_steps = n // tc
    kern = functools.partial(_combine_kernel, tc=tc)
    tok_tile = lambda i: (i, 0, 0)
    return pl.pallas_call(
        kern,
        grid=(n_steps,),
        in_specs=[pl.BlockSpec((TOP_K, tc), lambda i: (0, i), memory_space=pltpu.SMEM),
                  pl.BlockSpec((TOP_K, tc), lambda i: (0, jnp.minimum(i + 1, n_steps - 1)),
                               memory_space=pltpu.SMEM),
                  pl.BlockSpec((TOP_K, tc), lambda i: (0, i), memory_space=pltpu.SMEM),
                  pl.BlockSpec((tc, rows, lanes), tok_tile),
                  pl.BlockSpec((1, rows, lanes), lambda i: ((i * tc) // s, 0, 0)),
                  pl.BlockSpec(memory_space=pl.ANY)],
        out_specs=pl.BlockSpec((tc, rows * lanes), lambda i: (i, 0)),
        out_shape=jax.ShapeDtypeStruct((n, rows * lanes), f32),
        scratch_shapes=[pltpu.VMEM((2, TOP_K, tc, rows, lanes), f32), pltpu.VMEM((tc * rows, lanes), f32),
                        pltpu.SemaphoreType.DMA((2,))],
        compiler_params=pltpu.CompilerParams(
            dimension_semantics=("arbitrary",), vmem_limit_bytes=VMEM_LIMIT),
        name="combine",
    )(slot, slot, gate, base, gtf, ys)


def _layer(x, c, posf, w_ada, b_ada, g_mix, w_in, q_norm_a, k_norm_a, q_norm_b, k_norm_b, w_out, g_ffn,
           w_router, router_bias, w_gate, w_up, w_down, ws_gate, ws_up, ws_down):
    bsz, s, d = x.shape
    n = bsz * s
    n_exp = w_router.shape[1]

    mod = _adaln(c, w_ada, b_ada)[:, None, :]
    sh_a, sc_a, gt_a, sh_f, sc_f, gt_f = (mod[..., k * d:(k + 1) * d] for k in range(6))

    w_perm = _take_runs(w_in.astype(bf16), _projection_columns(), axis=1)
    gains = {"qa": q_norm_a, "ka": k_norm_a, "qb": q_norm_b, "kb": k_norm_b}
    gain_a = jnp.concatenate([gains[kind][:HALF] for kind, _ in _SLOTS[:N_NORM_SLOTS]])[None, :].astype(f32)
    gain_b = jnp.concatenate([gains[kind][HALF:] for kind, _ in _SLOTS[:N_NORM_SLOTS]])[None, :].astype(f32)

    qa, ka, va, qi, ki, wi, qb_, kb_, vb_ = _project(x, posf, sc_a, sh_a, g_mix[None, :], w_perm, gain_a, gain_b)
    o_a = _sparse_attention(qa, ka, va, qi, ki, wi)
    o_b = _dilated_attention(qb_, kb_, vb_)

    wa = N_HEADS_A * HEAD_DIM
    rows_a = np.concatenate([np.arange(h * HEAD_DIM, (h + 1) * HEAD_DIM) for h in QA_PAIR_ORDER])
    woa = _take_runs(w_out, rows_a, axis=0).astype(bf16)
    wob = w_out[wa:].astype(bf16)
    wsgu = jnp.concatenate([ws_gate, ws_up], axis=1).astype(bf16)
    wsd = ws_down.astype(bf16)
    wrt = w_router.T.astype(bf16)
    t_mid = min(MID_ROWS, n)
    rbias_rep = jnp.broadcast_to(router_bias.astype(f32)[:, None], (n_exp, t_mid))

    base, h2, sel, gate, rank, cnt = _mid(
        o_a.reshape(n, wa), o_b.reshape(n, -1), x.reshape(n, d), gt_a, sc_f, sh_f, gt_f,
        woa, wob, g_ffn[None, :], wsgu, wsd, wrt, rbias_rep, s)
    base = base.reshape(n, d // LANES, LANES)
    h2 = h2.reshape(n, d // LANES, LANES)

    counts = cnt[:, 0].astype(i32)
    padded = (counts + MOE_BLOCK - 1) // MOE_BLOCK * MOE_BLOCK
    pad_end = jnp.cumsum(padded)
    pad_start = pad_end - padded
    onehot = sel[:, :, None] == jnp.arange(n_exp, dtype=i32)[None, None, :]
    slot = jnp.sum(jnp.where(onehot, pad_start[None, None, :], 0), axis=-1) + rank
    n_blocks = -(-(n * TOP_K) // MOE_BLOCK) + n_exp
    n_slots = n_blocks * MOE_BLOCK
    block_start = jnp.arange(n_blocks, dtype=i32) * MOE_BLOCK
    block_expert = jnp.sum((pad_end[None, :] <= block_start[:, None]).astype(i32), axis=1)
    block_expert = jnp.minimum(block_expert, n_exp - 1)
    n_used = (pad_end[-1] // MOE_BLOCK).astype(i32)[None]
    unit_start = jnp.arange(n_slots // ZERO_ROWS, dtype=i32) * ZERO_ROWS
    unit_expert = jnp.sum((pad_end[None, :] <= unit_start[:, None]).astype(i32), axis=1)
    real_end = pad_start + counts
    unit_real_end = jnp.sum(jnp.where(unit_expert[:, None] == jnp.arange(n_exp, dtype=i32)[None, :],
                                      real_end[None, :], 0), axis=1)
    has_pad = (unit_start + ZERO_ROWS > unit_real_end) | (unit_start >= pad_end[-1])
    zero_start = jnp.where(has_pad, unit_start, -1).astype(i32)

    xs = _dispatch(zero_start, slot, h2, n_slots)
    ys = _experts(block_expert, n_used, xs, w_gate, w_up, w_down)
    out = _combine(slot, gate, base, gt_f.reshape(bsz, d // LANES, LANES), ys, s)
    return out.reshape(bsz, s, d)


def kernel(x, c, positions, w_ada, b_ada, g_mix, w_in, q_norm_a, k_norm_a, q_norm_b, k_norm_b, w_out, g_ffn,
           w_router, router_bias, w_gate, w_up, w_down, ws_gate, ws_up, ws_down):
    posf = positions.astype(f32)[..., None]
    for l in range(w_ada.shape[0]):
        x = _layer(x, c, posf, w_ada[l], b_ada[l], g_mix[l], w_in[l], q_norm_a[l], k_norm_a[l], q_norm_b[l],
                   k_norm_b[l], w_out[l], g_ffn[l], w_router[l], router_bias[l], w_gate[l], w_up[l], w_down[l],
                   ws_gate[l], ws_up[l], ws_down[l])
    return x
```

```python
import functools
import math

import numpy as np
import jax
import jax.numpy as jnp
from jax import lax
from jax.experimental import pallas as pl
from jax.experimental.pallas import tpu as pltpu

f32 = jnp.float32
bf16 = jnp.bfloat16
i32 = jnp.int32

HEAD_DIM = 64
HALF = HEAD_DIM // 2
N_HEADS_A = 10
N_KV_A = 2
N_HEADS_B = 6
N_IDX_HEADS = 8
IDX_DIM = 64
TOPK_MAX = 256
DILATED_PATTERNS = ((128, 1), (512, 4), (2048, 16))
ROPE_THETA = 10000.0
EPS = 1e-6
TOP_K = 8
N_GROUPS = 8
TOPK_GROUPS = 4
ROUTED_SCALE = 2.5
MOE_BLOCK = 512
ZERO_ROWS = 64

LANES = 128
SUBLANES = 8
VMEM_LIMIT = 56 * 1024 * 1024

ADALN_COLS = 512
PROJ_ROWS = 512
ATTN_QUERY_ROWS = 512
ATTN_KEY_TILE = 512
DILATED_TILE = 512
MID_ROWS = 512
DISPATCH_ROWS = 512
COMBINE_ROWS = 256

NEG_BIG = -1e30
KEY_NEG_INF = int(np.int32(np.uint32(0xFF800000) ^ np.uint32(0x7FFFFFFF)))
KEY_POS_INF = 0x7F800000
SEARCH_INTERP_STEPS = 16
SEARCH_MAX_STEPS = SEARCH_INTERP_STEPS + 34

_OFF_QA = 0
_OFF_KA = _OFF_QA + N_HEADS_A * HEAD_DIM
_OFF_VA = _OFF_KA + N_KV_A * HEAD_DIM
_OFF_QI = _OFF_VA + N_KV_A * HEAD_DIM
_OFF_KI = _OFF_QI + N_IDX_HEADS * IDX_DIM
_OFF_WI = _OFF_KI + IDX_DIM
_OFF_QB = _OFF_WI + N_IDX_HEADS
_OFF_KB = _OFF_QB + N_HEADS_B * HEAD_DIM
_OFF_VB = _OFF_KB + N_HEADS_B * HEAD_DIM
D_IN = _OFF_VB + N_HEADS_B * HEAD_DIM

QA_PAIR_ORDER = (0, 5, 1, 6, 2, 7, 3, 8, 4, 9)

_SLOTS = (
    [("qa", h) for h in QA_PAIR_ORDER[:8]] + [("qa", 4), ("qa", 9), ("ka", 0), ("ka", 1)]
    + [("qb", h) for h in range(6)] + [("kb", h) for h in range(6)]
    + [("qi", h) for h in range(8)] + [("ki", 0), ("ki", 0), ("pad", 0), ("pad", 0)]
)
N_NORM_SLOTS = 24
N_CHUNKS = len(_SLOTS) // 4
SLAB = N_CHUNKS * LANES
_COL_VA = 2 * SLAB
_COL_VB = _COL_VA + N_KV_A * HEAD_DIM
_COL_WI = _COL_VB + N_HEADS_B * HEAD_DIM
N_COL = _COL_WI + LANES


def _slot_offset(kind, h):
    base = {"qa": _OFF_QA, "ka": _OFF_KA, "qb": _OFF_QB, "kb": _OFF_KB, "qi": _OFF_QI, "ki": _OFF_KI}
    return base[kind] + h * HEAD_DIM


def _projection_columns():
    zero_col = D_IN
    cols_a, cols_b = [], []
    for kind, h in _SLOTS:
        if kind == "pad":
            cols_a += [zero_col] * HALF
            cols_b += [zero_col] * HALF
        else:
            off = _slot_offset(kind, h)
            cols_a += list(range(off, off + HALF))
            cols_b += list(range(off + HALF, off + HEAD_DIM))
    cols = cols_a + cols_b
    cols += list(range(_OFF_VA, _OFF_VA + N_KV_A * HEAD_DIM))
    cols += list(range(_OFF_VB, _OFF_VB + N_HEADS_B * HEAD_DIM))
    cols += list(range(_OFF_WI, _OFF_WI + N_IDX_HEADS)) + [zero_col] * (LANES - N_IDX_HEADS)
    assert len(cols) == N_COL
    return np.asarray(cols, np.int32)


def _take_runs(w, idx, axis):
    size = w.shape[axis]
    pieces, a = [], 0
    idx = [int(v) for v in idx]
    while a < len(idx):
        b = a + 1
        if idx[a] == size:
            while b < len(idx) and idx[b] == size:
                b += 1
            shape = list(w.shape)
            shape[axis] = b - a
            pieces.append(jnp.zeros(shape, w.dtype))
        else:
            while b < len(idx) and idx[b] == idx[b - 1] + 1:
                b += 1
            pieces.append(lax.slice_in_dim(w, idx[a], idx[b - 1] + 1, axis=axis))
        a = b
    return jnp.concatenate(pieces, axis=axis)


def _interleave_matrix():
    p = np.zeros((2 * LANES, 2 * LANES), np.float32)
    for head in range(4):
        for i in range(HALF):
            p[HALF * head + i, HEAD_DIM * head + i] = 1.0
            p[LANES + HALF * head + i, HEAD_DIM * head + HALF + i] = 1.0
    return p


def _group_sum_matrix():
    g = np.zeros((LANES, LANES), np.float32)
    for k in range(LANES // HALF):
        g[HALF * k:HALF * (k + 1), HALF * k:HALF * (k + 1)] = 1.0
    return g


def _dilated_bias(tq):
    max_win = max(w for w, _ in DILATED_PATTERNS)
    nd = max_win // tq + 1
    d = np.arange(nd)[:, None, None] * tq + np.arange(tq)[None, :, None] - np.arange(tq)[None, None, :]
    mult = np.zeros(d.shape, np.float64)
    for win, dil in DILATED_PATTERNS:
        mult += ((d >= 0) & (d <= win) & (d % dil == 0)).astype(np.float64)
    with np.errstate(divide="ignore"):
        bias = np.where(mult > 0, np.log(np.maximum(mult, 1.0)), NEG_BIG)
    return bias.astype(np.float32), nd


def _nt_dot(a, b):
    return lax.dot_general(a, b, (((1,), (1,)), ((), ())), preferred_element_type=f32)


def _adaln_kernel(c_ref, w_ref, b_ref, o_ref):
    c = c_ref[...]
    a = c * jax.nn.sigmoid(c)
    o_ref[...] = jnp.dot(a, w_ref[...], preferred_element_type=f32) + b_ref[...]


def _adaln(c, w_ada, b_ada):
    bsz, d = c.shape
    n = w_ada.shape[1]
    rows = -(-bsz // 8) * 8
    c_pad = jnp.zeros((rows, d), f32).at[:bsz].set(c)
    tn = min(ADALN_COLS, n)
    out = pl.pallas_call(
        _adaln_kernel,
        grid=(n // tn,),
        in_specs=[pl.BlockSpec((rows, d), lambda j: (0, 0)),
                  pl.BlockSpec((d, tn), lambda j: (0, j)),
                  pl.BlockSpec((1, tn), lambda j: (0, j))],
        out_specs=pl.BlockSpec((rows, tn), lambda j: (0, j)),
        out_shape=jax.ShapeDtypeStruct((rows, n), f32),
        name="adaln",
    )(c_pad, w_ada, b_ada.reshape(1, n))
    return out[:bsz]


def _proj_kernel(x_ref, pos_ref, sc_ref, sh_ref, g_ref, w_ref, ga_ref, gb_ref, gsum_ref, perm_ref, invf_ref,
                 qa_ref, ka_ref, va_ref, qi_ref, ki_ref, wi_ref, qb_ref, kb_ref, vb_ref):
    x = x_ref[0]
    ms = jnp.mean(x * x, axis=-1, keepdims=True)
    h = (x * lax.rsqrt(ms + EPS)) * g_ref[...]
    h = h * (1.0 + sc_ref[0]) + sh_ref[0]
    proj = jnp.dot(h.astype(bf16), w_ref[...], preferred_element_type=f32)

    ang = pos_ref[0] * invf_ref[...]
    cos = jnp.cos(ang)
    sin = jnp.sin(ang)
    gsum = gsum_ref[...]
    perm = perm_ref[...]
    heads = []
    for c in range(N_CHUNKS):
        a = proj[:, LANES * c:LANES * (c + 1)]
        b = proj[:, SLAB + LANES * c:SLAB + LANES * (c + 1)]
        if 4 * c < N_NORM_SLOTS:
            ss = a * a + b * b
            hi = ss.astype(bf16)
            lo = (ss - hi.astype(f32)).astype(bf16)
            tot = (jnp.dot(hi, gsum, preferred_element_type=f32)
                   + jnp.dot(lo, gsum, preferred_element_type=f32))
            inv = lax.rsqrt(tot * (1.0 / HEAD_DIM) + EPS)
            a = a * inv * ga_ref[:, LANES * c:LANES * (c + 1)]
            b = b * inv * gb_ref[:, LANES * c:LANES * (c + 1)]
        ra = a * cos - b * sin
        rb = b * cos + a * sin
        ab = jnp.concatenate([ra, rb], axis=1).astype(bf16)
        heads.append(jnp.dot(ab, perm, preferred_element_type=f32).astype(bf16))

    pair = LANES
    qa_ref[0, :, 0:2 * pair] = heads[0]
    qa_ref[0, :, 2 * pair:4 * pair] = heads[1]
    qa_ref[0, :, 4 * pair:5 * pair] = heads[2][:, 0:pair]
    ka_ref[0] = heads[2][:, pair:2 * pair]
    qb_ref[0, :, 0:2 * pair] = heads[3]
    qb_ref[0, :, 2 * pair:3 * pair] = heads[4][:, 0:pair]
    kb_ref[0, :, 0:pair] = heads[4][:, pair:2 * pair]
    kb_ref[0, :, pair:3 * pair] = heads[5]
    qi_ref[0, :, 0:2 * pair] = heads[6]
    qi_ref[0, :, 2 * pair:4 * pair] = heads[7]
    ki_ref[0] = heads[8][:, 0:pair]
    va_ref[0] = proj[:, _COL_VA:_COL_VB].astype(bf16)
    vb_ref[0] = proj[:, _COL_VB:_COL_WI].astype(bf16)
    wi_ref[0] = proj[:, _COL_WI:N_COL]


def _project(x, posf, sc, sh, g, w_perm, gain_a, gain_b):
    bsz, s, d = x.shape
    ts = min(PROJ_ROWS, s)
    gsum = jnp.asarray(_group_sum_matrix(), bf16)
    perm = jnp.asarray(_interleave_matrix(), bf16)
    inv = ROPE_THETA ** (-jnp.arange(HALF, dtype=f32) / HALF)
    invf = jnp.tile(inv, LANES // HALF)[None, :]
    wa = N_HEADS_A * HEAD_DIM
    wb = N_HEADS_B * HEAD_DIM
    wq = N_IDX_HEADS * IDX_DIM
    const = lambda b, i: (0, 0)
    tile = lambda b, i: (b, i, 0)
    per_b = lambda b, i: (b, 0, 0)
    out_shape = [jax.ShapeDtypeStruct((bsz, s, w), dt) for w, dt in
                 ((wa, bf16), (LANES, bf16), (LANES, bf16), (wq, bf16), (LANES, bf16), (LANES, f32),
                  (wb, bf16), (wb, bf16), (wb, bf16))]
    out_specs = [pl.BlockSpec((1, ts, sh_.shape[2]), tile) for sh_ in out_shape]
    return pl.pallas_call(
        _proj_kernel,
        grid=(bsz, s // ts),
        in_specs=[pl.BlockSpec((1, ts, d), tile),
                  pl.BlockSpec((1, ts, 1), tile),
                  pl.BlockSpec((1, 1, d), per_b),
                  pl.BlockSpec((1, 1, d), per_b),
                  pl.BlockSpec((1, d), const),
                  pl.BlockSpec((d, N_COL), const),
                  pl.BlockSpec((1, N_NORM_SLOTS * HALF), const),
                  pl.BlockSpec((1, N_NORM_SLOTS * HALF), const),
                  pl.BlockSpec((LANES, LANES), const),
                  pl.BlockSpec((2 * LANES, 2 * LANES), const),
                  pl.BlockSpec((1, LANES), const)],
        out_specs=out_specs,
        out_shape=out_shape,
        compiler_params=pltpu.CompilerParams(
            dimension_semantics=("arbitrary", "arbitrary"), vmem_limit_bytes=VMEM_LIMIT),
        name="in_proj",
    )(x, posf, sc, sh, g, w_perm, gain_a, gain_b, gsum, perm, invf)


def _sparse_attn_kernel(qa_ref, ka_ref, va_ref, qi_ref, ki_ref, wi_ref, triu_ref, o_ref,
                        score_ref, qis_ref, qas_ref, wrep_ref, m_ref, acc_ref,
                        *, qb, tk, n_sel, w_scale):
    i = pl.program_id(1)
    n_pairs_a = N_HEADS_A // 2
    lane = lax.broadcasted_iota(i32, (qb, LANES), 1)
    left = lane < HEAD_DIM
    mask_l = left.astype(f32).astype(bf16)
    mask_r = (1.0 - left.astype(f32)).astype(bf16)

    for c in range(N_IDX_HEADS // 2):
        ch = qi_ref[0, :, LANES * c:LANES * (c + 1)]
        qis_ref[(2 * c) * qb:(2 * c + 1) * qb, :] = ch * mask_l
        qis_ref[(2 * c + 1) * qb:(2 * c + 2) * qb, :] = ch * mask_r
    q_scale = jnp.asarray(HEAD_DIM ** -0.5, bf16)
    for c in range(n_pairs_a):
        ch = qa_ref[0, :, LANES * c:LANES * (c + 1)] * q_scale
        qas_ref[0, c * qb:(c + 1) * qb, :] = ch * mask_l
        qas_ref[1, c * qb:(c + 1) * qb, :] = ch * mask_r
    wi = wi_ref[0] * w_scale
    for h in range(N_IDX_HEADS):
        wrep_ref[h] = jnp.broadcast_to(wi[:, h:h + 1], (qb, LANES))

    n_tiles = (i * qb) // tk + 1
    reps = tk // LANES
    row = lax.broadcasted_iota(i32, (qb, tk), 0) + i * qb
    col = lax.broadcasted_iota(i32, (qb, tk), 1)

    def score_body(j, rmax):
        start = pl.multiple_of(j * tk, tk)
        kt = ki_ref[0, pl.ds(start, tk), :]
        lg = _nt_dot(qis_ref[...], kt)
        acc = jnp.zeros((qb, tk), f32)
        for h in range(N_IDX_HEADS):
            wr = jnp.concatenate([wrep_ref[h]] * reps, axis=1)
            acc = acc + jnp.maximum(lg[h * qb:(h + 1) * qb], 0.0) * wr
        acc = jnp.where(col + j * tk <= row, acc, -jnp.inf)
        score_ref[j] = acc
        for r in range(reps):
            rmax = jnp.maximum(rmax, acc[:, LANES * r:LANES * (r + 1)])
        return rmax

    rmax = lax.fori_loop(0, n_tiles, score_body, jnp.full((qb, LANES), -jnp.inf, f32))
    rmax = jnp.broadcast_to(jnp.max(rmax, axis=1, keepdims=True), (qb, LANES))

    def key_to_float(key):
        return lax.bitcast_convert_type(key ^ ((key >> 31) & 0x7FFFFFFF), f32)

    def float_to_key(v):
        bits = lax.bitcast_convert_type(v, i32)
        return bits ^ ((bits >> 31) & 0x7FFFFFFF)

    sweep_rows = min(qb, LANES)
    ones_rows = jnp.ones((SUBLANES, LANES), bf16)

    def to_rows(x_lane):
        return jnp.transpose(jnp.broadcast_to(x_lane[0:1, :], (LANES, qb)))

    def to_lanes(cnt):
        return _nt_dot(ones_rows, cnt.astype(bf16))

    def count_ge(trial_lane):
        trial_rows = to_rows(trial_lane)
        parts = []
        for r0 in range(0, qb, sweep_rows):
            trial_r = trial_rows[r0:r0 + sweep_rows]

            def body(j, cnt, r0=r0, trial_r=trial_r):
                for r in range(reps):
                    ch = score_ref[j, r0:r0 + sweep_rows, LANES * r:LANES * (r + 1)]
                    cnt = cnt + jnp.where(ch >= trial_r, 1.0, 0.0)
                return cnt
            parts.append(lax.fori_loop(0, n_tiles, body, jnp.zeros((sweep_rows, LANES), f32)))
        return to_lanes(jnp.concatenate(parts, axis=0))

    log_target = math.log(n_sel - 0.5)
    rmax = jnp.transpose(rmax)[0:SUBLANES, :]
    rmax_pad = rmax + jnp.abs(rmax) * 2.0 ** -20 + 1e-30

    def zero_counts(r0):
        def body(j, cnts):
            ge, gt = cnts
            for r in range(reps):
                ch = score_ref[j, r0:r0 + sweep_rows, LANES * r:LANES * (r + 1)]
                ge = ge + jnp.where(ch >= 0.0, 1.0, 0.0)
                gt = gt + jnp.where(ch > 0.0, 1.0, 0.0)
            return ge, gt
        z = jnp.zeros((sweep_rows, LANES), f32)
        ge, gt = lax.fori_loop(0, n_tiles, body, (z, z))
        return to_lanes(ge), to_lanes(gt)

    zero_parts = [zero_counts(r0) for r0 in range(0, qb, sweep_rows)]
    ge0 = jnp.concatenate([p[0] for p in zero_parts], axis=1)
    gt0 = jnp.concatenate([p[1] for p in zero_parts], axis=1)
    zeros = jnp.zeros((SUBLANES, qb), f32)
    total = zeros + (n_tiles * tk).astype(f32)
    above = ge0 >= n_sel
    lo_v0 = jnp.where(above, 0.0, -jnp.inf)
    lo_c0 = jnp.where(above, ge0, total)
    hi_v0 = jnp.where(above, jnp.inf, 0.0)
    hi_c0 = jnp.where(above, jnp.where(gt0 < n_sel, gt0, 0.0), ge0)
    done0 = (above & (gt0 < n_sel)) | (lo_c0 == n_sel)

    def search_cond(state):
        return (state[0] < SEARCH_MAX_STEPS) & state[-1]

    def search_body(state):
        it, lo_v, lo_c, f_lo, hi_v, hi_c, f_hi, last, done, _ = state
        lo_k = float_to_key(lo_v)
        hi_k = float_to_key(hi_v)
        hi_eff = jnp.where(hi_v == jnp.inf, rmax_pad, hi_v)
        t_int = lo_v + (hi_eff - lo_v) * (f_lo / (f_lo - f_hi))
        t_int = key_to_float(float_to_key(t_int))
        use_int = (lo_v > -jnp.inf) & (t_int > lo_v) & (t_int < hi_v) & (it < SEARCH_INTERP_STEPS)
        mid_k = (lo_k >> 1) + (hi_k >> 1) + (lo_k & hi_k & 1)
        t = jnp.where(use_int, t_int, key_to_float(mid_k))
        c = count_ge(t)
        f = jnp.log(jnp.maximum(c, 0.5)) - log_target
        active = done < 0.5
        is_lo = (c >= n_sel) & active
        is_hi = (c < n_sel) & active
        f_hi = jnp.where(is_lo & (last > 0.0), f_hi * 0.5, f_hi)
        f_lo = jnp.where(is_hi & (last < 0.0), f_lo * 0.5, f_lo)
        lo_v = jnp.where(is_lo, t, lo_v)
        lo_c = jnp.where(is_lo, c, lo_c)
        f_lo = jnp.where(is_lo, f, f_lo)
        hi_v = jnp.where(is_hi, t, hi_v)
        hi_c = jnp.where(is_hi, c, hi_c)
        f_hi = jnp.where(is_hi, f, f_hi)
        last = jnp.where(is_lo, 1.0, jnp.where(is_hi, -1.0, last))
        settled = (lo_c == n_sel) | (float_to_key(lo_v) + 1 >= float_to_key(hi_v))
        done = jnp.where(settled, 1.0, done)
        unresolved = jnp.min(done) < 0.5
        return it + 1, lo_v, lo_c, f_lo, hi_v, hi_c, f_hi, last, done, unresolved

    done0 = jnp.where(done0, 1.0, 0.0)
    init = (jnp.int32(0), lo_v0, lo_c0, jnp.log(lo_c0) - log_target,
            hi_v0, hi_c0, jnp.log(jnp.maximum(hi_c0, 0.5)) - log_target,
            zeros, done0, jnp.min(done0) < 0.5)
    final = lax.while_loop(search_cond, search_body, init)
    thr, lo_c, hi_c = final[1], final[2], final[5]
    thr_t = jnp.concatenate([to_rows(thr)] * reps, axis=1)
    tied = lo_c != n_sel
    any_tied = jnp.max(jnp.where(tied, 1.0, 0.0)) > 0.0
    quota = to_rows(jnp.where(tied, n_sel - hi_c, 2.0 * tk * (n_tiles + 1).astype(f32)))
    quota_t = jnp.concatenate([quota] * reps, axis=1)

    m_ref[...] = jnp.full(m_ref.shape, NEG_BIG, f32)
    acc_ref[...] = jnp.zeros(acc_ref.shape, f32)
    lane_k = lax.broadcasted_iota(i32, (tk, LANES), 1)
    kmask_l = (lane_k < HEAD_DIM).astype(f32).astype(bf16)
    kmask_r = (lane_k >= HEAD_DIM).astype(f32).astype(bf16)

    def attn_body(j, n_ties, with_ties):
        start = pl.multiple_of(j * tk, tk)
        sc = score_ref[j]
        if with_ties:
            eq = jnp.where(sc == thr_t, 1.0, 0.0)
            before = (jnp.dot(eq.astype(bf16), triu_ref[...], preferred_element_type=f32)
                      + jnp.concatenate([n_ties] * reps, axis=1))
            sel = (sc > thr_t) | ((sc == thr_t) & (before < quota_t))
            n_ties = n_ties + jnp.broadcast_to(jnp.sum(eq, axis=1, keepdims=True), (qb, LANES))
        else:
            sel = sc >= thr_t
        sel = sel & (col + j * tk <= row)
        bias = jnp.where(sel, 0.0, NEG_BIG)
        bias = jnp.concatenate([bias] * n_pairs_a, axis=0)
        kk = ka_ref[0, pl.ds(start, tk), :]
        vv = va_ref[0, pl.ds(start, tk), :]
        v_ext = (vv * kmask_l + kmask_r, vv * kmask_r + kmask_l)
        for g in range(N_KV_A):
            s = _nt_dot(qas_ref[g], kk) + bias
            m_old = m_ref[g]
            m_new = jnp.maximum(m_old, jnp.max(s, axis=1, keepdims=True))
            alpha = jnp.exp(m_old - m_new)
            p = jnp.exp(s - jnp.concatenate([m_new] * reps, axis=1))
            acc_ref[g] = alpha * acc_ref[g] + jnp.dot(p.astype(bf16), v_ext[g], preferred_element_type=f32)
            m_ref[g] = m_new
        return n_ties

    no_ties = jnp.zeros((qb, LANES), f32)

    @pl.when(any_tied)
    def _():
        lax.fori_loop(0, n_tiles, functools.partial(attn_body, with_ties=True), no_ties)

    @pl.when(jnp.logical_not(any_tied))
    def _():
        lax.fori_loop(0, n_tiles, functools.partial(attn_body, with_ties=False), no_ties)

    for c in range(n_pairs_a):
        rows = slice(c * qb, (c + 1) * qb)
        a0 = acc_ref[0, rows, :]
        a1 = acc_ref[1, rows, :]
        o0 = a0 / pltpu.roll(a0, HEAD_DIM, axis=1)
        o1 = a1 / pltpu.roll(a1, HEAD_DIM, axis=1)
        o_ref[0, :, LANES * c:LANES * (c + 1)] = jnp.where(left, o0, o1).astype(bf16)


def _sparse_attention(qa, ka, va, qi, ki, wi):
    bsz, s, wa = qa.shape
    qb = min(ATTN_QUERY_ROWS, s)
    tk = min(ATTN_KEY_TILE, s)
    n_sel = min(TOPK_MAX, s // 4)
    w_scale = N_IDX_HEADS ** -0.5 * IDX_DIM ** -0.5
    n_pairs_a = N_HEADS_A // 2
    tile = lambda b, i: (b, i, 0)
    per_b = lambda b, i: (b, 0, 0)
    kern = functools.partial(_sparse_attn_kernel, qb=qb, tk=tk, n_sel=float(n_sel), w_scale=w_scale)
    return pl.pallas_call(
        kern,
        grid=(bsz, s // qb),
        in_specs=[pl.BlockSpec((1, qb, wa), tile),
                  pl.BlockSpec((1, s, LANES), per_b),
                  pl.BlockSpec((1, s, LANES), per_b),
                  pl.BlockSpec((1, qb, qi.shape[2]), tile),
                  pl.BlockSpec((1, s, LANES), per_b),
                  pl.BlockSpec((1, qb, LANES), tile),
                  pl.BlockSpec((tk, tk), lambda b, i: (0, 0))],
        out_specs=pl.BlockSpec((1, qb, wa), tile),
        out_shape=jax.ShapeDtypeStruct((bsz, s, wa), bf16),
        scratch_shapes=[pltpu.VMEM((s // tk, qb, tk), f32),
                        pltpu.VMEM((N_IDX_HEADS * qb, LANES), bf16),
                        pltpu.VMEM((N_KV_A, n_pairs_a * qb, LANES), bf16),
                        pltpu.VMEM((N_IDX_HEADS, qb, LANES), f32),
                        pltpu.VMEM((N_KV_A, n_pairs_a * qb, LANES), f32),
                        pltpu.VMEM((N_KV_A, n_pairs_a * qb, LANES), f32)],
        compiler_params=pltpu.CompilerParams(
            dimension_semantics=("arbitrary", "arbitrary"), vmem_limit_bytes=VMEM_LIMIT),
        name="sparse_attn",
    )(qa, ka, va, qi, ki, wi, jnp.asarray(np.triu(np.ones((tk, tk), np.float32), k=1), bf16))


def _dilated_kernel(q_ref, k_ref, v_ref, bias_ref, o_ref, qs_ref, m_ref, acc_ref, *, tq, nd):
    i = pl.program_id(2)
    lane = lax.broadcasted_iota(i32, (tq, LANES), 1)
    left = lane < HEAD_DIM
    mask_l = left.astype(f32).astype(bf16)
    mask_r = (1.0 - left.astype(f32)).astype(bf16)
    q = q_ref[0] * jnp.asarray(HEAD_DIM ** -0.5, bf16)
    qs_ref[0:tq, :] = q * mask_l
    qs_ref[tq:2 * tq, :] = q * mask_r
    ones = jnp.ones((tq, LANES), bf16)

    for d in range(nd):
        j = i - d
        start = pl.multiple_of(jnp.maximum(j, 0) * tq, tq)
        kk = k_ref[0, pl.ds(start, tq), :]
        v_ext = jnp.concatenate([v_ref[0, pl.ds(start, tq), :], ones], axis=1)
        b = bias_ref[d] + jnp.where(j >= 0, 0.0, NEG_BIG)
        s = _nt_dot(qs_ref[...], kk) + jnp.concatenate([b, b], axis=0)
        m_d = jnp.broadcast_to(jnp.max(s, axis=1, keepdims=True), (2 * tq, LANES))
        p = jnp.exp(s - jnp.concatenate([m_d] * (tq // LANES), axis=1))
        acc_ref[d] = jnp.dot(p.astype(bf16), v_ext, preferred_element_type=f32)
        m_ref[d] = m_d

    m = m_ref[0]
    for d in range(1, nd):
        m = jnp.maximum(m, m_ref[d])
    acc = jnp.zeros((2 * tq, 2 * LANES), f32)
    for d in range(nd):
        w = jnp.exp(m_ref[d] - m)
        acc = acc + jnp.concatenate([w, w], axis=1) * acc_ref[d]
    o0 = acc[0:tq, 0:LANES] / acc[0:tq, LANES:2 * LANES]
    o1 = acc[tq:2 * tq, 0:LANES] / acc[tq:2 * tq, LANES:2 * LANES]
    o_ref[0] = jnp.where(left, o0, o1).astype(bf16)


def _dilated_attention(qb_, kb_, vb_):
    bsz, s, wb = qb_.shape
    tq = min(DILATED_TILE, s)
    bias_np, nd = _dilated_bias(tq)
    bias = jnp.asarray(bias_np)
    n_pairs = wb // LANES
    kern = functools.partial(_dilated_kernel, tq=tq, nd=nd)
    return pl.pallas_call(
        kern,
        grid=(bsz, n_pairs, s // tq),
        in_specs=[pl.BlockSpec((1, tq, LANES), lambda b, p, i: (b, i, p)),
                  pl.BlockSpec((1, s, LANES), lambda b, p, i: (b, 0, p)),
                  pl.BlockSpec((1, s, LANES), lambda b, p, i: (b, 0, p)),
                  pl.BlockSpec((nd, tq, tq), lambda b, p, i: (0, 0, 0))],
        out_specs=pl.BlockSpec((1, tq, LANES), lambda b, p, i: (b, i, p)),
        out_shape=jax.ShapeDtypeStruct((bsz, s, wb), bf16),
        scratch_shapes=[pltpu.VMEM((2 * tq, LANES), bf16),
                        pltpu.VMEM((nd, 2 * tq, LANES), f32),
                        pltpu.VMEM((nd, 2 * tq, 2 * LANES), f32)],
        compiler_params=pltpu.CompilerParams(
            dimension_semantics=("arbitrary", "arbitrary", "arbitrary"), vmem_limit_bytes=VMEM_LIMIT),
        name="dilated_attn",
    )(qb_, kb_, vb_, bias)


def _mid_kernel(oa_ref, ob_ref, x_ref, gta_ref, scf_ref, shf_ref, gtf_ref, woa_ref, wob_ref, gffn_ref,
                wsgu_ref, wsd_ref, wrt_ref, rbias_ref, triu_ref,
                base_ref, h2_ref, sel_ref, gate_ref, rank_ref, cnt_ref, carry_ref, *, n_exp, t, d_sh):
    step = pl.program_id(0)

    @pl.when(step == 0)
    def _():
        carry_ref[...] = jnp.zeros(carry_ref.shape, f32)

    mix = (jnp.dot(oa_ref[...], woa_ref[...], preferred_element_type=f32)
           + jnp.dot(ob_ref[...], wob_ref[...], preferred_element_type=f32))
    x1 = x_ref[...] + gta_ref[0] * mix
    ms = jnp.mean(x1 * x1, axis=-1, keepdims=True)
    h2 = ((x1 * lax.rsqrt(ms + EPS)) * gffn_ref[...]) * (1.0 + scf_ref[0]) + shf_ref[0]
    h2b = h2.astype(bf16)
    h2f = h2b.astype(f32)
    n_sub = h2_ref.shape[0] // t
    for a in range(n_sub):
        h2_ref[pl.ds(a, t, stride=n_sub), :] = h2f[:, LANES * a:LANES * (a + 1)]

    gu = jnp.dot(h2b, wsgu_ref[...], preferred_element_type=f32)
    g = gu[:, :d_sh]
    u = gu[:, d_sh:]
    act = (g * jax.nn.sigmoid(g)) * u
    shared = jnp.dot(act.astype(bf16), wsd_ref[...], preferred_element_type=f32)
    base = x1 + gtf_ref[0] * shared
    for a in range(n_sub):
        base_ref[pl.ds(a, t, stride=n_sub), :] = base[:, LANES * a:LANES * (a + 1)]

    scores = jax.nn.sigmoid(_nt_dot(wrt_ref[...], h2b))
    biased = scores + rbias_ref[...]
    per = n_exp // N_GROUPS
    neg_inf = jnp.float32(-jnp.inf)
    ri_g = lax.broadcasted_iota(i32, (per, t), 0).astype(f32)
    gs = []
    for grp in range(N_GROUPS):
        blk = biased[grp * per:(grp + 1) * per]
        m1 = jnp.max(blk, axis=0, keepdims=True)
        idx1 = jnp.min(jnp.where(blk == m1, ri_g, float(per)), axis=0, keepdims=True)
        m2 = jnp.max(jnp.where(ri_g == idx1, neg_inf, blk), axis=0, keepdims=True)
        gs.append(m1 + m2)
    masked_rows = []
    for grp in range(N_GROUPS):
        beaten = jnp.zeros((1, t), f32)
        for g2 in range(N_GROUPS):
            if g2 == grp:
                continue
            wins = gs[g2] > gs[grp]
            if g2 < grp:
                wins = wins | (gs[g2] == gs[grp])
            beaten = beaten + jnp.where(wins, 1.0, 0.0)
        keep = jnp.broadcast_to(beaten < TOPK_GROUPS, (per, t))
        masked_rows.append(jnp.where(keep, biased[grp * per:(grp + 1) * per], neg_inf))
    masked = jnp.concatenate(masked_rows, axis=0)

    ri = lax.broadcasted_iota(i32, (n_exp, t), 0).astype(f32)
    selmask = jnp.zeros((n_exp, t), f32)
    idxs, graw = [], []
    for _ in range(TOP_K):
        m = jnp.max(masked, axis=0, keepdims=True)
        idx = jnp.min(jnp.where(masked == m, ri, float(n_exp)), axis=0, keepdims=True)
        onehot = ri == idx
        graw.append(jnp.sum(jnp.where(onehot, scores, 0.0), axis=0, keepdims=True))
        masked = jnp.where(onehot, neg_inf, masked)
        selmask = jnp.where(onehot, 1.0, selmask)
        idxs.append(idx)
    den = graw[0]
    for k in range(1, TOP_K):
        den = den + graw[k]

    prefix = jnp.dot(selmask.astype(bf16), triu_ref[...], preferred_element_type=f32)
    prefix = prefix + jnp.concatenate([carry_ref[...]] * (t // LANES), axis=1)
    for k in range(TOP_K):
        rank_k = jnp.sum(jnp.where(ri == idxs[k], prefix, 0.0), axis=0, keepdims=True)
        sel_ref[k:k + 1, :] = idxs[k].astype(i32)
        rank_ref[k:k + 1, :] = rank_k.astype(i32)
        gate_ref[k:k + 1, :] = graw[k] / den * ROUTED_SCALE
    carry_ref[...] = carry_ref[...] + jnp.broadcast_to(
        jnp.sum(selmask, axis=1, keepdims=True), carry_ref.shape)
    cnt_ref[...] = carry_ref[...]


def _mid(oa, ob, x2, gta, scf, shf, gtf, woa, wob, gffn, wsgu, wsd, wrt, rbias_rep, s):
    n, d = x2.shape
    t = rbias_rep.shape[1]
    n_exp = wrt.shape[0]
    d_sh = wsd.shape[0]
    triu = jnp.asarray(np.triu(np.ones((t, t), np.float32), k=1), bf16)
    tile = lambda i: (i, 0)
    const = lambda i: (0, 0)
    per_b = lambda i: ((i * t) // s, 0, 0)
    lane_tile = lambda i: (0, i)
    n_sub = d // LANES
    tok_tiles = jax.ShapeDtypeStruct((n * n_sub, LANES), f32)
    tok_spec = pl.BlockSpec((t * n_sub, LANES), tile)
    out_shape = [tok_tiles, tok_tiles,
                 jax.ShapeDtypeStruct((TOP_K, n), i32), jax.ShapeDtypeStruct((TOP_K, n), f32),
                 jax.ShapeDtypeStruct((TOP_K, n), i32), jax.ShapeDtypeStruct((n_exp, LANES), f32)]
    out_specs = [tok_spec, tok_spec,
                 pl.BlockSpec((TOP_K, t), lane_tile), pl.BlockSpec((TOP_K, t), lane_tile),
                 pl.BlockSpec((TOP_K, t), lane_tile), pl.BlockSpec((n_exp, LANES), const)]
    kern = functools.partial(_mid_kernel, n_exp=n_exp, t=t, d_sh=d_sh)
    return pl.pallas_call(
        kern,
        grid=(n // t,),
        in_specs=[pl.BlockSpec((t, oa.shape[1]), tile),
                  pl.BlockSpec((t, ob.shape[1]), tile),
                  pl.BlockSpec((t, d), tile),
                  pl.BlockSpec((1, 1, d), per_b),
                  pl.BlockSpec((1, 1, d), per_b),
                  pl.BlockSpec((1, 1, d), per_b),
                  pl.BlockSpec((1, 1, d), per_b),
                  pl.BlockSpec(woa.shape, const),
                  pl.BlockSpec(wob.shape, const),
                  pl.BlockSpec((1, d), const),
                  pl.BlockSpec(wsgu.shape, const),
                  pl.BlockSpec(wsd.shape, const),
                  pl.BlockSpec(wrt.shape, const),
                  pl.BlockSpec(rbias_rep.shape, const),
                  pl.BlockSpec((t, t), const)],
        out_specs=out_specs,
        out_shape=out_shape,
        scratch_shapes=[pltpu.VMEM((n_exp, LANES), f32)],
        compiler_params=pltpu.CompilerParams(
            dimension_semantics=("arbitrary",), vmem_limit_bytes=VMEM_LIMIT),
        name="mid",
    )(oa, ob, x2, gta, scf, shf, gtf, woa, wob, gffn, wsgu, wsd, wrt, rbias_rep, triu)


def _dispatch_kernel(zb_ref, slot_ref, h_ref, xs_ref, zero_ref, sem_ref, *, td, n_zero):
    step = pl.program_id(0)

    def zero_copy(b):
        return pltpu.make_async_copy(zero_ref, xs_ref.at[pl.ds(zb_ref[b], ZERO_ROWS)], sem_ref.at[0])

    @pl.when(step == 0)
    def _():
        zero_ref[...] = jnp.zeros(zero_ref.shape, f32)

        def start(b, c):
            @pl.when(zb_ref[b] >= 0)
            def _():
                zero_copy(b).start()
            return c

        def wait(b, c):
            @pl.when(zb_ref[b] >= 0)
            def _():
                zero_copy(b).wait()
            return c

        lax.fori_loop(0, n_zero, start, 0)
        lax.fori_loop(0, n_zero, wait, 0)

    def start_rows(tok, c):
        for k in range(TOP_K):
            pltpu.make_async_copy(h_ref.at[pl.ds(tok, 1)], xs_ref.at[pl.ds(slot_ref[k, tok], 1)],
                                  sem_ref.at[1]).start(priority=k % 2)
        return c

    lax.fori_loop(0, td, start_rows, 0)
    for k in range(TOP_K):
        pltpu.make_async_copy(h_ref, xs_ref.at[pl.ds(0, td)], sem_ref.at[1]).wait()


def _dispatch(zero_start, slot, h2, n_slots):
    n, rows, lanes = h2.shape
    td = min(DISPATCH_ROWS, n)
    kern = functools.partial(_dispatch_kernel, td=td, n_zero=zero_start.shape[0])
    grid_spec = pltpu.PrefetchScalarGridSpec(
        num_scalar_prefetch=1,
        grid=(n // td,),
        in_specs=[pl.BlockSpec((TOP_K, td), lambda i, tail: (0, i), memory_space=pltpu.SMEM),
                  pl.BlockSpec((td, rows, lanes), lambda i, tail: (i, 0, 0))],
        out_specs=pl.BlockSpec(memory_space=pl.ANY),
        scratch_shapes=[pltpu.VMEM((ZERO_ROWS, rows, lanes), f32), pltpu.SemaphoreType.DMA((2,))],
    )
    return pl.pallas_call(
        kern,
        grid_spec=grid_spec,
        out_shape=jax.ShapeDtypeStruct((n_slots, rows, lanes), f32),
        compiler_params=pltpu.CompilerParams(
            dimension_semantics=("arbitrary",), vmem_limit_bytes=VMEM_LIMIT),
        name="dispatch",
    )(zero_start, slot, h2)


def _expert_kernel(be_ref, nu_ref, xs_ref, wg_ref, wu_ref, wd_ref, ys_ref, wgu_s, wd_s, *, n_sub):
    i = pl.program_id(0)
    f = wd_s.shape[0]
    rows = xs_ref.shape[0] // n_sub

    @pl.when(i >= nu_ref[0])
    def _():
        ys_ref[...] = jnp.zeros(ys_ref.shape, f32)

    @pl.when(i < nu_ref[0])
    def _():
        @pl.when((i == 0) | (be_ref[i] != be_ref[jnp.maximum(i - 1, 0)]))
        def _():
            wgu_s[:, 0:f] = wg_ref[0].astype(bf16)
            wgu_s[:, f:2 * f] = wu_ref[0].astype(bf16)
            wd_s[...] = wd_ref[0].astype(bf16)

        xb = jnp.concatenate([xs_ref[pl.ds(a, rows, stride=n_sub), :] for a in range(n_sub)],
                             axis=1).astype(bf16)
        gu = jnp.dot(xb, wgu_s[...], preferred_element_type=f32)
        g = gu[:, 0:f]
        u = gu[:, f:2 * f]
        act = (g * jax.nn.sigmoid(g)) * u
        y = jnp.dot(act.astype(bf16), wd_s[...], preferred_element_type=f32)
        for a in range(n_sub):
            ys_ref[pl.ds(a, rows, stride=n_sub), :] = y[:, LANES * a:LANES * (a + 1)]


def _experts(block_expert, n_used, xs, w_gate, w_up, w_down):
    n_slots, n_sub, lanes = xs.shape
    d = n_sub * lanes
    n_blocks = n_slots // MOE_BLOCK
    f = w_gate.shape[2]
    blk = lambda i, be, nu: (jnp.minimum(i, nu[0] - 1), 0)
    wsel = lambda i, be, nu: (be[jnp.minimum(i, nu[0] - 1)], 0, 0)
    grid_spec = pltpu.PrefetchScalarGridSpec(
        num_scalar_prefetch=2,
        grid=(n_blocks,),
        in_specs=[pl.BlockSpec((MOE_BLOCK * n_sub, lanes), blk),
                  pl.BlockSpec((1, d, f), wsel),
                  pl.BlockSpec((1, d, f), wsel),
                  pl.BlockSpec((1, f, d), wsel)],
        out_specs=pl.BlockSpec((MOE_BLOCK * n_sub, lanes), lambda i, be, nu: (i, 0)),
        scratch_shapes=[pltpu.VMEM((d, 2 * f), bf16), pltpu.VMEM((f, d), bf16)],
    )
    ys = pl.pallas_call(
        functools.partial(_expert_kernel, n_sub=n_sub),
        grid_spec=grid_spec,
        out_shape=jax.ShapeDtypeStruct((n_slots * n_sub, lanes), f32),
        compiler_params=pltpu.CompilerParams(
            dimension_semantics=("arbitrary",), vmem_limit_bytes=VMEM_LIMIT),
        name="experts",
    )(block_expert, n_used, xs.reshape(n_slots * n_sub, lanes), w_gate, w_up, w_down)
    return ys.reshape(n_slots, n_sub, lanes)


def _combine_kernel(slot_ref, nslot_ref, gate_ref, base_ref, gtf_ref, ys_ref, o_ref, buf_ref, res_ref, sem_ref,
                    *, tc):
    step = pl.program_id(0)
    cur = step % 2

    def start_gathers(idx_ref, b):
        def body(tok, c):
            for k in range(TOP_K):
                pltpu.make_async_copy(ys_ref.at[pl.ds(idx_ref[k, tok], 1)],
                                      buf_ref.at[b, k, pl.ds(tok, 1)], sem_ref.at[b]).start(priority=k % 2)
            return c
        lax.fori_loop(0, tc, body, 0)

    @pl.when(step == 0)
    def _():
        start_gathers(slot_ref, 0)

    @pl.when(step + 1 < pl.num_programs(0))
    def _():
        start_gathers(nslot_ref, 1 - cur)

    for k in range(TOP_K):
        pltpu.make_async_copy(ys_ref.at[pl.ds(0, tc)], buf_ref.at[cur, k], sem_ref.at[cur]).wait()

    gtf = gtf_ref[0]

    def reduce_token(tok, c):
        routed = buf_ref[cur, 0, tok] * gate_ref[0, tok]
        for k in range(1, TOP_K):
            routed = routed + buf_ref[cur, k, tok] * gate_ref[k, tok]
        res_ref[pl.ds(pl.multiple_of(tok * n_sub, n_sub), n_sub), :] = base_ref[tok] + gtf * routed
        return c

    n_sub = buf_ref.shape[3]
    lax.fori_loop(0, tc, reduce_token, 0, unroll=4)
    for a in range(n_sub):
        o_ref[:, LANES * a:LANES * (a + 1)] = res_ref[pl.ds(a, tc, stride=n_sub), :]


def _combine(slot, gate, base, gtf, ys, s):
    n, rows, lanes = base.shape
    tc = min(COMBINE_ROWS, n)
    n_steps = n // tc
    kern = functools.partial(_combine_kernel, tc=tc)
    tok_tile = lambda i: (i, 0, 0)
    return pl.pallas_call(
        kern,
        grid=(n_steps,),
        in_specs=[pl.BlockSpec((TOP_K, tc), lambda i: (0, i), memory_space=pltpu.SMEM),
                  pl.BlockSpec((TOP_K, tc), lambda i: (0, jnp.minimum(i + 1, n_steps - 1)),
                               memory_space=pltpu.SMEM),
                  pl.BlockSpec((TOP_K, tc), lambda i: (0, i), memory_space=pltpu.SMEM),
                  pl.BlockSpec((tc, rows, lanes), tok_tile),
                  pl.BlockSpec((1, rows, lanes), lambda i: ((i * tc) // s, 0, 0)),
                  pl.BlockSpec(memory_space=pl.ANY)],
        out_specs=pl.BlockSpec((tc, rows * lanes), lambda i: (i, 0)),
        out_shape=jax.ShapeDtypeStruct((n, rows * lanes), f32),
        scratch_shapes=[pltpu.VMEM((2, TOP_K, tc, rows, lanes), f32), pltpu.VMEM((tc * rows, lanes), f32),
                        pltpu.SemaphoreType.DMA((2,))],
        compiler_params=pltpu.CompilerParams(
            dimension_semantics=("arbitrary",), vmem_limit_bytes=VMEM_LIMIT),
        name="combine",
    )(slot, slot, gate, base, gtf, ys)


def _layer(x, c, posf, w_ada, b_ada, g_mix, w_in, q_norm_a, k_norm_a, q_norm_b, k_norm_b, w_out, g_ffn,
           w_router, router_bias, w_gate, w_up, w_down, ws_gate, ws_up, ws_down):
    bsz, s, d = x.shape
    n = bsz * s
    n_exp = w_router.shape[1]

    mod = _adaln(c, w_ada, b_ada)[:, None, :]
    sh_a, sc_a, gt_a, sh_f, sc_f, gt_f = (mod[..., k * d:(k + 1) * d] for k in range(6))

    w_perm = _take_runs(w_in.astype(bf16), _projection_columns(), axis=1)
    gains = {"qa": q_norm_a, "ka": k_norm_a, "qb": q_norm_b, "kb": k_norm_b}
    gain_a = jnp.concatenate([gains[kind][:HALF] for kind, _ in _SLOTS[:N_NORM_SLOTS]])[None, :].astype(f32)
    gain_b = jnp.concatenate([gains[kind][HALF:] for kind, _ in _SLOTS[:N_NORM_SLOTS]])[None, :].astype(f32)

    qa, ka, va, qi, ki, wi, qb_, kb_, vb_ = _project(x, posf, sc_a, sh_a, g_mix[None, :], w_perm, gain_a, gain_b)
    o_a = _sparse_attention(qa, ka, va, qi, ki, wi)
    o_b = _dilated_attention(qb_, kb_, vb_)

    wa = N_HEADS_A * HEAD_DIM
    rows_a = np.concatenate([np.arange(h * HEAD_DIM, (h + 1) * HEAD_DIM) for h in QA_PAIR_ORDER])
    woa = _take_runs(w_out, rows_a, axis=0).astype(bf16)
    wob = w_out[wa:].astype(bf16)
    wsgu = jnp.concatenate([ws_gate, ws_up], axis=1).astype(bf16)
    wsd = ws_down.astype(bf16)
    wrt = w_router.T.astype(bf16)
    t_mid = min(MID_ROWS, n)
    rbias_rep = jnp.broadcast_to(router_bias.astype(f32)[:, None], (n_exp, t_mid))

    base, h2, sel, gate, rank, cnt = _mid(
        o_a.reshape(n, wa), o_b.reshape(n, -1), x.reshape(n, d), gt_a, sc_f, sh_f, gt_f,
        woa, wob, g_ffn[None, :], wsgu, wsd, wrt, rbias_rep, s)
    base = base.reshape(n, d // LANES, LANES)
    h2 = h2.reshape(n, d // LANES, LANES)

    counts = cnt[:, 0].astype(i32)
    padded = (counts + MOE_BLOCK - 1) // MOE_BLOCK * MOE_BLOCK
    pad_end = jnp.cumsum(padded)
    pad_start = pad_end - padded
    onehot = sel[:, :, None] == jnp.arange(n_exp, dtype=i32)[None, None, :]
    slot = jnp.sum(jnp.where(onehot, pad_start[None, None, :], 0), axis=-1) + rank
    n_blocks = -(-(n * TOP_K) // MOE_BLOCK) + n_exp
    n_slots = n_blocks * MOE_BLOCK
    block_start = jnp.arange(n_blocks, dtype=i32) * MOE_BLOCK
    block_expert = jnp.sum((pad_end[None, :] <= block_start[:, None]).astype(i32), axis=1)
    block_expert = jnp.minimum(block_expert, n_exp - 1)
    n_used = (pad_end[-1] // MOE_BLOCK).astype(i32)[None]
    unit_start = jnp.arange(n_slots // ZERO_ROWS, dtype=i32) * ZERO_ROWS
    unit_expert = jnp.sum((pad_end[None, :] <= unit_start[:, None]).astype(i32), axis=1)
    real_end = pad_start + counts
    unit_real_end = jnp.sum(jnp.where(unit_expert[:, None] == jnp.arange(n_exp, dtype=i32)[None, :],
                                      real_end[None, :], 0), axis=1)
    has_pad = (unit_start + ZERO_ROWS > unit_real_end) | (unit_start >= pad_end[-1])
    zero_start = jnp.where(has_pad, unit_start, -1).astype(i32)

    xs = _dispatch(zero_start, slot, h2, n_slots)
    ys = _experts(block_expert, n_used, xs, w_gate, w_up, w_down)
    out = _combine(slot, gate, base, gt_f.reshape(bsz, d // LANES, LANES), ys, s)
    return out.reshape(bsz, s, d)


def kernel(x, c, positions, w_ada, b_ada, g_mix, w_in, q_norm_a, k_norm_a, q_norm_b, k_norm_b, w_out, g_ffn,
           w_router, router_bias, w_gate, w_up, w_down, ws_gate, ws_up, ws_down):
    posf = positions.astype(f32)[..., None]
    for l in range(w_ada.shape[0]):
        x = _layer(x, c, posf, w_ada[l], b_ada[l], g_mix[l], w_in[l], q_norm_a[l], k_norm_a[l], q_norm_b[l],
                   k_norm_b[l], w_out[l], g_ffn[l], w_router[l], router_bias[l], w_gate[l], w_up[l], w_down[l],
                   ws_gate[l], ws_up[l], ws_down[l])
    return x
```

```python
import functools
import math

import numpy as np
import jax
import jax.numpy as jnp
from jax import lax
from jax.experimental import pallas as pl
from jax.experimental.pallas import tpu as pltpu

f32 = jnp.float32
bf16 = jnp.bfloat16
i32 = jnp.int32

HEAD_DIM = 64
HALF = HEAD_DIM // 2
N_HEADS_A = 10
N_KV_A = 2
N_HEADS_B = 6
N_IDX_HEADS = 8
IDX_DIM = 64
TOPK_MAX = 256
DILATED_PATTERNS = ((128, 1), (512, 4), (2048, 16))
ROPE_THETA = 10000.0
EPS = 1e-6
TOP_K = 8
N_GROUPS = 8
TOPK_GROUPS = 4
ROUTED_SCALE = 2.5
MOE_BLOCK = 512

LANES = 128
SUBLANES = 8
VMEM_LIMIT = 56 * 1024 * 1024

ADALN_COLS = 512
PROJ_ROWS = 512
ATTN_QUERY_ROWS = 512
ATTN_KEY_TILE = 512
DILATED_TILE = 512
MID_ROWS = 512
COMBINE_ROWS = 256

NEG_BIG = -1e30
KEY_NEG_INF = int(np.int32(np.uint32(0xFF800000) ^ np.uint32(0x7FFFFFFF)))
KEY_POS_INF = 0x7F800000
SEARCH_INTERP_STEPS = 16
SEARCH_MAX_STEPS = SEARCH_INTERP_STEPS + 34

_OFF_QA = 0
_OFF_KA = _OFF_QA + N_HEADS_A * HEAD_DIM
_OFF_VA = _OFF_KA + N_KV_A * HEAD_DIM
_OFF_QI = _OFF_VA + N_KV_A * HEAD_DIM
_OFF_KI = _OFF_QI + N_IDX_HEADS * IDX_DIM
_OFF_WI = _OFF_KI + IDX_DIM
_OFF_QB = _OFF_WI + N_IDX_HEADS
_OFF_KB = _OFF_QB + N_HEADS_B * HEAD_DIM
_OFF_VB = _OFF_KB + N_HEADS_B * HEAD_DIM
D_IN = _OFF_VB + N_HEADS_B * HEAD_DIM

QA_PAIR_ORDER = (0, 5, 1, 6, 2, 7, 3, 8, 4, 9)

_SLOTS = (
    [("qa", h) for h in QA_PAIR_ORDER[:8]] + [("qa", 4), ("qa", 9), ("ka", 0), ("ka", 1)]
    + [("qb", h) for h in range(6)] + [("kb", h) for h in range(6)]
    + [("qi", h) for h in range(8)] + [("ki", 0), ("ki", 0), ("pad", 0), ("pad", 0)]
)
N_NORM_SLOTS = 24
N_CHUNKS = len(_SLOTS) // 4
SLAB = N_CHUNKS * LANES
_COL_VA = 2 * SLAB
_COL_VB = _COL_VA + N_KV_A * HEAD_DIM
_COL_WI = _COL_VB + N_HEADS_B * HEAD_DIM
N_COL = _COL_WI + LANES


def _slot_offset(kind, h):
    base = {"qa": _OFF_QA, "ka": _OFF_KA, "qb": _OFF_QB, "kb": _OFF_KB, "qi": _OFF_QI, "ki": _OFF_KI}
    return base[kind] + h * HEAD_DIM


def _projection_columns():
    zero_col = D_IN
    cols_a, cols_b = [], []
    for kind, h in _SLOTS:
        if kind == "pad":
            cols_a += [zero_col] * HALF
            cols_b += [zero_col] * HALF
        else:
            off = _slot_offset(kind, h)
            cols_a += list(range(off, off + HALF))
            cols_b += list(range(off + HALF, off + HEAD_DIM))
    cols = cols_a + cols_b
    cols += list(range(_OFF_VA, _OFF_VA + N_KV_A * HEAD_DIM))
    cols += list(range(_OFF_VB, _OFF_VB + N_HEADS_B * HEAD_DIM))
    cols += list(range(_OFF_WI, _OFF_WI + N_IDX_HEADS)) + [zero_col] * (LANES - N_IDX_HEADS)
    assert len(cols) == N_COL
    return np.asarray(cols, np.int32)


def _take_runs(w, idx, axis):
    size = w.shape[axis]
    pieces, a = [], 0
    idx = [int(v) for v in idx]
    while a < len(idx):
        b = a + 1
        if idx[a] == size:
            while b < len(idx) and idx[b] == size:
                b += 1
            shape = list(w.shape)
            shape[axis] = b - a
            pieces.append(jnp.zeros(shape, w.dtype))
        else:
            while b < len(idx) and idx[b] == idx[b - 1] + 1:
                b += 1
            pieces.append(lax.slice_in_dim(w, idx[a], idx[b - 1] + 1, axis=axis))
        a = b
    return jnp.concatenate(pieces, axis=axis)


def _interleave_matrix():
    p = np.zeros((2 * LANES, 2 * LANES), np.float32)
    for head in range(4):
        for i in range(HALF):
            p[HALF * head + i, HEAD_DIM * head + i] = 1.0
            p[LANES + HALF * head + i, HEAD_DIM * head + HALF + i] = 1.0
    return p


def _group_sum_matrix():
    g = np.zeros((LANES, LANES), np.float32)
    for k in range(LANES // HALF):
        g[HALF * k:HALF * (k + 1), HALF * k:HALF * (k + 1)] = 1.0
    return g


def _dilated_bias(tq):
    max_win = max(w for w, _ in DILATED_PATTERNS)
    nd = max_win // tq + 1
    d = np.arange(nd)[:, None, None] * tq + np.arange(tq)[None, :, None] - np.arange(tq)[None, None, :]
    mult = np.zeros(d.shape, np.float64)
    for win, dil in DILATED_PATTERNS:
        mult += ((d >= 0) & (d <= win) & (d % dil == 0)).astype(np.float64)
    with np.errstate(divide="ignore"):
        bias = np.where(mult > 0, np.log(np.maximum(mult, 1.0)), NEG_BIG)
    return bias.astype(np.float32), nd


def _nt_dot(a, b):
    return lax.dot_general(a, b, (((1,), (1,)), ((), ())), preferred_element_type=f32)


def _adaln_kernel(c_ref, w_ref, b_ref, o_ref):
    c = c_ref[...]
    a = c * jax.nn.sigmoid(c)
    o_ref[...] = jnp.dot(a, w_ref[...], preferred_element_type=f32) + b_ref[...]


def _adaln(c, w_ada, b_ada):
    bsz, d = c.shape
    n = w_ada.shape[1]
    rows = -(-bsz // 8) * 8
    c_pad = jnp.zeros((rows, d), f32).at[:bsz].set(c)
    tn = min(ADALN_COLS, n)
    out = pl.pallas_call(
        _adaln_kernel,
        grid=(n // tn,),
        in_specs=[pl.BlockSpec((rows, d), lambda j: (0, 0)),
                  pl.BlockSpec((d, tn), lambda j: (0, j)),
                  pl.BlockSpec((1, tn), lambda j: (0, j))],
        out_specs=pl.BlockSpec((rows, tn), lambda j: (0, j)),
        out_shape=jax.ShapeDtypeStruct((rows, n), f32),
        name="adaln",
    )(c_pad, w_ada, b_ada.reshape(1, n))
    return out[:bsz]


def _proj_kernel(x_ref, pos_ref, sc_ref, sh_ref, g_ref, w_ref, ga_ref, gb_ref, gsum_ref, perm_ref, invf_ref,
                 qa_ref, ka_ref, va_ref, qi_ref, ki_ref, wi_ref, qb_ref, kb_ref, vb_ref):
    x = x_ref[0]
    ms = jnp.mean(x * x, axis=-1, keepdims=True)
    h = (x * lax.rsqrt(ms + EPS)) * g_ref[...]
    h = h * (1.0 + sc_ref[0]) + sh_ref[0]
    proj = jnp.dot(h.astype(bf16), w_ref[...], preferred_element_type=f32)

    ang = pos_ref[0] * invf_ref[...]
    cos = jnp.cos(ang)
    sin = jnp.sin(ang)
    gsum = gsum_ref[...]
    perm = perm_ref[...]
    heads = []
    for c in range(N_CHUNKS):
        a = proj[:, LANES * c:LANES * (c + 1)]
        b = proj[:, SLAB + LANES * c:SLAB + LANES * (c + 1)]
        if 4 * c < N_NORM_SLOTS:
            ss = a * a + b * b
            hi = ss.astype(bf16)
            lo = (ss - hi.astype(f32)).astype(bf16)
            tot = (jnp.dot(hi, gsum, preferred_element_type=f32)
                   + jnp.dot(lo, gsum, preferred_element_type=f32))
            inv = lax.rsqrt(tot * (1.0 / HEAD_DIM) + EPS)
            a = a * inv * ga_ref[:, LANES * c:LANES * (c + 1)]
            b = b * inv * gb_ref[:, LANES * c:LANES * (c + 1)]
        ra = a * cos - b * sin
        rb = b * cos + a * sin
        ab = jnp.concatenate([ra, rb], axis=1).astype(bf16)
        heads.append(jnp.dot(ab, perm, preferred_element_type=f32).astype(bf16))

    pair = LANES
    qa_ref[0, :, 0:2 * pair] = heads[0]
    qa_ref[0, :, 2 * pair:4 * pair] = heads[1]
    qa_ref[0, :, 4 * pair:5 * pair] = heads[2][:, 0:pair]
    ka_ref[0] = heads[2][:, pair:2 * pair]
    qb_ref[0, :, 0:2 * pair] = heads[3]
    qb_ref[0, :, 2 * pair:3 * pair] = heads[4][:, 0:pair]
    kb_ref[0, :, 0:pair] = heads[4][:, pair:2 * pair]
    kb_ref[0, :, pair:3 * pair] = heads[5]
    qi_ref[0, :, 0:2 * pair] = heads[6]
    qi_ref[0, :, 2 * pair:4 * pair] = heads[7]
    ki_ref[0] = heads[8][:, 0:pair]
    va_ref[0] = proj[:, _COL_VA:_COL_VB].astype(bf16)
    vb_ref[0] = proj[:, _COL_VB:_COL_WI].astype(bf16)
    wi_ref[0] = proj[:, _COL_WI:N_COL]


def _project(x, posf, sc, sh, g, w_perm, gain_a, gain_b):
    bsz, s, d = x.shape
    ts = min(PROJ_ROWS, s)
    gsum = jnp.asarray(_group_sum_matrix(), bf16)
    perm = jnp.asarray(_interleave_matrix(), bf16)
    inv = ROPE_THETA ** (-jnp.arange(HALF, dtype=f32) / HALF)
    invf = jnp.tile(inv, LANES // HALF)[None, :]
    wa = N_HEADS_A * HEAD_DIM
    wb = N_HEADS_B * HEAD_DIM
    wq = N_IDX_HEADS * IDX_DIM
    const = lambda b, i: (0, 0)
    tile = lambda b, i: (b, i, 0)
    per_b = lambda b, i: (b, 0, 0)
    out_shape = [jax.ShapeDtypeStruct((bsz, s, w), dt) for w, dt in
                 ((wa, bf16), (LANES, bf16), (LANES, bf16), (wq, bf16), (LANES, bf16), (LANES, f32),
                  (wb, bf16), (wb, bf16), (wb, bf16))]
    out_specs = [pl.BlockSpec((1, ts, sh_.shape[2]), tile) for sh_ in out_shape]
    return pl.pallas_call(
        _proj_kernel,
        grid=(bsz, s // ts),
        in_specs=[pl.BlockSpec((1, ts, d), tile),
                  pl.BlockSpec((1, ts, 1), tile),
                  pl.BlockSpec((1, 1, d), per_b),
                  pl.BlockSpec((1, 1, d), per_b),
                  pl.BlockSpec((1, d), const),
                  pl.BlockSpec((d, N_COL), const),
                  pl.BlockSpec((1, N_NORM_SLOTS * HALF), const),
                  pl.BlockSpec((1, N_NORM_SLOTS * HALF), const),
                  pl.BlockSpec((LANES, LANES), const),
                  pl.BlockSpec((2 * LANES, 2 * LANES), const),
                  pl.BlockSpec((1, LANES), const)],
        out_specs=out_specs,
        out_shape=out_shape,
        compiler_params=pltpu.CompilerParams(
            dimension_semantics=("arbitrary", "arbitrary"), vmem_limit_bytes=VMEM_LIMIT),
        name="in_proj",
    )(x, posf, sc, sh, g, w_perm, gain_a, gain_b, gsum, perm, invf)


def _sparse_attn_kernel(qa_ref, ka_ref, va_ref, qi_ref, ki_ref, wi_ref, triu_ref, o_ref,
                        score_ref, qis_ref, qas_ref, wrep_ref, m_ref, acc_ref,
                        *, qb, tk, n_sel, w_scale):
    i = pl.program_id(1)
    n_pairs_a = N_HEADS_A // 2
    lane = lax.broadcasted_iota(i32, (qb, LANES), 1)
    left = lane < HEAD_DIM
    mask_l = left.astype(f32).astype(bf16)
    mask_r = (1.0 - left.astype(f32)).astype(bf16)

    for c in range(N_IDX_HEADS // 2):
        ch = qi_ref[0, :, LANES * c:LANES * (c + 1)]
        qis_ref[(2 * c) * qb:(2 * c + 1) * qb, :] = ch * mask_l
        qis_ref[(2 * c + 1) * qb:(2 * c + 2) * qb, :] = ch * mask_r
    q_scale = jnp.asarray(HEAD_DIM ** -0.5, bf16)
    for c in range(n_pairs_a):
        ch = qa_ref[0, :, LANES * c:LANES * (c + 1)] * q_scale
        qas_ref[0, c * qb:(c + 1) * qb, :] = ch * mask_l
        qas_ref[1, c * qb:(c + 1) * qb, :] = ch * mask_r
    wi = wi_ref[0] * w_scale
    for h in range(N_IDX_HEADS):
        wrep_ref[h] = jnp.broadcast_to(wi[:, h:h + 1], (qb, LANES))

    n_tiles = (i * qb) // tk + 1
    reps = tk // LANES
    row = lax.broadcasted_iota(i32, (qb, tk), 0) + i * qb
    col = lax.broadcasted_iota(i32, (qb, tk), 1)

    def score_body(j, rmax):
        start = pl.multiple_of(j * tk, tk)
        kt = ki_ref[0, pl.ds(start, tk), :]
        lg = _nt_dot(qis_ref[...], kt)
        acc = jnp.zeros((qb, tk), f32)
        for h in range(N_IDX_HEADS):
            wr = jnp.concatenate([wrep_ref[h]] * reps, axis=1)
            acc = acc + jnp.maximum(lg[h * qb:(h + 1) * qb], 0.0) * wr
        acc = jnp.where(col + j * tk <= row, acc, -jnp.inf)
        score_ref[j] = acc
        for r in range(reps):
            rmax = jnp.maximum(rmax, acc[:, LANES * r:LANES * (r + 1)])
        return rmax

    rmax = lax.fori_loop(0, n_tiles, score_body, jnp.full((qb, LANES), -jnp.inf, f32))
    rmax = jnp.broadcast_to(jnp.max(rmax, axis=1, keepdims=True), (qb, LANES))

    def key_to_float(key):
        return lax.bitcast_convert_type(key ^ ((key >> 31) & 0x7FFFFFFF), f32)

    def float_to_key(v):
        bits = lax.bitcast_convert_type(v, i32)
        return bits ^ ((bits >> 31) & 0x7FFFFFFF)

    sweep_rows = min(qb, LANES)
    ones_rows = jnp.ones((SUBLANES, LANES), bf16)

    def to_rows(x_lane):
        return jnp.transpose(jnp.broadcast_to(x_lane[0:1, :], (LANES, qb)))

    def to_lanes(cnt):
        return _nt_dot(ones_rows, cnt.astype(bf16))

    def count_ge(trial_lane):
        trial_rows = to_rows(trial_lane)
        parts = []
        for r0 in range(0, qb, sweep_rows):
            trial_r = trial_rows[r0:r0 + sweep_rows]

            def body(j, cnt, r0=r0, trial_r=trial_r):
                for r in range(reps):
                    ch = score_ref[j, r0:r0 + sweep_rows, LANES * r:LANES * (r + 1)]
                    cnt = cnt + jnp.where(ch >= trial_r, 1.0, 0.0)
                return cnt
            parts.append(lax.fori_loop(0, n_tiles, body, jnp.zeros((sweep_rows, LANES), f32)))
        return to_lanes(jnp.concatenate(parts, axis=0))

    log_target = math.log(n_sel - 0.5)
    rmax = jnp.transpose(rmax)[0:SUBLANES, :]
    rmax_pad = rmax + jnp.abs(rmax) * 2.0 ** -20 + 1e-30

    def zero_counts(r0):
        def body(j, cnts):
            ge, gt = cnts
            for r in range(reps):
                ch = score_ref[j, r0:r0 + sweep_rows, LANES * r:LANES * (r + 1)]
                ge = ge + jnp.where(ch >= 0.0, 1.0, 0.0)
                gt = gt + jnp.where(ch > 0.0, 1.0, 0.0)
            return ge, gt
        z = jnp.zeros((sweep_rows, LANES), f32)
        ge, gt = lax.fori_loop(0, n_tiles, body, (z, z))
        return to_lanes(ge), to_lanes(gt)

    zero_parts = [zero_counts(r0) for r0 in range(0, qb, sweep_rows)]
    ge0 = jnp.concatenate([p[0] for p in zero_parts], axis=1)
    gt0 = jnp.concatenate([p[1] for p in zero_parts], axis=1)
    zeros = jnp.zeros((SUBLANES, qb), f32)
    total = zeros + (n_tiles * tk).astype(f32)
    above = ge0 >= n_sel
    lo_v0 = jnp.where(above, 0.0, -jnp.inf)
    lo_c0 = jnp.where(above, ge0, total)
    hi_v0 = jnp.where(above, jnp.inf, 0.0)
    hi_c0 = jnp.where(above, jnp.where(gt0 < n_sel, gt0, 0.0), ge0)
    done0 = (above & (gt0 < n_sel)) | (lo_c0 == n_sel)

    def search_cond(state):
        return (state[0] < SEARCH_MAX_STEPS) & state[-1]

    def search_body(state):
        it, lo_v, lo_c, f_lo, hi_v, hi_c, f_hi, last, done, _ = state
        lo_k = float_to_key(lo_v)
        hi_k = float_to_key(hi_v)
        hi_eff = jnp.where(hi_v == jnp.inf, rmax_pad, hi_v)
        t_int = lo_v + (hi_eff - lo_v) * (f_lo / (f_lo - f_hi))
        t_int = key_to_float(float_to_key(t_int))
        use_int = (lo_v > -jnp.inf) & (t_int > lo_v) & (t_int < hi_v) & (it < SEARCH_INTERP_STEPS)
        mid_k = (lo_k >> 1) + (hi_k >> 1) + (lo_k & hi_k & 1)
        t = jnp.where(use_int, t_int, key_to_float(mid_k))
        c = count_ge(t)
        f = jnp.log(jnp.maximum(c, 0.5)) - log_target
        active = done < 0.5
        is_lo = (c >= n_sel) & active
        is_hi = (c < n_sel) & active
        f_hi = jnp.where(is_lo & (last > 0.0), f_hi * 0.5, f_hi)
        f_lo = jnp.where(is_hi & (last < 0.0), f_lo * 0.5, f_lo)
        lo_v = jnp.where(is_lo, t, lo_v)
        lo_c = jnp.where(is_lo, c, lo_c)
        f_lo = jnp.where(is_lo, f, f_lo)
        hi_v = jnp.where(is_hi, t, hi_v)
        hi_c = jnp.where(is_hi, c, hi_c)
        f_hi = jnp.where(is_hi, f, f_hi)
        last = jnp.where(is_lo, 1.0, jnp.where(is_hi, -1.0, last))
        settled = (lo_c == n_sel) | (float_to_key(lo_v) + 1 >= float_to_key(hi_v))
        done = jnp.where(settled, 1.0, done)
        unresolved = jnp.min(done) < 0.5
        return it + 1, lo_v, lo_c, f_lo, hi_v, hi_c, f_hi, last, done, unresolved

    done0 = jnp.where(done0, 1.0, 0.0)
    init = (jnp.int32(0), lo_v0, lo_c0, jnp.log(lo_c0) - log_target,
            hi_v0, hi_c0, jnp.log(jnp.maximum(hi_c0, 0.5)) - log_target,
            zeros, done0, jnp.min(done0) < 0.5)
    final = lax.while_loop(search_cond, search_body, init)
    thr, lo_c, hi_c = final[1], final[2], final[5]
    thr_t = jnp.concatenate([to_rows(thr)] * reps, axis=1)
    tied = lo_c != n_sel
    any_tied = jnp.max(jnp.where(tied, 1.0, 0.0)) > 0.0
    quota = to_rows(jnp.where(tied, n_sel - hi_c, 2.0 * tk * (n_tiles + 1).astype(f32)))
    quota_t = jnp.concatenate([quota] * reps, axis=1)

    m_ref[...] = jnp.full(m_ref.shape, NEG_BIG, f32)
    acc_ref[...] = jnp.zeros(acc_ref.shape, f32)
    lane_k = lax.broadcasted_iota(i32, (tk, LANES), 1)
    kmask_l = (lane_k < HEAD_DIM).astype(f32).astype(bf16)
    kmask_r = (lane_k >= HEAD_DIM).astype(f32).astype(bf16)

    def attn_body(j, n_ties, with_ties):
        start = pl.multiple_of(j * tk, tk)
        sc = score_ref[j]
        if with_ties:
            eq = jnp.where(sc == thr_t, 1.0, 0.0)
            before = (jnp.dot(eq.astype(bf16), triu_ref[...], preferred_element_type=f32)
                      + jnp.concatenate([n_ties] * reps, axis=1))
            sel = (sc > thr_t) | ((sc == thr_t) & (before < quota_t))
            n_ties = n_ties + jnp.broadcast_to(jnp.sum(eq, axis=1, keepdims=True), (qb, LANES))
        else:
            sel = sc >= thr_t
        sel = sel & (col + j * tk <= row)
        bias = jnp.where(sel, 0.0, NEG_BIG)
        bias = jnp.concatenate([bias] * n_pairs_a, axis=0)
        kk = ka_ref[0, pl.ds(start, tk), :]
        vv = va_ref[0, pl.ds(start, tk), :]
        v_ext = (vv * kmask_l + kmask_r, vv * kmask_r + kmask_l)
        for g in range(N_KV_A):
            s = _nt_dot(qas_ref[g], kk) + bias
            m_old = m_ref[g]
            m_new = jnp.maximum(m_old, jnp.max(s, axis=1, keepdims=True))
            alpha = jnp.exp(m_old - m_new)
            p = jnp.exp(s - jnp.concatenate([m_new] * reps, axis=1))
            acc_ref[g] = alpha * acc_ref[g] + jnp.dot(p.astype(bf16), v_ext[g], preferred_element_type=f32)
            m_ref[g] = m_new
        return n_ties

    no_ties = jnp.zeros((qb, LANES), f32)

    @pl.when(any_tied)
    def _():
        lax.fori_loop(0, n_tiles, functools.partial(attn_body, with_ties=True), no_ties)

    @pl.when(jnp.logical_not(any_tied))
    def _():
        lax.fori_loop(0, n_tiles, functools.partial(attn_body, with_ties=False), no_ties)

    for c in range(n_pairs_a):
        rows = slice(c * qb, (c + 1) * qb)
        a0 = acc_ref[0, rows, :]
        a1 = acc_ref[1, rows, :]
        o0 = a0 / pltpu.roll(a0, HEAD_DIM, axis=1)
        o1 = a1 / pltpu.roll(a1, HEAD_DIM, axis=1)
        o_ref[0, :, LANES * c:LANES * (c + 1)] = jnp.where(left, o0, o1).astype(bf16)


def _sparse_attention(qa, ka, va, qi, ki, wi):
    bsz, s, wa = qa.shape
    qb = min(ATTN_QUERY_ROWS, s)
    tk = min(ATTN_KEY_TILE, s)
    n_sel = min(TOPK_MAX, s // 4)
    w_scale = N_IDX_HEADS ** -0.5 * IDX_DIM ** -0.5
    n_pairs_a = N_HEADS_A // 2
    tile = lambda b, i: (b, i, 0)
    per_b = lambda b, i: (b, 0, 0)
    kern = functools.partial(_sparse_attn_kernel, qb=qb, tk=tk, n_sel=float(n_sel), w_scale=w_scale)
    return pl.pallas_call(
        kern,
        grid=(bsz, s // qb),
        in_specs=[pl.BlockSpec((1, qb, wa), tile),
                  pl.BlockSpec((1, s, LANES), per_b),
                  pl.BlockSpec((1, s, LANES), per_b),
                  pl.BlockSpec((1, qb, qi.shape[2]), tile),
                  pl.BlockSpec((1, s, LANES), per_b),
                  pl.BlockSpec((1, qb, LANES), tile),
                  pl.BlockSpec((tk, tk), lambda b, i: (0, 0))],
        out_specs=pl.BlockSpec((1, qb, wa), tile),
        out_shape=jax.ShapeDtypeStruct((bsz, s, wa), bf16),
        scratch_shapes=[pltpu.VMEM((s // tk, qb, tk), f32),
                        pltpu.VMEM((N_IDX_HEADS * qb, LANES), bf16),
                        pltpu.VMEM((N_KV_A, n_pairs_a * qb, LANES), bf16),
                        pltpu.VMEM((N_IDX_HEADS, qb, LANES), f32),
                        pltpu.VMEM((N_KV_A, n_pairs_a * qb, LANES), f32),
                        pltpu.VMEM((N_KV_A, n_pairs_a * qb, LANES), f32)],
        compiler_params=pltpu.CompilerParams(
            dimension_semantics=("arbitrary", "arbitrary"), vmem_limit_bytes=VMEM_LIMIT),
        name="sparse_attn",
    )(qa, ka, va, qi, ki, wi, jnp.asarray(np.triu(np.ones((tk, tk), np.float32), k=1), bf16))


def _dilated_kernel(q_ref, k_ref, v_ref, bias_ref, o_ref, qs_ref, m_ref, acc_ref, *, tq, nd):
    i = pl.program_id(2)
    lane = lax.broadcasted_iota(i32, (tq, LANES), 1)
    left = lane < HEAD_DIM
    mask_l = left.astype(f32).astype(bf16)
    mask_r = (1.0 - left.astype(f32)).astype(bf16)
    q = q_ref[0] * jnp.asarray(HEAD_DIM ** -0.5, bf16)
    qs_ref[0:tq, :] = q * mask_l
    qs_ref[tq:2 * tq, :] = q * mask_r
    ones = jnp.ones((tq, LANES), bf16)

    for d in range(nd):
        j = i - d
        start = pl.multiple_of(jnp.maximum(j, 0) * tq, tq)
        kk = k_ref[0, pl.ds(start, tq), :]
        v_ext = jnp.concatenate([v_ref[0, pl.ds(start, tq), :], ones], axis=1)
        b = bias_ref[d] + jnp.where(j >= 0, 0.0, NEG_BIG)
        s = _nt_dot(qs_ref[...], kk) + jnp.concatenate([b, b], axis=0)
        m_d = jnp.broadcast_to(jnp.max(s, axis=1, keepdims=True), (2 * tq, LANES))
        p = jnp.exp(s - jnp.concatenate([m_d] * (tq // LANES), axis=1))
        acc_ref[d] = jnp.dot(p.astype(bf16), v_ext, preferred_element_type=f32)
        m_ref[d] = m_d

    m = m_ref[0]
    for d in range(1, nd):
        m = jnp.maximum(m, m_ref[d])
    acc = jnp.zeros((2 * tq, 2 * LANES), f32)
    for d in range(nd):
        w = jnp.exp(m_ref[d] - m)
        acc = acc + jnp.concatenate([w, w], axis=1) * acc_ref[d]
    o0 = acc[0:tq, 0:LANES] / acc[0:tq, LANES:2 * LANES]
    o1 = acc[tq:2 * tq, 0:LANES] / acc[tq:2 * tq, LANES:2 * LANES]
    o_ref[0] = jnp.where(left, o0, o1).astype(bf16)


def _dilated_attention(qb_, kb_, vb_):
    bsz, s, wb = qb_.shape
    tq = min(DILATED_TILE, s)
    bias_np, nd = _dilated_bias(tq)
    bias = jnp.asarray(bias_np)
    n_pairs = wb // LANES
    kern = functools.partial(_dilated_kernel, tq=tq, nd=nd)
    return pl.pallas_call(
        kern,
        grid=(bsz, n_pairs, s // tq),
        in_specs=[pl.BlockSpec((1, tq, LANES), lambda b, p, i: (b, i, p)),
                  pl.BlockSpec((1, s, LANES), lambda b, p, i: (b, 0, p)),
                  pl.BlockSpec((1, s, LANES), lambda b, p, i: (b, 0, p)),
                  pl.BlockSpec((nd, tq, tq), lambda b, p, i: (0, 0, 0))],
        out_specs=pl.BlockSpec((1, tq, LANES), lambda b, p, i: (b, i, p)),
        out_shape=jax.ShapeDtypeStruct((bsz, s, wb), bf16),
        scratch_shapes=[pltpu.VMEM((2 * tq, LANES), bf16),
                        pltpu.VMEM((nd, 2 * tq, LANES), f32),
                        pltpu.VMEM((nd, 2 * tq, 2 * LANES), f32)],
        compiler_params=pltpu.CompilerParams(
            dimension_semantics=("arbitrary", "arbitrary", "arbitrary"), vmem_limit_bytes=VMEM_LIMIT),
        name="dilated_attn",
    )(qb_, kb_, vb_, bias)


def _mid_kernel(oa_ref, ob_ref, x_ref, gta_ref, scf_ref, shf_ref, gtf_ref, woa_ref, wob_ref, gffn_ref,
                wsgu_ref, wsd_ref, wrt_ref, rbias_ref, triu_ref,
                base_ref, h2_ref, sel_ref, gate_ref, rank_ref, cnt_ref, carry_ref, *, n_exp, t, d_sh):
    step = pl.program_id(0)

    @pl.when(step == 0)
    def _():
        carry_ref[...] = jnp.zeros(carry_ref.shape, f32)

    mix = (jnp.dot(oa_ref[...], woa_ref[...], preferred_element_type=f32)
           + jnp.dot(ob_ref[...], wob_ref[...], preferred_element_type=f32))
    x1 = x_ref[...] + gta_ref[0] * mix
    ms = jnp.mean(x1 * x1, axis=-1, keepdims=True)
    h2 = ((x1 * lax.rsqrt(ms + EPS)) * gffn_ref[...]) * (1.0 + scf_ref[0]) + shf_ref[0]
    h2b = h2.astype(bf16)
    h2f = h2b.astype(f32)
    n_sub = h2_ref.shape[0] // t
    for a in range(n_sub):
        h2_ref[pl.ds(a, t, stride=n_sub), :] = h2f[:, LANES * a:LANES * (a + 1)]

    gu = jnp.dot(h2b, wsgu_ref[...], preferred_element_type=f32)
    g = gu[:, :d_sh]
    u = gu[:, d_sh:]
    act = (g * jax.nn.sigmoid(g)) * u
    shared = jnp.dot(act.astype(bf16), wsd_ref[...], preferred_element_type=f32)
    base = x1 + gtf_ref[0] * shared
    for a in range(n_sub):
        base_ref[pl.ds(a, t, stride=n_sub), :] = base[:, LANES * a:LANES * (a + 1)]

    scores = jax.nn.sigmoid(_nt_dot(wrt_ref[...], h2b))
    biased = scores + rbias_ref[...]
    per = n_exp // N_GROUPS
    neg_inf = jnp.float32(-jnp.inf)
    ri_g = lax.broadcasted_iota(i32, (per, t), 0).astype(f32)
    gs = []
    for grp in range(N_GROUPS):
        blk = biased[grp * per:(grp + 1) * per]
        m1 = jnp.max(blk, axis=0, keepdims=True)
        idx1 = jnp.min(jnp.where(blk == m1, ri_g, float(per)), axis=0, keepdims=True)
        m2 = jnp.max(jnp.where(ri_g == idx1, neg_inf, blk), axis=0, keepdims=True)
        gs.append(m1 + m2)
    masked_rows = []
    for grp in range(N_GROUPS):
        beaten = jnp.zeros((1, t), f32)
        for g2 in range(N_GROUPS):
            if g2 == grp:
                continue
            wins = gs[g2] > gs[grp]
            if g2 < grp:
                wins = wins | (gs[g2] == gs[grp])
            beaten = beaten + jnp.where(wins, 1.0, 0.0)
        keep = jnp.broadcast_to(beaten < TOPK_GROUPS, (per, t))
        masked_rows.append(jnp.where(keep, biased[grp * per:(grp + 1) * per], neg_inf))
    masked = jnp.concatenate(masked_rows, axis=0)

    ri = lax.broadcasted_iota(i32, (n_exp, t), 0).astype(f32)
    selmask = jnp.zeros((n_exp, t), f32)
    idxs, graw = [], []
    for _ in range(TOP_K):
        m = jnp.max(masked, axis=0, keepdims=True)
        idx = jnp.min(jnp.where(masked == m, ri, float(n_exp)), axis=0, keepdims=True)
        onehot = ri == idx
        graw.append(jnp.sum(jnp.where(onehot, scores, 0.0), axis=0, keepdims=True))
        masked = jnp.where(onehot, neg_inf, masked)
        selmask = jnp.where(onehot, 1.0, selmask)
        idxs.append(idx)
    den = graw[0]
    for k in range(1, TOP_K):
        den = den + graw[k]

    prefix = jnp.dot(selmask.astype(bf16), triu_ref[...], preferred_element_type=f32)
    prefix = prefix + jnp.concatenate([carry_ref[...]] * (t // LANES), axis=1)
    for k in range(TOP_K):
        rank_k = jnp.sum(jnp.where(ri == idxs[k], prefix, 0.0), axis=0, keepdims=True)
        sel_ref[k:k + 1, :] = idxs[k].astype(i32)
        rank_ref[k:k + 1, :] = rank_k.astype(i32)
        gate_ref[k:k + 1, :] = graw[k] / den * ROUTED_SCALE
    carry_ref[...] = carry_ref[...] + jnp.broadcast_to(
        jnp.sum(selmask, axis=1, keepdims=True), carry_ref.shape)
    cnt_ref[...] = carry_ref[...]


def _mid(oa, ob, x2, gta, scf, shf, gtf, woa, wob, gffn, wsgu, wsd, wrt, rbias_rep, s):
    n, d = x2.shape
    t = rbias_rep.shape[1]
    n_exp = wrt.shape[0]
    d_sh = wsd.shape[0]
    triu = jnp.asarray(np.triu(np.ones((t, t), np.float32), k=1), bf16)
    tile = lambda i: (i, 0)
    const = lambda i: (0, 0)
    per_b = lambda i: ((i * t) // s, 0, 0)
    lane_tile = lambda i: (0, i)
    n_sub = d // LANES
    tok_tiles = jax.ShapeDtypeStruct((n * n_sub, LANES), f32)
    tok_spec = pl.BlockSpec((t * n_sub, LANES), tile)
    out_shape = [tok_tiles, tok_tiles,
                 jax.ShapeDtypeStruct((TOP_K, n), i32), jax.ShapeDtypeStruct((TOP_K, n), f32),
                 jax.ShapeDtypeStruct((TOP_K, n), i32), jax.ShapeDtypeStruct((n_exp, LANES), f32)]
    out_specs = [tok_spec, tok_spec,
                 pl.BlockSpec((TOP_K, t), lane_tile), pl.BlockSpec((TOP_K, t), lane_tile),
                 pl.BlockSpec((TOP_K, t), lane_tile), pl.BlockSpec((n_exp, LANES), const)]
    kern = functools.partial(_mid_kernel, n_exp=n_exp, t=t, d_sh=d_sh)
    return pl.pallas_call(
        kern,
        grid=(n // t,),
        in_specs=[pl.BlockSpec((t, oa.shape[1]), tile),
                  pl.BlockSpec((t, ob.shape[1]), tile),
                  pl.BlockSpec((t, d), tile),
                  pl.BlockSpec((1, 1, d), per_b),
                  pl.BlockSpec((1, 1, d), per_b),
                  pl.BlockSpec((1, 1, d), per_b),
                  pl.BlockSpec((1, 1, d), per_b),
                  pl.BlockSpec(woa.shape, const),
                  pl.BlockSpec(wob.shape, const),
                  pl.BlockSpec((1, d), const),
                  pl.BlockSpec(wsgu.shape, const),
                  pl.BlockSpec(wsd.shape, const),
                  pl.BlockSpec(wrt.shape, const),
                  pl.BlockSpec(rbias_rep.shape, const),
                  pl.BlockSpec((t, t), const)],
        out_specs=out_specs,
        out_shape=out_shape,
        scratch_shapes=[pltpu.VMEM((n_exp, LANES), f32)],
        compiler_params=pltpu.CompilerParams(
            dimension_semantics=("arbitrary",), vmem_limit_bytes=VMEM_LIMIT),
        name="mid",
    )(oa, ob, x2, gta, scf, shf, gtf, woa, wob, gffn, wsgu, wsd, wrt, rbias_rep, triu)


def _expert_kernel(be_ref, nu_ref, tok_ref, ntok_ref, h_ref, wg_ref, wu_ref, wd_ref, ys_ref,
                   xbuf, wgu_s, wd_s, sem_ref, *, n_sub):
    i = pl.program_id(0)
    n_used = nu_ref[0]
    cur = i % 2
    f = wd_s.shape[0]
    rows = ys_ref.shape[0] // n_sub

    def start_gathers(idx_ref, b):
        def body(pair, c):
            for half in range(2):
                r = 2 * pair + half
                src = h_ref.at[pl.ds(pl.multiple_of(idx_ref[0, 0, r] * n_sub, n_sub), n_sub), :]
                dst = xbuf.at[b, pl.ds(pl.multiple_of(r * n_sub, n_sub), n_sub), :]
                pltpu.make_async_copy(src, dst, sem_ref.at[b]).start(priority=half)
            return c
        lax.fori_loop(0, rows // 2, body, 0)

    @pl.when(i == 0)
    def _():
        start_gathers(tok_ref, 0)

    @pl.when(i + 1 < n_used)
    def _():
        start_gathers(ntok_ref, 1 - cur)

    @pl.when(i >= n_used)
    def _():
        ys_ref[...] = jnp.zeros(ys_ref.shape, f32)

    @pl.when(i < n_used)
    def _():
        @pl.when((i == 0) | (be_ref[i] != be_ref[jnp.maximum(i - 1, 0)]))
        def _():
            wgu_s[:, 0:f] = wg_ref[0].astype(bf16)
            wgu_s[:, f:2 * f] = wu_ref[0].astype(bf16)
            wd_s[...] = wd_ref[0].astype(bf16)

        pltpu.make_async_copy(h_ref.at[pl.ds(0, rows * n_sub), :], xbuf.at[cur], sem_ref.at[cur]).wait()
        xb = jnp.concatenate([xbuf[cur, pl.ds(a, rows, stride=n_sub), :] for a in range(n_sub)],
                             axis=1).astype(bf16)
        gu = jnp.dot(xb, wgu_s[...], preferred_element_type=f32)
        g = gu[:, 0:f]
        u = gu[:, f:2 * f]
        act = (g * jax.nn.sigmoid(g)) * u
        y = jnp.dot(act.astype(bf16), wd_s[...], preferred_element_type=f32)
        for a in range(n_sub):
            ys_ref[pl.ds(a, rows, stride=n_sub), :] = y[:, LANES * a:LANES * (a + 1)]


def _experts(block_expert, n_used, block_tok, h2, w_gate, w_up, w_down):
    n_blocks = block_tok.shape[0]
    n_sub = w_gate.shape[1] // LANES
    lanes = LANES
    d = n_sub * lanes
    f = w_gate.shape[2]
    last = lambda i, be, nu: jnp.minimum(i, nu[0] - 1)
    wsel = lambda i, be, nu: (be[last(i, be, nu)], 0, 0)
    grid_spec = pltpu.PrefetchScalarGridSpec(
        num_scalar_prefetch=2,
        grid=(n_blocks,),
        in_specs=[pl.BlockSpec((1, 1, MOE_BLOCK), lambda i, be, nu: (last(i, be, nu), 0, 0),
                               memory_space=pltpu.SMEM),
                  pl.BlockSpec((1, 1, MOE_BLOCK), lambda i, be, nu: (last(i + 1, be, nu), 0, 0),
                               memory_space=pltpu.SMEM),
                  pl.BlockSpec(memory_space=pl.ANY),
                  pl.BlockSpec((1, d, f), wsel),
                  pl.BlockSpec((1, d, f), wsel),
                  pl.BlockSpec((1, f, d), wsel)],
        out_specs=pl.BlockSpec((MOE_BLOCK * n_sub, lanes), lambda i, be, nu: (i, 0)),
        scratch_shapes=[pltpu.VMEM((2, MOE_BLOCK * n_sub, lanes), f32),
                        pltpu.VMEM((d, 2 * f), bf16), pltpu.VMEM((f, d), bf16),
                        pltpu.SemaphoreType.DMA((2,))],
    )
    ys = pl.pallas_call(
        functools.partial(_expert_kernel, n_sub=n_sub),
        grid_spec=grid_spec,
        out_shape=jax.ShapeDtypeStruct((n_blocks * MOE_BLOCK * n_sub, lanes), f32),
        compiler_params=pltpu.CompilerParams(
            dimension_semantics=("arbitrary",), vmem_limit_bytes=VMEM_LIMIT),
        name="experts",
    )(block_expert, n_used, block_tok, block_tok, h2, w_gate, w_up, w_down)
    return ys.reshape(n_blocks * MOE_BLOCK, n_sub, lanes)


def _combine_kernel(slot_ref, nslot_ref, gate_ref, base_ref, gtf_ref, ys_ref, o_ref, buf_ref, res_ref, sem_ref,
                    *, tc):
    step = pl.program_id(0)
    cur = step % 2

    def start_gathers(idx_ref, b):
        def body(tok, c):
            for k in range(TOP_K):
                pltpu.make_async_copy(ys_ref.at[pl.ds(idx_ref[k, tok], 1)],
                                      buf_ref.at[b, k, pl.ds(tok, 1)], sem_ref.at[b]).start(priority=k % 2)
            return c
        lax.fori_loop(0, tc, body, 0)

    @pl.when(step == 0)
    def _():
        start_gathers(slot_ref, 0)

    @pl.when(step + 1 < pl.num_programs(0))
    def _():
        start_gathers(nslot_ref, 1 - cur)

    for k in range(TOP_K):
        pltpu.make_async_copy(ys_ref.at[pl.ds(0, tc)], buf_ref.at[cur, k], sem_ref.at[cur]).wait()

    gtf = gtf_ref[0]

    def reduce_token(tok, c):
        routed = buf_ref[cur, 0, tok] * gate_ref[0, tok]
        for k in range(1, TOP_K):
            routed = routed + buf_ref[cur, k, tok] * gate_ref[k, tok]
        res_ref[pl.ds(pl.multiple_of(tok * n_sub, n_sub), n_sub), :] = base_ref[tok] + gtf * routed
        return c

    n_sub = buf_ref.shape[3]
    lax.fori_loop(0, tc, reduce_token, 0, unroll=4)
    for a in range(n_sub):
        o_ref[:, LANES * a:LANES * (a + 1)] = res_ref[pl.ds(a, tc, stride=n_sub), :]


def _combine(slot, gate, base, gtf, ys, s):
    n, rows, lanes = base.shape
    tc = min(COMBINE_ROWS, n)
    n_steps = n // tc
    kern = functools.partial(_combine_kernel, tc=tc)
    tok_tile = lambda i: (i, 0, 0)
    return pl.pallas_call(
        kern,
        grid=(n_steps,),
        in_specs=[pl.BlockSpec((TOP_K, tc), lambda i: (0, i), memory_space=pltpu.SMEM),
                  pl.BlockSpec((TOP_K, tc), lambda i: (0, jnp.minimum(i + 1, n_steps - 1)),
                               memory_space=pltpu.SMEM),
                  pl.BlockSpec((TOP_K, tc), lambda i: (0, i), memory_space=pltpu.SMEM),
                  pl.BlockSpec((tc, rows, lanes), tok_tile),
                  pl.BlockSpec((1, rows, lanes), lambda i: ((i * tc) // s, 0, 0)),
                  pl.BlockSpec(memory_space=pl.ANY)],
        out_specs=pl.BlockSpec((tc, rows * lanes), lambda i: (i, 0)),
        out_shape=jax.ShapeDtypeStruct((n, rows * lanes), f32),
        scratch_shapes=[pltpu.VMEM((2, TOP_K, tc, rows, lanes), f32), pltpu.VMEM((tc * rows, lanes), f32),
                        pltpu.SemaphoreType.DMA((2,))],
        compiler_params=pltpu.CompilerParams(
            dimension_semantics=("arbitrary",), vmem_limit_bytes=VMEM_LIMIT),
        name="combine",
    )(slot, slot, gate, base, gtf, ys)


def _layer(x, c, posf, w_ada, b_ada, g_mix, w_in, q_norm_a, k_norm_a, q_norm_b, k_norm_b, w_out, g_ffn,
           w_router, router_bias, w_gate, w_up, w_down, ws_gate, ws_up, ws_down):
    bsz, s, d = x.shape
    n = bsz * s
    n_exp = w_router.shape[1]

    mod = _adaln(c, w_ada, b_ada)[:, None, :]
    sh_a, sc_a, gt_a, sh_f, sc_f, gt_f = (mod[..., k * d:(k + 1) * d] for k in range(6))

    w_perm = _take_runs(w_in.astype(bf16), _projection_columns(), axis=1)
    gains = {"qa": q_norm_a, "ka": k_norm_a, "qb": q_norm_b, "kb": k_norm_b}
    gain_a = jnp.concatenate([gains[kind][:HALF] for kind, _ in _SLOTS[:N_NORM_SLOTS]])[None, :].astype(f32)
    gain_b = jnp.concatenate([gains[kind][HALF:] for kind, _ in _SLOTS[:N_NORM_SLOTS]])[None, :].astype(f32)

    qa, ka, va, qi, ki, wi, qb_, kb_, vb_ = _project(x, posf, sc_a, sh_a, g_mix[None, :], w_perm, gain_a, gain_b)
    o_a = _sparse_attention(qa, ka, va, qi, ki, wi)
    o_b = _dilated_attention(qb_, kb_, vb_)

    wa = N_HEADS_A * HEAD_DIM
    rows_a = np.concatenate([np.arange(h * HEAD_DIM, (h + 1) * HEAD_DIM) for h in QA_PAIR_ORDER])
    woa = _take_runs(w_out, rows_a, axis=0).astype(bf16)
    wob = w_out[wa:].astype(bf16)
    wsgu = jnp.concatenate([ws_gate, ws_up], axis=1).astype(bf16)
    wsd = ws_down.astype(bf16)
    wrt = w_router.T.astype(bf16)
    t_mid = min(MID_ROWS, n)
    rbias_rep = jnp.broadcast_to(router_bias.astype(f32)[:, None], (n_exp, t_mid))

    base, h2, sel, gate, rank, cnt = _mid(
        o_a.reshape(n, wa), o_b.reshape(n, -1), x.reshape(n, d), gt_a, sc_f, sh_f, gt_f,
        woa, wob, g_ffn[None, :], wsgu, wsd, wrt, rbias_rep, s)
    base = base.reshape(n, d // LANES, LANES)

    counts = cnt[:, 0].astype(i32)
    padded = (counts + MOE_BLOCK - 1) // MOE_BLOCK * MOE_BLOCK
    pad_end = jnp.cumsum(padded)
    pad_start = pad_end - padded
    onehot = sel[:, :, None] == jnp.arange(n_exp, dtype=i32)[None, None, :]
    slot = jnp.sum(jnp.where(onehot, pad_start[None, None, :], 0), axis=-1) + rank
    n_blocks = -(-(n * TOP_K) // MOE_BLOCK) + n_exp
    block_start = jnp.arange(n_blocks, dtype=i32) * MOE_BLOCK
    block_expert = jnp.sum((pad_end[None, :] <= block_start[:, None]).astype(i32), axis=1)
    block_expert = jnp.minimum(block_expert, n_exp - 1)
    n_used = (pad_end[-1] // MOE_BLOCK).astype(i32)[None]
    dense_start = jnp.cumsum(counts) - counts
    dense_pos = jnp.sum(jnp.where(onehot, dense_start[None, None, :], 0), axis=-1) + rank
    tok_sorted = (jnp.argsort(dense_pos.reshape(-1)) % n).astype(i32)
    tok_sorted = jnp.concatenate([tok_sorted, jnp.zeros((MOE_BLOCK,), i32)])
    first_row = jnp.maximum(block_start - pad_start[block_expert], 0)
    window = jnp.minimum(dense_start[block_expert] + first_row, n * TOP_K)
    block_tok = jax.vmap(lambda w: lax.dynamic_slice(tok_sorted, (w,), (MOE_BLOCK,)))(window)

    ys = _experts(block_expert, n_used, block_tok[:, None, :], h2, w_gate, w_up, w_down)
    out = _combine(slot, gate, base, gt_f.reshape(bsz, d // LANES, LANES), ys, s)
    return out.reshape(bsz, s, d)


def kernel(x, c, positions, w_ada, b_ada, g_mix, w_in, q_norm_a, k_norm_a, q_norm_b, k_norm_b, w_out, g_ffn,
           w_router, router_bias, w_gate, w_up, w_down, ws_gate, ws_up, ws_down):
    posf = positions.astype(f32)[..., None]
    for l in range(w_ada.shape[0]):
        x = _layer(x, c, posf, w_ada[l], b_ada[l], g_mix[l], w_in[l], q_norm_a[l], k_norm_a[l], q_norm_b[l],
                   k_norm_b[l], w_out[l], g_ffn[l], w_router[l], router_bias[l], w_gate[l], w_up[l], w_down[l],
                   ws_gate[l], ws_up[l], ws_down[l])
    return x
```

```python
import functools
import math

import numpy as np
import jax
import jax.numpy as jnp
from jax import lax
from jax.experimental import pallas as pl
from jax.experimental.pallas import tpu as pltpu

f32 = jnp.float32
bf16 = jnp.bfloat16
i32 = jnp.int32

HEAD_DIM = 64
HALF = HEAD_DIM // 2
N_HEADS_A = 10
N_KV_A = 2
N_HEADS_B = 6
N_IDX_HEADS = 8
IDX_DIM = 64
TOPK_MAX = 256
DILATED_PATTERNS = ((128, 1), (512, 4), (2048, 16))
ROPE_THETA = 10000.0
EPS = 1e-6
TOP_K = 8
N_GROUPS = 8
TOPK_GROUPS = 4
ROUTED_SCALE = 2.5
MOE_BLOCK = 512
ZERO_ROWS = 64

LANES = 128
SUBLANES = 8
VMEM_LIMIT = 56 * 1024 * 1024

ADALN_COLS = 512
PROJ_ROWS = 512
ATTN_QUERY_ROWS = 512
ATTN_KEY_TILE = 512
DILATED_TILE = 512
MID_ROWS = 512
DISPATCH_ROWS = 512
COMBINE_ROWS = 256

NEG_BIG = -1e30
KEY_NEG_INF = int(np.int32(np.uint32(0xFF800000) ^ np.uint32(0x7FFFFFFF)))
KEY_POS_INF = 0x7F800000
SEARCH_INTERP_STEPS = 16
SEARCH_MAX_STEPS = SEARCH_INTERP_STEPS + 34

_OFF_QA = 0
_OFF_KA = _OFF_QA + N_HEADS_A * HEAD_DIM
_OFF_VA = _OFF_KA + N_KV_A * HEAD_DIM
_OFF_QI = _OFF_VA + N_KV_A * HEAD_DIM
_OFF_KI = _OFF_QI + N_IDX_HEADS * IDX_DIM
_OFF_WI = _OFF_KI + IDX_DIM
_OFF_QB = _OFF_WI + N_IDX_HEADS
_OFF_KB = _OFF_QB + N_HEADS_B * HEAD_DIM
_OFF_VB = _OFF_KB + N_HEADS_B * HEAD_DIM
D_IN = _OFF_VB + N_HEADS_B * HEAD_DIM

QA_PAIR_ORDER = (0, 5, 1, 6, 2, 7, 3, 8, 4, 9)

_SLOTS = (
    [("qa", h) for h in QA_PAIR_ORDER[:8]] + [("qa", 4), ("qa", 9), ("ka", 0), ("ka", 1)]
    + [("qb", h) for h in range(6)] + [("kb", h) for h in range(6)]
    + [("qi", h) for h in range(8)] + [("ki", 0), ("ki", 0), ("pad", 0), ("pad", 0)]
)
N_NORM_SLOTS = 24
N_CHUNKS = len(_SLOTS) // 4
SLAB = N_CHUNKS * LANES
_COL_VA = 2 * SLAB
_COL_VB = _COL_VA + N_KV_A * HEAD_DIM
_COL_WI = _COL_VB + N_HEADS_B * HEAD_DIM
N_COL = _COL_WI + LANES


def _slot_offset(kind, h):
    base = {"qa": _OFF_QA, "ka": _OFF_KA, "qb": _OFF_QB, "kb": _OFF_KB, "qi": _OFF_QI, "ki": _OFF_KI}
    return base[kind] + h * HEAD_DIM


def _projection_columns():
    zero_col = D_IN
    cols_a, cols_b = [], []
    for kind, h in _SLOTS:
        if kind == "pad":
            cols_a += [zero_col] * HALF
            cols_b += [zero_col] * HALF
        else:
            off = _slot_offset(kind, h)
            cols_a += list(range(off, off + HALF))
            cols_b += list(range(off + HALF, off + HEAD_DIM))
    cols = cols_a + cols_b
    cols += list(range(_OFF_VA, _OFF_VA + N_KV_A * HEAD_DIM))
    cols += list(range(_OFF_VB, _OFF_VB + N_HEADS_B * HEAD_DIM))
    cols += list(range(_OFF_WI, _OFF_WI + N_IDX_HEADS)) + [zero_col] * (LANES - N_IDX_HEADS)
    assert len(cols) == N_COL
    return np.asarray(cols, np.int32)


def _take_runs(w, idx, axis):
    size = w.shape[axis]
    pieces, a = [], 0
    idx = [int(v) for v in idx]
    while a < len(idx):
        b = a + 1
        if idx[a] == size:
            while b < len(idx) and idx[b] == size:
                b += 1
            shape = list(w.shape)
            shape[axis] = b - a
            pieces.append(jnp.zeros(shape, w.dtype))
        else:
            while b < len(idx) and idx[b] == idx[b - 1] + 1:
                b += 1
            pieces.append(lax.slice_in_dim(w, idx[a], idx[b - 1] + 1, axis=axis))
        a = b
    return jnp.concatenate(pieces, axis=axis)


def _interleave_matrix():
    p = np.zeros((2 * LANES, 2 * LANES), np.float32)
    for head in range(4):
        for i in range(HALF):
            p[HALF * head + i, HEAD_DIM * head + i] = 1.0
            p[LANES + HALF * head + i, HEAD_DIM * head + HALF + i] = 1.0
    return p


def _group_sum_matrix():
    g = np.zeros((LANES, LANES), np.float32)
    for k in range(LANES // HALF):
        g[HALF * k:HALF * (k + 1), HALF * k:HALF * (k + 1)] = 1.0
    return g


def _dilated_bias(tq):
    max_win = max(w for w, _ in DILATED_PATTERNS)
    nd = max_win // tq + 1
    d = np.arange(nd)[:, None, None] * tq + np.arange(tq)[None, :, None] - np.arange(tq)[None, None, :]
    mult = np.zeros(d.shape, np.float64)
    for win, dil in DILATED_PATTERNS:
        mult += ((d >= 0) & (d <= win) & (d % dil == 0)).astype(np.float64)
    with np.errstate(divide="ignore"):
        bias = np.where(mult > 0, np.log(np.maximum(mult, 1.0)), NEG_BIG)
    return bias.astype(np.float32), nd


def _nt_dot(a, b):
    return lax.dot_general(a, b, (((1,), (1,)), ((), ())), preferred_element_type=f32)


def _adaln_kernel(c_ref, w_ref, b_ref, o_ref):
    c = c_ref[...]
    a = c * jax.nn.sigmoid(c)
    o_ref[...] = jnp.dot(a, w_ref[...], preferred_element_type=f32) + b_ref[...]


def _adaln(c, w_ada, b_ada):
    bsz, d = c.shape
    n = w_ada.shape[1]
    rows = -(-bsz // 8) * 8
    c_pad = jnp.zeros((rows, d), f32).at[:bsz].set(c)
    tn = min(ADALN_COLS, n)
    out = pl.pallas_call(
        _adaln_kernel,
        grid=(n // tn,),
        in_specs=[pl.BlockSpec((rows, d), lambda j: (0, 0)),
                  pl.BlockSpec((d, tn), lambda j: (0, j)),
                  pl.BlockSpec((1, tn), lambda j: (0, j))],
        out_specs=pl.BlockSpec((rows, tn), lambda j: (0, j)),
        out_shape=jax.ShapeDtypeStruct((rows, n), f32),
        name="adaln",
    )(c_pad, w_ada, b_ada.reshape(1, n))
    return out[:bsz]


def _proj_kernel(x_ref, pos_ref, sc_ref, sh_ref, g_ref, w_ref, ga_ref, gb_ref, gsum_ref, perm_ref, invf_ref,
                 qa_ref, ka_ref, va_ref, qi_ref, ki_ref, wi_ref, qb_ref, kb_ref, vb_ref):
    x = x_ref[0]
    ms = jnp.mean(x * x, axis=-1, keepdims=True)
    h = (x * lax.rsqrt(ms + EPS)) * g_ref[...]
    h = h * (1.0 + sc_ref[0]) + sh_ref[0]
    proj = jnp.dot(h.astype(bf16), w_ref[...], preferred_element_type=f32)

    ang = pos_ref[0] * invf_ref[...]
    cos = jnp.cos(ang)
    sin = jnp.sin(ang)
    gsum = gsum_ref[...]
    perm = perm_ref[...]
    heads = []
    for c in range(N_CHUNKS):
        a = proj[:, LANES * c:LANES * (c + 1)]
        b = proj[:, SLAB + LANES * c:SLAB + LANES * (c + 1)]
        if 4 * c < N_NORM_SLOTS:
            ss = a * a + b * b
            hi = ss.astype(bf16)
            lo = (ss - hi.astype(f32)).astype(bf16)
            tot = (jnp.dot(hi, gsum, preferred_element_type=f32)
                   + jnp.dot(lo, gsum, preferred_element_type=f32))
            inv = lax.rsqrt(tot * (1.0 / HEAD_DIM) + EPS)
            a = a * inv * ga_ref[:, LANES * c:LANES * (c + 1)]
            b = b * inv * gb_ref[:, LANES * c:LANES * (c + 1)]
        ra = a * cos - b * sin
        rb = b * cos + a * sin
        ab = jnp.concatenate([ra, rb], axis=1).astype(bf16)
        heads.append(jnp.dot(ab, perm, preferred_element_type=f32).astype(bf16))

    pair = LANES
    qa_ref[0, :, 0:2 * pair] = heads[0]
    qa_ref[0, :, 2 * pair:4 * pair] = heads[1]
    qa_ref[0, :, 4 * pair:5 * pair] = heads[2][:, 0:pair]
    ka_ref[0] = heads[2][:, pair:2 * pair]
    qb_ref[0, :, 0:2 * pair] = heads[3]
    qb_ref[0, :, 2 * pair:3 * pair] = heads[4][:, 0:pair]
    kb_ref[0, :, 0:pair] = heads[4][:, pair:2 * pair]
    kb_ref[0, :, pair:3 * pair] = heads[5]
    qi_ref[0, :, 0:2 * pair] = heads[6]
    qi_ref[0, :, 2 * pair:4 * pair] = heads[7]
    ki_ref[0] = heads[8][:, 0:pair]
    va_ref[0] = proj[:, _COL_VA:_COL_VB].astype(bf16)
    vb_ref[0] = proj[:, _COL_VB:_COL_WI].astype(bf16)
    wi_ref[0] = proj[:, _COL_WI:N_COL]


def _project(x, posf, sc, sh, g, w_perm, gain_a, gain_b):
    bsz, s, d = x.shape
    ts = min(PROJ_ROWS, s)
    gsum = jnp.asarray(_group_sum_matrix(), bf16)
    perm = jnp.asarray(_interleave_matrix(), bf16)
    inv = ROPE_THETA ** (-jnp.arange(HALF, dtype=f32) / HALF)
    invf = jnp.tile(inv, LANES // HALF)[None, :]
    wa = N_HEADS_A * HEAD_DIM
    wb = N_HEADS_B * HEAD_DIM
    wq = N_IDX_HEADS * IDX_DIM
    const = lambda b, i: (0, 0)
    tile = lambda b, i: (b, i, 0)
    per_b = lambda b, i: (b, 0, 0)
    out_shape = [jax.ShapeDtypeStruct((bsz, s, w), dt) for w, dt in
                 ((wa, bf16), (LANES, bf16), (LANES, bf16), (wq, bf16), (LANES, bf16), (LANES, f32),
                  (wb, bf16), (wb, bf16), (wb, bf16))]
    out_specs = [pl.BlockSpec((1, ts, sh_.shape[2]), tile) for sh_ in out_shape]
    return pl.pallas_call(
        _proj_kernel,
        grid=(bsz, s // ts),
        in_specs=[pl.BlockSpec((1, ts, d), tile),
                  pl.BlockSpec((1, ts, 1), tile),
                  pl.BlockSpec((1, 1, d), per_b),
                  pl.BlockSpec((1, 1, d), per_b),
                  pl.BlockSpec((1, d), const),
                  pl.BlockSpec((d, N_COL), const),
                  pl.BlockSpec((1, N_NORM_SLOTS * HALF), const),
                  pl.BlockSpec((1, N_NORM_SLOTS * HALF), const),
                  pl.BlockSpec((LANES, LANES), const),
                  pl.BlockSpec((2 * LANES, 2 * LANES), const),
                  pl.BlockSpec((1, LANES), const)],
        out_specs=out_specs,
        out_shape=out_shape,
        compiler_params=pltpu.CompilerParams(
            dimension_semantics=("arbitrary", "arbitrary"), vmem_limit_bytes=VMEM_LIMIT),
        name="in_proj",
    )(x, posf, sc, sh, g, w_perm, gain_a, gain_b, gsum, perm, invf)


def _sparse_attn_kernel(qa_ref, ka_ref, va_ref, qi_ref, ki_ref, wi_ref, triu_ref, o_ref,
                        score_ref, qis_ref, qas_ref, wrep_ref, m_ref, acc_ref,
                        *, qb, tk, n_sel, w_scale):
    i = pl.program_id(1)
    n_pairs_a = N_HEADS_A // 2
    lane = lax.broadcasted_iota(i32, (qb, LANES), 1)
    left = lane < HEAD_DIM
    mask_l = left.astype(f32).astype(bf16)
    mask_r = (1.0 - left.astype(f32)).astype(bf16)

    for c in range(N_IDX_HEADS // 2):
        ch = qi_ref[0, :, LANES * c:LANES * (c + 1)]
        qis_ref[(2 * c) * qb:(2 * c + 1) * qb, :] = ch * mask_l
        qis_ref[(2 * c + 1) * qb:(2 * c + 2) * qb, :] = ch * mask_r
    q_scale = jnp.asarray(HEAD_DIM ** -0.5, bf16)
    for c in range(n_pairs_a):
        ch = qa_ref[0, :, LANES * c:LANES * (c + 1)] * q_scale
        qas_ref[0, c * qb:(c + 1) * qb, :] = ch * mask_l
        qas_ref[1, c * qb:(c + 1) * qb, :] = ch * mask_r
    wi = wi_ref[0] * w_scale
    for h in range(N_IDX_HEADS):
        wrep_ref[h] = jnp.broadcast_to(wi[:, h:h + 1], (qb, LANES))

    n_tiles = ((i + 1) * qb - 1) // tk + 1
    reps = tk // LANES
    row = lax.broadcasted_iota(i32, (qb, tk), 0) + i * qb
    col = lax.broadcasted_iota(i32, (qb, tk), 1)

    def score_body(j, rmax):
        start = pl.multiple_of(j * tk, tk)
        kt = ki_ref[0, pl.ds(start, tk), :]
        lg = _nt_dot(qis_ref[...], kt)
        acc = jnp.zeros((qb, tk), f32)
        for h in range(N_IDX_HEADS):
            wr = jnp.concatenate([wrep_ref[h]] * reps, axis=1)
            acc = acc + jnp.maximum(lg[h * qb:(h + 1) * qb], 0.0) * wr
        acc = jnp.where(col + j * tk <= row, acc, -jnp.inf)
        score_ref[j] = acc
        for r in range(reps):
            rmax = jnp.maximum(rmax, acc[:, LANES * r:LANES * (r + 1)])
        return rmax

    rmax = lax.fori_loop(0, n_tiles, score_body, jnp.full((qb, LANES), -jnp.inf, f32))
    rmax = jnp.broadcast_to(jnp.max(rmax, axis=1, keepdims=True), (qb, LANES))

    def key_to_float(key):
        return lax.bitcast_convert_type(key ^ ((key >> 31) & 0x7FFFFFFF), f32)

    def float_to_key(v):
        bits = lax.bitcast_convert_type(v, i32)
        return bits ^ ((bits >> 31) & 0x7FFFFFFF)

    sweep_rows = min(qb, LANES)
    ones_rows = jnp.ones((SUBLANES, LANES), bf16)

    def to_rows(x_lane):
        return jnp.transpose(jnp.broadcast_to(x_lane[0:1, :], (LANES, qb)))

    def to_lanes(cnt):
        return _nt_dot(ones_rows, cnt.astype(bf16))

    def count_ge(trial_lane):
        trial_rows = to_rows(trial_lane)
        parts = []
        for r0 in range(0, qb, sweep_rows):
            trial_r = trial_rows[r0:r0 + sweep_rows]

            def body(j, cnt, r0=r0, trial_r=trial_r):
                for r in range(reps):
                    ch = score_ref[j, r0:r0 + sweep_rows, LANES * r:LANES * (r + 1)]
                    cnt = cnt + jnp.where(ch >= trial_r, 1.0, 0.0)
                return cnt
            parts.append(lax.fori_loop(0, n_tiles, body, jnp.zeros((sweep_rows, LANES), f32)))
        return to_lanes(jnp.concatenate(parts, axis=0))

    log_target = math.log(n_sel - 0.5)
    rmax = jnp.transpose(rmax)[0:SUBLANES, :]
    rmax_pad = rmax + jnp.abs(rmax) * 2.0 ** -20 + 1e-30

    def zero_counts(r0):
        def body(j, cnts):
            ge, gt = cnts
            for r in range(reps):
                ch = score_ref[j, r0:r0 + sweep_rows, LANES * r:LANES * (r + 1)]
                ge = ge + jnp.where(ch >= 0.0, 1.0, 0.0)
                gt = gt + jnp.where(ch > 0.0, 1.0, 0.0)
            return ge, gt
        z = jnp.zeros((sweep_rows, LANES), f32)
        ge, gt = lax.fori_loop(0, n_tiles, body, (z, z))
        return to_lanes(ge), to_lanes(gt)

    zero_parts = [zero_counts(r0) for r0 in range(0, qb, sweep_rows)]
    ge0 = jnp.concatenate([p[0] for p in zero_parts], axis=1)
    gt0 = jnp.concatenate([p[1] for p in zero_parts], axis=1)
    zeros = jnp.zeros((SUBLANES, qb), f32)
    total = zeros + (n_tiles * tk).astype(f32)
    above = ge0 >= n_sel
    lo_v0 = jnp.where(above, 0.0, -jnp.inf)
    lo_c0 = jnp.where(above, ge0, total)
    hi_v0 = jnp.where(above, jnp.inf, 0.0)
    hi_c0 = jnp.where(above, jnp.where(gt0 < n_sel, gt0, 0.0), ge0)
    done0 = (above & (gt0 < n_sel)) | (lo_c0 == n_sel)

    def search_cond(state):
        return (state[0] < SEARCH_MAX_STEPS) & state[-1]

    def search_body(state):
        it, lo_v, lo_c, f_lo, hi_v, hi_c, f_hi, last, done, _ = state
        lo_k = float_to_key(lo_v)
        hi_k = float_to_key(hi_v)
        hi_eff = jnp.where(hi_v == jnp.inf, rmax_pad, hi_v)
        t_int = lo_v + (hi_eff - lo_v) * (f_lo / (f_lo - f_hi))
        t_int = key_to_float(float_to_key(t_int))
        use_int = (lo_v > -jnp.inf) & (t_int > lo_v) & (t_int < hi_v) & (it < SEARCH_INTERP_STEPS)
        mid_k = (lo_k >> 1) + (hi_k >> 1) + (lo_k & hi_k & 1)
        t = jnp.where(use_int, t_int, key_to_float(mid_k))
        c = count_ge(t)
        f = jnp.log(jnp.maximum(c, 0.5)) - log_target
        active = done < 0.5
        is_lo = (c >= n_sel) & active
        is_hi = (c < n_sel) & active
        f_hi = jnp.where(is_lo & (last > 0.0), f_hi * 0.5, f_hi)
        f_lo = jnp.where(is_hi & (last < 0.0), f_lo * 0.5, f_lo)
        lo_v = jnp.where(is_lo, t, lo_v)
        lo_c = jnp.where(is_lo, c, lo_c)
        f_lo = jnp.where(is_lo, f, f_lo)
        hi_v = jnp.where(is_hi, t, hi_v)
        hi_c = jnp.where(is_hi, c, hi_c)
        f_hi = jnp.where(is_hi, f, f_hi)
        last = jnp.where(is_lo, 1.0, jnp.where(is_hi, -1.0, last))
        settled = (lo_c == n_sel) | (float_to_key(lo_v) + 1 >= float_to_key(hi_v))
        done = jnp.where(settled, 1.0, done)
        unresolved = jnp.min(done) < 0.5
        return it + 1, lo_v, lo_c, f_lo, hi_v, hi_c, f_hi, last, done, unresolved

    done0 = jnp.where(done0, 1.0, 0.0)
    init = (jnp.int32(0), lo_v0, lo_c0, jnp.log(lo_c0) - log_target,
            hi_v0, hi_c0, jnp.log(jnp.maximum(hi_c0, 0.5)) - log_target,
            zeros, done0, jnp.min(done0) < 0.5)
    final = lax.while_loop(search_cond, search_body, init)
    thr, lo_c, hi_c = final[1], final[2], final[5]
    thr_t = jnp.concatenate([to_rows(thr)] * reps, axis=1)
    tied = lo_c != n_sel
    any_tied = jnp.max(jnp.where(tied, 1.0, 0.0)) > 0.0
    quota = to_rows(jnp.where(tied, n_sel - hi_c, 2.0 * tk * (n_tiles + 1).astype(f32)))
    quota_t = jnp.concatenate([quota] * reps, axis=1)

    m_ref[...] = jnp.full(m_ref.shape, NEG_BIG, f32)
    acc_ref[...] = jnp.zeros(acc_ref.shape, f32)
    lane_k = lax.broadcasted_iota(i32, (tk, LANES), 1)
    kmask_l = (lane_k < HEAD_DIM).astype(f32).astype(bf16)
    kmask_r = (lane_k >= HEAD_DIM).astype(f32).astype(bf16)

    def attn_body(j, n_ties, with_ties):
        start = pl.multiple_of(j * tk, tk)
        sc = score_ref[j]
        if with_ties:
            eq = jnp.where(sc == thr_t, 1.0, 0.0)
            before = (jnp.dot(eq.astype(bf16), triu_ref[...], preferred_element_type=f32)
                      + jnp.concatenate([n_ties] * reps, axis=1))
            sel = (sc > thr_t) | ((sc == thr_t) & (before < quota_t))
            n_ties = n_ties + jnp.broadcast_to(jnp.sum(eq, axis=1, keepdims=True), (qb, LANES))
        else:
            sel = sc >= thr_t
        sel = sel & (col + j * tk <= row)
        bias = jnp.where(sel, 0.0, NEG_BIG)
        bias = jnp.concatenate([bias] * n_pairs_a, axis=0)
        kk = ka_ref[0, pl.ds(start, tk), :]
        vv = va_ref[0, pl.ds(start, tk), :]
        v_ext = (vv * kmask_l + kmask_r, vv * kmask_r + kmask_l)
        for g in range(N_KV_A):
            s = _nt_dot(qas_ref[g], kk) + bias
            m_old = m_ref[g]
            m_new = jnp.maximum(m_old, jnp.max(s, axis=1, keepdims=True))
            alpha = jnp.exp(m_old - m_new)
            p = jnp.exp(s - jnp.concatenate([m_new] * reps, axis=1))
            acc_ref[g] = alpha * acc_ref[g] + jnp.dot(p.astype(bf16), v_ext[g], preferred_element_type=f32)
            m_ref[g] = m_new
        return n_ties

    no_ties = jnp.zeros((qb, LANES), f32)

    @pl.when(any_tied)
    def _():
        lax.fori_loop(0, n_tiles, functools.partial(attn_body, with_ties=True), no_ties)

    @pl.when(jnp.logical_not(any_tied))
    def _():
        lax.fori_loop(0, n_tiles, functools.partial(attn_body, with_ties=False), no_ties)

    for c in range(n_pairs_a):
        rows = slice(c * qb, (c + 1) * qb)
        a0 = acc_ref[0, rows, :]
        a1 = acc_ref[1, rows, :]
        o0 = a0 / pltpu.roll(a0, HEAD_DIM, axis=1)
        o1 = a1 / pltpu.roll(a1, HEAD_DIM, axis=1)
        o_ref[0, :, LANES * c:LANES * (c + 1)] = jnp.where(left, o0, o1).astype(bf16)


def _sparse_attention(qa, ka, va, qi, ki, wi):
    bsz, s, wa = qa.shape
    qb = min(ATTN_QUERY_ROWS, s)
    tk = min(ATTN_KEY_TILE, s)
    n_sel = min(TOPK_MAX, s // 4)
    w_scale = N_IDX_HEADS ** -0.5 * IDX_DIM ** -0.5
    n_pairs_a = N_HEADS_A // 2
    tile = lambda b, i: (b, i, 0)
    per_b = lambda b, i: (b, 0, 0)
    kern = functools.partial(_sparse_attn_kernel, qb=qb, tk=tk, n_sel=float(n_sel), w_scale=w_scale)
    return pl.pallas_call(
        kern,
        grid=(bsz, s // qb),
        in_specs=[pl.BlockSpec((1, qb, wa), tile),
                  pl.BlockSpec((1, s, LANES), per_b),
                  pl.BlockSpec((1, s, LANES), per_b),
                  pl.BlockSpec((1, qb, qi.shape[2]), tile),
                  pl.BlockSpec((1, s, LANES), per_b),
                  pl.BlockSpec((1, qb, LANES), tile),
                  pl.BlockSpec((tk, tk), lambda b, i: (0, 0))],
        out_specs=pl.BlockSpec((1, qb, wa), tile),
        out_shape=jax.ShapeDtypeStruct((bsz, s, wa), bf16),
        scratch_shapes=[pltpu.VMEM((s // tk, qb, tk), f32),
                        pltpu.VMEM((N_IDX_HEADS * qb, LANES), bf16),
                        pltpu.VMEM((N_KV_A, n_pairs_a * qb, LANES), bf16),
                        pltpu.VMEM((N_IDX_HEADS, qb, LANES), f32),
                        pltpu.VMEM((N_KV_A, n_pairs_a * qb, LANES), f32),
                        pltpu.VMEM((N_KV_A, n_pairs_a * qb, LANES), f32)],
        compiler_params=pltpu.CompilerParams(
            dimension_semantics=("arbitrary", "arbitrary"), vmem_limit_bytes=VMEM_LIMIT),
        name="sparse_attn",
    )(qa, ka, va, qi, ki, wi, jnp.asarray(np.triu(np.ones((tk, tk), np.float32), k=1), bf16))


def _dilated_kernel(q_ref, k_ref, v_ref, bias_ref, o_ref, qs_ref, m_ref, acc_ref, *, tq, nd):
    i = pl.program_id(2)
    lane = lax.broadcasted_iota(i32, (tq, LANES), 1)
    left = lane < HEAD_DIM
    mask_l = left.astype(f32).astype(bf16)
    mask_r = (1.0 - left.astype(f32)).astype(bf16)
    q = q_ref[0] * jnp.asarray(HEAD_DIM ** -0.5, bf16)
    qs_ref[0:tq, :] = q * mask_l
    qs_ref[tq:2 * tq, :] = q * mask_r
    ones = jnp.ones((tq, LANES), bf16)

    for d in range(nd):
        j = i - d
        start = pl.multiple_of(jnp.maximum(j, 0) * tq, tq)
        kk = k_ref[0, pl.ds(start, tq), :]
        v_ext = jnp.concatenate([v_ref[0, pl.ds(start, tq), :], ones], axis=1)
        b = bias_ref[d] + jnp.where(j >= 0, 0.0, NEG_BIG)
        s = _nt_dot(qs_ref[...], kk) + jnp.concatenate([b, b], axis=0)
        m_d = jnp.broadcast_to(jnp.max(s, axis=1, keepdims=True), (2 * tq, LANES))
        p = jnp.exp(s - jnp.concatenate([m_d] * (tq // LANES), axis=1))
        acc_ref[d] = jnp.dot(p.astype(bf16), v_ext, preferred_element_type=f32)
        m_ref[d] = m_d

    m = m_ref[0]
    for d in range(1, nd):
        m = jnp.maximum(m, m_ref[d])
    acc = jnp.zeros((2 * tq, 2 * LANES), f32)
    for d in range(nd):
        w = jnp.exp(m_ref[d] - m)
        acc = acc + jnp.concatenate([w, w], axis=1) * acc_ref[d]
    o0 = acc[0:tq, 0:LANES] / acc[0:tq, LANES:2 * LANES]
    o1 = acc[tq:2 * tq, 0:LANES] / acc[tq:2 * tq, LANES:2 * LANES]
    o_ref[0] = jnp.where(left, o0, o1).astype(bf16)


def _dilated_attention(qb_, kb_, vb_):
    bsz, s, wb = qb_.shape
    tq = min(DILATED_TILE, s)
    bias_np, nd = _dilated_bias(tq)
    bias = jnp.asarray(bias_np)
    n_pairs = wb // LANES
    kern = functools.partial(_dilated_kernel, tq=tq, nd=nd)
    return pl.pallas_call(
        kern,
        grid=(bsz, n_pairs, s // tq),
        in_specs=[pl.BlockSpec((1, tq, LANES), lambda b, p, i: (b, i, p)),
                  pl.BlockSpec((1, s, LANES), lambda b, p, i: (b, 0, p)),
                  pl.BlockSpec((1, s, LANES), lambda b, p, i: (b, 0, p)),
                  pl.BlockSpec((nd, tq, tq), lambda b, p, i: (0, 0, 0))],
        out_specs=pl.BlockSpec((1, tq, LANES), lambda b, p, i: (b, i, p)),
        out_shape=jax.ShapeDtypeStruct((bsz, s, wb), bf16),
        scratch_shapes=[pltpu.VMEM((2 * tq, LANES), bf16),
                        pltpu.VMEM((nd, 2 * tq, LANES), f32),
                        pltpu.VMEM((nd, 2 * tq, 2 * LANES), f32)],
        compiler_params=pltpu.CompilerParams(
            dimension_semantics=("arbitrary", "arbitrary", "arbitrary"), vmem_limit_bytes=VMEM_LIMIT),
        name="dilated_attn",
    )(qb_, kb_, vb_, bias)


def _mid_kernel(oa_ref, ob_ref, x_ref, gta_ref, scf_ref, shf_ref, gtf_ref, woa_ref, wob_ref, gffn_ref,
                wsgu_ref, wsd_ref, wrt_ref, rbias_ref, triu_ref,
                base_ref, h2_ref, sel_ref, gate_ref, rank_ref, cnt_ref, carry_ref, *, n_exp, t, d_sh):
    step = pl.program_id(0)

    @pl.when(step == 0)
    def _():
        carry_ref[...] = jnp.zeros(carry_ref.shape, f32)

    mix = (jnp.dot(oa_ref[...], woa_ref[...], preferred_element_type=f32)
           + jnp.dot(ob_ref[...], wob_ref[...], preferred_element_type=f32))
    x1 = x_ref[...] + gta_ref[0] * mix
    ms = jnp.mean(x1 * x1, axis=-1, keepdims=True)
    h2 = ((x1 * lax.rsqrt(ms + EPS)) * gffn_ref[...]) * (1.0 + scf_ref[0]) + shf_ref[0]
    h2b = h2.astype(bf16)
    h2f = h2b.astype(f32)
    n_sub = h2_ref.shape[0] // t
    for a in range(n_sub):
        h2_ref[pl.ds(a, t, stride=n_sub), :] = h2f[:, LANES * a:LANES * (a + 1)]

    gu = jnp.dot(h2b, wsgu_ref[...], preferred_element_type=f32)
    g = gu[:, :d_sh]
    u = gu[:, d_sh:]
    act = (g * jax.nn.sigmoid(g)) * u
    shared = jnp.dot(act.astype(bf16), wsd_ref[...], preferred_element_type=f32)
    base = x1 + gtf_ref[0] * shared
    for a in range(n_sub):
        base_ref[pl.ds(a, t, stride=n_sub), :] = base[:, LANES * a:LANES * (a + 1)]

    scores = jax.nn.sigmoid(_nt_dot(wrt_ref[...], h2b))
    biased = scores + rbias_ref[...]
    per = n_exp // N_GROUPS
    neg_inf = jnp.float32(-jnp.inf)
    ri_g = lax.broadcasted_iota(i32, (per, t), 0).astype(f32)
    gs = []
    for grp in range(N_GROUPS):
        blk = biased[grp * per:(grp + 1) * per]
        m1 = jnp.max(blk, axis=0, keepdims=True)
        idx1 = jnp.min(jnp.where(blk == m1, ri_g, float(per)), axis=0, keepdims=True)
        m2 = jnp.max(jnp.where(ri_g == idx1, neg_inf, blk), axis=0, keepdims=True)
        gs.append(m1 + m2)
    masked_rows = []
    for grp in range(N_GROUPS):
        beaten = jnp.zeros((1, t), f32)
        for g2 in range(N_GROUPS):
            if g2 == grp:
                continue
            wins = gs[g2] > gs[grp]
            if g2 < grp:
                wins = wins | (gs[g2] == gs[grp])
            beaten = beaten + jnp.where(wins, 1.0, 0.0)
        keep = jnp.broadcast_to(beaten < TOPK_GROUPS, (per, t))
        masked_rows.append(jnp.where(keep, biased[grp * per:(grp + 1) * per], neg_inf))
    masked = jnp.concatenate(masked_rows, axis=0)

    ri = lax.broadcasted_iota(i32, (n_exp, t), 0).astype(f32)
    selmask = jnp.zeros((n_exp, t), f32)
    idxs, graw = [], []
    for _ in range(TOP_K):
        m = jnp.max(masked, axis=0, keepdims=True)
        idx = jnp.min(jnp.where(masked == m, ri, float(n_exp)), axis=0, keepdims=True)
        onehot = ri == idx
        graw.append(jnp.sum(jnp.where(onehot, scores, 0.0), axis=0, keepdims=True))
        masked = jnp.where(onehot, neg_inf, masked)
        selmask = jnp.where(onehot, 1.0, selmask)
        idxs.append(idx)
    den = graw[0]
    for k in range(1, TOP_K):
        den = den + graw[k]

    prefix = jnp.dot(selmask.astype(bf16), triu_ref[...], preferred_element_type=f32)
    prefix = prefix + jnp.concatenate([carry_ref[...]] * (t // LANES), axis=1)
    for k in range(TOP_K):
        rank_k = jnp.sum(jnp.where(ri == idxs[k], prefix, 0.0), axis=0, keepdims=True)
        sel_ref[k:k + 1, :] = idxs[k].astype(i32)
        rank_ref[k:k + 1, :] = rank_k.astype(i32)
        gate_ref[k:k + 1, :] = graw[k] / den * ROUTED_SCALE
    carry_ref[...] = carry_ref[...] + jnp.broadcast_to(
        jnp.sum(selmask, axis=1, keepdims=True), carry_ref.shape)
    cnt_ref[...] = carry_ref[...]


def _mid(oa, ob, x2, gta, scf, shf, gtf, woa, wob, gffn, wsgu, wsd, wrt, rbias_rep, s):
    n, d = x2.shape
    t = rbias_rep.shape[1]
    n_exp = wrt.shape[0]
    d_sh = wsd.shape[0]
    triu = jnp.asarray(np.triu(np.ones((t, t), np.float32), k=1), bf16)
    tile = lambda i: (i, 0)
    const = lambda i: (0, 0)
    per_b = lambda i: ((i * t) // s, 0, 0)
    lane_tile = lambda i: (0, i)
    n_sub = d // LANES
    tok_tiles = jax.ShapeDtypeStruct((n * n_sub, LANES), f32)
    tok_spec = pl.BlockSpec((t * n_sub, LANES), tile)
    out_shape = [tok_tiles, tok_tiles,
                 jax.ShapeDtypeStruct((TOP_K, n), i32), jax.ShapeDtypeStruct((TOP_K, n), f32),
                 jax.ShapeDtypeStruct((TOP_K, n), i32), jax.ShapeDtypeStruct((n_exp, LANES), f32)]
    out_specs = [tok_spec, tok_spec,
                 pl.BlockSpec((TOP_K, t), lane_tile), pl.BlockSpec((TOP_K, t), lane_tile),
                 pl.BlockSpec((TOP_K, t), lane_tile), pl.BlockSpec((n_exp, LANES), const)]
    kern = functools.partial(_mid_kernel, n_exp=n_exp, t=t, d_sh=d_sh)
    return pl.pallas_call(
        kern,
        grid=(n // t,),
        in_specs=[pl.BlockSpec((t, oa.shape[1]), tile),
                  pl.BlockSpec((t, ob.shape[1]), tile),
                  pl.BlockSpec((t, d), tile),
                  pl.BlockSpec((1, 1, d), per_b),
                  pl.BlockSpec((1, 1, d), per_b),
                  pl.BlockSpec((1, 1, d), per_b),
                  pl.BlockSpec((1, 1, d), per_b),
                  pl.BlockSpec(woa.shape, const),
                  pl.BlockSpec(wob.shape, const),
                  pl.BlockSpec((1, d), const),
                  pl.BlockSpec(wsgu.shape, const),
                  pl.BlockSpec(wsd.shape, const),
                  pl.BlockSpec(wrt.shape, const),
                  pl.BlockSpec(rbias_rep.shape, const),
                  pl.BlockSpec((t, t), const)],
        out_specs=out_specs,
        out_shape=out_shape,
        scratch_shapes=[pltpu.VMEM((n_exp, LANES), f32)],
        compiler_params=pltpu.CompilerParams(
            dimension_semantics=("arbitrary",), vmem_limit_bytes=VMEM_LIMIT),
        name="mid",
    )(oa, ob, x2, gta, scf, shf, gtf, woa, wob, gffn, wsgu, wsd, wrt, rbias_rep, triu)


def _dispatch_kernel(zb_ref, slot_ref, h_ref, xs_ref, zero_ref, sem_ref, *, td, n_zero):
    step = pl.program_id(0)

    def zero_copy(b):
        return pltpu.make_async_copy(zero_ref, xs_ref.at[pl.ds(zb_ref[b], ZERO_ROWS)], sem_ref.at[0])

    @pl.when(step == 0)
    def _():
        zero_ref[...] = jnp.zeros(zero_ref.shape, f32)

        def start(b, c):
            @pl.when(zb_ref[b] >= 0)
            def _():
                zero_copy(b).start()
            return c

        def wait(b, c):
            @pl.when(zb_ref[b] >= 0)
            def _():
                zero_copy(b).wait()
            return c

        lax.fori_loop(0, n_zero, start, 0)
        lax.fori_loop(0, n_zero, wait, 0)

    def start_rows(tok, c):
        for k in range(TOP_K):
            pltpu.make_async_copy(h_ref.at[pl.ds(tok, 1)], xs_ref.at[pl.ds(slot_ref[k, tok], 1)],
                                  sem_ref.at[1]).start(priority=k % 2)
        return c

    lax.fori_loop(0, td, start_rows, 0)
    for k in range(TOP_K):
        pltpu.make_async_copy(h_ref, xs_ref.at[pl.ds(0, td)], sem_ref.at[1]).wait()


def _dispatch(zero_start, slot, h2, n_slots):
    n, rows, lanes = h2.shape
    td = min(DISPATCH_ROWS, n)
    kern = functools.partial(_dispatch_kernel, td=td, n_zero=zero_start.shape[0])
    grid_spec = pltpu.PrefetchScalarGridSpec(
        num_scalar_prefetch=1,
        grid=(n // td,),
        in_specs=[pl.BlockSpec((TOP_K, td), lambda i, tail: (0, i), memory_space=pltpu.SMEM),
                  pl.BlockSpec((td, rows, lanes), lambda i, tail: (i, 0, 0))],
        out_specs=pl.BlockSpec(memory_space=pl.ANY),
        scratch_shapes=[pltpu.VMEM((ZERO_ROWS, rows, lanes), f32), pltpu.SemaphoreType.DMA((2,))],
    )
    return pl.pallas_call(
        kern,
        grid_spec=grid_spec,
        out_shape=jax.ShapeDtypeStruct((n_slots, rows, lanes), f32),
        compiler_params=pltpu.CompilerParams(
            dimension_semantics=("arbitrary",), vmem_limit_bytes=VMEM_LIMIT),
        name="dispatch",
    )(zero_start, slot, h2)


def _expert_kernel(be_ref, nu_ref, xs_ref, wg_ref, wu_ref, wd_ref, ys_ref, wgu_s, wd_s, *, n_sub):
    i = pl.program_id(0)
    f = wd_s.shape[0]
    rows = xs_ref.shape[0] // n_sub

    @pl.when(i >= nu_ref[0])
    def _():
        ys_ref[...] = jnp.zeros(ys_ref.shape, f32)

    @pl.when(i < nu_ref[0])
    def _():
        @pl.when((i == 0) | (be_ref[i] != be_ref[jnp.maximum(i - 1, 0)]))
        def _():
            wgu_s[:, 0:f] = wg_ref[0].astype(bf16)
            wgu_s[:, f:2 * f] = wu_ref[0].astype(bf16)
            wd_s[...] = wd_ref[0].astype(bf16)

        xb = jnp.concatenate([xs_ref[pl.ds(a, rows, stride=n_sub), :] for a in range(n_sub)],
                             axis=1).astype(bf16)
        gu = jnp.dot(xb, wgu_s[...], preferred_element_type=f32)
        g = gu[:, 0:f]
        u = gu[:, f:2 * f]
        act = (g * jax.nn.sigmoid(g)) * u
        y = jnp.dot(act.astype(bf16), wd_s[...], preferred_element_type=f32)
        for a in range(n_sub):
            ys_ref[pl.ds(a, rows, stride=n_sub), :] = y[:, LANES * a:LANES * (a + 1)]


def _experts(block_expert, n_used, xs, w_gate, w_up, w_down):
    n_slots, n_sub, lanes = xs.shape
    d = n_sub * lanes
    n_blocks = n_slots // MOE_BLOCK
    f = w_gate.shape[2]
    blk = lambda i, be, nu: (jnp.minimum(i, nu[0] - 1), 0)
    wsel = lambda i, be, nu: (be[jnp.minimum(i, nu[0] - 1)], 0, 0)
    grid_spec = pltpu.PrefetchScalarGridSpec(
        num_scalar_prefetch=2,
        grid=(n_blocks,),
        in_specs=[pl.BlockSpec((MOE_BLOCK * n_sub, lanes), blk),
                  pl.BlockSpec((1, d, f), wsel),
                  pl.BlockSpec((1, d, f), wsel),
                  pl.BlockSpec((1, f, d), wsel)],
        out_specs=pl.BlockSpec((MOE_BLOCK * n_sub, lanes), lambda i, be, nu: (i, 0)),
        scratch_shapes=[pltpu.VMEM((d, 2 * f), bf16), pltpu.VMEM((f, d), bf16)],
    )
    ys = pl.pallas_call(
        functools.partial(_expert_kernel, n_sub=n_sub),
        grid_spec=grid_spec,
        out_shape=jax.ShapeDtypeStruct((n_slots * n_sub, lanes), f32),
        compiler_params=pltpu.CompilerParams(
            dimension_semantics=("arbitrary",), vmem_limit_bytes=VMEM_LIMIT),
        name="experts",
    )(block_expert, n_used, xs.reshape(n_slots * n_sub, lanes), w_gate, w_up, w_down)
    return ys.reshape(n_slots, n_sub, lanes)


def _combine_kernel(slot_ref, nslot_ref, gate_ref, base_ref, gtf_ref, ys_ref, o_ref, buf_ref, res_ref, sem_ref,
                    *, tc):
    step = pl.program_id(0)
    cur = step % 2

    def start_gathers(idx_ref, b):
        def body(tok, c):
            for k in range(TOP_K):
                pltpu.make_async_copy(ys_ref.at[pl.ds(idx_ref[k, tok], 1)],
                                      buf_ref.at[b, k, pl.ds(tok, 1)], sem_ref.at[b]).start(priority=k % 2)
            return c
        lax.fori_loop(0, tc, body, 0)

    @pl.when(step == 0)
    def _():
        start_gathers(slot_ref, 0)

    @pl.when(step + 1 < pl.num_programs(0))
    def _():
        start_gathers(nslot_ref, 1 - cur)

    for k in range(TOP_K):
        pltpu.make_async_copy(ys_ref.at[pl.ds(0, tc)], buf_ref.at[cur, k], sem_ref.at[cur]).wait()

    gtf = gtf_ref[0]

    def reduce_token(tok, c):
        routed = buf_ref[cur, 0, tok] * gate_ref[0, tok]
        for k in range(1, TOP_K):
            routed = routed + buf_ref[cur, k, tok] * gate_ref[k, tok]
        res_ref[pl.ds(pl.multiple_of(tok * n_sub, n_sub), n_sub), :] = base_ref[tok] + gtf * routed
        return c

    n_sub = buf_ref.shape[3]
    lax.fori_loop(0, tc, reduce_token, 0, unroll=4)
    for a in range(n_sub):
        o_ref[:, LANES * a:LANES * (a + 1)] = res_ref[pl.ds(a, tc, stride=n_sub), :]


def _combine(slot, gate, base, gtf, ys, s):
    n, rows, lanes = base.shape
    tc = min(COMBINE_ROWS, n)
    n_steps = n // tc
    kern = functools.partial(_combine_kernel, tc=tc)
    tok_tile = lambda i: (i, 0, 0)
    return pl.pallas_call(
        kern,
        grid=(n_steps,),
        in_specs=[pl.BlockSpec((TOP_K, tc), lambda i: (0, i), memory_space=pltpu.SMEM),
                  pl.BlockSpec((TOP_K, tc), lambda i: (0, jnp.minimum(i + 1, n_steps - 1)),
                               memory_space=pltpu.SMEM),
                  pl.BlockSpec((TOP_K, tc), lambda i: (0, i), memory_space=pltpu.SMEM),
                  pl.BlockSpec((tc, rows, lanes), tok_tile),
                  pl.BlockSpec((1, rows, lanes), lambda i: ((i * tc) // s, 0, 0)),
                  pl.BlockSpec(memory_space=pl.ANY)],
        out_specs=pl.BlockSpec((tc, rows * lanes), lambda i: (i, 0)),
        out_shape=jax.ShapeDtypeStruct((n, rows * lanes), f32),
        scratch_shapes=[pltpu.VMEM((2, TOP_K, tc, rows, lanes), f32), pltpu.VMEM((tc * rows, lanes), f32),
                        pltpu.SemaphoreType.DMA((2,))],
        compiler_params=pltpu.CompilerParams(
            dimension_semantics=("arbitrary",), vmem_limit_bytes=VMEM_LIMIT),
        name="combine",
    )(slot, slot, gate, base, gtf, ys)


def _layer(x, c, posf, w_ada, b_ada, g_mix, w_in, q_norm_a, k_norm_a, q_norm_b, k_norm_b, w_out, g_ffn,
           w_router, router_bias, w_gate, w_up, w_down, ws_gate, ws_up, ws_down):
    bsz, s, d = x.shape
    n = bsz * s
    n_exp = w_router.shape[1]

    mod = _adaln(c, w_ada, b_ada)[:, None, :]
    sh_a, sc_a, gt_a, sh_f, sc_f, gt_f = (mod[..., k * d:(k + 1) * d] for k in range(6))

    w_perm = _take_runs(w_in.astype(bf16), _projection_columns(), axis=1)
    gains = {"qa": q_norm_a, "ka": k_norm_a, "qb": q_norm_b, "kb": k_norm_b}
    gain_a = jnp.concatenate([gains[kind][:HALF] for kind, _ in _SLOTS[:N_NORM_SLOTS]])[None, :].astype(f32)
    gain_b = jnp.concatenate([gains[kind][HALF:] for kind, _ in _SLOTS[:N_NORM_SLOTS]])[None, :].astype(f32)

    qa, ka, va, qi, ki, wi, qb_, kb_, vb_ = _project(x, posf, sc_a, sh_a, g_mix[None, :], w_perm, gain_a, gain_b)
    o_a = _sparse_attention(qa, ka, va, qi, ki, wi)
    o_b = _dilated_attention(qb_, kb_, vb_)

    wa = N_HEADS_A * HEAD_DIM
    rows_a = np.concatenate([np.arange(h * HEAD_DIM, (h + 1) * HEAD_DIM) for h in QA_PAIR_ORDER])
    woa = _take_runs(w_out, rows_a, axis=0).astype(bf16)
    wob = w_out[wa:].astype(bf16)
    wsgu = jnp.concatenate([ws_gate, ws_up], axis=1).astype(bf16)
    wsd = ws_down.astype(bf16)
    wrt = w_router.T.astype(bf16)
    t_mid = min(MID_ROWS, n)
    rbias_rep = jnp.broadcast_to(router_bias.astype(f32)[:, None], (n_exp, t_mid))

    base, h2, sel, gate, rank, cnt = _mid(
        o_a.reshape(n, wa), o_b.reshape(n, -1), x.reshape(n, d), gt_a, sc_f, sh_f, gt_f,
        woa, wob, g_ffn[None, :], wsgu, wsd, wrt, rbias_rep, s)
    base = base.reshape(n, d // LANES, LANES)
    h2 = h2.reshape(n, d // LANES, LANES)

    counts = cnt[:, 0].astype(i32)
    padded = (counts + MOE_BLOCK - 1) // MOE_BLOCK * MOE_BLOCK
    pad_end = jnp.cumsum(padded)
    pad_start = pad_end - padded
    onehot = sel[:, :, None] == jnp.arange(n_exp, dtype=i32)[None, None, :]
    slot = jnp.sum(jnp.where(onehot, pad_start[None, None, :], 0), axis=-1) + rank
    n_blocks = -(-(n * TOP_K) // MOE_BLOCK) + n_exp
    n_slots = n_blocks * MOE_BLOCK
    block_start = jnp.arange(n_blocks, dtype=i32) * MOE_BLOCK
    block_expert = jnp.sum((pad_end[None, :] <= block_start[:, None]).astype(i32), axis=1)
    block_expert = jnp.minimum(block_expert, n_exp - 1)
    n_used = (pad_end[-1] // MOE_BLOCK).astype(i32)[None]
    unit_start = jnp.arange(n_slots // ZERO_ROWS, dtype=i32) * ZERO_ROWS
    unit_expert = jnp.sum((pad_end[None, :] <= unit_start[:, None]).astype(i32), axis=1)
    real_end = pad_start + counts
    unit_real_end = jnp.sum(jnp.where(unit_expert[:, None] == jnp.arange(n_exp, dtype=i32)[None, :],
                                      real_end[None, :], 0), axis=1)
    has_pad = (unit_start + ZERO_ROWS > unit_real_end) | (unit_start >= pad_end[-1])
    zero_start = jnp.where(has_pad, unit_start, -1).astype(i32)

    xs = _dispatch(zero_start, slot, h2, n_slots)
    ys = _experts(block_expert, n_used, xs, w_gate, w_up, w_down)
    out = _combine(slot, gate, base, gt_f.reshape(bsz, d // LANES, LANES), ys, s)
    return out.reshape(bsz, s, d)


def kernel(x, c, positions, w_ada, b_ada, g_mix, w_in, q_norm_a, k_norm_a, q_norm_b, k_norm_b, w_out, g_ffn,
           w_router, router_bias, w_gate, w_up, w_down, ws_gate, ws_up, ws_down):
    posf = positions.astype(f32)[..., None]
    for l in range(w_ada.shape[0]):
        x = _layer(x, c, posf, w_ada[l], b_ada[l], g_mix[l], w_in[l], q_norm_a[l], k_norm_a[l], q_norm_b[l],
                   k_norm_b[l], w_out[l], g_ffn[l], w_router[l], router_bias[l], w_gate[l], w_up[l], w_down[l],
                   ws_gate[l], ws_up[l], ws_down[l])
    return x
```

```python
import functools
import math

import numpy as np
import jax
import jax.numpy as jnp
from jax import lax
from jax.experimental import pallas as pl
from jax.experimental.pallas import tpu as pltpu

f32 = jnp.float32
bf16 = jnp.bfloat16
i32 = jnp.int32

HEAD_DIM = 64
HALF = HEAD_DIM // 2
N_HEADS_A = 10
N_KV_A = 2
N_HEADS_B = 6
N_IDX_HEADS = 8
IDX_DIM = 64
TOPK_MAX = 256
DILATED_PATTERNS = ((128, 1), (512, 4), (2048, 16))
ROPE_THETA = 10000.0
EPS = 1e-6
TOP_K = 8
N_GROUPS = 8
TOPK_GROUPS = 4
ROUTED_SCALE = 2.5
MOE_BLOCK = 512
ZERO_ROWS = 64

LANES = 128
SUBLANES = 8
VMEM_LIMIT = 56 * 1024 * 1024

ADALN_COLS = 512
PROJ_ROWS = 512
ATTN_QUERY_ROWS = 512
ATTN_KEY_TILE = 512
DILATED_TILE = 512
MID_ROWS = 512
DISPATCH_ROWS = 512
COMBINE_ROWS = 256

NEG_BIG = -1e30
KEY_NEG_INF = int(np.int32(np.uint32(0xFF800000) ^ np.uint32(0x7FFFFFFF)))
KEY_POS_INF = 0x7F800000
SEARCH_INTERP_STEPS = 16
SEARCH_MAX_STEPS = SEARCH_INTERP_STEPS + 34

_OFF_QA = 0
_OFF_KA = _OFF_QA + N_HEADS_A * HEAD_DIM
_OFF_VA = _OFF_KA + N_KV_A * HEAD_DIM
_OFF_QI = _OFF_VA + N_KV_A * HEAD_DIM
_OFF_KI = _OFF_QI + N_IDX_HEADS * IDX_DIM
_OFF_WI = _OFF_KI + IDX_DIM
_OFF_QB = _OFF_WI + N_IDX_HEADS
_OFF_KB = _OFF_QB + N_HEADS_B * HEAD_DIM
_OFF_VB = _OFF_KB + N_HEADS_B * HEAD_DIM
D_IN = _OFF_VB + N_HEADS_B * HEAD_DIM

QA_PAIR_ORDER = (0, 5, 1, 6, 2, 7, 3, 8, 4, 9)

_SLOTS = (
    [("qa", h) for h in QA_PAIR_ORDER[:8]] + [("qa", 4), ("qa", 9), ("ka", 0), ("ka", 1)]
    + [("qb", h) for h in range(6)] + [("kb", h) for h in range(6)]
    + [("qi", h) for h in range(8)] + [("ki", 0), ("ki", 0), ("pad", 0), ("pad", 0)]
)
N_NORM_SLOTS = 24
N_CHUNKS = len(_SLOTS) // 4
SLAB = N_CHUNKS * LANES
_COL_VA = 2 * SLAB
_COL_VB = _COL_VA + N_KV_A * HEAD_DIM
_COL_WI = _COL_VB + N_HEADS_B * HEAD_DIM
N_COL = _COL_WI + LANES


def _slot_offset(kind, h):
    base = {"qa": _OFF_QA, "ka": _OFF_KA, "qb": _OFF_QB, "kb": _OFF_KB, "qi": _OFF_QI, "ki": _OFF_KI}
    return base[kind] + h * HEAD_DIM


def _projection_columns():
    zero_col = D_IN
    cols_a, cols_b = [], []
    for kind, h in _SLOTS:
        if kind == "pad":
            cols_a += [zero_col] * HALF
            cols_b += [zero_col] * HALF
        else:
            off = _slot_offset(kind, h)
            cols_a += list(range(off, off + HALF))
            cols_b += list(range(off + HALF, off + HEAD_DIM))
    cols = cols_a + cols_b
    cols += list(range(_OFF_VA, _OFF_VA + N_KV_A * HEAD_DIM))
    cols += list(range(_OFF_VB, _OFF_VB + N_HEADS_B * HEAD_DIM))
    cols += list(range(_OFF_WI, _OFF_WI + N_IDX_HEADS)) + [zero_col] * (LANES - N_IDX_HEADS)
    assert len(cols) == N_COL
    return np.asarray(cols, np.int32)


def _take_runs(w, idx, axis):
    size = w.shape[axis]
    pieces, a = [], 0
    idx = [int(v) for v in idx]
    while a < len(idx):
        b = a + 1
        if idx[a] == size:
            while b < len(idx) and idx[b] == size:
                b += 1
            shape = list(w.shape)
            shape[axis] = b - a
            pieces.append(jnp.zeros(shape, w.dtype))
        else:
            while b < len(idx) and idx[b] == idx[b - 1] + 1:
                b += 1
            pieces.append(lax.slice_in_dim(w, idx[a], idx[b - 1] + 1, axis=axis))
        a = b
    return jnp.concatenate(pieces, axis=axis)


def _interleave_matrix():
    p = np.zeros((2 * LANES, 2 * LANES), np.float32)
    for head in range(4):
        for i in range(HALF):
            p[HALF * head + i, HEAD_DIM * head + i] = 1.0
            p[LANES + HALF * head + i, HEAD_DIM * head + HALF + i] = 1.0
    return p


def _group_sum_matrix():
    g = np.zeros((LANES, LANES), np.float32)
    for k in range(LANES // HALF):
        g[HALF * k:HALF * (k + 1), HALF * k:HALF * (k + 1)] = 1.0
    return g


def _dilated_bias(tq):
    max_win = max(w for w, _ in DILATED_PATTERNS)
    nd = max_win // tq + 1
    d = np.arange(nd)[:, None, None] * tq + np.arange(tq)[None, :, None] - np.arange(tq)[None, None, :]
    mult = np.zeros(d.shape, np.float64)
    for win, dil in DILATED_PATTERNS:
        mult += ((d >= 0) & (d <= win) & (d % dil == 0)).astype(np.float64)
    with np.errstate(divide="ignore"):
        bias = np.where(mult > 0, np.log(np.maximum(mult, 1.0)), NEG_BIG)
    return bias.astype(np.float32), nd


def _nt_dot(a, b):
    return lax.dot_general(a, b, (((1,), (1,)), ((), ())), preferred_element_type=f32)


def _adaln_kernel(c_ref, w_ref, b_ref, o_ref):
    c = c_ref[...]
    a = c * jax.nn.sigmoid(c)
    o_ref[...] = jnp.dot(a, w_ref[...], preferred_element_type=f32) + b_ref[...]


def _adaln(c, w_ada, b_ada):
    bsz, d = c.shape
    n = w_ada.shape[1]
    rows = -(-bsz // 8) * 8
    c_pad = jnp.zeros((rows, d), f32).at[:bsz].set(c)
    tn = min(ADALN_COLS, n)
    out = pl.pallas_call(
        _adaln_kernel,
        grid=(n // tn,),
        in_specs=[pl.BlockSpec((rows, d), lambda j: (0, 0)),
                  pl.BlockSpec((d, tn), lambda j: (0, j)),
                  pl.BlockSpec((1, tn), lambda j: (0, j))],
        out_specs=pl.BlockSpec((rows, tn), lambda j: (0, j)),
        out_shape=jax.ShapeDtypeStruct((rows, n), f32),
        name="adaln",
    )(c_pad, w_ada, b_ada.reshape(1, n))
    return out[:bsz]


def _proj_kernel(x_ref, pos_ref, sc_ref, sh_ref, g_ref, w_ref, ga_ref, gb_ref, gsum_ref, perm_ref, invf_ref,
                 qa_ref, ka_ref, va_ref, qi_ref, ki_ref, wi_ref, qb_ref, kb_ref, vb_ref):
    x = x_ref[0]
    ms = jnp.mean(x * x, axis=-1, keepdims=True)
    h = (x * lax.rsqrt(ms + EPS)) * g_ref[...]
    h = h * (1.0 + sc_ref[0]) + sh_ref[0]
    proj = jnp.dot(h.astype(bf16), w_ref[...], preferred_element_type=f32)

    ang = pos_ref[0] * invf_ref[...]
    cos = jnp.cos(ang)
    sin = jnp.sin(ang)
    gsum = gsum_ref[...]
    perm = perm_ref[...]
    heads = []
    for c in range(N_CHUNKS):
        a = proj[:, LANES * c:LANES * (c + 1)]
        b = proj[:, SLAB + LANES * c:SLAB + LANES * (c + 1)]
        if 4 * c < N_NORM_SLOTS:
            ss = a * a + b * b
            hi = ss.astype(bf16)
            lo = (ss - hi.astype(f32)).astype(bf16)
            tot = (jnp.dot(hi, gsum, preferred_element_type=f32)
                   + jnp.dot(lo, gsum, preferred_element_type=f32))
            inv = lax.rsqrt(tot * (1.0 / HEAD_DIM) + EPS)
            a = a * inv * ga_ref[:, LANES * c:LANES * (c + 1)]
            b = b * inv * gb_ref[:, LANES * c:LANES * (c + 1)]
        ra = a * cos - b * sin
        rb = b * cos + a * sin
        ab = jnp.concatenate([ra, rb], axis=1).astype(bf16)
        heads.append(jnp.dot(ab, perm, preferred_element_type=f32).astype(bf16))

    pair = LANES
    qa_ref[0, :, 0:2 * pair] = heads[0]
    qa_ref[0, :, 2 * pair:4 * pair] = heads[1]
    qa_ref[0, :, 4 * pair:5 * pair] = heads[2][:, 0:pair]
    ka_ref[0] = heads[2][:, pair:2 * pair]
    qb_ref[0, :, 0:2 * pair] = heads[3]
    qb_ref[0, :, 2 * pair:3 * pair] = heads[4][:, 0:pair]
    kb_ref[0, :, 0:pair] = heads[4][:, pair:2 * pair]
    kb_ref[0, :, pair:3 * pair] = heads[5]
    qi_ref[0, :, 0:2 * pair] = heads[6]
    qi_ref[0, :, 2 * pair:4 * pair] = heads[7]
    ki_ref[0] = heads[8][:, 0:pair]
    va_ref[0] = proj[:, _COL_VA:_COL_VB].astype(bf16)
    vb_ref[0] = proj[:, _COL_VB:_COL_WI].astype(bf16)
    wi_ref[0] = proj[:, _COL_WI:N_COL]


def _project(x, posf, sc, sh, g, w_perm, gain_a, gain_b):
    bsz, s, d = x.shape
    ts = min(PROJ_ROWS, s)
    gsum = jnp.asarray(_group_sum_matrix(), bf16)
    perm = jnp.asarray(_interleave_matrix(), bf16)
    inv = ROPE_THETA ** (-jnp.arange(HALF, dtype=f32) / HALF)
    invf = jnp.tile(inv, LANES // HALF)[None, :]
    wa = N_HEADS_A * HEAD_DIM
    wb = N_HEADS_B * HEAD_DIM
    wq = N_IDX_HEADS * IDX_DIM
    const = lambda b, i: (0, 0)
    tile = lambda b, i: (b, i, 0)
    per_b = lambda b, i: (b, 0, 0)
    out_shape = [jax.ShapeDtypeStruct((bsz, s, w), dt) for w, dt in
                 ((wa, bf16), (LANES, bf16), (LANES, bf16), (wq, bf16), (LANES, bf16), (LANES, f32),
                  (wb, bf16), (wb, bf16), (wb, bf16))]
    out_specs = [pl.BlockSpec((1, ts, sh_.shape[2]), tile) for sh_ in out_shape]
    return pl.pallas_call(
        _proj_kernel,
        grid=(bsz, s // ts),
        in_specs=[pl.BlockSpec((1, ts, d), tile),
                  pl.BlockSpec((1, ts, 1), tile),
                  pl.BlockSpec((1, 1, d), per_b),
                  pl.BlockSpec((1, 1, d), per_b),
                  pl.BlockSpec((1, d), const),
                  pl.BlockSpec((d, N_COL), const),
                  pl.BlockSpec((1, N_NORM_SLOTS * HALF), const),
                  pl.BlockSpec((1, N_NORM_SLOTS * HALF), const),
                  pl.BlockSpec((LANES, LANES), const),
                  pl.BlockSpec((2 * LANES, 2 * LANES), const),
                  pl.BlockSpec((1, LANES), const)],
        out_specs=out_specs,
        out_shape=out_shape,
        compiler_params=pltpu.CompilerParams(
            dimension_semantics=("arbitrary", "arbitrary"), vmem_limit_bytes=VMEM_LIMIT),
        name="in_proj",
    )(x, posf, sc, sh, g, w_perm, gain_a, gain_b, gsum, perm, invf)


def _sparse_attn_kernel(qa_ref, ka_ref, va_ref, qi_ref, ki_ref, wi_ref, triu_ref, o_ref,
                        score_ref, qis_ref, qas_ref, wrep_ref, m_ref, acc_ref,
                        *, qb, tk, n_sel, w_scale):
    i = pl.program_id(1)
    n_pairs_a = N_HEADS_A // 2
    lane = lax.broadcasted_iota(i32, (qb, LANES), 1)
    left = lane < HEAD_DIM
    mask_l = left.astype(f32).astype(bf16)
    mask_r = (1.0 - left.astype(f32)).astype(bf16)

    for c in range(N_IDX_HEADS // 2):
        ch = qi_ref[0, :, LANES * c:LANES * (c + 1)]
        qis_ref[(2 * c) * qb:(2 * c + 1) * qb, :] = ch * mask_l
        qis_ref[(2 * c + 1) * qb:(2 * c + 2) * qb, :] = ch * mask_r
    q_scale = jnp.asarray(HEAD_DIM ** -0.5, bf16)
    for c in range(n_pairs_a):
        ch = qa_ref[0, :, LANES * c:LANES * (c + 1)] * q_scale
        qas_ref[0, c * qb:(c + 1) * qb, :] = ch * mask_l
        qas_ref[1, c * qb:(c + 1) * qb, :] = ch * mask_r
    wi = wi_ref[0] * w_scale
    for h in range(N_IDX_HEADS):
        wrep_ref[h] = jnp.broadcast_to(wi[:, h:h + 1], (qb, LANES))

    n_tiles = ((i + 1) * qb - 1) // tk + 1
    reps = tk // LANES
    row = lax.broadcasted_iota(i32, (qb, tk), 0) + i * qb
    col = lax.broadcasted_iota(i32, (qb, tk), 1)

    def score_body(j, rmax):
        start = pl.multiple_of(j * tk, tk)
        kt = ki_ref[0, pl.ds(start, tk), :]
        lg = _nt_dot(qis_ref[...], kt)
        acc = jnp.zeros((qb, tk), f32)
        for h in range(N_IDX_HEADS):
            wr = jnp.concatenate([wrep_ref[h]] * reps, axis=1)
            acc = acc + jnp.maximum(lg[h * qb:(h + 1) * qb], 0.0) * wr
        acc = jnp.where(col + j * tk <= row, acc, -jnp.inf)
        score_ref[j] = acc
        for r in range(reps):
            rmax = jnp.maximum(rmax, acc[:, LANES * r:LANES * (r + 1)])
        return rmax

    rmax = lax.fori_loop(0, n_tiles, score_body, jnp.full((qb, LANES), -jnp.inf, f32))
    rmax = jnp.broadcast_to(jnp.max(rmax, axis=1, keepdims=True), (qb, LANES))

    def key_to_float(key):
        return lax.bitcast_convert_type(key ^ ((key >> 31) & 0x7FFFFFFF), f32)

    def float_to_key(v):
        bits = lax.bitcast_convert_type(v, i32)
        return bits ^ ((bits >> 31) & 0x7FFFFFFF)

    sweep_rows = min(qb, LANES)
    ones_rows = jnp.ones((SUBLANES, LANES), bf16)

    def to_rows(x_lane):
        return jnp.transpose(jnp.broadcast_to(x_lane[0:1, :], (LANES, qb)))

    def to_lanes(cnt):
        return _nt_dot(ones_rows, cnt.astype(bf16))

    def count_ge(trial_lane):
        trial_rows = to_rows(trial_lane)
        parts = []
        for r0 in range(0, qb, sweep_rows):
            trial_r = trial_rows[r0:r0 + sweep_rows]

            def body(j, cnt, r0=r0, trial_r=trial_r):
                for r in range(reps):
                    ch = score_ref[j, r0:r0 + sweep_rows, LANES * r:LANES * (r + 1)]
                    cnt = cnt + jnp.where(ch >= trial_r, 1.0, 0.0)
                return cnt
            parts.append(lax.fori_loop(0, n_tiles, body, jnp.zeros((sweep_rows, LANES), f32)))
        return to_lanes(jnp.concatenate(parts, axis=0))

    log_target = math.log(n_sel - 0.5)
    rmax = jnp.transpose(rmax)[0:SUBLANES, :]
    rmax_pad = rmax + jnp.abs(rmax) * 2.0 ** -20 + 1e-30

    def zero_counts(r0):
        def body(j, cnts):
            ge, gt = cnts
            for r in range(reps):
                ch = score_ref[j, r0:r0 + sweep_rows, LANES * r:LANES * (r + 1)]
                ge = ge + jnp.where(ch >= 0.0, 1.0, 0.0)
                gt = gt + jnp.where(ch > 0.0, 1.0, 0.0)
            return ge, gt
        z = jnp.zeros((sweep_rows, LANES), f32)
        ge, gt = lax.fori_loop(0, n_tiles, body, (z, z))
        return to_lanes(ge), to_lanes(gt)

    zero_parts = [zero_counts(r0) for r0 in range(0, qb, sweep_rows)]
    ge0 = jnp.concatenate([p[0] for p in zero_parts], axis=1)
    gt0 = jnp.concatenate([p[1] for p in zero_parts], axis=1)
    zeros = jnp.zeros((SUBLANES, qb), f32)
    total = zeros + (n_tiles * tk).astype(f32)
    above = ge0 >= n_sel
    lo_v0 = jnp.where(above, 0.0, -jnp.inf)
    lo_c0 = jnp.where(above, ge0, total)
    hi_v0 = jnp.where(above, jnp.inf, 0.0)
    hi_c0 = jnp.where(above, jnp.where(gt0 < n_sel, gt0, 0.0), ge0)
    done0 = (above & (gt0 < n_sel)) | (lo_c0 == n_sel)

    def search_cond(state):
        return (state[0] < SEARCH_MAX_STEPS) & state[-1]

    def search_body(state):
        it, lo_v, lo_c, f_lo, hi_v, hi_c, f_hi, last, done, _ = state
        lo_k = float_to_key(lo_v)
        hi_k = float_to_key(hi_v)
        hi_eff = jnp.where(hi_v == jnp.inf, rmax_pad, hi_v)
        t_int = lo_v + (hi_eff - lo_v) * (f_lo / (f_lo - f_hi))
        t_int = key_to_float(float_to_key(t_int))
        use_int = (lo_v > -jnp.inf) & (t_int > lo_v) & (t_int < hi_v) & (it < SEARCH_INTERP_STEPS)
        mid_k = (lo_k >> 1) + (hi_k >> 1) + (lo_k & hi_k & 1)
        t = jnp.where(use_int, t_int, key_to_float(mid_k))
        c = count_ge(t)
        f = jnp.log(jnp.maximum(c, 0.5)) - log_target
        active = done < 0.5
        is_lo = (c >= n_sel) & active
        is_hi = (c < n_sel) & active
        f_hi = jnp.where(is_lo & (last > 0.0), f_hi * 0.5, f_hi)
        f_lo = jnp.where(is_hi & (last < 0.0), f_lo * 0.5, f_lo)
        lo_v = jnp.where(is_lo, t, lo_v)
        lo_c = jnp.where(is_lo, c, lo_c)
        f_lo = jnp.where(is_lo, f, f_lo)
        hi_v = jnp.where(is_hi, t, hi_v)
        hi_c = jnp.where(is_hi, c, hi_c)
        f_hi = jnp.where(is_hi, f, f_hi)
        last = jnp.where(is_lo, 1.0, jnp.where(is_hi, -1.0, last))
        settled = (lo_c == n_sel) | (float_to_key(lo_v) + 1 >= float_to_key(hi_v))
        done = jnp.where(settled, 1.0, done)
        unresolved = jnp.min(done) < 0.5
        return it + 1, lo_v, lo_c, f_lo, hi_v, hi_c, f_hi, last, done, unresolved

    done0 = jnp.where(done0, 1.0, 0.0)
    init = (jnp.int32(0), lo_v0, lo_c0, jnp.log(lo_c0) - log_target,
            hi_v0, hi_c0, jnp.log(jnp.maximum(hi_c0, 0.5)) - log_target,
            zeros, done0, jnp.min(done0) < 0.5)
    final = lax.while_loop(search_cond, search_body, init)
    thr, lo_c, hi_c = final[1], final[2], final[5]
    thr_t = jnp.concatenate([to_rows(thr)] * reps, axis=1)
    tied = lo_c != n_sel
    any_tied = jnp.max(jnp.where(tied, 1.0, 0.0)) > 0.0
    quota = to_rows(jnp.where(tied, n_sel - hi_c, 2.0 * tk * (n_tiles + 1).astype(f32)))
    quota_t = jnp.concatenate([quota] * reps, axis=1)

    m_ref[...] = jnp.full(m_ref.shape, NEG_BIG, f32)
    acc_ref[...] = jnp.zeros(acc_ref.shape, f32)
    lane_k = lax.broadcasted_iota(i32, (tk, LANES), 1)
    kmask_l = (lane_k < HEAD_DIM).astype(f32).astype(bf16)
    kmask_r = (lane_k >= HEAD_DIM).astype(f32).astype(bf16)

    def attn_body(j, n_ties, with_ties):
        start = pl.multiple_of(j * tk, tk)
        sc = score_ref[j]
        if with_ties:
            eq = jnp.where(sc == thr_t, 1.0, 0.0)
            before = (jnp.dot(eq.astype(bf16), triu_ref[...], preferred_element_type=f32)
                      + jnp.concatenate([n_ties] * reps, axis=1))
            sel = (sc > thr_t) | ((sc == thr_t) & (before < quota_t))
            n_ties = n_ties + jnp.broadcast_to(jnp.sum(eq, axis=1, keepdims=True), (qb, LANES))
        else:
            sel = sc >= thr_t
        sel = sel & (col + j * tk <= row)
        bias = jnp.where(sel, 0.0, NEG_BIG)
        bias = jnp.concatenate([bias] * n_pairs_a, axis=0)
        kk = ka_ref[0, pl.ds(start, tk), :]
        vv = va_ref[0, pl.ds(start, tk), :]
        v_ext = (vv * kmask_l + kmask_r, vv * kmask_r + kmask_l)
        for g in range(N_KV_A):
            s = _nt_dot(qas_ref[g], kk) + bias
            m_old = m_ref[g]
            m_new = jnp.maximum(m_old, jnp.max(s, axis=1, keepdims=True))
            alpha = jnp.exp(m_old - m_new)
            p = jnp.exp(s - jnp.concatenate([m_new] * reps, axis=1))
            acc_ref[g] = alpha * acc_ref[g] + jnp.dot(p.astype(bf16), v_ext[g], preferred_element_type=f32)
            m_ref[g] = m_new
        return n_ties

    no_ties = jnp.zeros((qb, LANES), f32)

    @pl.when(any_tied)
    def _():
        lax.fori_loop(0, n_tiles, functools.partial(attn_body, with_ties=True), no_ties)

    @pl.when(jnp.logical_not(any_tied))
    def _():
        lax.fori_loop(0, n_tiles, functools.partial(attn_body, with_ties=False), no_ties)

    for c in range(n_pairs_a):
        rows = slice(c * qb, (c + 1) * qb)
        a0 = acc_ref[0, rows, :]
        a1 = acc_ref[1, rows, :]
        o0 = a0 / pltpu.roll(a0, HEAD_DIM, axis=1)
        o1 = a1 / pltpu.roll(a1, HEAD_DIM, axis=1)
        o_ref[0, :, LANES * c:LANES * (c + 1)] = jnp.where(left, o0, o1).astype(bf16)


def _sparse_attention(qa, ka, va, qi, ki, wi):
    bsz, s, wa = qa.shape
    qb = min(ATTN_QUERY_ROWS, s)
    tk = min(ATTN_KEY_TILE, s)
    n_sel = min(TOPK_MAX, s // 4)
    w_scale = N_IDX_HEADS ** -0.5 * IDX_DIM ** -0.5
    n_pairs_a = N_HEADS_A // 2
    tile = lambda b, i: (b, i, 0)
    per_b = lambda b, i: (b, 0, 0)
    kern = functools.partial(_sparse_attn_kernel, qb=qb, tk=tk, n_sel=float(n_sel), w_scale=w_scale)
    return pl.pallas_call(
        kern,
        grid=(bsz, s // qb),
        in_specs=[pl.BlockSpec((1, qb, wa), tile),
                  pl.BlockSpec((1, s, LANES), per_b),
                  pl.BlockSpec((1, s, LANES), per_b),
                  pl.BlockSpec((1, qb, qi.shape[2]), tile),
                  pl.BlockSpec((1, s, LANES), per_b),
                  pl.BlockSpec((1, qb, LANES), tile),
                  pl.BlockSpec((tk, tk), lambda b, i: (0, 0))],
        out_specs=pl.BlockSpec((1, qb, wa), tile),
        out_shape=jax.ShapeDtypeStruct((bsz, s, wa), bf16),
        scratch_shapes=[pltpu.VMEM((s // tk, qb, tk), f32),
                        pltpu.VMEM((N_IDX_HEADS * qb, LANES), bf16),
                        pltpu.VMEM((N_KV_A, n_pairs_a * qb, LANES), bf16),
                        pltpu.VMEM((N_IDX_HEADS, qb, LANES), f32),
                        pltpu.VMEM((N_KV_A, n_pairs_a * qb, LANES), f32),
                        pltpu.VMEM((N_KV_A, n_pairs_a * qb, LANES), f32)],
        compiler_params=pltpu.CompilerParams(
            dimension_semantics=("arbitrary", "arbitrary"), vmem_limit_bytes=VMEM_LIMIT),
        name="sparse_attn",
    )(qa, ka, va, qi, ki, wi, jnp.asarray(np.triu(np.ones((tk, tk), np.float32), k=1), bf16))


def _dilated_kernel(q_ref, k_ref, v_ref, bias_ref, o_ref, qs_ref, m_ref, acc_ref, *, tq, nd):
    i = pl.program_id(2)
    lane = lax.broadcasted_iota(i32, (tq, LANES), 1)
    left = lane < HEAD_DIM
    mask_l = left.astype(f32).astype(bf16)
    mask_r = (1.0 - left.astype(f32)).astype(bf16)
    q = q_ref[0] * jnp.asarray(HEAD_DIM ** -0.5, bf16)
    qs_ref[0:tq, :] = q * mask_l
    qs_ref[tq:2 * tq, :] = q * mask_r
    ones = jnp.ones((tq, LANES), bf16)

    for d in range(nd):
        j = i - d
        start = pl.multiple_of(jnp.maximum(j, 0) * tq, tq)
        kk = k_ref[0, pl.ds(start, tq), :]
        v_ext = jnp.concatenate([v_ref[0, pl.ds(start, tq), :], ones], axis=1)
        b = bias_ref[d] + jnp.where(j >= 0, 0.0, NEG_BIG)
        s = _nt_dot(qs_ref[...], kk) + jnp.concatenate([b, b], axis=0)
        m_d = jnp.broadcast_to(jnp.max(s, axis=1, keepdims=True), (2 * tq, LANES))
        p = jnp.exp(s - jnp.concatenate([m_d] * (tq // LANES), axis=1))
        acc_ref[d] = jnp.dot(p.astype(bf16), v_ext, preferred_element_type=f32)
        m_ref[d] = m_d

    m = m_ref[0]
    for d in range(1, nd):
        m = jnp.maximum(m, m_ref[d])
    acc = jnp.zeros((2 * tq, 2 * LANES), f32)
    for d in range(nd):
        w = jnp.exp(m_ref[d] - m)
        acc = acc + jnp.concatenate([w, w], axis=1) * acc_ref[d]
    o0 = acc[0:tq, 0:LANES] / acc[0:tq, LANES:2 * LANES]
    o1 = acc[tq:2 * tq, 0:LANES] / acc[tq:2 * tq, LANES:2 * LANES]
    o_ref[0] = jnp.where(left, o0, o1).astype(bf16)


def _dilated_attention(qb_, kb_, vb_):
    bsz, s, wb = qb_.shape
    tq = min(DILATED_TILE, s)
    bias_np, nd = _dilated_bias(tq)
    bias = jnp.asarray(bias_np)
    n_pairs = wb // LANES
    kern = functools.partial(_dilated_kernel, tq=tq, nd=nd)
    return pl.pallas_call(
        kern,
        grid=(bsz, n_pairs, s // tq),
        in_specs=[pl.BlockSpec((1, tq, LANES), lambda b, p, i: (b, i, p)),
                  pl.BlockSpec((1, s, LANES), lambda b, p, i: (b, 0, p)),
                  pl.BlockSpec((1, s, LANES), lambda b, p, i: (b, 0, p)),
                  pl.BlockSpec((nd, tq, tq), lambda b, p, i: (0, 0, 0))],
        out_specs=pl.BlockSpec((1, tq, LANES), lambda b, p, i: (b, i, p)),
        out_shape=jax.ShapeDtypeStruct((bsz, s, wb), bf16),
        scratch_shapes=[pltpu.VMEM((2 * tq, LANES), bf16),
                        pltpu.VMEM((nd, 2 * tq, LANES), f32),
                        pltpu.VMEM((nd, 2 * tq, 2 * LANES), f32)],
        compiler_params=pltpu.CompilerParams(
            dimension_semantics=("arbitrary", "arbitrary", "arbitrary"), vmem_limit_bytes=VMEM_LIMIT),
        name="dilated_attn",
    )(qb_, kb_, vb_, bias)


def _mid_kernel(oa_ref, ob_ref, x_ref, gta_ref, scf_ref, shf_ref, gtf_ref, woa_ref, wob_ref, gffn_ref,
                wsgu_ref, wsd_ref, wrt_ref, rbias_ref, triu_ref,
                base_ref, h2_ref, sel_ref, gate_ref, rank_ref, cnt_ref, carry_ref, *, n_exp, t, d_sh):
    step = pl.program_id(0)

    @pl.when(step == 0)
    def _():
        carry_ref[...] = jnp.zeros(carry_ref.shape, f32)

    mix = (jnp.dot(oa_ref[...], woa_ref[...], preferred_element_type=f32)
           + jnp.dot(ob_ref[...], wob_ref[...], preferred_element_type=f32))
    x1 = x_ref[...] + gta_ref[0] * mix
    ms = jnp.mean(x1 * x1, axis=-1, keepdims=True)
    h2 = ((x1 * lax.rsqrt(ms + EPS)) * gffn_ref[...]) * (1.0 + scf_ref[0]) + shf_ref[0]
    h2b = h2.astype(bf16)
    h2f = h2b.astype(f32)
    n_sub = h2_ref.shape[0] // t
    for a in range(n_sub):
        h2_ref[pl.ds(a, t, stride=n_sub), :] = h2f[:, LANES * a:LANES * (a + 1)]

    gu = jnp.dot(h2b, wsgu_ref[...], preferred_element_type=f32)
    g = gu[:, :d_sh]
    u = gu[:, d_sh:]
    act = (g * jax.nn.sigmoid(g)) * u
    shared = jnp.dot(act.astype(bf16), wsd_ref[...], preferred_element_type=f32)
    base = x1 + gtf_ref[0] * shared
    for a in range(n_sub):
        base_ref[pl.ds(a, t, stride=n_sub), :] = base[:, LANES * a:LANES * (a + 1)]

    scores = jax.nn.sigmoid(_nt_dot(wrt_ref[...], h2b))
    biased = scores + rbias_ref[...]
    per = n_exp // N_GROUPS
    neg_inf = jnp.float32(-jnp.inf)
    ri_g = lax.broadcasted_iota(i32, (per, t), 0).astype(f32)
    gs = []
    for grp in range(N_GROUPS):
        blk = biased[grp * per:(grp + 1) * per]
        m1 = jnp.max(blk, axis=0, keepdims=True)
        idx1 = jnp.min(jnp.where(blk == m1, ri_g, float(per)), axis=0, keepdims=True)
        m2 = jnp.max(jnp.where(ri_g == idx1, neg_inf, blk), axis=0, keepdims=True)
        gs.append(m1 + m2)
    masked_rows = []
    for grp in range(N_GROUPS):
        beaten = jnp.zeros((1, t), f32)
        for g2 in range(N_GROUPS):
            if g2 == grp:
                continue
            wins = gs[g2] > gs[grp]
            if g2 < grp:
                wins = wins | (gs[g2] == gs[grp])
            beaten = beaten + jnp.where(wins, 1.0, 0.0)
        keep = jnp.broadcast_to(beaten < TOPK_GROUPS, (per, t))
        masked_rows.append(jnp.where(keep, biased[grp * per:(grp + 1) * per], neg_inf))
    masked = jnp.concatenate(masked_rows, axis=0)

    ri = lax.broadcasted_iota(i32, (n_exp, t), 0).astype(f32)
    selmask = jnp.zeros((n_exp, t), f32)
    idxs, graw = [], []
    for _ in range(TOP_K):
        m = jnp.max(masked, axis=0, keepdims=True)
        idx = jnp.min(jnp.where(masked == m, ri, float(n_exp)), axis=0, keepdims=True)
        onehot = ri == idx
        graw.append(jnp.sum(jnp.where(onehot, scores, 0.0), axis=0, keepdims=True))
        masked = jnp.where(onehot, neg_inf, masked)
        selmask = jnp.where(onehot, 1.0, selmask)
        idxs.append(idx)
    den = graw[0]
    for k in range(1, TOP_K):
        den = den + graw[k]

    prefix = jnp.dot(selmask.astype(bf16), triu_ref[...], preferred_element_type=f32)
    prefix = prefix + jnp.concatenate([carry_ref[...]] * (t // LANES), axis=1)
    for k in range(TOP_K):
        rank_k = jnp.sum(jnp.where(ri == idxs[k], prefix, 0.0), axis=0, keepdims=True)
        sel_ref[k:k + 1, :] = idxs[k].astype(i32)
        rank_ref[k:k + 1, :] = rank_k.astype(i32)
        gate_ref[k:k + 1, :] = graw[k] / den * ROUTED_SCALE
    carry_ref[...] = carry_ref[...] + jnp.broadcast_to(
        jnp.sum(selmask, axis=1, keepdims=True), carry_ref.shape)
    cnt_ref[...] = carry_ref[...]


def _mid(oa, ob, x2, gta, scf, shf, gtf, woa, wob, gffn, wsgu, wsd, wrt, rbias_rep, s):
    n, d = x2.shape
    t = rbias_rep.shape[1]
    n_exp = wrt.shape[0]
    d_sh = wsd.shape[0]
    triu = jnp.asarray(np.triu(np.ones((t, t), np.float32), k=1), bf16)
    tile = lambda i: (i, 0)
    const = lambda i: (0, 0)
    per_b = lambda i: ((i * t) // s, 0, 0)
    lane_tile = lambda i: (0, i)
    n_sub = d // LANES
    tok_tiles = jax.ShapeDtypeStruct((n * n_sub, LANES), f32)
    tok_spec = pl.BlockSpec((t * n_sub, LANES), tile)
    out_shape = [tok_tiles, tok_tiles,
                 jax.ShapeDtypeStruct((TOP_K, n), i32), jax.ShapeDtypeStruct((TOP_K, n), f32),
                 jax.ShapeDtypeStruct((TOP_K, n), i32), jax.ShapeDtypeStruct((n_exp, LANES), f32)]
    out_specs = [tok_spec, tok_spec,
                 pl.BlockSpec((TOP_K, t), lane_tile), pl.BlockSpec((TOP_K, t), lane_tile),
                 pl.BlockSpec((TOP_K, t), lane_tile), pl.BlockSpec((n_exp, LANES), const)]
    kern = functools.partial(_mid_kernel, n_exp=n_exp, t=t, d_sh=d_sh)
    return pl.pallas_call(
        kern,
        grid=(n // t,),
        in_specs=[pl.BlockSpec((t, oa.shape[1]), tile),
                  pl.BlockSpec((t, ob.shape[1]), tile),
                  pl.BlockSpec((t, d), tile),
                  pl.BlockSpec((1, 1, d), per_b),
                  pl.BlockSpec((1, 1, d), per_b),
                  pl.BlockSpec((1, 1, d), per_b),
                  pl.BlockSpec((1, 1, d), per_b),
                  pl.BlockSpec(woa.shape, const),
                  pl.BlockSpec(wob.shape, const),
                  pl.BlockSpec((1, d), const),
                  pl.BlockSpec(wsgu.shape, const),
                  pl.BlockSpec(wsd.shape, const),
                  pl.BlockSpec(wrt.shape, const),
                  pl.BlockSpec(rbias_rep.shape, const),
                  pl.BlockSpec((t, t), const)],
        out_specs=out_specs,
        out_shape=out_shape,
        scratch_shapes=[pltpu.VMEM((n_exp, LANES), f32)],
        compiler_params=pltpu.CompilerParams(
            dimension_semantics=("arbitrary",), vmem_limit_bytes=VMEM_LIMIT),
        name="mid",
    )(oa, ob, x2, gta, scf, shf, gtf, woa, wob, gffn, wsgu, wsd, wrt, rbias_rep, triu)


def _dispatch_kernel(zb_ref, slot_ref, h_ref, xs_ref, zero_ref, sem_ref, *, td, n_zero):
    step = pl.program_id(0)

    def zero_copy(b):
        return pltpu.make_async_copy(zero_ref, xs_ref.at[pl.ds(zb_ref[b], ZERO_ROWS)], sem_ref.at[0])

    @pl.when(step == 0)
    def _():
        zero_ref[...] = jnp.zeros(zero_ref.shape, f32)

        def start(b, c):
            @pl.when(zb_ref[b] >= 0)
            def _():
                zero_copy(b).start()
            return c

        def wait(b, c):
            @pl.when(zb_ref[b] >= 0)
            def _():
                zero_copy(b).wait()
            return c

        lax.fori_loop(0, n_zero, start, 0)
        lax.fori_loop(0, n_zero, wait, 0)

    def start_rows(tok, c):
        for k in range(TOP_K):
            pltpu.make_async_copy(h_ref.at[pl.ds(tok, 1)], xs_ref.at[pl.ds(slot_ref[k, tok], 1)],
                                  sem_ref.at[1]).start(priority=k % 2)
        return c

    lax.fori_loop(0, td, start_rows, 0)
    for k in range(TOP_K):
        pltpu.make_async_copy(h_ref, xs_ref.at[pl.ds(0, td)], sem_ref.at[1]).wait()


def _dispatch(zero_start, slot, h2, n_slots):
    n, rows, lanes = h2.shape
    td = min(DISPATCH_ROWS, n)
    kern = functools.partial(_dispatch_kernel, td=td, n_zero=zero_start.shape[0])
    grid_spec = pltpu.PrefetchScalarGridSpec(
        num_scalar_prefetch=1,
        grid=(n // td,),
        in_specs=[pl.BlockSpec((TOP_K, td), lambda i, tail: (0, i), memory_space=pltpu.SMEM),
                  pl.BlockSpec((td, rows, lanes), lambda i, tail: (i, 0, 0))],
        out_specs=pl.BlockSpec(memory_space=pl.ANY),
        scratch_shapes=[pltpu.VMEM((ZERO_ROWS, rows, lanes), f32), pltpu.SemaphoreType.DMA((2,))],
    )
    return pl.pallas_call(
        kern,
        grid_spec=grid_spec,
        out_shape=jax.ShapeDtypeStruct((n_slots, rows, lanes), f32),
        compiler_params=pltpu.CompilerParams(
            dimension_semantics=("arbitrary",), vmem_limit_bytes=VMEM_LIMIT),
        name="dispatch",
    )(zero_start, slot, h2)


def _expert_kernel(be_ref, nu_ref, first_ref, par_ref, nexte_ref, hasnext_ref,
                   xs_ref, wg_hbm, wu_hbm, wd_hbm, ys_ref,
                   wg_buf, wu_buf, wd_buf, wgu_s, wd_s, sem_ref, *, n_sub):
    i = pl.program_id(0)
    f = wd_s.shape[0]
    rows = xs_ref.shape[0] // n_sub

    def weight_copies(e, slot):
        return (pltpu.make_async_copy(wg_hbm.at[e], wg_buf.at[slot], sem_ref.at[slot]),
                pltpu.make_async_copy(wu_hbm.at[e], wu_buf.at[slot], sem_ref.at[slot]),
                pltpu.make_async_copy(wd_hbm.at[e], wd_buf.at[slot], sem_ref.at[slot]))

    @pl.when(i >= nu_ref[0])
    def _():
        ys_ref[...] = jnp.zeros(ys_ref.shape, f32)

    @pl.when(i < nu_ref[0])
    def _():
        @pl.when(first_ref[i] == 1)
        def _():
            slot = par_ref[i]

            @pl.when(i == 0)
            def _():
                for cp in weight_copies(be_ref[0], 0):
                    cp.start()

            for cp in weight_copies(be_ref[i], slot):
                cp.wait()

            @pl.when(hasnext_ref[i] == 1)
            def _():
                for cp in weight_copies(nexte_ref[i], 1 - slot):
                    cp.start()

            wgu_s[:, 0:f] = wg_buf[slot].astype(bf16)
            wgu_s[:, f:2 * f] = wu_buf[slot].astype(bf16)
            wd_s[...] = wd_buf[slot].astype(bf16)

        xb = jnp.concatenate([xs_ref[pl.ds(a, rows, stride=n_sub), :] for a in range(n_sub)],
                             axis=1).astype(bf16)
        gu = jnp.dot(xb, wgu_s[...], preferred_element_type=f32)
        g = gu[:, 0:f]
        u = gu[:, f:2 * f]
        act = (g * jax.nn.sigmoid(g)) * u
        y = jnp.dot(act.astype(bf16), wd_s[...], preferred_element_type=f32)
        for a in range(n_sub):
            ys_ref[pl.ds(a, rows, stride=n_sub), :] = y[:, LANES * a:LANES * (a + 1)]


def _experts(block_expert, n_used, run_first, run_parity, next_expert, has_next, xs, w_gate, w_up, w_down):
    n_slots, n_sub, lanes = xs.shape
    d = n_sub * lanes
    n_blocks = n_slots // MOE_BLOCK
    f = w_gate.shape[2]
    blk = lambda i, be, nu, *_: (jnp.minimum(i, nu[0] - 1), 0)
    hbm = pl.BlockSpec(memory_space=pl.ANY)
    grid_spec = pltpu.PrefetchScalarGridSpec(
        num_scalar_prefetch=6,
        grid=(n_blocks,),
        in_specs=[pl.BlockSpec((MOE_BLOCK * n_sub, lanes), blk), hbm, hbm, hbm],
        out_specs=pl.BlockSpec((MOE_BLOCK * n_sub, lanes), lambda i, *_: (i, 0)),
        scratch_shapes=[pltpu.VMEM((2, d, f), f32), pltpu.VMEM((2, d, f), f32), pltpu.VMEM((2, f, d), f32),
                        pltpu.VMEM((d, 2 * f), bf16), pltpu.VMEM((f, d), bf16),
                        pltpu.SemaphoreType.DMA((2,))],
    )
    ys = pl.pallas_call(
        functools.partial(_expert_kernel, n_sub=n_sub),
        grid_spec=grid_spec,
        out_shape=jax.ShapeDtypeStruct((n_slots * n_sub, lanes), f32),
        compiler_params=pltpu.CompilerParams(
            dimension_semantics=("arbitrary",), vmem_limit_bytes=VMEM_LIMIT),
        name="experts",
    )(block_expert, n_used, run_first, run_parity, next_expert, has_next,
      xs.reshape(n_slots * n_sub, lanes), w_gate, w_up, w_down)
    return ys.reshape(n_slots, n_sub, lanes)


def _combine_kernel(slot_ref, nslot_ref, gate_ref, base_ref, gtf_ref, ys_ref, o_ref, buf_ref, res_ref, sem_ref,
                    *, tc):
    step = pl.program_id(0)
    cur = step % 2

    def start_gathers(idx_ref, b):
        def body(tok, c):
            for k in range(TOP_K):
                pltpu.make_async_copy(ys_ref.at[pl.ds(idx_ref[k, tok], 1)],
                                      buf_ref.at[b, k, pl.ds(tok, 1)], sem_ref.at[b]).start(priority=k % 2)
            return c
        lax.fori_loop(0, tc, body, 0)

    @pl.when(step == 0)
    def _():
        start_gathers(slot_ref, 0)

    @pl.when(step + 1 < pl.num_programs(0))
    def _():
        start_gathers(nslot_ref, 1 - cur)

    for k in range(TOP_K):
        pltpu.make_async_copy(ys_ref.at[pl.ds(0, tc)], buf_ref.at[cur, k], sem_ref.at[cur]).wait()

    gtf = gtf_ref[0]

    def reduce_token(tok, c):
        routed = buf_ref[cur, 0, tok] * gate_ref[0, tok]
        for k in range(1, TOP_K):
            routed = routed + buf_ref[cur, k, tok] * gate_ref[k, tok]
        res_ref[pl.ds(pl.multiple_of(tok * n_sub, n_sub), n_sub), :] = base_ref[tok] + gtf * routed
        return c

    n_sub = buf_ref.shape[3]
    lax.fori_loop(0, tc, reduce_token, 0, unroll=4)
    for a in range(n_sub):
        o_ref[:, LANES * a:LANES * (a + 1)] = res_ref[pl.ds(a, tc, stride=n_sub), :]


def _combine(slot, gate, base, gtf, ys, s):
    n, rows, lanes = base.shape
    tc = min(COMBINE_ROWS, n)
    n_steps = n // tc
    kern = functools.partial(_combine_kernel, tc=tc)
    tok_tile = lambda i: (i, 0, 0)
    return pl.pallas_call(
        kern,
        grid=(n_steps,),
        in_specs=[pl.BlockSpec((TOP_K, tc), lambda i: (0, i), memory_space=pltpu.SMEM),
                  pl.BlockSpec((TOP_K, tc), lambda i: (0, jnp.minimum(i + 1, n_steps - 1)),
                               memory_space=pltpu.SMEM),
                  pl.BlockSpec((TOP_K, tc), lambda i: (0, i), memory_space=pltpu.SMEM),
                  pl.BlockSpec((tc, rows, lanes), tok_tile),
                  pl.BlockSpec((1, rows, lanes), lambda i: ((i * tc) // s, 0, 0)),
                  pl.BlockSpec(memory_space=pl.ANY)],
        out_specs=pl.BlockSpec((tc, rows * lanes), lambda i: (i, 0)),
        out_shape=jax.ShapeDtypeStruct((n, rows * lanes), f32),
        scratch_shapes=[pltpu.VMEM((2, TOP_K, tc, rows, lanes), f32), pltpu.VMEM((tc * rows, lanes), f32),
                        pltpu.SemaphoreType.DMA((2,))],
        compiler_params=pltpu.CompilerParams(
            dimension_semantics=("arbitrary",), vmem_limit_bytes=VMEM_LIMIT),
        name="combine",
    )(slot, slot, gate, base, gtf, ys)


def _layer(x, c, posf, w_ada, b_ada, g_mix, w_in, q_norm_a, k_norm_a, q_norm_b, k_norm_b, w_out, g_ffn,
           w_router, router_bias, w_gate, w_up, w_down, ws_gate, ws_up, ws_down):
    bsz, s, d = x.shape
    n = bsz * s
    n_exp = w_router.shape[1]

    mod = _adaln(c, w_ada, b_ada)[:, None, :]
    sh_a, sc_a, gt_a, sh_f, sc_f, gt_f = (mod[..., k * d:(k + 1) * d] for k in range(6))

    w_perm = _take_runs(w_in.astype(bf16), _projection_columns(), axis=1)
    gains = {"qa": q_norm_a, "ka": k_norm_a, "qb": q_norm_b, "kb": k_norm_b}
    gain_a = jnp.concatenate([gains[kind][:HALF] for kind, _ in _SLOTS[:N_NORM_SLOTS]])[None, :].astype(f32)
    gain_b = jnp.concatenate([gains[kind][HALF:] for kind, _ in _SLOTS[:N_NORM_SLOTS]])[None, :].astype(f32)

    qa, ka, va, qi, ki, wi, qb_, kb_, vb_ = _project(x, posf, sc_a, sh_a, g_mix[None, :], w_perm, gain_a, gain_b)
    o_a = _sparse_attention(qa, ka, va, qi, ki, wi)
    o_b = _dilated_attention(qb_, kb_, vb_)

    wa = N_HEADS_A * HEAD_DIM
    rows_a = np.concatenate([np.arange(h * HEAD_DIM, (h + 1) * HEAD_DIM) for h in QA_PAIR_ORDER])
    woa = _take_runs(w_out, rows_a, axis=0).astype(bf16)
    wob = w_out[wa:].astype(bf16)
    wsgu = jnp.concatenate([ws_gate, ws_up], axis=1).astype(bf16)
    wsd = ws_down.astype(bf16)
    wrt = w_router.T.astype(bf16)
    t_mid = min(MID_ROWS, n)
    rbias_rep = jnp.broadcast_to(router_bias.astype(f32)[:, None], (n_exp, t_mid))

    base, h2, sel, gate, rank, cnt = _mid(
        o_a.reshape(n, wa), o_b.reshape(n, -1), x.reshape(n, d), gt_a, sc_f, sh_f, gt_f,
        woa, wob, g_ffn[None, :], wsgu, wsd, wrt, rbias_rep, s)
    base = base.reshape(n, d // LANES, LANES)
    h2 = h2.reshape(n, d // LANES, LANES)

    counts = cnt[:, 0].astype(i32)
    padded = (counts + MOE_BLOCK - 1) // MOE_BLOCK * MOE_BLOCK
    pad_end = jnp.cumsum(padded)
    pad_start = pad_end - padded
    onehot = sel[:, :, None] == jnp.arange(n_exp, dtype=i32)[None, None, :]
    slot = jnp.sum(jnp.where(onehot, pad_start[None, None, :], 0), axis=-1) + rank
    n_blocks = -(-(n * TOP_K) // MOE_BLOCK) + n_exp
    n_slots = n_blocks * MOE_BLOCK
    block_start = jnp.arange(n_blocks, dtype=i32) * MOE_BLOCK
    block_expert = jnp.sum((pad_end[None, :] <= block_start[:, None]).astype(i32), axis=1)
    block_expert = jnp.minimum(block_expert, n_exp - 1)
    n_used = (pad_end[-1] // MOE_BLOCK).astype(i32)[None]
    unit_start = jnp.arange(n_slots // ZERO_ROWS, dtype=i32) * ZERO_ROWS
    unit_expert = jnp.sum((pad_end[None, :] <= unit_start[:, None]).astype(i32), axis=1)
    real_end = pad_start + counts
    unit_real_end = jnp.sum(jnp.where(unit_expert[:, None] == jnp.arange(n_exp, dtype=i32)[None, :],
                                      real_end[None, :], 0), axis=1)
    has_pad = (unit_start + ZERO_ROWS > unit_real_end) | (unit_start >= pad_end[-1])
    zero_start = jnp.where(has_pad, unit_start, -1).astype(i32)

    xs = _dispatch(zero_start, slot, h2, n_slots)
    run_first = jnp.concatenate([jnp.ones((1,), i32), (block_expert[1:] != block_expert[:-1]).astype(i32)])
    run_parity = ((jnp.cumsum(run_first) - 1) % 2).astype(i32)
    run_end = pad_end[block_expert] // MOE_BLOCK
    has_next = (run_end < n_used[0]).astype(i32)
    next_expert = block_expert[jnp.minimum(run_end, n_blocks - 1)]
    ys = _experts(block_expert, n_used, run_first, run_parity, next_expert, has_next, xs, w_gate, w_up, w_down)
    out = _combine(slot, gate, base, gt_f.reshape(bsz, d // LANES, LANES), ys, s)
    return out.reshape(bsz, s, d)


def kernel(x, c, positions, w_ada, b_ada, g_mix, w_in, q_norm_a, k_norm_a, q_norm_b, k_norm_b, w_out, g_ffn,
           w_router, router_bias, w_gate, w_up, w_down, ws_gate, ws_up, ws_down):
    posf = positions.astype(f32)[..., None]
    for l in range(w_ada.shape[0]):
        x = _layer(x, c, posf, w_ada[l], b_ada[l], g_mix[l], w_in[l], q_norm_a[l], k_norm_a[l], q_norm_b[l],
                   k_norm_b[l], w_out[l], g_ffn[l], w_router[l], router_bias[l], w_gate[l], w_up[l], w_down[l],
                   ws_gate[l], ws_up[l], ws_down[l])
    return x
```
